```python
import jax
import jax.numpy as jnp
from jax import lax
import numpy as np

D_MODEL = 1024
BATCH = 16
SEQ = 256
DEPTH = 2
DEC_BATCH = 8
DEC_SEQ = 4096
PAST_LEN = 256

GRID_W = 64
N_BRANCHES = 3
RET_HEADS = 4
RET_DK = D_MODEL // (2 * RET_HEADS)
RET_DV = D_MODEL // RET_HEADS
RET_CHUNK = 128
ROPE_BASE = 10000.0
GLA_HEADS = 4
GLA_DK = D_MODEL // (2 * GLA_HEADS)
GLA_DV = D_MODEL // GLA_HEADS
GLA_CHUNK = 64
GLA_LOWRANK = 16
GLA_GATE_NORM = 16.0
RWKV_HEAD_DIM = 64
RWKV_HEADS = D_MODEL // RWKV_HEAD_DIM
RWKV_W_LORA = 64
RWKV_A_LORA = 64
RWKV_G_LORA = 128
N_GROUPS = 4
EXPERTS_PER_GROUP = 4
N_EXPERTS = N_GROUPS * EXPERTS_PER_GROUP
TOP_K_IN_GROUP = 2
EXPERT_FF = 512
ALPHA = (2 * DEPTH) ** 0.25
BETA = (8 * DEPTH) ** -0.25
NORM_EPS = 1e-5

RET_GLA_COLS = (RET_HEADS * RET_DK, RET_HEADS * RET_DK, RET_HEADS * RET_DV, RET_HEADS * RET_DV,
                GLA_HEADS * GLA_DK, GLA_HEADS * GLA_DK, GLA_HEADS * GLA_DV, GLA_HEADS * GLA_DV,
                GLA_LOWRANK)
RWKV_COLS = (D_MODEL, D_MODEL, D_MODEL, RWKV_W_LORA, RWKV_A_LORA, RWKV_G_LORA)
N_RWKV_COLS = sum(RWKV_COLS)
N_IN_COLS = sum(RET_GLA_COLS) + N_RWKV_COLS

kernel_name = 'bidir_ret_gla_rwkv7_hmoe_diffusion_step'


def _split(x, sizes):
    return jnp.split(x, np.cumsum(sizes)[:-1].tolist(), axis=-1)


def _layernorm(x):
    xf = x.astype(jnp.float32)
    mu = jnp.mean(xf, axis=-1, keepdims=True)
    var = jnp.mean(jnp.square(xf - mu), axis=-1, keepdims=True)
    return (xf - mu) * lax.rsqrt(var + NORM_EPS)


def _ln_affine(x, gain, bias):
    return (_layernorm(x) * gain + bias).astype(x.dtype)


def _head_groupnorm(y, gain, bias=None):
    H, d = y.shape[-2:]
    out = _layernorm(y) * gain.reshape(H, d)
    return out if bias is None else out + bias.reshape(H, d)


def _head_rmsnorm(y, gain):
    H, d = y.shape[-2:]
    yf = y.astype(jnp.float32)
    return yf * lax.rsqrt(jnp.mean(jnp.square(yf), -1, keepdims=True) + NORM_EPS) * gain.reshape(H, d)


def _flip(t):
    return jnp.flip(t, axis=1)


def _axial_rope(x, n_rows):
    dk = x.shape[-1]
    half, quarter = dk // 2, dk // 4
    t = jnp.arange(n_rows * GRID_W)
    rows = (t // GRID_W).astype(jnp.float32)
    cols = (t % GRID_W).astype(jnp.float32)
    freqs = ROPE_BASE ** (-jnp.arange(quarter, dtype=jnp.float32) / quarter)
    ang = jnp.concatenate([rows[:, None] * freqs, cols[:, None] * freqs], -1)[:, None, :]
    cos, sin = jnp.cos(ang), jnp.sin(ang)
    x1, x2 = x[..., :half], x[..., half:]
    return jnp.concatenate([x1 * cos - x2 * sin, x1 * sin + x2 * cos], -1)


def _centred_shift(x, taps):
    L = x.shape[1]
    xp = jnp.pad(x, ((0, 0), (1, 1), (0, 0)))
    return taps[0] * xp[:, :L] + taps[1] * xp[:, 1:L + 1] + taps[2] * xp[:, 2:]


def _retention_chunked(q, k, v, log_gamma, s0):
    B, L, H, dk = q.shape
    dv = v.shape[-1]
    C = RET_CHUNK
    n = L // C
    q = q.reshape(B, n, C, H, dk)
    k = k.reshape(B, n, C, H, dk)
    v = v.reshape(B, n, C, H, dv)
    pos = jnp.arange(C, dtype=jnp.float32)
    rel = pos[:, None] - pos[None, :]
    decay = jnp.where(rel[None] >= 0,
                      jnp.exp(jnp.maximum(rel, 0.0)[None] * log_gamma[:, None, None]), 0.0)
    scores = jnp.einsum('bnihd,bnjhd->bnhij', q, k) * decay
    intra = jnp.einsum('bnhij,bnjhe->bnihe', scores, v)
    k_dec = jnp.exp((C - 1.0 - pos)[:, None] * log_gamma[None, :])
    kv = jnp.einsum('bnjhd,jh,bnjhe->nbhde', k, k_dec, v)
    g_chunk = jnp.exp(C * log_gamma)[:, None, None]

    def step(s, kv_c):
        return s * g_chunk + kv_c, s

    s_fin, s_prev = lax.scan(step, s0.astype(jnp.float32), kv)
    q_dec = jnp.exp((pos + 1.0)[:, None] * log_gamma[None, :])
    cross = jnp.einsum('bnihd,ih,nbhde->bnihe', q, q_dec, s_prev)
    return (intra + cross).reshape(B, L, H, dv), s_fin


def _gla_chunked(q, k, v, log_a, s0):
    B, L, H, dk = q.shape
    dv = v.shape[-1]
    C = GLA_CHUNK
    n = L // C

    def blocks(t):
        return t.reshape(B, n, C, H, t.shape[-1]).transpose(1, 0, 3, 2, 4)

    causal = jnp.tril(jnp.ones((C, C), dtype=bool))[:, :, None]

    def step(s, blk):
        qc, kc, vc, gc = blk
        b = jnp.cumsum(gc, axis=2)
        o_inter = jnp.einsum('bhcd,bhde->bhce', qc * jnp.exp(b), s)
        rel = jnp.exp(jnp.where(causal, b[:, :, :, None, :] - b[:, :, None, :, :], -jnp.inf))
        attn = jnp.einsum('bhid,bhjd,bhijd->bhij', qc, kc, rel)
        o = o_inter + jnp.einsum('bhij,bhje->bhie', attn, vc)
        b_end = b[:, :, -1:, :]
        s = s * jnp.exp(b_end[:, :, 0, :, None]) + jnp.einsum('bhjd,bhje->bhde', kc * jnp.exp(b_end - b), vc)
        return s, o

    s_fin, o = lax.scan(step, s0.astype(jnp.float32), (blocks(q), blocks(k), blocks(v), blocks(log_a)))
    return o.transpose(1, 0, 3, 2, 4).reshape(B, L, H, dv), s_fin


def _rwkv7_scan(r, decay, k, v, kk, a, s0):
    def step(s, inp):
        r_t, w_t, k_t, v_t, kk_t, a_t = inp
        sa = jnp.einsum('bhvk,bhk->bhv', s, -kk_t)
        s = (s * w_t[:, :, None, :] + sa[..., None] * (kk_t * a_t)[:, :, None, :]
             + v_t[..., None] * k_t[:, :, None, :])
        return s, jnp.einsum('bhvk,bhk->bhv', s, r_t)

    xs = tuple(jnp.moveaxis(t, 1, 0) for t in (r, decay, k, v, kk, a))
    s_fin, y = lax.scan(step, s0.astype(jnp.float32), xs)
    return jnp.moveaxis(y, 0, 1), s_fin


def _token_mixer(u, p, s_in, n_rows):
    B, L, _ = u.shape
    f32 = jnp.float32
    rq, rk, rv, rg, gq, gk, gv, gg, glr, rw = _split(u @ p['w_in'], RET_GLA_COLS + (N_RWKV_COLS,))

    q = rq.reshape(B, L, RET_HEADS, RET_DK).astype(f32)
    k = rk.reshape(B, L, RET_HEADS, RET_DK).astype(f32) * RET_DK ** -0.5
    v = rv.reshape(B, L, RET_HEADS, RET_DV).astype(f32)
    if n_rows is not None:
        q = _axial_rope(q, n_rows)
        k = _axial_rope(k, n_rows)
    lg = jnp.log(1.0 - 2.0 ** (-5.0 - jnp.arange(RET_HEADS, dtype=f32)))
    o_f, sr_f = _retention_chunked(q, k, v, lg, s_in[0])
    o_b, sr_b = _retention_chunked(_flip(q), _flip(k), _flip(v), lg[::-1], s_in[1])
    y = _head_groupnorm(o_f + _flip(o_b), p['ret_gn']).reshape(B, L, -1)
    ret_out = (jax.nn.silu(rg.astype(f32)) * y).astype(u.dtype) @ p['w_br'][0]

    q = gq.reshape(B, L, GLA_HEADS, GLA_DK).astype(f32) * GLA_DK ** -0.5
    k = gk.reshape(B, L, GLA_HEADS, GLA_DK).astype(f32)
    v = gv.reshape(B, L, GLA_HEADS, GLA_DV).astype(f32)
    log_a = [(jax.nn.log_sigmoid((glr @ p['gla_w2'][d] + p['gla_b'][d]).astype(f32)) / GLA_GATE_NORM)
             .reshape(B, L, GLA_HEADS, GLA_DK) for d in range(2)]
    o_f, sg_f = _gla_chunked(q, k, v, log_a[0], s_in[2])
    o_b, sg_b = _gla_chunked(_flip(q), _flip(k), _flip(v), _flip(log_a[1]), s_in[3])
    y = _head_rmsnorm(o_f + _flip(o_b), p['gla_gn']).reshape(B, L, -1)
    gla_out = (jax.nn.silu(gg.astype(f32)) * y).astype(u.dtype) @ p['w_br'][1]

    H, N = RWKV_HEADS, RWKV_HEAD_DIM
    xr, xk, xv, xw, xa, xg = _split(_centred_shift(rw, p['rwkv_shift']), RWKV_COLS)
    r = xr.reshape(B, L, H, N).astype(f32)
    k = xk.reshape(B, L, H, N).astype(f32)
    v = xv.reshape(B, L, H, N).astype(f32)
    kk = k * p['rwkv_kk'].reshape(H, N)
    kk = kk * lax.rsqrt(jnp.maximum(jnp.sum(jnp.square(kk), -1, keepdims=True), 1e-24))
    w_hidden = jnp.tanh(xw)
    ys, s_rw, bonus = [], [], 0.0
    for d in range(2):
        w = -jax.nn.softplus(-(p['rwkv_w0'][d] + w_hidden @ p['rwkv_wb'][d]).astype(f32)) - 0.5
        decay = jnp.exp(-jnp.exp(w)).reshape(B, L, H, N)
        a = jax.nn.sigmoid((p['rwkv_a0'][d] + xa @ p['rwkv_ab'][d]).astype(f32)).reshape(B, L, H, N)
        kd = k * (1.0 + (a - 1.0) * p['rwkv_ka'].reshape(H, N))
        bonus = bonus + jnp.sum(r * kd * p['rwkv_rk'], -1, keepdims=True) * v
        seq = (r, decay, kd, v, kk, a)
        if d == 1:
            seq = tuple(_flip(t) for t in seq)
        y_d, s_d = _rwkv7_scan(*seq, s_in[4 + d])
        ys.append(y_d if d == 0 else _flip(y_d))
        s_rw.append(s_d)
    y = _head_groupnorm(ys[0] + ys[1], p['rwkv_gn_g'], p['rwkv_gn_b']) + bonus
    g = jax.nn.sigmoid(xg) @ p['rwkv_gb']
    rwkv_out = (y.reshape(B, L, -1) * g).astype(u.dtype) @ p['w_br'][2]

    gates = jax.nn.sigmoid(u @ p['w_merge'] + p['b_merge'])
    g_ret, g_gla, g_rw = jnp.split(gates, N_BRANCHES, axis=-1)
    mixed = g_ret * ret_out + g_gla * gla_out + g_rw * rwkv_out
    return mixed @ p['w_out'], (sr_f, sr_b, sg_f, sg_b, s_rw[0], s_rw[1])


def _hier_moe(u, p):
    B, L, D = u.shape
    x = u.reshape(B * L, D)
    g_prob = jax.nn.softmax((x @ p['w_rg'] + p['b_rg']).astype(jnp.float32), axis=-1)
    g_w, g_idx = lax.top_k(g_prob, 1)
    e_logits = (x @ p['w_re'] + p['b_re']).astype(jnp.float32).reshape(-1, N_GROUPS, EXPERTS_PER_GROUP)
    e_in = jnp.take_along_axis(e_logits, g_idx[:, :, None], axis=1)[:, 0]
    e_w, e_idx = lax.top_k(jax.nn.softmax(e_in, axis=-1), TOP_K_IN_GROUP)
    e_w = e_w / jnp.sum(e_w, -1, keepdims=True)
    gid = g_idx * EXPERTS_PER_GROUP + e_idx
    combine = jnp.einsum('tk,tke->te', g_w * e_w,
                         jax.nn.one_hot(gid, N_EXPERTS, dtype=jnp.float32)).astype(u.dtype)
    out = jnp.zeros_like(x)
    for e in range(N_EXPERTS):
        h = jax.nn.silu(x @ p['w_eg'][e]) * (x @ p['w_eu'][e])
        out = out + combine[:, e:e + 1] * (h @ p['w_ed'][e])
    return out.reshape(B, L, D)


def _layer(x, mod, s_in, n_rows, p):
    sh1, sc1, gt1, sh2, sc2, gt2 = jnp.split(mod, 6, axis=-1)
    u = (_layernorm(x) * (1.0 + sc1) + sh1).astype(x.dtype)
    mix, s_out = _token_mixer(u, p, s_in, n_rows)
    x = _ln_affine(ALPHA * x + gt1 * mix, p['ln1_g'], p['ln1_b'])
    u = (_layernorm(x) * (1.0 + sc2) + sh2).astype(x.dtype)
    x = _ln_affine(ALPHA * x + gt2 * _hier_moe(u, p), p['ln2_g'], p['ln2_b'])
    return x, s_out


def setup_inputs(seed: int = 0) -> dict:
    key = jax.random.key(seed)
    ks = iter(jax.random.split(key, 64))
    D = D_MODEL

    def nrm(shape, scale):
        return jax.random.normal(next(ks), shape, jnp.float32) * scale

    return {
        'x_prompt': nrm((BATCH, SEQ, D), 1.0),
        'x_sample': nrm((DEC_BATCH, DEC_SEQ, D), 1.0),
        'state_ret': nrm((DEC_BATCH, DEPTH, 2, RET_HEADS, RET_DK, RET_DV), 0.3),
        'state_gla': nrm((DEC_BATCH, DEPTH, 2, GLA_HEADS, GLA_DK, GLA_DV), 0.3),
        'state_rwkv': nrm((DEC_BATCH, DEPTH, 2, RWKV_HEADS, RWKV_HEAD_DIM, RWKV_HEAD_DIM), 0.3),
        'c': nrm((DEC_BATCH, D), 1.0),
        'c_ctx': nrm((D,), 1.0),
        'w_ada': nrm((DEPTH, D, 6 * D), 0.5 * D ** -0.5),
        'b_ada': nrm((DEPTH, 6 * D), 0.02),
        'w_in': nrm((DEPTH, D, N_IN_COLS), D ** -0.5),
        'rwkv_shift': jnp.array([0.25, 0.5, 0.25], jnp.float32)[None, :, None] + nrm((DEPTH, 3, N_RWKV_COLS), 0.05),
        'ret_gn': 1.0 + nrm((DEPTH, RET_HEADS * RET_DV), 0.05),
        'gla_w2': nrm((DEPTH, 2, GLA_LOWRANK, GLA_HEADS * GLA_DK), GLA_LOWRANK ** -0.5),
        'gla_b': nrm((DEPTH, 2, GLA_HEADS * GLA_DK), 0.5),
        'gla_gn': 1.0 + nrm((DEPTH, GLA_HEADS * GLA_DV), 0.05),
        'rwkv_w0': -3.0 + nrm((DEPTH, 2, D), 1.0),
        'rwkv_wb': nrm((DEPTH, 2, RWKV_W_LORA, D), 0.1),
        'rwkv_a0': nrm((DEPTH, 2, D), 0.5),
        'rwkv_ab': nrm((DEPTH, 2, RWKV_A_LORA, D), 0.1),
        'rwkv_gb': nrm((DEPTH, RWKV_G_LORA, D), RWKV_G_LORA ** -0.5),
        'rwkv_kk': 0.85 + nrm((DEPTH, D), 0.05),
        'rwkv_ka': 1.0 + nrm((DEPTH, D), 0.05),
        'rwkv_rk': nrm((DEPTH, RWKV_HEADS, RWKV_HEAD_DIM), 0.1),
        'rwkv_gn_g': 1.0 + nrm((DEPTH, D), 0.05),
        'rwkv_gn_b': nrm((DEPTH, D), 0.02),
        'w_br': nrm((DEPTH, N_BRANCHES, D, D), BETA * D ** -0.5),
        'w_merge': nrm((DEPTH, D, N_BRANCHES * D), D ** -0.5),
        'b_merge': nrm((DEPTH, N_BRANCHES * D), 0.02),
        'w_out': nrm((DEPTH, D, D), BETA * D ** -0.5),
        'ln1_g': 1.0 + nrm((DEPTH, D), 0.05),
        'ln1_b': nrm((DEPTH, D), 0.02),
        'ln2_g': 1.0 + nrm((DEPTH, D), 0.05),
        'ln2_b': nrm((DEPTH, D), 0.02),
        'w_rg': nrm((DEPTH, D, N_GROUPS), D ** -0.5),
        'b_rg': nrm((DEPTH, N_GROUPS), 0.01),
        'w_re': nrm((DEPTH, D, N_EXPERTS), D ** -0.5),
        'b_re': nrm((DEPTH, N_EXPERTS), 0.01),
        'w_eg': nrm((DEPTH, N_EXPERTS, D, EXPERT_FF), D ** -0.5),
        'w_eu': nrm((DEPTH, N_EXPERTS, D, EXPERT_FF), D ** -0.5),
        'w_ed': nrm((DEPTH, N_EXPERTS, EXPERT_FF, D), BETA * EXPERT_FF ** -0.5),
    }


def reference(x_prompt, x_sample, state_ret, state_gla, state_rwkv, c, c_ctx, w_ada, b_ada, w_in,
              rwkv_shift, ret_gn, gla_w2, gla_b, gla_gn, rwkv_w0, rwkv_wb, rwkv_a0, rwkv_ab, rwkv_gb,
              rwkv_kk, rwkv_ka, rwkv_rk, rwkv_gn_g, rwkv_gn_b, w_br, w_merge, b_merge, w_out,
              ln1_g, ln1_b, ln2_g, ln2_b, w_rg, b_rg, w_re, b_re, w_eg, w_eu, w_ed):
    n_rows = x_sample.shape[1] // GRID_W
    bc = x_prompt.shape[0]
    z_ret = jnp.zeros((bc, RET_HEADS, RET_DK, RET_DV), jnp.float32)
    z_gla = jnp.zeros((bc, GLA_HEADS, GLA_DK, GLA_DV), jnp.float32)
    z_rw = jnp.zeros((bc, RWKV_HEADS, RWKV_HEAD_DIM, RWKV_HEAD_DIM), jnp.float32)
    zero_states = (z_ret, z_ret, z_gla, z_gla, z_rw, z_rw)
    h_ctx, h_lat = x_prompt, x_sample
    new_ret, new_gla, new_rw = [], [], []
    for l in range(DEPTH):
        p = dict(w_in=w_in[l], rwkv_shift=rwkv_shift[l], ret_gn=ret_gn[l], gla_w2=gla_w2[l], gla_b=gla_b[l],
                 gla_gn=gla_gn[l], rwkv_w0=rwkv_w0[l], rwkv_wb=rwkv_wb[l], rwkv_a0=rwkv_a0[l],
                 rwkv_ab=rwkv_ab[l], rwkv_gb=rwkv_gb[l], rwkv_kk=rwkv_kk[l], rwkv_ka=rwkv_ka[l],
                 rwkv_rk=rwkv_rk[l], rwkv_gn_g=rwkv_gn_g[l], rwkv_gn_b=rwkv_gn_b[l], w_br=w_br[l],
                 w_merge=w_merge[l], b_merge=b_merge[l], w_out=w_out[l], ln1_g=ln1_g[l], ln1_b=ln1_b[l],
                 ln2_g=ln2_g[l], ln2_b=ln2_b[l], w_rg=w_rg[l], b_rg=b_rg[l], w_re=w_re[l], b_re=b_re[l],
                 w_eg=w_eg[l], w_eu=w_eu[l], w_ed=w_ed[l])
        mod_ctx = (jax.nn.silu(c_ctx) @ w_ada[l] + b_ada[l])[None, None, :]
        mod_lat = (jax.nn.silu(c) @ w_ada[l] + b_ada[l])[:, None, :]
        h_ctx, s_ctx = _layer(h_ctx, mod_ctx, zero_states, None, p)
        new_ret.append(jnp.stack([s_ctx[0], s_ctx[1]], axis=1))
        new_gla.append(jnp.stack([s_ctx[2], s_ctx[3]], axis=1))
        new_rw.append(jnp.stack([s_ctx[4], s_ctx[5]], axis=1))
        cached = (state_ret[:, l, 0], state_ret[:, l, 1], state_gla[:, l, 0], state_gla[:, l, 1],
                  state_rwkv[:, l, 0], state_rwkv[:, l, 1])
        h_lat, _ = _layer(h_lat, mod_lat, cached, n_rows, p)
    new_state_ret = jnp.stack(new_ret, axis=1).astype(x_prompt.dtype)
    new_state_gla = jnp.stack(new_gla, axis=1).astype(x_prompt.dtype)
    new_state_rwkv = jnp.stack(new_rw, axis=1).astype(x_prompt.dtype)
    return (h_ctx, h_lat, new_state_ret, new_state_gla, new_state_rwkv)
```

```python
import functools

import jax
import jax.numpy as jnp
import numpy as np
from jax import lax
from jax.experimental import pallas as pl
from jax.experimental.pallas import tpu as pltpu

F32 = jnp.float32
BF16 = jnp.bfloat16
HI = lax.Precision.HIGHEST

D = 1024
DEPTH = 2
GRID_W = 64
RET_H, RET_DK, RET_DV = 4, 128, 256
GLA_H, GLA_DK, GLA_DV = 4, 128, 256
GLA_LOWRANK = 16
GLA_GATE_NORM = 16.0
RW_H, RW_N = 16, 64
RW_W_LORA, RW_A_LORA, RW_G_LORA = 64, 64, 128
N_RW_COLS = 3 * D + RW_W_LORA + RW_A_LORA + RW_G_LORA
N_GROUPS, EPG, N_EXPERTS, EXPERT_FF = 4, 4, 16, 512
ALPHA = (2 * DEPTH) ** 0.25
EPS = 1e-5
ROPE_BASE = 10000.0

LANES = 128
VMEM_LIMIT = 56 * 1024 * 1024

RET_C = 128
GLA_C = 64
GLA_LEVELS = 6
RW_C = 32
RW_LEVELS = 5
RW_TL = 256
RW_HALF = 8 * RW_N


def _cparams(sem):
    return pltpu.CompilerParams(dimension_semantics=sem, vmem_limit_bytes=VMEM_LIMIT)


def _mm(a, b):
    return jnp.dot(a.astype(BF16), b.astype(BF16), preferred_element_type=F32)


def _mm_nt(a, b):
    return lax.dot_general(a.astype(BF16), b.astype(BF16), (((1,), (1,)), ((), ())), preferred_element_type=F32)


def _mm_tn(a, b):
    return lax.dot_general(a.astype(BF16), b.astype(BF16), (((0,), (0,)), ((), ())), preferred_element_type=F32)


def _mm_hi(a, b):
    return jnp.dot(a, b, preferred_element_type=F32, precision=HI)


def _sigmoid(x):
    return 1.0 / (1.0 + jnp.exp(-x))


def _silu(x):
    return x * _sigmoid(x)


def _log_sigmoid(x):
    return jnp.minimum(x, 0.0) - jnp.log(1.0 + jnp.exp(-jnp.abs(x)))


def _softplus(x):
    return jnp.maximum(x, 0.0) + jnp.log(1.0 + jnp.exp(-jnp.abs(x)))


def _ln(x):
    mu = jnp.mean(x, axis=-1, keepdims=True)
    xc = x - mu
    var = jnp.mean(xc * xc, axis=-1, keepdims=True)
    return xc * lax.rsqrt(var + EPS)


def _mod_kernel(c_ref, w_ref, b_ref, o_ref):
    o_ref[...] = _mm_hi(_silu(c_ref[...]), w_ref[...]) + b_ref[...]


def _modulation(c_all, w, b):
    m, n, tn = c_all.shape[0], w.shape[1], 512
    return pl.pallas_call(
        _mod_kernel,
        grid=(n // tn,),
        in_specs=[pl.BlockSpec((m, D), lambda j: (0, 0)),
                  pl.BlockSpec((D, tn), lambda j: (0, j)),
                  pl.BlockSpec((1, tn), lambda j: (0, j))],
        out_specs=pl.BlockSpec((m, tn), lambda j: (0, j)),
        out_shape=jax.ShapeDtypeStruct((m, n), F32),
        compiler_params=_cparams(("arbitrary",)),
        name="adaln_mod",
    )(c_all, w, b.reshape(1, n))


def _lnmod_kernel(x_ref, sh_ref, sc_ref, u_ref):
    u_ref[0] = (_ln(x_ref[0]) * (1.0 + sc_ref[0]) + sh_ref[0]).astype(u_ref.dtype)


def _lnmod(x, sh, sc):
    b, l, _ = x.shape
    tm = 512 if l % 512 == 0 else 256
    vec = pl.BlockSpec((1, 1, D), lambda i, j: (i, 0, 0))
    return pl.pallas_call(
        _lnmod_kernel,
        grid=(b, l // tm),
        in_specs=[pl.BlockSpec((1, tm, D), lambda i, j: (i, j, 0)), vec, vec],
        out_specs=pl.BlockSpec((1, tm, D), lambda i, j: (i, j, 0)),
        out_shape=jax.ShapeDtypeStruct((b, l, D), BF16),
        compiler_params=_cparams(("arbitrary", "arbitrary")),
        name="ln_mod",
    )(x, sh, sc)


def _proj_kernel(u_ref, w_ref, b_ref, o_ref, *, act):
    y = jnp.dot(u_ref[...], w_ref[...], preferred_element_type=F32) + b_ref[...]
    if act == "sigmoid":
        y = _sigmoid(y)
    o_ref[...] = y.astype(o_ref.dtype)


def _proj(u2d, w, bias, tn, act=None):
    t, n = u2d.shape[0], w.shape[1]
    tm = 512
    return pl.pallas_call(
        functools.partial(_proj_kernel, act=act),
        grid=(t // tm, n // tn),
        in_specs=[pl.BlockSpec((tm, D), lambda i, j: (i, 0)),
                  pl.BlockSpec((D, tn), lambda i, j: (0, j)),
                  pl.BlockSpec((1, tn), lambda i, j: (0, j))],
        out_specs=pl.BlockSpec((tm, tn), lambda i, j: (i, j)),
        out_shape=jax.ShapeDtypeStruct((t, n), F32),
        compiler_params=_cparams(("arbitrary", "arbitrary")),
        name="proj",
    )(u2d, w, bias.reshape(1, n))


def _rope(x, cos2, sin2):
    return x * cos2 + pltpu.roll(x, RET_DK // 2, 1) * sin2


def _ret_kernel(lg_ref, q_ref, k_ref, v_ref, g_ref, cos_ref, sin_ref, gn_ref, s0_ref,
                z_ref, sfin_ref, oacc_ref, s_ref, *, use_rope, n_chunks):
    c = RET_C
    h, d, n = pl.program_id(1), pl.program_id(2), pl.program_id(3)
    fwd = d == 0
    cidx = jnp.where(fwd, n, n_chunks - 1 - n)

    @pl.when(n == 0)
    def _():
        s_ref[...] = s0_ref[0, 0, 0]

    lgam = lg_ref[d, h]
    q = q_ref[0]
    k = k_ref[0] * (RET_DK ** -0.5)
    if use_rope:
        q = _rope(q, cos_ref[...], sin_ref[...])
        k = _rope(k, cos_ref[...], sin_ref[...])
    v = v_ref[0]
    ri = lax.broadcasted_iota(jnp.int32, (c, c), 0)
    ci = lax.broadcasted_iota(jnp.int32, (c, c), 1)
    rel = jnp.where(fwd, ri - ci, ci - ri).astype(F32)
    decay = jnp.where(rel >= 0, jnp.exp(jnp.maximum(rel, 0.0) * lgam), 0.0)
    pos = lax.broadcasted_iota(jnp.int32, (c, RET_DK), 0)
    pe = jnp.where(fwd, pos, c - 1 - pos).astype(F32)
    qd = q * jnp.exp((pe + 1.0) * lgam)
    kd = k * jnp.exp((c - 1.0 - pe) * lgam)
    s = s_ref[...]
    o = _mm(_mm_nt(q, k) * decay, v) + _mm(qd, s)
    s_ref[...] = s * jnp.exp(c * lgam) + _mm_tn(kd, v)
    rows = pl.ds(pl.multiple_of(cidx * c, c), c)

    @pl.when(fwd)
    def _():
        oacc_ref[rows, :] = o

    @pl.when(d == 1)
    def _():
        y = _ln(oacc_ref[rows, :] + o) * gn_ref[...]
        z_ref[0] = (_silu(g_ref[0]) * y).astype(z_ref.dtype)

    @pl.when(n == n_chunks - 1)
    def _():
        sfin_ref[0, 0, 0] = s_ref[...]


def _retention(pr, cos2, sin2, gn, s0, use_rope):
    b, l, _ = pr.shape
    nc = l // RET_C
    lg = np.log(1.0 - 2.0 ** (-5.0 - np.arange(RET_H)))
    lg = jnp.asarray(np.stack([lg, lg[::-1]]), F32)

    def chunk(dd, n):
        return jnp.where(dd == 0, n, nc - 1 - n)

    def out_chunk(dd, n):
        return jnp.where(dd == 0, nc - 1, nc - 1 - n)

    kq, kv = RET_H, 2 * RET_H * RET_DK // RET_DV
    in_specs = [
        pl.BlockSpec(memory_space=pltpu.SMEM),
        pl.BlockSpec((1, RET_C, RET_DK), lambda i, h, dd, n: (i, chunk(dd, n), h)),
        pl.BlockSpec((1, RET_C, RET_DK), lambda i, h, dd, n: (i, chunk(dd, n), kq + h)),
        pl.BlockSpec((1, RET_C, RET_DV), lambda i, h, dd, n: (i, chunk(dd, n), kv + h)),
        pl.BlockSpec((1, RET_C, RET_DV), lambda i, h, dd, n: (i, chunk(dd, n), kv + RET_H + h)),
        pl.BlockSpec((RET_C, RET_DK), lambda i, h, dd, n: (chunk(dd, n), 0)),
        pl.BlockSpec((RET_C, RET_DK), lambda i, h, dd, n: (chunk(dd, n), 0)),
        pl.BlockSpec((1, RET_DV), lambda i, h, dd, n: (0, h)),
        pl.BlockSpec((1, 1, 1, RET_DK, RET_DV), lambda i, h, dd, n: (i, dd, h, 0, 0)),
    ]
    out_specs = [
        pl.BlockSpec((1, RET_C, RET_DV), lambda i, h, dd, n: (i, out_chunk(dd, n), h)),
        pl.BlockSpec((1, 1, 1, RET_DK, RET_DV), lambda i, h, dd, n: (i, dd, h, 0, 0)),
    ]
    return pl.pallas_call(
        functools.partial(_ret_kernel, use_rope=use_rope, n_chunks=nc),
        grid=(b, RET_H, 2, nc),
        in_specs=in_specs,
        out_specs=out_specs,
        out_shape=[jax.ShapeDtypeStruct((b, l, RET_H * RET_DV), BF16),
                   jax.ShapeDtypeStruct((b, 2, RET_H, RET_DK, RET_DV), F32)],
        scratch_shapes=[pltpu.VMEM((l, RET_DV), F32), pltpu.VMEM((RET_DK, RET_DV), F32)],
        compiler_params=_cparams(("arbitrary",) * 4),
        name="retention",
    )(lg, pr, pr, pr, pr, cos2, sin2, gn.reshape(1, -1), s0)


def _gla_sum_matrices():
    c = GLA_C
    out = np.zeros((2, (2 + GLA_LEVELS) * c, c), np.float32)
    for d in range(2):
        tau = np.arange(c) if d == 0 else c - 1 - np.arange(c)
        ti, tj = tau[:, None], tau[None, :]
        out[d, 0:c] = tj <= ti
        out[d, c:2 * c] = tj > ti
        for lv in range(GLA_LEVELS):
            m = 1 << lv
            te = (ti // (2 * m)) * 2 * m + m - 1
            second = (ti // m) % 2 == 1
            out[d, (2 + lv) * c:(3 + lv) * c] = np.where(second, (tj > te) & (tj <= ti), (tj > ti) & (tj <= te))
    return out


def _gla_kernel(m_ref, q_ref, k_ref, v_ref, g_ref, lr_ref, w2_ref, b2_ref, gn_ref, s0_ref,
                z_ref, sfin_ref, oacc_ref, st_ref, *, n_chunks):
    c = GLA_C
    d, n = pl.program_id(2), pl.program_id(3)
    fwd = d == 0
    cidx = jnp.where(fwd, n, n_chunks - 1 - n)

    @pl.when(n == 0)
    def _():
        st_ref[...] = s0_ref[0, 0, 0]

    q = q_ref[0] * (GLA_DK ** -0.5)
    k = k_ref[0]
    v = v_ref[0]
    gate = _log_sigmoid(_mm_hi(lr_ref[0], w2_ref[0, 0]) + b2_ref[0, 0]) * (1.0 / GLA_GATE_NORM)
    sums = _mm_hi(m_ref[0], gate)
    b_inc = sums[0:c]
    b_rest = sums[c:2 * c]
    ri = lax.broadcasted_iota(jnp.int32, (c, c), 0)
    ci = lax.broadcasted_iota(jnp.int32, (c, c), 1)
    ti = jnp.where(fwd, ri, c - 1 - ri)
    tj = jnp.where(fwd, ci, c - 1 - ci)
    trow = lax.broadcasted_iota(jnp.int32, (c, GLA_DK), 0)
    trow = jnp.where(fwd, trow, c - 1 - trow)
    attn = jnp.where(ri == ci, jnp.sum(q * k, axis=-1, keepdims=True), 0.0)
    for lv in range(GLA_LEVELS):
        e = jnp.exp(sums[(2 + lv) * c:(3 + lv) * c])
        second = jnp.bitwise_and(jnp.right_shift(trow, lv), 1) == 1
        ql = jnp.where(second, q * e, 0.0)
        kl = jnp.where(second, 0.0, k * e)
        same = jnp.right_shift(ti, lv + 1) == jnp.right_shift(tj, lv + 1)
        attn = attn + jnp.where(same, _mm_nt(ql, kl), 0.0)
    st = st_ref[...]
    o = _mm(attn, v) + _mm_nt(q * jnp.exp(b_inc), st)
    b_end = b_inc[0:1] + b_rest[0:1]
    st_ref[...] = st * jnp.exp(b_end) + _mm_tn(v, k * jnp.exp(b_rest))
    rows = pl.ds(pl.multiple_of(cidx * c, c), c)

    @pl.when(fwd)
    def _():
        oacc_ref[rows, :] = o

    @pl.when(d == 1)
    def _():
        y = oacc_ref[rows, :] + o
        y = y * lax.rsqrt(jnp.mean(y * y, axis=-1, keepdims=True) + EPS) * gn_ref[...]
        z_ref[0] = (_silu(g_ref[0]) * y).astype(z_ref.dtype)

    @pl.when(n == n_chunks - 1)
    def _():
        sfin_ref[0, 0, 0] = st_ref[...]


def _gla(pg, w2p, b2, gn, s0t):
    b, l, _ = pg.shape
    nc = l // GLA_C
    msum = jnp.asarray(_gla_sum_matrices())
    nrows = msum.shape[1]

    def chunk(dd, n):
        return jnp.where(dd == 0, n, nc - 1 - n)

    def out_chunk(dd, n):
        return jnp.where(dd == 0, nc - 1, nc - 1 - n)

    kq, kv = GLA_H, 2 * GLA_H * GLA_DK // GLA_DV
    lr_blk = (2 * GLA_H * GLA_DK + 2 * GLA_H * GLA_DV) // LANES
    in_specs = [
        pl.BlockSpec((1, nrows, GLA_C), lambda i, h, dd, n: (dd, 0, 0)),
        pl.BlockSpec((1, GLA_C, GLA_DK), lambda i, h, dd, n: (i, chunk(dd, n), h)),
        pl.BlockSpec((1, GLA_C, GLA_DK), lambda i, h, dd, n: (i, chunk(dd, n), kq + h)),
        pl.BlockSpec((1, GLA_C, GLA_DV), lambda i, h, dd, n: (i, chunk(dd, n), kv + h)),
        pl.BlockSpec((1, GLA_C, GLA_DV), lambda i, h, dd, n: (i, chunk(dd, n), kv + GLA_H + h)),
        pl.BlockSpec((1, GLA_C, LANES), lambda i, h, dd, n: (i, chunk(dd, n), lr_blk)),
        pl.BlockSpec((1, 1, LANES, GLA_DK), lambda i, h, dd, n: (dd, h, 0, 0)),
        pl.BlockSpec((1, 1, 1, GLA_DK), lambda i, h, dd, n: (dd, h, 0, 0)),
        pl.BlockSpec((1, GLA_DV), lambda i, h, dd, n: (0, h)),
        pl.BlockSpec((1, 1, 1, GLA_DV, GLA_DK), lambda i, h, dd, n: (i, dd, h, 0, 0)),
    ]
    out_specs = [
        pl.BlockSpec((1, GLA_C, GLA_DV), lambda i, h, dd, n: (i, out_chunk(dd, n), h)),
        pl.BlockSpec((1, 1, 1, GLA_DV, GLA_DK), lambda i, h, dd, n: (i, dd, h, 0, 0)),
    ]
    return pl.pallas_call(
        functools.partial(_gla_kernel, n_chunks=nc),
        grid=(b, GLA_H, 2, nc),
        in_specs=in_specs,
        out_specs=out_specs,
        out_shape=[jax.ShapeDtypeStruct((b, l, GLA_H * GLA_DV), BF16),
                   jax.ShapeDtypeStruct((b, 2, GLA_H, GLA_DV, GLA_DK), F32)],
        scratch_shapes=[pltpu.VMEM((l, GLA_DV), F32), pltpu.VMEM((GLA_DV, GLA_DK), F32)],
        compiler_params=_cparams(("arbitrary",) * 4),
        name="gla",
    )(msum, pg, pg, pg, pg, pg, w2p, b2, gn.reshape(1, -1), s0t)


def _gla_gate_params(w2, b2):
    w2p = w2.reshape(2, GLA_LOWRANK, GLA_H, GLA_DK).transpose(0, 2, 1, 3)
    w2p = jnp.pad(w2p, ((0, 0), (0, 0), (0, LANES - GLA_LOWRANK), (0, 0)))
    return w2p, b2.reshape(2, GLA_H, 1, GLA_DK)


def _split3(x):
    hi = x.astype(BF16)
    r1 = x - hi.astype(F32)
    mid = r1.astype(BF16)
    lo = (r1 - mid.astype(F32)).astype(BF16)
    return hi, mid, lo


def _head_sum(x):
    i = lax.broadcasted_iota(jnp.int32, (LANES, LANES), 0)
    j = lax.broadcasted_iota(jnp.int32, (LANES, LANES), 1)
    ones = jnp.where(jnp.right_shift(i, 6) == jnp.right_shift(j, 6), 1.0, 0.0).astype(BF16)
    out = []
    for t in range(x.shape[1] // LANES):
        parts = _split3(x[:, t * LANES:(t + 1) * LANES])
        out.append(sum(jnp.dot(p, ones, preferred_element_type=F32) for p in parts))
    return jnp.concatenate(out, axis=1)


RW_TM = 256
HALO = 8


def _rw_prep_kernel(x_ref, xp_ref, xn_ref, taps_ref, kkp_ref, ka_ref, rk_ref, w0_ref, wb_ref, a0_ref, ab_ref,
                    gb_ref, r_ref, v_ref, kk_ref, lw_ref, kd_ref, be_ref, bonus_ref, g_ref, xbuf, *, n_tiles):
    tm = RW_TM
    j = pl.program_id(1)
    xbuf[HALO:HALO + tm, :] = x_ref[0]
    xbuf[0:HALO, :] = jnp.where(j == 0, 0.0, xp_ref[0])
    xbuf[HALO + tm:2 * HALO + tm, :] = jnp.where(j == n_tiles - 1, 0.0, xn_ref[0])

    def shifted(c0, c1):
        return (taps_ref[0:1, c0:c1] * xbuf[HALO - 1:HALO - 1 + tm, c0:c1]
                + taps_ref[1:2, c0:c1] * xbuf[HALO:HALO + tm, c0:c1]
                + taps_ref[2:3, c0:c1] * xbuf[HALO + 1:HALO + 1 + tm, c0:c1])

    r = shifted(0, D)
    k = shifted(D, 2 * D)
    v = shifted(2 * D, 3 * D)
    lora = shifted(3 * D, N_RW_COLS)
    xw = lora[:, 0:RW_W_LORA]
    xa = lora[:, RW_W_LORA:RW_W_LORA + RW_A_LORA]
    xg = lora[:, RW_W_LORA + RW_A_LORA:]
    r_ref[0] = r
    v_ref[0] = v
    kk = k * kkp_ref[...]
    kk = kk * lax.rsqrt(jnp.maximum(_head_sum(kk * kk), 1e-24))
    kk_ref[0] = kk
    g_ref[0] = _mm(_sigmoid(xg), gb_ref[...])
    wh = jnp.tanh(xw)
    kd_sum = jnp.zeros_like(k)
    for d in range(2):
        w = -_softplus(-(w0_ref[d:d + 1, :] + _mm(wh, wb_ref[d]))) - 0.5
        lw_ref[d, 0] = -jnp.exp(w)
        a = _sigmoid(a0_ref[d:d + 1, :] + _mm(xa, ab_ref[d]))
        kd = k * (1.0 + (a - 1.0) * ka_ref[...])
        kd_ref[d, 0] = kd
        be_ref[d, 0] = kk * a
        kd_sum = kd_sum + kd
    bonus_ref[0] = _head_sum(r * kd_sum * rk_ref[...]) * v


def _rw_prep(rw, taps, kkp, ka, rk, w0, wb, a0, ab, gb):
    b, l, _ = rw.shape
    tm = RW_TM
    nt = l // tm
    hb = tm // HALO
    tok = pl.BlockSpec((1, tm, D), lambda i, j: (i, j, 0))
    tokd = pl.BlockSpec((2, 1, tm, D), lambda i, j: (0, i, j, 0))

    def full(*shape):
        return pl.BlockSpec(shape, lambda i, j: (0,) * len(shape))

    in_specs = [
        pl.BlockSpec((1, tm, N_RW_COLS), lambda i, j: (i, j, 0)),
        pl.BlockSpec((1, HALO, N_RW_COLS), lambda i, j: (i, jnp.maximum(j * hb - 1, 0), 0)),
        pl.BlockSpec((1, HALO, N_RW_COLS), lambda i, j: (i, jnp.minimum((j + 1) * hb, l // HALO - 1), 0)),
        full(3, N_RW_COLS), full(1, D), full(1, D), full(1, D),
        full(2, D), full(2, RW_W_LORA, D), full(2, D), full(2, RW_A_LORA, D), full(RW_G_LORA, D),
    ]
    sd = jax.ShapeDtypeStruct
    return pl.pallas_call(
        functools.partial(_rw_prep_kernel, n_tiles=nt),
        grid=(b, nt),
        in_specs=in_specs,
        out_specs=[tok, tok, tok, tokd, tokd, tokd, tok, tok],
        out_shape=[sd((b, l, D), F32)] * 3 + [sd((2, b, l, D), F32)] * 3 + [sd((b, l, D), F32)] * 2,
        scratch_shapes=[pltpu.VMEM((tm + 2 * HALO, N_RW_COLS), F32)],
        compiler_params=_cparams(("arbitrary", "arbitrary")),
        name="rwkv_prep",
    )(rw, rw, rw, taps, kkp.reshape(1, D), ka.reshape(1, D), rk.reshape(1, D), w0, wb, a0, ab, gb)


def _tile8(x):
    return jnp.concatenate([x] * 8, axis=0)


def _rw_scan_kernel(r_ref, v_ref, kk_ref, lw_ref, kd_ref, be_ref, s0_ref, y_ref, sfin_ref, zt_ref, *, n_tiles):
    c, hw = RW_C, RW_HALF
    pk = 8 * c
    n_chunks = RW_TL // c
    d, j = pl.program_id(1), pl.program_id(2)
    fwd = d == 0

    @pl.when(j == 0)
    def _():
        zt_ref[...] = jnp.zeros_like(zt_ref)
        for g in range(4):
            for jj in range(4):
                zt_ref[g, RW_N * jj:RW_N * (jj + 1), RW_N * jj:RW_N * (jj + 1)] = s0_ref[0, 0, 4 * g + jj]

    ri = lax.broadcasted_iota(jnp.int32, (c, pk), 0)
    cs = jnp.bitwise_and(lax.broadcasted_iota(jnp.int32, (c, pk), 1), c - 1)
    ti = jnp.where(fwd, ri, c - 1 - ri)
    ts = jnp.where(fwd, cs, c - 1 - cs)
    strict = ts < ti
    incl = ts <= ti
    eye = (ri == cs).astype(F32)
    lvl = []
    for lv in range(RW_LEVELS):
        same = jnp.right_shift(ti, lv + 1) == jnp.right_shift(ts, lv + 1)
        lower = (jnp.bitwise_and(jnp.right_shift(ti, lv), 1) == 1) & (jnp.bitwise_and(jnp.right_shift(ts, lv), 1) == 0)
        lvl.append(same & lower)
    bi = lax.broadcasted_iota(jnp.int32, (pk, pk), 0)
    bj = lax.broadcasted_iota(jnp.int32, (pk, pk), 1)
    bd_p = jnp.right_shift(bi, 5) == jnp.right_shift(bj, 5)
    bd_z = jnp.right_shift(bi, 6) == jnp.right_shift(bj, 6)
    wi = lax.broadcasted_iota(jnp.int32, (pk, hw), 0)
    wj = lax.broadcasted_iota(jnp.int32, (pk, hw), 1)
    bd_w = jnp.right_shift(wi, 5) == jnp.right_shift(wj, 6)
    rowi = lax.broadcasted_iota(jnp.int32, (c, hw), 0)

    def bdp(x):
        return jnp.where(bd_p, _tile8(x), 0.0)

    def bdw(x):
        return jnp.where(bd_w, _tile8(x), 0.0)

    def chunk(i, carry):
        cidx = jnp.where(fwd, i, n_chunks - 1 - i)
        rows = pl.ds(pl.multiple_of(cidx * c, c), c)
        for half in range(2):
            ln = slice(half * hw, (half + 1) * hw)
            r, v, kk = r_ref[0, rows, ln], v_ref[0, rows, ln], kk_ref[0, rows, ln]
            lw, kd, be = lw_ref[0, 0, rows, ln], kd_ref[0, 0, rows, ln], be_ref[0, 0, rows, ln]
            bf = lw
            for sh in (1, 2, 4, 8, 16):
                bf = bf + jnp.where(rowi >= sh, pltpu.roll(bf, sh, 0), 0.0)
            tot = bf[c - 1:c]
            b = jnp.where(fwd, bf, tot - bf + lw)
            en = jnp.exp(-b)
            es = jnp.exp(tot - b)
            lhs = jnp.concatenate([kk * jnp.exp(b - lw), r * jnp.exp(b)], axis=0)
            gb = _mm_nt(lhs, bdw(be * en))
            gk = _mm_nt(lhs, bdw(kd * en))
            a_ab = jnp.where(strict, gb[0:c], 0.0)
            a_rb = jnp.where(incl, gb[c:], 0.0)
            a_ak = jnp.where(strict, gk[0:c], 0.0)
            a_rk = jnp.where(incl, gk[c:], 0.0)
            t_inv = eye
            for lv in range(RW_LEVELS):
                x = _mm(jnp.where(lvl[lv], a_ab, 0.0), bdp(t_inv))
                t_inv = t_inv - _mm(t_inv, bdp(x))
            zz = jnp.concatenate([_mm_nt(lhs[:, g * pk:(g + 1) * pk], zt_ref[2 * half + g]) for g in range(2)], axis=1)
            vbd = bdw(v)
            u = _mm(t_inv, bdw(-(zz[0:c] + _mm(a_ak, vbd))))
            y_ref[0, 0, rows, ln] = zz[c:] + _mm(a_rb, bdw(u)) + _mm(a_rk, vbd)
            uv = jnp.concatenate([u, v], axis=0)
            bk = jnp.concatenate([be * es, kd * es], axis=0)
            etot = jnp.exp(tot)
            for g in range(2):
                gl = slice(g * pk, (g + 1) * pk)
                upd = jnp.where(bd_z, _mm_tn(uv[:, gl], bk[:, gl]), 0.0)
                zt_ref[2 * half + g] = zt_ref[2 * half + g] * etot[:, gl] + upd
        return carry

    lax.fori_loop(0, n_chunks, chunk, 0)

    @pl.when(j == n_tiles - 1)
    def _():
        for g in range(4):
            for jj in range(4):
                sfin_ref[0, 0, 4 * g + jj] = zt_ref[g, RW_N * jj:RW_N * (jj + 1), RW_N * jj:RW_N * (jj + 1)]


def _rw_scan(r, v, kk, lw, kd, be, s0):
    b, l, _ = r.shape
    tl = RW_TL
    nt = l // tl

    def tile(dd, j):
        return jnp.where(dd == 0, j, nt - 1 - j)

    tok = pl.BlockSpec((1, tl, D), lambda i, dd, j: (i, tile(dd, j), 0))
    tokd = pl.BlockSpec((1, 1, tl, D), lambda i, dd, j: (dd, i, tile(dd, j), 0))
    st = pl.BlockSpec((1, 1, RW_H, RW_N, RW_N), lambda i, dd, j: (i, dd, 0, 0, 0))
    return pl.pallas_call(
        functools.partial(_rw_scan_kernel, n_tiles=nt),
        grid=(b, 2, nt),
        in_specs=[tok, tok, tok, tokd, tokd, tokd, st],
        out_specs=[tokd, st],
        out_shape=[jax.ShapeDtypeStruct((2, b, l, D), F32), jax.ShapeDtypeStruct((b, 2, RW_H, RW_N, RW_N), F32)],
        scratch_shapes=[pltpu.VMEM((4, 4 * RW_N, 4 * RW_N), F32)],
        compiler_params=_cparams(("arbitrary",) * 3),
        name="rwkv_scan",
    )(r, v, kk, lw, kd, be, s0)


MERGE_TM = 256
ROUTER_LANES = LANES


def _merge_kernel(zr_ref, zg_ref, y0_ref, y1_ref, bonus_ref, grw_ref, gates_ref, x_ref, gt_ref, sh_ref, sc_ref,
                  wbr_ref, wout_ref, gng_ref, gnb_ref, l1g_ref, l1b_ref, wr_ref, br_ref,
                  x1_ref, u2_ref, lg_ref):
    y = y0_ref[0, 0] + y1_ref[0, 0]
    mu = _head_sum(y) * (1.0 / RW_N)
    yc = y - mu
    var = _head_sum(yc * yc) * (1.0 / RW_N)
    yn = yc * lax.rsqrt(var + EPS) * gng_ref[...] + gnb_ref[...] + bonus_ref[0]
    rw_out = _mm(yn * grw_ref[0], wbr_ref[2])
    ret_out = jnp.dot(zr_ref[0], wbr_ref[0], preferred_element_type=F32)
    gla_out = jnp.dot(zg_ref[0], wbr_ref[1], preferred_element_type=F32)
    gates = gates_ref[0]
    mixed = gates[:, 0:D] * ret_out + gates[:, D:2 * D] * gla_out + gates[:, 2 * D:] * rw_out
    mix = _mm(mixed, wout_ref[...])
    x1 = _ln(ALPHA * x_ref[0] + gt_ref[0] * mix) * l1g_ref[...] + l1b_ref[...]
    x1_ref[0] = x1
    u2 = _ln(x1) * (1.0 + sc_ref[0]) + sh_ref[0]
    u2_ref[0] = u2.astype(u2_ref.dtype)
    lg_ref[0] = _mm_hi(u2, wr_ref[...]) + br_ref[...]


def _merge(zr, zg, y, bonus, grw, gates, x, gt1, sh2, sc2, wbr, wout, gng, gnb, l1g, l1b, wr, br):
    b, l, _ = x.shape
    tm = MERGE_TM
    tok = pl.BlockSpec((1, tm, D), lambda i, j: (i, j, 0))
    vec = pl.BlockSpec((1, 1, D), lambda i, j: (i, 0, 0))
    row = pl.BlockSpec((1, D), lambda i, j: (0, 0))
    in_specs = [
        tok, tok,
        pl.BlockSpec((1, 1, tm, D), lambda i, j: (0, i, j, 0)),
        pl.BlockSpec((1, 1, tm, D), lambda i, j: (1, i, j, 0)),
        tok, tok,
        pl.BlockSpec((1, tm, 3 * D), lambda i, j: (i, j, 0)),
        tok, vec, vec, vec,
        pl.BlockSpec((3, D, D), lambda i, j: (0, 0, 0)),
        pl.BlockSpec((D, D), lambda i, j: (0, 0)),
        row, row, row, row,
        pl.BlockSpec((D, ROUTER_LANES), lambda i, j: (0, 0)),
        pl.BlockSpec((1, ROUTER_LANES), lambda i, j: (0, 0)),
    ]
    sd = jax.ShapeDtypeStruct
    return pl.pallas_call(
        _merge_kernel,
        grid=(b, l // tm),
        in_specs=in_specs,
        out_specs=[tok, tok, pl.BlockSpec((1, tm, ROUTER_LANES), lambda i, j: (i, j, 0))],
        out_shape=[sd((b, l, D), F32), sd((b, l, D), BF16), sd((b, l, ROUTER_LANES), F32)],
        compiler_params=_cparams(("arbitrary", "arbitrary")),
        name="merge",
    )(zr, zg, y, y, bonus, grw, gates, x, gt1, sh2, sc2, wbr, wout,
      gng.reshape(1, D), gnb.reshape(1, D), l1g.reshape(1, D), l1b.reshape(1, D), wr, br)


MOE_TM = 512


def _first_argmax(x, lane):
    m = jnp.max(x, axis=-1, keepdims=True)
    idx = jnp.min(jnp.where(x == m, lane, ROUTER_LANES), axis=-1, keepdims=True)
    return m, idx


def _combine_weights(logits):
    lane = lax.broadcasted_iota(jnp.int32, logits.shape, 1)
    neg = -jnp.inf
    gl = jnp.where(lane < N_GROUPS, logits, neg)
    gmax, gidx = _first_argmax(gl, lane)
    g_w = 1.0 / jnp.sum(jnp.exp(gl - gmax), axis=-1, keepdims=True)
    lo = N_GROUPS + EPG * gidx
    el = jnp.where((lane >= lo) & (lane < lo + EPG), logits, neg)
    m1, i1 = _first_argmax(el, lane)
    m2, i2 = _first_argmax(jnp.where(lane == i1, neg, el), lane)
    e2 = jnp.exp(m2 - m1)
    w1 = 1.0 / (1.0 + e2)
    return g_w * (jnp.where(lane == i1, w1, 0.0) + jnp.where(lane == i2, e2 * w1, 0.0))


def _moe_kernel(u_ref, x_ref, lg_ref, gt_ref, wg_ref, wu_ref, wd_ref, l2g_ref, l2b_ref, o_ref, acc_ref, comb_ref):
    e = pl.program_id(2)

    @pl.when(e == 0)
    def _():
        acc_ref[...] = jnp.zeros_like(acc_ref)
        comb_ref[...] = _combine_weights(lg_ref[0])

    u = u_ref[0]
    hid = _silu(jnp.dot(u, wg_ref[0], preferred_element_type=F32)) * jnp.dot(u, wu_ref[0], preferred_element_type=F32)
    lane = lax.broadcasted_iota(jnp.int32, comb_ref.shape, 1)
    c_e = jnp.sum(jnp.where(lane == e + N_GROUPS, comb_ref[...], 0.0), axis=-1, keepdims=True)
    acc_ref[...] += c_e * _mm(hid, wd_ref[0])

    @pl.when(e == N_EXPERTS - 1)
    def _():
        o_ref[0] = _ln(ALPHA * x_ref[0] + gt_ref[0] * acc_ref[...]) * l2g_ref[...] + l2b_ref[...]


def _moe(u2, x1, logits, gt2, wg, wu, wd, l2g, l2b):
    b, l, _ = x1.shape
    tm = MOE_TM if l % MOE_TM == 0 else 256
    tok = pl.BlockSpec((1, tm, D), lambda i, j, e: (i, j, 0))
    row = pl.BlockSpec((1, D), lambda i, j, e: (0, 0))
    in_specs = [
        tok, tok,
        pl.BlockSpec((1, tm, ROUTER_LANES), lambda i, j, e: (i, j, 0)),
        pl.BlockSpec((1, 1, D), lambda i, j, e: (i, 0, 0)),
        pl.BlockSpec((1, D, EXPERT_FF), lambda i, j, e: (e, 0, 0)),
        pl.BlockSpec((1, D, EXPERT_FF), lambda i, j, e: (e, 0, 0)),
        pl.BlockSpec((1, EXPERT_FF, D), lambda i, j, e: (e, 0, 0)),
        row, row,
    ]
    return pl.pallas_call(
        _moe_kernel,
        grid=(b, l // tm, N_EXPERTS),
        in_specs=in_specs,
        out_specs=tok,
        out_shape=jax.ShapeDtypeStruct((b, l, D), F32),
        scratch_shapes=[pltpu.VMEM((tm, D), F32), pltpu.VMEM((tm, ROUTER_LANES), F32)],
        compiler_params=_cparams(("arbitrary",) * 3),
        name="moe",
    )(u2, x1, logits, gt2, wg, wu, wd, l2g.reshape(1, D), l2b.reshape(1, D))


def _rope_tables(l):
    t = np.arange(l)
    quarter = RET_DK // 4
    freqs = (np.float32(ROPE_BASE) ** (-np.arange(quarter, dtype=np.float32) / quarter)).astype(np.float32)
    rows = (t // GRID_W).astype(np.float32)
    cols = (t % GRID_W).astype(np.float32)
    ang = jnp.asarray(np.concatenate([rows[:, None] * freqs, cols[:, None] * freqs], -1))
    cos, sin = jnp.cos(ang), jnp.sin(ang)
    return jnp.concatenate([cos, cos], -1), jnp.concatenate([-sin, sin], -1)


def _layer_weights(p):
    n_rg = 2 * (RET_H * RET_DK + RET_H * RET_DV)
    w_in = p["w_in"]
    w_lr = jnp.pad(w_in[:, 2 * n_rg:2 * n_rg + GLA_LOWRANK], ((0, 0), (0, LANES - GLA_LOWRANK)))
    w2p, b2p = _gla_gate_params(p["gla_w2"], p["gla_b"])
    pad = ROUTER_LANES - N_GROUPS - N_EXPERTS
    return dict(
        p,
        w_ret=w_in[:, :n_rg].astype(BF16),
        w_gla=jnp.concatenate([w_in[:, n_rg:2 * n_rg], w_lr], axis=1).astype(BF16),
        w_rw=w_in[:, 2 * n_rg + GLA_LOWRANK:].astype(BF16),
        w_merge=p["w_merge"].astype(BF16),
        w_br=p["w_br"].astype(BF16),
        w_out=p["w_out"].astype(BF16),
        gla_w2p=w2p, gla_b2p=b2p,
        w_router=jnp.pad(jnp.concatenate([p["w_rg"], p["w_re"]], axis=1), ((0, 0), (0, pad))),
        b_router=jnp.pad(jnp.concatenate([p["b_rg"], p["b_re"]]), (0, pad)).reshape(1, ROUTER_LANES),
        w_eg=p["w_eg"].astype(BF16), w_eu=p["w_eu"].astype(BF16), w_ed=p["w_ed"].astype(BF16),
    )


def _layer(x, mod, s_ret, s_gla_t, s_rw, rope, p):
    b, l, _ = x.shape
    sh1, sc1, gt1, sh2, sc2, gt2 = [m.reshape(b, 1, D) for m in jnp.split(mod, 6, axis=-1)]
    u = _lnmod(x, sh1, sc1)
    u2d = u.reshape(b * l, D)
    zero = lambda n: jnp.zeros((n,), F32)
    pr = _proj(u2d, p["w_ret"], zero(p["w_ret"].shape[1]), 1024).reshape(b, l, -1)
    pg = _proj(u2d, p["w_gla"], zero(p["w_gla"].shape[1]), 640).reshape(b, l, -1)
    rw = _proj(u2d, p["w_rw"], zero(N_RW_COLS), 1664).reshape(b, l, -1)
    gates = _proj(u2d, p["w_merge"], p["b_merge"], 1024, act="sigmoid").reshape(b, l, -1)

    cos2, sin2 = rope if rope is not None else (jnp.ones((l, RET_DK), F32), jnp.zeros((l, RET_DK), F32))
    zr, ret_fin = _retention(pr, cos2, sin2, p["ret_gn"], s_ret, rope is not None)
    zg, gla_fin_t = _gla(pg, p["gla_w2p"], p["gla_b2p"], p["gla_gn"], s_gla_t)
    r, v, kk, lw, kd, be, bonus, grw = _rw_prep(rw, p["rwkv_shift"], p["rwkv_kk"], p["rwkv_ka"],
                                                p["rwkv_rk"].reshape(-1), p["rwkv_w0"], p["rwkv_wb"],
                                                p["rwkv_a0"], p["rwkv_ab"], p["rwkv_gb"])
    y, rw_fin = _rw_scan(r, v, kk, lw, kd, be, s_rw)
    x1, u2, logits = _merge(zr, zg, y, bonus, grw, gates, x, gt1, sh2, sc2, p["w_br"], p["w_out"],
                            p["rwkv_gn_g"], p["rwkv_gn_b"], p["ln1_g"], p["ln1_b"], p["w_router"], p["b_router"])
    x2 = _moe(u2, x1, logits, gt2, p["w_eg"], p["w_eu"], p["w_ed"], p["ln2_g"], p["ln2_b"])
    return x2, (ret_fin, gla_fin_t, rw_fin)


_PARAM_NAMES = ("w_in", "rwkv_shift", "ret_gn", "gla_w2", "gla_b", "gla_gn", "rwkv_w0", "rwkv_wb", "rwkv_a0",
                "rwkv_ab", "rwkv_gb", "rwkv_kk", "rwkv_ka", "rwkv_rk", "rwkv_gn_g", "rwkv_gn_b", "w_br", "w_merge",
                "b_merge", "w_out", "ln1_g", "ln1_b", "ln2_g", "ln2_b", "w_rg", "b_rg", "w_re", "b_re", "w_eg",
                "w_eu", "w_ed")


def kernel(x_prompt, x_sample, state_ret, state_gla, state_rwkv, c, c_ctx, w_ada, b_ada, w_in, rwkv_shift, ret_gn,
           gla_w2, gla_b, gla_gn, rwkv_w0, rwkv_wb, rwkv_a0, rwkv_ab, rwkv_gb, rwkv_kk, rwkv_ka, rwkv_rk, rwkv_gn_g,
           rwkv_gn_b, w_br, w_merge, b_merge, w_out, ln1_g, ln1_b, ln2_g, ln2_b, w_rg, b_rg, w_re, b_re, w_eg, w_eu,
           w_ed):
    params = dict(zip(_PARAM_NAMES, (w_in, rwkv_shift, ret_gn, gla_w2, gla_b, gla_gn, rwkv_w0, rwkv_wb, rwkv_a0,
                                     rwkv_ab, rwkv_gb, rwkv_kk, rwkv_ka, rwkv_rk, rwkv_gn_g, rwkv_gn_b, w_br, w_merge,
                                     b_merge, w_out, ln1_g, ln1_b, ln2_g, ln2_b, w_rg, b_rg, w_re, b_re, w_eg, w_eu,
                                     w_ed)))
    bc, bl = x_prompt.shape[0], x_sample.shape[0]
    rope = _rope_tables(x_sample.shape[1])
    mod_rows = 16
    c_all = jnp.concatenate([c, c_ctx[None, :], jnp.zeros((mod_rows - bl - 1, D), F32)], axis=0)
    z_ret = jnp.zeros((bc, 2, RET_H, RET_DK, RET_DV), F32)
    z_gla_t = jnp.zeros((bc, 2, GLA_H, GLA_DV, GLA_DK), F32)
    z_rw = jnp.zeros((bc, 2, RW_H, RW_N, RW_N), F32)
    h_ctx, h_lat = x_prompt, x_sample
    new_ret, new_gla, new_rw = [], [], []
    for layer in range(DEPTH):
        p = _layer_weights({k: v[layer] for k, v in params.items()})
        mod = _modulation(c_all, w_ada[layer], b_ada[layer])
        mod_ctx = jnp.broadcast_to(mod[bl:bl + 1], (bc, 6 * D))
        h_ctx, (s_ret, s_gla_t, s_rw) = _layer(h_ctx, mod_ctx, z_ret, z_gla_t, z_rw, None, p)
        new_ret.append(s_ret)
        new_gla.append(jnp.swapaxes(s_gla_t, -1, -2))
        new_rw.append(s_rw)
        h_lat, _ = _layer(h_lat, mod[:bl], state_ret[:, layer], jnp.swapaxes(state_gla[:, layer], -1, -2),
                          state_rwkv[:, layer], rope, p)
    return (h_ctx, h_lat, jnp.stack(new_ret, axis=1), jnp.stack(new_gla, axis=1), jnp.stack(new_rw, axis=1))
```

```python
import functools

import jax
import jax.numpy as jnp
import numpy as np
from jax import lax
from jax.experimental import pallas as pl
from jax.experimental.pallas import tpu as pltpu

F32 = jnp.float32
BF16 = jnp.bfloat16
HI = lax.Precision.HIGHEST

D = 1024
DEPTH = 2
GRID_W = 64
RET_H, RET_DK, RET_DV = 4, 128, 256
GLA_H, GLA_DK, GLA_DV = 4, 128, 256
GLA_LOWRANK = 16
GLA_GATE_NORM = 16.0
RW_H, RW_N = 16, 64
RW_W_LORA, RW_A_LORA, RW_G_LORA = 64, 64, 128
N_RW_COLS = 3 * D + RW_W_LORA + RW_A_LORA + RW_G_LORA
N_GROUPS, EPG, N_EXPERTS, EXPERT_FF = 4, 4, 16, 512
ALPHA = (2 * DEPTH) ** 0.25
EPS = 1e-5
ROPE_BASE = 10000.0

LANES = 128
SUBLANES = 8
VMEM_LIMIT = 56 * 1024 * 1024

RET_C = 128
GLA_C = 64
GLA_LEVELS = 6
RW_C = 32
RW_LEVELS = 5
RW_TL = 256
RW_HALF = 8 * RW_N


def _cparams(sem):
    return pltpu.CompilerParams(dimension_semantics=sem, vmem_limit_bytes=VMEM_LIMIT)


def _mm(a, b):
    return jnp.dot(a.astype(BF16), b.astype(BF16), preferred_element_type=F32)


def _mm_nt(a, b):
    return lax.dot_general(a.astype(BF16), b.astype(BF16), (((1,), (1,)), ((), ())), preferred_element_type=F32)


def _mm_tn(a, b):
    return lax.dot_general(a.astype(BF16), b.astype(BF16), (((0,), (0,)), ((), ())), preferred_element_type=F32)


def _mm_hi(a, b):
    return jnp.dot(a, b, preferred_element_type=F32, precision=HI)


def _sigmoid(x):
    return 1.0 / (1.0 + jnp.exp(-x))


def _silu(x):
    return x * _sigmoid(x)


def _log_sigmoid(x):
    return jnp.minimum(x, 0.0) - jnp.log(1.0 + jnp.exp(-jnp.abs(x)))


def _softplus(x):
    return jnp.maximum(x, 0.0) + jnp.log(1.0 + jnp.exp(-jnp.abs(x)))


def _ln(x):
    mu = jnp.mean(x, axis=-1, keepdims=True)
    xc = x - mu
    var = jnp.mean(xc * xc, axis=-1, keepdims=True)
    return xc * lax.rsqrt(var + EPS)


def _scan_rows(x, reverse):
    n = x.shape[0]
    row = lax.broadcasted_iota(jnp.int32, x.shape, 0)
    sh = 1
    while sh < n:
        if reverse:
            x = x + jnp.where(row < n - sh, pltpu.roll(x, n - sh, 0), 0.0)
        else:
            x = x + jnp.where(row >= sh, pltpu.roll(x, sh, 0), 0.0)
        sh *= 2
    return x


def _mod_kernel(c_ref, w_ref, b_ref, o_ref):
    o_ref[...] = _mm_hi(_silu(c_ref[...]), w_ref[...]) + b_ref[...]


def _modulation(c_all, w, b):
    m, n, tn = c_all.shape[0], w.shape[1], 512
    return pl.pallas_call(
        _mod_kernel,
        grid=(n // tn,),
        in_specs=[pl.BlockSpec((m, D), lambda j: (0, 0)),
                  pl.BlockSpec((D, tn), lambda j: (0, j)),
                  pl.BlockSpec((1, tn), lambda j: (0, j))],
        out_specs=pl.BlockSpec((m, tn), lambda j: (0, j)),
        out_shape=jax.ShapeDtypeStruct((m, n), F32),
        compiler_params=_cparams(("arbitrary",)),
        name="adaln_mod",
    )(c_all, w, b.reshape(1, n))


def _lnmod_kernel(x_ref, sh_ref, sc_ref, u_ref):
    u_ref[0] = (_ln(x_ref[0]) * (1.0 + sc_ref[0]) + sh_ref[0]).astype(u_ref.dtype)


def _lnmod(x, sh, sc):
    b, l, _ = x.shape
    tm = 512 if l % 512 == 0 else 256
    vec = pl.BlockSpec((1, 1, D), lambda i, j: (i, 0, 0))
    return pl.pallas_call(
        _lnmod_kernel,
        grid=(b, l // tm),
        in_specs=[pl.BlockSpec((1, tm, D), lambda i, j: (i, j, 0)), vec, vec],
        out_specs=pl.BlockSpec((1, tm, D), lambda i, j: (i, j, 0)),
        out_shape=jax.ShapeDtypeStruct((b, l, D), BF16),
        compiler_params=_cparams(("arbitrary", "arbitrary")),
        name="ln_mod",
    )(x, sh, sc)


def _proj_kernel(u_ref, w_ref, b_ref, o_ref, *, act):
    y = jnp.dot(u_ref[...], w_ref[...], preferred_element_type=F32) + b_ref[...]
    if act == "sigmoid":
        y = _sigmoid(y)
    o_ref[...] = y.astype(o_ref.dtype)


def _proj(u2d, w, bias, tn, act=None):
    t, n = u2d.shape[0], w.shape[1]
    tm = 512
    return pl.pallas_call(
        functools.partial(_proj_kernel, act=act),
        grid=(t // tm, n // tn),
        in_specs=[pl.BlockSpec((tm, D), lambda i, j: (i, 0)),
                  pl.BlockSpec((D, tn), lambda i, j: (0, j)),
                  pl.BlockSpec((1, tn), lambda i, j: (0, j))],
        out_specs=pl.BlockSpec((tm, tn), lambda i, j: (i, j)),
        out_shape=jax.ShapeDtypeStruct((t, n), F32),
        compiler_params=_cparams(("arbitrary", "arbitrary")),
        name="proj",
    )(u2d, w, bias.reshape(1, n))


def _ret_log_gamma(d, h):
    hh = h if d == 0 else RET_H - 1 - h
    return float(np.log(1.0 - 2.0 ** (-5.0 - hh)))


def _ret_tables():
    c = RET_C
    i = np.arange(c, dtype=np.float64)
    dec = np.zeros((2 * RET_H, c, c), np.float32)
    qd = np.zeros((2 * RET_H, c, RET_DK), np.float32)
    kd = np.zeros((2 * RET_H, c, RET_DK), np.float32)
    for d in range(2):
        tau = i if d == 0 else c - 1 - i
        rel = tau[:, None] - tau[None, :]
        for h in range(RET_H):
            lg = _ret_log_gamma(d, h)
            dec[d * RET_H + h] = np.where(rel >= 0, np.exp(np.maximum(rel, 0.0) * lg), 0.0)
            qd[d * RET_H + h] = np.exp((tau + 1.0) * lg)[:, None]
            kd[d * RET_H + h] = np.exp((c - 1.0 - tau) * lg)[:, None]
    return jnp.asarray(dec), jnp.asarray(qd), jnp.asarray(kd)


def _rope(x, cos2, sin2):
    return x * cos2 + pltpu.roll(x, RET_DK // 2, 1) * sin2


def _ret_kernel(qf_ref, kf_ref, vf_ref, qb_ref, kb_ref, vb_ref, cosf_ref, sinf_ref, cosb_ref, sinb_ref,
                dec_ref, qd_ref, kd_ref, s0_ref, of_ref, ob_ref, sfin_ref, s_ref, *, use_rope, n_chunks):
    n = pl.program_id(1)

    @pl.when(n == 0)
    def _():
        s_ref[...] = s0_ref[0]

    ins = ((qf_ref, kf_ref, vf_ref, cosf_ref, sinf_ref), (qb_ref, kb_ref, vb_ref, cosb_ref, sinb_ref))
    outs = (of_ref, ob_ref)
    chains = [(d, h) for d in range(2) for h in range(RET_H)]
    q, k, v = {}, {}, {}
    for d, h in chains:
        q_ref, k_ref, v_ref, cos_ref, sin_ref = ins[d]
        dk = slice(h * RET_DK, (h + 1) * RET_DK)
        q[d, h] = q_ref[0, :, dk]
        k[d, h] = k_ref[0, :, dk] * (RET_DK ** -0.5)
        if use_rope:
            q[d, h] = _rope(q[d, h], cos_ref[...], sin_ref[...])
            k[d, h] = _rope(k[d, h], cos_ref[...], sin_ref[...])
        v[d, h] = v_ref[0, :, h * RET_DV:(h + 1) * RET_DV]
    sc = {ch: _mm_nt(q[ch], k[ch]) * dec_ref[ch[0] * RET_H + ch[1]] for ch in chains}
    s_old = {ch: s_ref[ch[0], ch[1]] for ch in chains}
    o = {ch: _mm(sc[ch], v[ch]) + _mm(q[ch] * qd_ref[ch[0] * RET_H + ch[1]], s_old[ch]) for ch in chains}
    s_new = {ch: s_old[ch] * float(np.exp(RET_C * _ret_log_gamma(*ch)))
             + _mm_tn(k[ch] * kd_ref[ch[0] * RET_H + ch[1]], v[ch]) for ch in chains}
    for d, h in chains:
        outs[d][0, :, h * RET_DV:(h + 1) * RET_DV] = o[d, h]
        s_ref[d, h] = s_new[d, h]

    @pl.when(n == n_chunks - 1)
    def _():
        sfin_ref[0] = s_ref[...]


def _retention(pr, cos2, sin2, s0, use_rope):
    b, l, _ = pr.shape
    c = RET_C
    nc = l // c
    dec, qd, kd = _ret_tables()
    qw, vw = RET_H * RET_DK, RET_H * RET_DV

    def fw(blk):
        return lambda i, n: (i, n, blk)

    def bw(blk):
        return lambda i, n: (i, nc - 1 - n, blk)

    def const(*shape):
        return pl.BlockSpec(shape, lambda i, n: (0,) * len(shape))

    st = pl.BlockSpec((1, 2, RET_H, RET_DK, RET_DV), lambda i, n: (i, 0, 0, 0, 0))
    in_specs = [
        pl.BlockSpec((1, c, qw), fw(0)), pl.BlockSpec((1, c, qw), fw(1)), pl.BlockSpec((1, c, vw), fw(1)),
        pl.BlockSpec((1, c, qw), bw(0)), pl.BlockSpec((1, c, qw), bw(1)), pl.BlockSpec((1, c, vw), bw(1)),
        pl.BlockSpec((c, RET_DK), lambda i, n: (n, 0)), pl.BlockSpec((c, RET_DK), lambda i, n: (n, 0)),
        pl.BlockSpec((c, RET_DK), lambda i, n: (nc - 1 - n, 0)), pl.BlockSpec((c, RET_DK), lambda i, n: (nc - 1 - n, 0)),
        const(2 * RET_H, c, c), const(2 * RET_H, c, RET_DK), const(2 * RET_H, c, RET_DK), st,
    ]
    out_specs = [pl.BlockSpec((1, c, vw), fw(0)), pl.BlockSpec((1, c, vw), bw(0)), st]
    sd = jax.ShapeDtypeStruct
    return pl.pallas_call(
        functools.partial(_ret_kernel, use_rope=use_rope, n_chunks=nc),
        grid=(b, nc),
        in_specs=in_specs,
        out_specs=out_specs,
        out_shape=[sd((b, l, vw), F32), sd((b, l, vw), F32), sd((b, 2, RET_H, RET_DK, RET_DV), F32)],
        scratch_shapes=[pltpu.VMEM((2, RET_H, RET_DK, RET_DV), F32)],
        compiler_params=_cparams(("arbitrary", "arbitrary")),
        name="retention",
    )(pr, pr, pr, pr, pr, pr, cos2, sin2, cos2, sin2, dec, qd, kd, s0)


def _gla_boundary(b, lv, reverse, rolls):
    c, w = b.shape
    m = 1 << lv
    if m >= SUBLANES:
        parts = []
        for p0 in range(0, c, 2 * m):
            e = p0 + m if reverse else p0 + m - 1
            parts.append(jnp.broadcast_to(b[e:e + 1, :], (2 * m, w)))
        return jnp.concatenate(parts, axis=0)

    def rolled(s):
        s %= c
        if s not in rolls:
            rolls[s] = b if s == 0 else pltpu.roll(b, s, 0)
        return rolls[s]

    row = lax.broadcasted_iota(jnp.int32, (c, w), 0)
    r = jnp.bitwise_and(row, m - 1)
    upper = jnp.bitwise_and(jnp.right_shift(row, lv), 1) == 1
    out = b
    for t in range(m):
        if reverse:
            out = jnp.where(upper & (r == t), rolled(t), out)
            out = jnp.where(jnp.logical_not(upper) & (r == t), rolled(-(m - t)), out)
        else:
            out = jnp.where(upper & (r == t), rolled(t + 1), out)
            out = jnp.where(jnp.logical_not(upper) & (r == t), rolled(-(m - 1 - t)), out)
    return out


def _gla_kernel(qf_ref, kf_ref, vf_ref, lrf_ref, qb_ref, kb_ref, vb_ref, lrb_ref, w2_ref, b2_ref, s0_ref,
                of_ref, ob_ref, sfin_ref, st_ref, *, n_chunks):
    c = GLA_C
    n = pl.program_id(1)

    @pl.when(n == 0)
    def _():
        st_ref[...] = s0_ref[0]

    ins = ((qf_ref, kf_ref, vf_ref, lrf_ref), (qb_ref, kb_ref, vb_ref, lrb_ref))
    outs = (of_ref, ob_ref)
    hw = GLA_H * GLA_DK
    row = lax.broadcasted_iota(jnp.int32, (c, hw), 0)
    ri = lax.broadcasted_iota(jnp.int32, (c, c), 0)
    ci = lax.broadcasted_iota(jnp.int32, (c, c), 1)
    q, k, v, b_inc, b_rest, tot, ql, kl = {}, {}, {}, {}, {}, {}, {}, {}
    for d in range(2):
        reverse = d == 1
        q_ref, k_ref, v_ref, lr_ref = ins[d]
        q[d] = q_ref[0] * (GLA_DK ** -0.5)
        k[d] = k_ref[0]
        v[d] = v_ref[0]
        gate = _log_sigmoid(_mm_hi(lr_ref[0], w2_ref[d]) + b2_ref[d]) * (1.0 / GLA_GATE_NORM)
        b = _scan_rows(gate, reverse)
        tot[d] = b[0:1] if reverse else b[c - 1:c]
        b_inc[d], b_rest[d] = b, tot[d] - b
        rolls = {}
        for lv in range(GLA_LEVELS):
            be = _gla_boundary(b, lv, reverse, rolls)
            upper = jnp.bitwise_and(jnp.right_shift(row, lv), 1) == 1
            second = jnp.logical_not(upper) if reverse else upper
            e = jnp.exp(jnp.where(second, b - be, be - b))
            ql[d, lv] = jnp.where(second, q[d] * e, 0.0)
            kl[d, lv] = jnp.where(second, 0.0, k[d] * e)
    chains = [(d, h) for d in range(2) for h in range(GLA_H)]

    def dk(h):
        return slice(h * GLA_DK, (h + 1) * GLA_DK)

    def dv(h):
        return slice(h * GLA_DV, (h + 1) * GLA_DV)

    attn = {(d, h): jnp.where(ri == ci, jnp.sum(q[d][:, dk(h)] * k[d][:, dk(h)], axis=-1, keepdims=True), 0.0)
            for d, h in chains}
    for lv in range(GLA_LEVELS):
        same = jnp.right_shift(ri, lv + 1) == jnp.right_shift(ci, lv + 1)
        for d, h in chains:
            attn[d, h] = attn[d, h] + jnp.where(same, _mm_nt(ql[d, lv][:, dk(h)], kl[d, lv][:, dk(h)]), 0.0)
    st_old = {ch: st_ref[ch[0], ch[1]] for ch in chains}
    o = {(d, h): _mm(attn[d, h], v[d][:, dv(h)]) + _mm_nt(q[d][:, dk(h)] * jnp.exp(b_inc[d][:, dk(h)]), st_old[d, h])
         for d, h in chains}
    st_new = {(d, h): st_old[d, h] * jnp.exp(tot[d][:, dk(h)])
              + _mm_tn(v[d][:, dv(h)], k[d][:, dk(h)] * jnp.exp(b_rest[d][:, dk(h)])) for d, h in chains}
    for d, h in chains:
        outs[d][0, :, dv(h)] = o[d, h]
        st_ref[d, h] = st_new[d, h]

    @pl.when(n == n_chunks - 1)
    def _():
        sfin_ref[0] = st_ref[...]


def _gla(pg, w2p, b2, s0t):
    b, l, _ = pg.shape
    c = GLA_C
    nc = l // c
    qw, vw = GLA_H * GLA_DK, GLA_H * GLA_DV
    lr_blk = (2 * qw + 2 * vw) // LANES

    def fw(blk):
        return lambda i, n: (i, n, blk)

    def bw(blk):
        return lambda i, n: (i, nc - 1 - n, blk)

    st = pl.BlockSpec((1, 2, GLA_H, GLA_DV, GLA_DK), lambda i, n: (i, 0, 0, 0, 0))
    in_specs = [
        pl.BlockSpec((1, c, qw), fw(0)), pl.BlockSpec((1, c, qw), fw(1)), pl.BlockSpec((1, c, vw), fw(1)),
        pl.BlockSpec((1, c, LANES), fw(lr_blk)),
        pl.BlockSpec((1, c, qw), bw(0)), pl.BlockSpec((1, c, qw), bw(1)), pl.BlockSpec((1, c, vw), bw(1)),
        pl.BlockSpec((1, c, LANES), bw(lr_blk)),
        pl.BlockSpec((2, LANES, qw), lambda i, n: (0, 0, 0)),
        pl.BlockSpec((2, 1, qw), lambda i, n: (0, 0, 0)),
        st,
    ]
    out_specs = [pl.BlockSpec((1, c, vw), fw(0)), pl.BlockSpec((1, c, vw), bw(0)), st]
    sd = jax.ShapeDtypeStruct
    return pl.pallas_call(
        functools.partial(_gla_kernel, n_chunks=nc),
        grid=(b, nc),
        in_specs=in_specs,
        out_specs=out_specs,
        out_shape=[sd((b, l, vw), F32), sd((b, l, vw), F32), sd((b, 2, GLA_H, GLA_DV, GLA_DK), F32)],
        scratch_shapes=[pltpu.VMEM((2, GLA_H, GLA_DV, GLA_DK), F32)],
        compiler_params=_cparams(("arbitrary", "arbitrary")),
        name="gla",
    )(pg, pg, pg, pg, pg, pg, pg, pg, w2p, b2, s0t)


def _gla_gate_params(w2, b2):
    w2p = jnp.pad(w2, ((0, 0), (0, LANES - GLA_LOWRANK), (0, 0)))
    return w2p, b2.reshape(2, 1, GLA_H * GLA_DK)


def _split3(x):
    hi = x.astype(BF16)
    r1 = x - hi.astype(F32)
    mid = r1.astype(BF16)
    lo = (r1 - mid.astype(F32)).astype(BF16)
    return hi, mid, lo


def _head_sum(x):
    i = lax.broadcasted_iota(jnp.int32, (LANES, LANES), 0)
    j = lax.broadcasted_iota(jnp.int32, (LANES, LANES), 1)
    ones = jnp.where(jnp.right_shift(i, 6) == jnp.right_shift(j, 6), 1.0, 0.0).astype(BF16)
    out = []
    for t in range(x.shape[1] // LANES):
        parts = _split3(x[:, t * LANES:(t + 1) * LANES])
        out.append(sum(jnp.dot(p, ones, preferred_element_type=F32) for p in parts))
    return jnp.concatenate(out, axis=1)


RW_TM = 256
HALO = SUBLANES


def _rw_prep_kernel(x_ref, xp_ref, xn_ref, taps_ref, kkp_ref, ka_ref, rk_ref, w0_ref, wb_ref, a0_ref, ab_ref,
                    gb_ref, r_ref, v_ref, kk_ref, lw_ref, kd_ref, be_ref, bonus_ref, g_ref, xbuf, *, n_tiles):
    tm = RW_TM
    j = pl.program_id(1)
    xbuf[HALO:HALO + tm, :] = x_ref[0]
    xbuf[0:HALO, :] = jnp.where(j == 0, 0.0, xp_ref[0])
    xbuf[HALO + tm:2 * HALO + tm, :] = jnp.where(j == n_tiles - 1, 0.0, xn_ref[0])

    def shifted(c0, c1):
        return (taps_ref[0:1, c0:c1] * xbuf[HALO - 1:HALO - 1 + tm, c0:c1]
                + taps_ref[1:2, c0:c1] * xbuf[HALO:HALO + tm, c0:c1]
                + taps_ref[2:3, c0:c1] * xbuf[HALO + 1:HALO + 1 + tm, c0:c1])

    r = shifted(0, D)
    k = shifted(D, 2 * D)
    v = shifted(2 * D, 3 * D)
    lora = shifted(3 * D, N_RW_COLS)
    xw = lora[:, 0:RW_W_LORA]
    xa = lora[:, RW_W_LORA:RW_W_LORA + RW_A_LORA]
    xg = lora[:, RW_W_LORA + RW_A_LORA:]
    r_ref[0] = r
    v_ref[0] = v
    kk = k * kkp_ref[...]
    kk = kk * lax.rsqrt(jnp.maximum(_head_sum(kk * kk), 1e-24))
    kk_ref[0] = kk
    g_ref[0] = _mm(_sigmoid(xg), gb_ref[...])
    wh = jnp.tanh(xw)
    kd_sum = jnp.zeros_like(k)
    for d in range(2):
        w = -_softplus(-(w0_ref[d:d + 1, :] + _mm(wh, wb_ref[d]))) - 0.5
        lw_ref[d, 0] = -jnp.exp(w)
        a = _sigmoid(a0_ref[d:d + 1, :] + _mm(xa, ab_ref[d]))
        kd = k * (1.0 + (a - 1.0) * ka_ref[...])
        kd_ref[d, 0] = kd
        be_ref[d, 0] = kk * a
        kd_sum = kd_sum + kd
    bonus_ref[0] = _head_sum(r * kd_sum * rk_ref[...]) * v


def _rw_prep(rw, taps, kkp, ka, rk, w0, wb, a0, ab, gb):
    b, l, _ = rw.shape
    tm = RW_TM
    nt = l // tm
    hb = tm // HALO
    tok = pl.BlockSpec((1, tm, D), lambda i, j: (i, j, 0))
    tokd = pl.BlockSpec((2, 1, tm, D), lambda i, j: (0, i, j, 0))

    def full(*shape):
        return pl.BlockSpec(shape, lambda i, j: (0,) * len(shape))

    in_specs = [
        pl.BlockSpec((1, tm, N_RW_COLS), lambda i, j: (i, j, 0)),
        pl.BlockSpec((1, HALO, N_RW_COLS), lambda i, j: (i, jnp.maximum(j * hb - 1, 0), 0)),
        pl.BlockSpec((1, HALO, N_RW_COLS), lambda i, j: (i, jnp.minimum((j + 1) * hb, l // HALO - 1), 0)),
        full(3, N_RW_COLS), full(1, D), full(1, D), full(1, D),
        full(2, D), full(2, RW_W_LORA, D), full(2, D), full(2, RW_A_LORA, D), full(RW_G_LORA, D),
    ]
    sd = jax.ShapeDtypeStruct
    return pl.pallas_call(
        functools.partial(_rw_prep_kernel, n_tiles=nt),
        grid=(b, nt),
        in_specs=in_specs,
        out_specs=[tok, tok, tok, tokd, tokd, tokd, tok, tok],
        out_shape=[sd((b, l, D), F32)] * 3 + [sd((2, b, l, D), F32)] * 3 + [sd((b, l, D), F32)] * 2,
        scratch_shapes=[pltpu.VMEM((tm + 2 * HALO, N_RW_COLS), F32)],
        compiler_params=_cparams(("arbitrary", "arbitrary")),
        name="rwkv_prep",
    )(rw, rw, rw, taps, kkp.reshape(1, D), ka.reshape(1, D), rk.reshape(1, D), w0, wb, a0, ab, gb)


def _tile8(x):
    return jnp.concatenate([x] * 8, axis=0)


def _rw_scan_kernel(rf_ref, vf_ref, kkf_ref, lwf_ref, kdf_ref, bef_ref, rb_ref, vb_ref, kkb_ref, lwb_ref, kdb_ref,
                    beb_ref, s0_ref, yf_ref, yb_ref, sfin_ref, zt_ref, *, n_tiles):
    c, hw = RW_C, RW_HALF
    pk = 8 * c
    n_chunks = RW_TL // c
    j = pl.program_id(1)

    @pl.when(j == 0)
    def _():
        zt_ref[...] = jnp.zeros_like(zt_ref)
        for d in range(2):
            for g in range(4):
                for jj in range(4):
                    blk = slice(RW_N * jj, RW_N * (jj + 1))
                    zt_ref[d, g, blk, blk] = s0_ref[0, d, 4 * g + jj]

    ins = ((rf_ref, vf_ref, kkf_ref, lwf_ref, kdf_ref, bef_ref), (rb_ref, vb_ref, kkb_ref, lwb_ref, kdb_ref, beb_ref))
    outs = (yf_ref, yb_ref)
    ri = lax.broadcasted_iota(jnp.int32, (c, pk), 0)
    cs = jnp.bitwise_and(lax.broadcasted_iota(jnp.int32, (c, pk), 1), c - 1)
    eye = (ri == cs).astype(F32)
    strict, incl, lvl = [], [], []
    for d in range(2):
        ti = ri if d == 0 else c - 1 - ri
        ts = cs if d == 0 else c - 1 - cs
        strict.append(ts < ti)
        incl.append(ts <= ti)
        lv_masks = []
        for lv in range(RW_LEVELS):
            same = jnp.right_shift(ti, lv + 1) == jnp.right_shift(ts, lv + 1)
            lower = ((jnp.bitwise_and(jnp.right_shift(ti, lv), 1) == 1)
                     & (jnp.bitwise_and(jnp.right_shift(ts, lv), 1) == 0))
            lv_masks.append(same & lower)
        lvl.append(lv_masks)
    bi = lax.broadcasted_iota(jnp.int32, (pk, pk), 0)
    bj = lax.broadcasted_iota(jnp.int32, (pk, pk), 1)
    bd_p = jnp.right_shift(bi, 5) == jnp.right_shift(bj, 5)
    bd_z = jnp.right_shift(bi, 6) == jnp.right_shift(bj, 6)
    wi = lax.broadcasted_iota(jnp.int32, (pk, hw), 0)
    wj = lax.broadcasted_iota(jnp.int32, (pk, hw), 1)
    bd_w = jnp.right_shift(wi, 5) == jnp.right_shift(wj, 6)

    def bdp(x):
        return jnp.where(bd_p, _tile8(x), 0.0)

    def bdw(x):
        return jnp.where(bd_w, _tile8(x), 0.0)

    chains = [(d, half) for d in range(2) for half in range(2)]

    def each(fn):
        return {ch: fn(ch) for ch in chains}

    def chunk(i, carry):
        rows = (pl.ds(pl.multiple_of(i * c, c), c), pl.ds(pl.multiple_of((n_chunks - 1 - i) * c, c), c))

        def load(ch):
            d, half = ch
            ln = slice(half * hw, (half + 1) * hw)
            refs = ins[d]
            return (refs[0][0, rows[d], ln], refs[1][0, rows[d], ln], refs[2][0, rows[d], ln],
                    refs[3][0, 0, rows[d], ln], refs[4][0, 0, rows[d], ln], refs[5][0, 0, rows[d], ln])

        x = each(load)
        r, v, kk = each(lambda ch: x[ch][0]), each(lambda ch: x[ch][1]), each(lambda ch: x[ch][2])
        lw, kd, be = each(lambda ch: x[ch][3]), each(lambda ch: x[ch][4]), each(lambda ch: x[ch][5])
        b = each(lambda ch: _scan_rows(lw[ch], ch[0] == 1))
        tot = each(lambda ch: b[ch][0:1] if ch[0] == 1 else b[ch][c - 1:c])
        en = each(lambda ch: jnp.exp(-b[ch]))
        es = each(lambda ch: jnp.exp(tot[ch] - b[ch]))
        lhs = each(lambda ch: jnp.concatenate([kk[ch] * jnp.exp(b[ch] - lw[ch]), r[ch] * jnp.exp(b[ch])], axis=0))
        gb = each(lambda ch: _mm_nt(lhs[ch], bdw(be[ch] * en[ch])))
        gk = each(lambda ch: _mm_nt(lhs[ch], bdw(kd[ch] * en[ch])))
        a_ab = each(lambda ch: jnp.where(strict[ch[0]], gb[ch][0:c], 0.0))
        a_rb = each(lambda ch: jnp.where(incl[ch[0]], gb[ch][c:], 0.0))
        a_ak = each(lambda ch: jnp.where(strict[ch[0]], gk[ch][0:c], 0.0))
        a_rk = each(lambda ch: jnp.where(incl[ch[0]], gk[ch][c:], 0.0))
        t_inv = each(lambda ch: eye - jnp.where(lvl[ch[0]][0], a_ab[ch], 0.0))
        for lv in range(1, RW_LEVELS):
            xm = each(lambda ch: _mm(jnp.where(lvl[ch[0]][lv], a_ab[ch], 0.0), bdp(t_inv[ch])))
            t_inv = each(lambda ch: t_inv[ch] - _mm(t_inv[ch], bdp(xm[ch])))
        zt_old = {(ch, g): zt_ref[ch[0], 2 * ch[1] + g] for ch in chains for g in range(2)}
        zz = each(lambda ch: jnp.concatenate(
            [_mm_nt(lhs[ch][:, g * pk:(g + 1) * pk], zt_old[ch, g]) for g in range(2)], axis=1))
        vbd = each(lambda ch: bdw(v[ch]))
        u = each(lambda ch: _mm(t_inv[ch], bdw(-(zz[ch][0:c] + _mm(a_ak[ch], vbd[ch])))))
        y = each(lambda ch: zz[ch][c:] + _mm(a_rb[ch], bdw(u[ch])) + _mm(a_rk[ch], vbd[ch]))
        uv = each(lambda ch: jnp.concatenate([u[ch], v[ch]], axis=0))
        bk = each(lambda ch: jnp.concatenate([be[ch] * es[ch], kd[ch] * es[ch]], axis=0))
        etot = each(lambda ch: jnp.exp(tot[ch]))
        zt_new = {}
        for ch in chains:
            for g in range(2):
                gl = slice(g * pk, (g + 1) * pk)
                upd = jnp.where(bd_z, _mm_tn(uv[ch][:, gl], bk[ch][:, gl]), 0.0)
                zt_new[ch, g] = zt_old[ch, g] * etot[ch][:, gl] + upd
        for ch in chains:
            d, half = ch
            outs[d][0, rows[d], half * hw:(half + 1) * hw] = y[ch]
            for g in range(2):
                zt_ref[d, 2 * half + g] = zt_new[ch, g]
        return carry

    lax.fori_loop(0, n_chunks, chunk, 0)

    @pl.when(j == n_tiles - 1)
    def _():
        for d in range(2):
            for g in range(4):
                for jj in range(4):
                    blk = slice(RW_N * jj, RW_N * (jj + 1))
                    sfin_ref[0, d, 4 * g + jj] = zt_ref[d, g, blk, blk]


def _rw_scan(r, v, kk, lw, kd, be, s0):
    b, l, _ = r.shape
    tl = RW_TL
    nt = l // tl
    tok_f = pl.BlockSpec((1, tl, D), lambda i, j: (i, j, 0))
    tok_b = pl.BlockSpec((1, tl, D), lambda i, j: (i, nt - 1 - j, 0))
    dir_f = pl.BlockSpec((1, 1, tl, D), lambda i, j: (0, i, j, 0))
    dir_b = pl.BlockSpec((1, 1, tl, D), lambda i, j: (1, i, nt - 1 - j, 0))
    st = pl.BlockSpec((1, 2, RW_H, RW_N, RW_N), lambda i, j: (i, 0, 0, 0, 0))
    sd = jax.ShapeDtypeStruct
    return pl.pallas_call(
        functools.partial(_rw_scan_kernel, n_tiles=nt),
        grid=(b, nt),
        in_specs=[tok_f, tok_f, tok_f, dir_f, dir_f, dir_f, tok_b, tok_b, tok_b, dir_b, dir_b, dir_b, st],
        out_specs=[tok_f, tok_b, st],
        out_shape=[sd((b, l, D), F32), sd((b, l, D), F32), sd((b, 2, RW_H, RW_N, RW_N), F32)],
        scratch_shapes=[pltpu.VMEM((2, 4, 4 * RW_N, 4 * RW_N), F32)],
        compiler_params=_cparams(("arbitrary", "arbitrary")),
        name="rwkv_scan",
    )(r, v, kk, lw, kd, be, r, v, kk, lw, kd, be, s0)


MERGE_TM = 128
ROUTER_LANES = LANES


def _merge_kernel(orf_ref, orb_ref, rg_ref, ogf_ref, ogb_ref, gg_ref, yf_ref, yb_ref, bonus_ref, grw_ref,
                  gates_ref, x_ref, gt_ref, sh_ref, sc_ref, wbr_ref, wout_ref, rgn_ref, ggn_ref, gng_ref, gnb_ref,
                  l1g_ref, l1b_ref, wr_ref, br_ref, x1_ref, u2_ref, lg_ref):
    o_ret = orf_ref[0] + orb_ref[0]
    o_gla = ogf_ref[0] + ogb_ref[0]
    y_ret, y_gla = [], []
    for h in range(RET_H):
        hs = slice(h * RET_DV, (h + 1) * RET_DV)
        y_ret.append(_ln(o_ret[:, hs]))
        og = o_gla[:, hs]
        y_gla.append(og * lax.rsqrt(jnp.mean(og * og, axis=-1, keepdims=True) + EPS))
    z_ret = _silu(rg_ref[0]) * (jnp.concatenate(y_ret, axis=1) * rgn_ref[...])
    z_gla = _silu(gg_ref[0]) * (jnp.concatenate(y_gla, axis=1) * ggn_ref[...])
    y = yf_ref[0] + yb_ref[0]
    mu = _head_sum(y) * (1.0 / RW_N)
    yc = y - mu
    var = _head_sum(yc * yc) * (1.0 / RW_N)
    z_rw = (yc * lax.rsqrt(var + EPS) * gng_ref[...] + gnb_ref[...] + bonus_ref[0]) * grw_ref[0]
    gates = gates_ref[0]
    mixed = (gates[:, 0:D] * _mm(z_ret, wbr_ref[0]) + gates[:, D:2 * D] * _mm(z_gla, wbr_ref[1])
             + gates[:, 2 * D:] * _mm(z_rw, wbr_ref[2]))
    mix = _mm(mixed, wout_ref[...])
    x1 = _ln(ALPHA * x_ref[0] + gt_ref[0] * mix) * l1g_ref[...] + l1b_ref[...]
    x1_ref[0] = x1
    u2 = _ln(x1) * (1.0 + sc_ref[0]) + sh_ref[0]
    u2_ref[0] = u2.astype(u2_ref.dtype)
    lg_ref[0] = _mm_hi(u2, wr_ref[...]) + br_ref[...]


def _merge(orf, orb, pr, ogf, ogb, pg, yf, yb, bonus, grw, gates, x, gt1, sh2, sc2, wbr, wout, rgn, ggn, gng, gnb,
           l1g, l1b, wr, br):
    b, l, _ = x.shape
    tm = MERGE_TM
    tok = pl.BlockSpec((1, tm, D), lambda i, j: (i, j, 0))
    gate_blk = pl.BlockSpec((1, tm, D), lambda i, j: (i, j, 2))
    vec = pl.BlockSpec((1, 1, D), lambda i, j: (i, 0, 0))
    row = pl.BlockSpec((1, D), lambda i, j: (0, 0))
    in_specs = [
        tok, tok, gate_blk, tok, tok, gate_blk, tok, tok, tok, tok,
        pl.BlockSpec((1, tm, 3 * D), lambda i, j: (i, j, 0)),
        tok, vec, vec, vec,
        pl.BlockSpec((3, D, D), lambda i, j: (0, 0, 0)),
        pl.BlockSpec((D, D), lambda i, j: (0, 0)),
        row, row, row, row, row, row,
        pl.BlockSpec((D, ROUTER_LANES), lambda i, j: (0, 0)),
        pl.BlockSpec((1, ROUTER_LANES), lambda i, j: (0, 0)),
    ]
    sd = jax.ShapeDtypeStruct
    r1 = lambda a: a.reshape(1, D)
    return pl.pallas_call(
        _merge_kernel,
        grid=(b, l // tm),
        in_specs=in_specs,
        out_specs=[tok, tok, pl.BlockSpec((1, tm, ROUTER_LANES), lambda i, j: (i, j, 0))],
        out_shape=[sd((b, l, D), F32), sd((b, l, D), BF16), sd((b, l, ROUTER_LANES), F32)],
        compiler_params=_cparams(("arbitrary", "arbitrary")),
        name="merge",
    )(orf, orb, pr, ogf, ogb, pg, yf, yb, bonus, grw, gates, x, gt1, sh2, sc2, wbr, wout,
      r1(rgn), r1(ggn), r1(gng), r1(gnb), r1(l1g), r1(l1b), wr, br)


MOE_TM = 512


def _first_argmax(x, lane):
    m = jnp.max(x, axis=-1, keepdims=True)
    idx = jnp.min(jnp.where(x == m, lane, ROUTER_LANES), axis=-1, keepdims=True)
    return m, idx


def _combine_weights(logits):
    lane = lax.broadcasted_iota(jnp.int32, logits.shape, 1)
    neg = -jnp.inf
    gl = jnp.where(lane < N_GROUPS, logits, neg)
    gmax, gidx = _first_argmax(gl, lane)
    g_w = 1.0 / jnp.sum(jnp.exp(gl - gmax), axis=-1, keepdims=True)
    lo = N_GROUPS + EPG * gidx
    el = jnp.where((lane >= lo) & (lane < lo + EPG), logits, neg)
    m1, i1 = _first_argmax(el, lane)
    m2, i2 = _first_argmax(jnp.where(lane == i1, neg, el), lane)
    e2 = jnp.exp(m2 - m1)
    w1 = 1.0 / (1.0 + e2)
    return g_w * (jnp.where(lane == i1, w1, 0.0) + jnp.where(lane == i2, e2 * w1, 0.0))


def _moe_kernel(u_ref, x_ref, lg_ref, gt_ref, wg_ref, wu_ref, wd_ref, l2g_ref, l2b_ref, o_ref, acc_ref, comb_ref):
    e = pl.program_id(2)

    @pl.when(e == 0)
    def _():
        acc_ref[...] = jnp.zeros_like(acc_ref)
        comb_ref[...] = _combine_weights(lg_ref[0])

    u = u_ref[0]
    hid = _silu(jnp.dot(u, wg_ref[0], preferred_element_type=F32)) * jnp.dot(u, wu_ref[0], preferred_element_type=F32)
    lane = lax.broadcasted_iota(jnp.int32, comb_ref.shape, 1)
    c_e = jnp.sum(jnp.where(lane == e + N_GROUPS, comb_ref[...], 0.0), axis=-1, keepdims=True)
    acc_ref[...] += c_e * _mm(hid, wd_ref[0])

    @pl.when(e == N_EXPERTS - 1)
    def _():
        o_ref[0] = _ln(ALPHA * x_ref[0] + gt_ref[0] * acc_ref[...]) * l2g_ref[...] + l2b_ref[...]


def _moe(u2, x1, logits, gt2, wg, wu, wd, l2g, l2b):
    b, l, _ = x1.shape
    tm = MOE_TM if l % MOE_TM == 0 else 256
    tok = pl.BlockSpec((1, tm, D), lambda i, j, e: (i, j, 0))
    row = pl.BlockSpec((1, D), lambda i, j, e: (0, 0))
    in_specs = [
        tok, tok,
        pl.BlockSpec((1, tm, ROUTER_LANES), lambda i, j, e: (i, j, 0)),
        pl.BlockSpec((1, 1, D), lambda i, j, e: (i, 0, 0)),
        pl.BlockSpec((1, D, EXPERT_FF), lambda i, j, e: (e, 0, 0)),
        pl.BlockSpec((1, D, EXPERT_FF), lambda i, j, e: (e, 0, 0)),
        pl.BlockSpec((1, EXPERT_FF, D), lambda i, j, e: (e, 0, 0)),
        row, row,
    ]
    return pl.pallas_call(
        _moe_kernel,
        grid=(b, l // tm, N_EXPERTS),
        in_specs=in_specs,
        out_specs=tok,
        out_shape=jax.ShapeDtypeStruct((b, l, D), F32),
        scratch_shapes=[pltpu.VMEM((tm, D), F32), pltpu.VMEM((tm, ROUTER_LANES), F32)],
        compiler_params=_cparams(("arbitrary",) * 3),
        name="moe",
    )(u2, x1, logits, gt2, wg, wu, wd, l2g.reshape(1, D), l2b.reshape(1, D))


def _rope_tables(l):
    t = np.arange(l)
    quarter = RET_DK // 4
    freqs = (np.float32(ROPE_BASE) ** (-np.arange(quarter, dtype=np.float32) / quarter)).astype(np.float32)
    rows = (t // GRID_W).astype(np.float32)
    cols = (t % GRID_W).astype(np.float32)
    ang = jnp.asarray(np.concatenate([rows[:, None] * freqs, cols[:, None] * freqs], -1))
    cos, sin = jnp.cos(ang), jnp.sin(ang)
    return jnp.concatenate([cos, cos], -1), jnp.concatenate([-sin, sin], -1)


def _layer_weights(p):
    n_rg = 2 * (RET_H * RET_DK + RET_H * RET_DV)
    w_in = p["w_in"]
    w_lr = jnp.pad(w_in[:, 2 * n_rg:2 * n_rg + GLA_LOWRANK], ((0, 0), (0, LANES - GLA_LOWRANK)))
    w2p, b2p = _gla_gate_params(p["gla_w2"], p["gla_b"])
    pad = ROUTER_LANES - N_GROUPS - N_EXPERTS
    return dict(
        p,
        w_ret=w_in[:, :n_rg].astype(BF16),
        w_gla=jnp.concatenate([w_in[:, n_rg:2 * n_rg], w_lr], axis=1).astype(BF16),
        w_rw=w_in[:, 2 * n_rg + GLA_LOWRANK:].astype(BF16),
        w_merge=p["w_merge"].astype(BF16),
        w_br=p["w_br"].astype(BF16),
        w_out=p["w_out"].astype(BF16),
        gla_w2p=w2p, gla_b2p=b2p,
        w_router=jnp.pad(jnp.concatenate([p["w_rg"], p["w_re"]], axis=1), ((0, 0), (0, pad))),
        b_router=jnp.pad(jnp.concatenate([p["b_rg"], p["b_re"]]), (0, pad)).reshape(1, ROUTER_LANES),
        w_eg=p["w_eg"].astype(BF16), w_eu=p["w_eu"].astype(BF16), w_ed=p["w_ed"].astype(BF16),
    )


def _layer(x, mod, s_ret, s_gla_t, s_rw, rope, p):
    b, l, _ = x.shape
    sh1, sc1, gt1, sh2, sc2, gt2 = [m.reshape(b, 1, D) for m in jnp.split(mod, 6, axis=-1)]
    u = _lnmod(x, sh1, sc1)
    u2d = u.reshape(b * l, D)
    zero = lambda n: jnp.zeros((n,), F32)
    pr = _proj(u2d, p["w_ret"], zero(p["w_ret"].shape[1]), 1024).reshape(b, l, -1)
    pg = _proj(u2d, p["w_gla"], zero(p["w_gla"].shape[1]), 640).reshape(b, l, -1)
    rw = _proj(u2d, p["w_rw"], zero(N_RW_COLS), 1664).reshape(b, l, -1)
    gates = _proj(u2d, p["w_merge"], p["b_merge"], 1024, act="sigmoid").reshape(b, l, -1)

    cos2, sin2 = rope if rope is not None else (jnp.ones((l, RET_DK), F32), jnp.zeros((l, RET_DK), F32))
    orf, orb, ret_fin = _retention(pr, cos2, sin2, s_ret, rope is not None)
    ogf, ogb, gla_fin_t = _gla(pg, p["gla_w2p"], p["gla_b2p"], s_gla_t)
    r, v, kk, lw, kd, be, bonus, grw = _rw_prep(rw, p["rwkv_shift"], p["rwkv_kk"], p["rwkv_ka"],
                                                p["rwkv_rk"].reshape(-1), p["rwkv_w0"], p["rwkv_wb"],
                                                p["rwkv_a0"], p["rwkv_ab"], p["rwkv_gb"])
    yf, yb, rw_fin = _rw_scan(r, v, kk, lw, kd, be, s_rw)
    x1, u2, logits = _merge(orf, orb, pr, ogf, ogb, pg, yf, yb, bonus, grw, gates, x, gt1, sh2, sc2,
                            p["w_br"], p["w_out"], p["ret_gn"], p["gla_gn"], p["rwkv_gn_g"], p["rwkv_gn_b"],
                            p["ln1_g"], p["ln1_b"], p["w_router"], p["b_router"])
    x2 = _moe(u2, x1, logits, gt2, p["w_eg"], p["w_eu"], p["w_ed"], p["ln2_g"], p["ln2_b"])
    return x2, (ret_fin, gla_fin_t, rw_fin)


_PARAM_NAMES = ("w_in", "rwkv_shift", "ret_gn", "gla_w2", "gla_b", "gla_gn", "rwkv_w0", "rwkv_wb", "rwkv_a0",
                "rwkv_ab", "rwkv_gb", "rwkv_kk", "rwkv_ka", "rwkv_rk", "rwkv_gn_g", "rwkv_gn_b", "w_br", "w_merge",
                "b_merge", "w_out", "ln1_g", "ln1_b", "ln2_g", "ln2_b", "w_rg", "b_rg", "w_re", "b_re", "w_eg",
                "w_eu", "w_ed")


def kernel(x_prompt, x_sample, state_ret, state_gla, state_rwkv, c, c_ctx, w_ada, b_ada, w_in, rwkv_shift, ret_gn,
           gla_w2, gla_b, gla_gn, rwkv_w0, rwkv_wb, rwkv_a0, rwkv_ab, rwkv_gb, rwkv_kk, rwkv_ka, rwkv_rk, rwkv_gn_g,
           rwkv_gn_b, w_br, w_merge, b_merge, w_out, ln1_g, ln1_b, ln2_g, ln2_b, w_rg, b_rg, w_re, b_re, w_eg, w_eu,
           w_ed):
    params = dict(zip(_PARAM_NAMES, (w_in, rwkv_shift, ret_gn, gla_w2, gla_b, gla_gn, rwkv_w0, rwkv_wb, rwkv_a0,
                                     rwkv_ab, rwkv_gb, rwkv_kk, rwkv_ka, rwkv_rk, rwkv_gn_g, rwkv_gn_b, w_br, w_merge,
                                     b_merge, w_out, ln1_g, ln1_b, ln2_g, ln2_b, w_rg, b_rg, w_re, b_re, w_eg, w_eu,
                                     w_ed)))
    bc, bl = x_prompt.shape[0], x_sample.shape[0]
    rope = _rope_tables(x_sample.shape[1])
    mod_rows = 2 * SUBLANES
    c_all = jnp.concatenate([c, c_ctx[None, :], jnp.zeros((mod_rows - bl - 1, D), F32)], axis=0)
    z_ret = jnp.zeros((bc, 2, RET_H, RET_DK, RET_DV), F32)
    z_gla_t = jnp.zeros((bc, 2, GLA_H, GLA_DV, GLA_DK), F32)
    z_rw = jnp.zeros((bc, 2, RW_H, RW_N, RW_N), F32)
    h_ctx, h_lat = x_prompt, x_sample
    new_ret, new_gla, new_rw = [], [], []
    for layer in range(DEPTH):
        p = _layer_weights({k: v[layer] for k, v in params.items()})
        mod = _modulation(c_all, w_ada[layer], b_ada[layer])
        mod_ctx = jnp.broadcast_to(mod[bl:bl + 1], (bc, 6 * D))
        h_ctx, (s_ret, s_gla_t, s_rw) = _layer(h_ctx, mod_ctx, z_ret, z_gla_t, z_rw, None, p)
        new_ret.append(s_ret)
        new_gla.append(jnp.swapaxes(s_gla_t, -1, -2))
        new_rw.append(s_rw)
        h_lat, _ = _layer(h_lat, mod[:bl], state_ret[:, layer], jnp.swapaxes(state_gla[:, layer], -1, -2),
                          state_rwkv[:, layer], rope, p)
    return (h_ctx, h_lat, jnp.stack(new_ret, axis=1), jnp.stack(new_gla, axis=1), jnp.stack(new_rw, axis=1))
```

```python
import functools

import jax
import jax.numpy as jnp
import numpy as np
from jax import lax
from jax.experimental import pallas as pl
from jax.experimental.pallas import tpu as pltpu

F32 = jnp.float32
BF16 = jnp.bfloat16
HI = lax.Precision.HIGHEST

D = 1024
DEPTH = 2
GRID_W = 64
RET_H, RET_DK, RET_DV = 4, 128, 256
GLA_H, GLA_DK, GLA_DV = 4, 128, 256
GLA_LOWRANK = 16
GLA_GATE_NORM = 16.0
RW_H, RW_N = 16, 64
RW_W_LORA, RW_A_LORA, RW_G_LORA = 64, 64, 128
N_RW_COLS = 3 * D + RW_W_LORA + RW_A_LORA + RW_G_LORA
N_GROUPS, EPG, N_EXPERTS, EXPERT_FF = 4, 4, 16, 512
ALPHA = (2 * DEPTH) ** 0.25
EPS = 1e-5
ROPE_BASE = 10000.0

LANES = 128
SUBLANES = 8
VMEM_LIMIT = 56 * 1024 * 1024

RET_C = 128
GLA_C = 64
GLA_LEVELS = 6
RW_C = 32
RW_LEVELS = 5
RW_TL = 128
RW_BB = 2
RW_G = 4
RW_GW = RW_G * RW_N
RW_SW = 2 * RW_N


def _cparams(sem):
    return pltpu.CompilerParams(dimension_semantics=sem, vmem_limit_bytes=VMEM_LIMIT)


def _mm(a, b):
    return jnp.dot(a.astype(BF16), b.astype(BF16), preferred_element_type=F32)


def _mm_nt(a, b):
    return lax.dot_general(a.astype(BF16), b.astype(BF16), (((1,), (1,)), ((), ())), preferred_element_type=F32)


def _mm_tn(a, b):
    return lax.dot_general(a.astype(BF16), b.astype(BF16), (((0,), (0,)), ((), ())), preferred_element_type=F32)


def _mm_hi(a, b):
    return jnp.dot(a, b, preferred_element_type=F32, precision=HI)


def _sigmoid(x):
    return 1.0 / (1.0 + jnp.exp(-x))


def _silu(x):
    return x * _sigmoid(x)


def _log_sigmoid(x):
    return jnp.minimum(x, 0.0) - jnp.log(1.0 + jnp.exp(-jnp.abs(x)))


def _softplus(x):
    return jnp.maximum(x, 0.0) + jnp.log(1.0 + jnp.exp(-jnp.abs(x)))


def _ln(x):
    mu = jnp.mean(x, axis=-1, keepdims=True)
    xc = x - mu
    var = jnp.mean(xc * xc, axis=-1, keepdims=True)
    return xc * lax.rsqrt(var + EPS)


def _scan_rows(x, reverse):
    n = x.shape[0]
    row = lax.broadcasted_iota(jnp.int32, x.shape, 0)
    sh = 1
    while sh < n:
        if reverse:
            x = x + jnp.where(row < n - sh, pltpu.roll(x, n - sh, 0), 0.0)
        else:
            x = x + jnp.where(row >= sh, pltpu.roll(x, sh, 0), 0.0)
        sh *= 2
    return x


def _mod_kernel(c_ref, w_ref, b_ref, o_ref):
    o_ref[...] = _mm_hi(_silu(c_ref[...]), w_ref[...]) + b_ref[...]


def _modulation(c_all, w, b):
    m, n, tn = c_all.shape[0], w.shape[1], 512
    return pl.pallas_call(
        _mod_kernel,
        grid=(n // tn,),
        in_specs=[pl.BlockSpec((m, D), lambda j: (0, 0)),
                  pl.BlockSpec((D, tn), lambda j: (0, j)),
                  pl.BlockSpec((1, tn), lambda j: (0, j))],
        out_specs=pl.BlockSpec((m, tn), lambda j: (0, j)),
        out_shape=jax.ShapeDtypeStruct((m, n), F32),
        compiler_params=_cparams(("arbitrary",)),
        name="adaln_mod",
    )(c_all, w, b.reshape(1, n))


def _lnmod_kernel(x_ref, sh_ref, sc_ref, u_ref):
    u_ref[0] = (_ln(x_ref[0]) * (1.0 + sc_ref[0]) + sh_ref[0]).astype(u_ref.dtype)


def _lnmod(x, sh, sc):
    b, l, _ = x.shape
    tm = 512 if l % 512 == 0 else 256
    vec = pl.BlockSpec((1, 1, D), lambda i, j: (i, 0, 0))
    return pl.pallas_call(
        _lnmod_kernel,
        grid=(b, l // tm),
        in_specs=[pl.BlockSpec((1, tm, D), lambda i, j: (i, j, 0)), vec, vec],
        out_specs=pl.BlockSpec((1, tm, D), lambda i, j: (i, j, 0)),
        out_shape=jax.ShapeDtypeStruct((b, l, D), BF16),
        compiler_params=_cparams(("arbitrary", "arbitrary")),
        name="ln_mod",
    )(x, sh, sc)


def _proj_kernel(u_ref, w_ref, b_ref, o_ref, *, act):
    y = jnp.dot(u_ref[...], w_ref[...], preferred_element_type=F32) + b_ref[...]
    if act == "sigmoid":
        y = _sigmoid(y)
    o_ref[...] = y.astype(o_ref.dtype)


def _proj(u2d, w, bias, tn, act=None):
    t, n = u2d.shape[0], w.shape[1]
    tm = 512
    return pl.pallas_call(
        functools.partial(_proj_kernel, act=act),
        grid=(t // tm, n // tn),
        in_specs=[pl.BlockSpec((tm, D), lambda i, j: (i, 0)),
                  pl.BlockSpec((D, tn), lambda i, j: (0, j)),
                  pl.BlockSpec((1, tn), lambda i, j: (0, j))],
        out_specs=pl.BlockSpec((tm, tn), lambda i, j: (i, j)),
        out_shape=jax.ShapeDtypeStruct((t, n), F32),
        compiler_params=_cparams(("arbitrary", "arbitrary")),
        name="proj",
    )(u2d, w, bias.reshape(1, n))


def _ret_log_gamma(d, h):
    hh = h if d == 0 else RET_H - 1 - h
    return float(np.log(1.0 - 2.0 ** (-5.0 - hh)))


def _ret_tables():
    c = RET_C
    i = np.arange(c, dtype=np.float64)
    dec = np.zeros((2 * RET_H, c, c), np.float32)
    qd = np.zeros((2 * RET_H, c, RET_DK), np.float32)
    kd = np.zeros((2 * RET_H, c, RET_DK), np.float32)
    for d in range(2):
        tau = i if d == 0 else c - 1 - i
        rel = tau[:, None] - tau[None, :]
        for h in range(RET_H):
            lg = _ret_log_gamma(d, h)
            dec[d * RET_H + h] = np.where(rel >= 0, np.exp(np.maximum(rel, 0.0) * lg), 0.0)
            qd[d * RET_H + h] = np.exp((tau + 1.0) * lg)[:, None]
            kd[d * RET_H + h] = np.exp((c - 1.0 - tau) * lg)[:, None]
    return jnp.asarray(dec), jnp.asarray(qd), jnp.asarray(kd)


def _rope(x, cos2, sin2):
    return x * cos2 + pltpu.roll(x, RET_DK // 2, 1) * sin2


def _ret_kernel(qf_ref, kf_ref, vf_ref, qb_ref, kb_ref, vb_ref, cosf_ref, sinf_ref, cosb_ref, sinb_ref,
                dec_ref, qd_ref, kd_ref, s0_ref, of_ref, ob_ref, sfin_ref, s_ref, *, use_rope, n_chunks):
    n = pl.program_id(1)

    @pl.when(n == 0)
    def _():
        s_ref[...] = s0_ref[0]

    ins = ((qf_ref, kf_ref, vf_ref, cosf_ref, sinf_ref), (qb_ref, kb_ref, vb_ref, cosb_ref, sinb_ref))
    outs = (of_ref, ob_ref)
    chains = [(d, h) for d in range(2) for h in range(RET_H)]
    q, k, v = {}, {}, {}
    for d, h in chains:
        q_ref, k_ref, v_ref, cos_ref, sin_ref = ins[d]
        dk = slice(h * RET_DK, (h + 1) * RET_DK)
        q[d, h] = q_ref[0, :, dk]
        k[d, h] = k_ref[0, :, dk] * (RET_DK ** -0.5)
        if use_rope:
            q[d, h] = _rope(q[d, h], cos_ref[...], sin_ref[...])
            k[d, h] = _rope(k[d, h], cos_ref[...], sin_ref[...])
        v[d, h] = v_ref[0, :, h * RET_DV:(h + 1) * RET_DV]
    sc = {ch: _mm_nt(q[ch], k[ch]) * dec_ref[ch[0] * RET_H + ch[1]] for ch in chains}
    s_old = {ch: s_ref[ch[0], ch[1]] for ch in chains}
    o = {ch: _mm(sc[ch], v[ch]) + _mm(q[ch] * qd_ref[ch[0] * RET_H + ch[1]], s_old[ch]) for ch in chains}
    s_new = {ch: s_old[ch] * float(np.exp(RET_C * _ret_log_gamma(*ch)))
             + _mm_tn(k[ch] * kd_ref[ch[0] * RET_H + ch[1]], v[ch]) for ch in chains}
    for d, h in chains:
        outs[d][0, :, h * RET_DV:(h + 1) * RET_DV] = o[d, h]
        s_ref[d, h] = s_new[d, h]

    @pl.when(n == n_chunks - 1)
    def _():
        sfin_ref[0] = s_ref[...]


def _retention(pr, cos2, sin2, s0, use_rope):
    b, l, _ = pr.shape
    c = RET_C
    nc = l // c
    dec, qd, kd = _ret_tables()
    qw, vw = RET_H * RET_DK, RET_H * RET_DV

    def fw(blk):
        return lambda i, n: (i, n, blk)

    def bw(blk):
        return lambda i, n: (i, nc - 1 - n, blk)

    def const(*shape):
        return pl.BlockSpec(shape, lambda i, n: (0,) * len(shape))

    st = pl.BlockSpec((1, 2, RET_H, RET_DK, RET_DV), lambda i, n: (i, 0, 0, 0, 0))
    in_specs = [
        pl.BlockSpec((1, c, qw), fw(0)), pl.BlockSpec((1, c, qw), fw(1)), pl.BlockSpec((1, c, vw), fw(1)),
        pl.BlockSpec((1, c, qw), bw(0)), pl.BlockSpec((1, c, qw), bw(1)), pl.BlockSpec((1, c, vw), bw(1)),
        pl.BlockSpec((c, RET_DK), lambda i, n: (n, 0)), pl.BlockSpec((c, RET_DK), lambda i, n: (n, 0)),
        pl.BlockSpec((c, RET_DK), lambda i, n: (nc - 1 - n, 0)), pl.BlockSpec((c, RET_DK), lambda i, n: (nc - 1 - n, 0)),
        const(2 * RET_H, c, c), const(2 * RET_H, c, RET_DK), const(2 * RET_H, c, RET_DK), st,
    ]
    out_specs = [pl.BlockSpec((1, c, vw), fw(0)), pl.BlockSpec((1, c, vw), bw(0)), st]
    sd = jax.ShapeDtypeStruct
    return pl.pallas_call(
        functools.partial(_ret_kernel, use_rope=use_rope, n_chunks=nc),
        grid=(b, nc),
        in_specs=in_specs,
        out_specs=out_specs,
        out_shape=[sd((b, l, vw), F32), sd((b, l, vw), F32), sd((b, 2, RET_H, RET_DK, RET_DV), F32)],
        scratch_shapes=[pltpu.VMEM((2, RET_H, RET_DK, RET_DV), F32)],
        compiler_params=_cparams(("arbitrary", "arbitrary")),
        name="retention",
    )(pr, pr, pr, pr, pr, pr, cos2, sin2, cos2, sin2, dec, qd, kd, s0)


def _gla_boundary(b, lv, reverse, rolls):
    c, w = b.shape
    m = 1 << lv
    if m >= SUBLANES:
        parts = []
        for p0 in range(0, c, 2 * m):
            e = p0 + m if reverse else p0 + m - 1
            parts.append(jnp.broadcast_to(b[e:e + 1, :], (2 * m, w)))
        return jnp.concatenate(parts, axis=0)

    def rolled(s):
        s %= c
        if s not in rolls:
            rolls[s] = b if s == 0 else pltpu.roll(b, s, 0)
        return rolls[s]

    row = lax.broadcasted_iota(jnp.int32, (c, w), 0)
    r = jnp.bitwise_and(row, m - 1)
    upper = jnp.bitwise_and(jnp.right_shift(row, lv), 1) == 1
    out = b
    for t in range(m):
        if reverse:
            out = jnp.where(upper & (r == t), rolled(t), out)
            out = jnp.where(jnp.logical_not(upper) & (r == t), rolled(-(m - t)), out)
        else:
            out = jnp.where(upper & (r == t), rolled(t + 1), out)
            out = jnp.where(jnp.logical_not(upper) & (r == t), rolled(-(m - 1 - t)), out)
    return out


def _gla_kernel(qf_ref, kf_ref, vf_ref, lrf_ref, qb_ref, kb_ref, vb_ref, lrb_ref, w2_ref, b2_ref, s0_ref,
                of_ref, ob_ref, sfin_ref, st_ref, *, n_chunks):
    c = GLA_C
    n = pl.program_id(1)

    @pl.when(n == 0)
    def _():
        st_ref[...] = s0_ref[0]

    ins = ((qf_ref, kf_ref, vf_ref, lrf_ref), (qb_ref, kb_ref, vb_ref, lrb_ref))
    outs = (of_ref, ob_ref)
    hw = GLA_H * GLA_DK
    row = lax.broadcasted_iota(jnp.int32, (c, hw), 0)
    ri = lax.broadcasted_iota(jnp.int32, (c, c), 0)
    ci = lax.broadcasted_iota(jnp.int32, (c, c), 1)
    q, k, v, b_inc, b_rest, tot, ql, kl = {}, {}, {}, {}, {}, {}, {}, {}
    for d in range(2):
        reverse = d == 1
        q_ref, k_ref, v_ref, lr_ref = ins[d]
        q[d] = q_ref[0] * (GLA_DK ** -0.5)
        k[d] = k_ref[0]
        v[d] = v_ref[0]
        gate = _log_sigmoid(_mm_hi(lr_ref[0], w2_ref[d]) + b2_ref[d]) * (1.0 / GLA_GATE_NORM)
        b = _scan_rows(gate, reverse)
        tot[d] = b[0:1] if reverse else b[c - 1:c]
        b_inc[d], b_rest[d] = b, tot[d] - b
        rolls = {}
        for lv in range(GLA_LEVELS):
            be = _gla_boundary(b, lv, reverse, rolls)
            upper = jnp.bitwise_and(jnp.right_shift(row, lv), 1) == 1
            second = jnp.logical_not(upper) if reverse else upper
            e = jnp.exp(jnp.where(second, b - be, be - b))
            ql[d, lv] = jnp.where(second, q[d] * e, 0.0)
            kl[d, lv] = jnp.where(second, 0.0, k[d] * e)
    chains = [(d, h) for d in range(2) for h in range(GLA_H)]

    def dk(h):
        return slice(h * GLA_DK, (h + 1) * GLA_DK)

    def dv(h):
        return slice(h * GLA_DV, (h + 1) * GLA_DV)

    attn = {(d, h): jnp.where(ri == ci, jnp.sum(q[d][:, dk(h)] * k[d][:, dk(h)], axis=-1, keepdims=True), 0.0)
            for d, h in chains}
    for lv in range(GLA_LEVELS):
        same = jnp.right_shift(ri, lv + 1) == jnp.right_shift(ci, lv + 1)
        for d, h in chains:
            attn[d, h] = attn[d, h] + jnp.where(same, _mm_nt(ql[d, lv][:, dk(h)], kl[d, lv][:, dk(h)]), 0.0)
    st_old = {ch: st_ref[ch[0], ch[1]] for ch in chains}
    o = {(d, h): _mm(attn[d, h], v[d][:, dv(h)]) + _mm_nt(q[d][:, dk(h)] * jnp.exp(b_inc[d][:, dk(h)]), st_old[d, h])
         for d, h in chains}
    st_new = {(d, h): st_old[d, h] * jnp.exp(tot[d][:, dk(h)])
              + _mm_tn(v[d][:, dv(h)], k[d][:, dk(h)] * jnp.exp(b_rest[d][:, dk(h)])) for d, h in chains}
    for d, h in chains:
        outs[d][0, :, dv(h)] = o[d, h]
        st_ref[d, h] = st_new[d, h]

    @pl.when(n == n_chunks - 1)
    def _():
        sfin_ref[0] = st_ref[...]


def _gla(pg, w2p, b2, s0t):
    b, l, _ = pg.shape
    c = GLA_C
    nc = l // c
    qw, vw = GLA_H * GLA_DK, GLA_H * GLA_DV
    lr_blk = (2 * qw + 2 * vw) // LANES

    def fw(blk):
        return lambda i, n: (i, n, blk)

    def bw(blk):
        return lambda i, n: (i, nc - 1 - n, blk)

    st = pl.BlockSpec((1, 2, GLA_H, GLA_DV, GLA_DK), lambda i, n: (i, 0, 0, 0, 0))
    in_specs = [
        pl.BlockSpec((1, c, qw), fw(0)), pl.BlockSpec((1, c, qw), fw(1)), pl.BlockSpec((1, c, vw), fw(1)),
        pl.BlockSpec((1, c, LANES), fw(lr_blk)),
        pl.BlockSpec((1, c, qw), bw(0)), pl.BlockSpec((1, c, qw), bw(1)), pl.BlockSpec((1, c, vw), bw(1)),
        pl.BlockSpec((1, c, LANES), bw(lr_blk)),
        pl.BlockSpec((2, LANES, qw), lambda i, n: (0, 0, 0)),
        pl.BlockSpec((2, 1, qw), lambda i, n: (0, 0, 0)),
        st,
    ]
    out_specs = [pl.BlockSpec((1, c, vw), fw(0)), pl.BlockSpec((1, c, vw), bw(0)), st]
    sd = jax.ShapeDtypeStruct
    return pl.pallas_call(
        functools.partial(_gla_kernel, n_chunks=nc),
        grid=(b, nc),
        in_specs=in_specs,
        out_specs=out_specs,
        out_shape=[sd((b, l, vw), F32), sd((b, l, vw), F32), sd((b, 2, GLA_H, GLA_DV, GLA_DK), F32)],
        scratch_shapes=[pltpu.VMEM((2, GLA_H, GLA_DV, GLA_DK), F32)],
        compiler_params=_cparams(("arbitrary", "arbitrary")),
        name="gla",
    )(pg, pg, pg, pg, pg, pg, pg, pg, w2p, b2, s0t)


def _gla_gate_params(w2, b2):
    w2p = jnp.pad(w2, ((0, 0), (0, LANES - GLA_LOWRANK), (0, 0)))
    return w2p, b2.reshape(2, 1, GLA_H * GLA_DK)


def _split3(x):
    hi = x.astype(BF16)
    r1 = x - hi.astype(F32)
    mid = r1.astype(BF16)
    lo = (r1 - mid.astype(F32)).astype(BF16)
    return hi, mid, lo


def _head_sum(x):
    i = lax.broadcasted_iota(jnp.int32, (LANES, LANES), 0)
    j = lax.broadcasted_iota(jnp.int32, (LANES, LANES), 1)
    ones = jnp.where(jnp.right_shift(i, 6) == jnp.right_shift(j, 6), 1.0, 0.0).astype(BF16)
    out = []
    for t in range(x.shape[1] // LANES):
        parts = _split3(x[:, t * LANES:(t + 1) * LANES])
        out.append(sum(jnp.dot(p, ones, preferred_element_type=F32) for p in parts))
    return jnp.concatenate(out, axis=1)


RW_TM = 256
HALO = SUBLANES


def _rw_prep_kernel(x_ref, xp_ref, xn_ref, taps_ref, kkp_ref, ka_ref, rk_ref, w0_ref, wb_ref, a0_ref, ab_ref,
                    gb_ref, r_ref, v_ref, kk_ref, lw_ref, kd_ref, be_ref, bonus_ref, g_ref, xbuf, *, n_tiles):
    tm = RW_TM
    j = pl.program_id(1)
    xbuf[HALO:HALO + tm, :] = x_ref[0]
    xbuf[0:HALO, :] = jnp.where(j == 0, 0.0, xp_ref[0])
    xbuf[HALO + tm:2 * HALO + tm, :] = jnp.where(j == n_tiles - 1, 0.0, xn_ref[0])

    def shifted(c0, c1):
        return (taps_ref[0:1, c0:c1] * xbuf[HALO - 1:HALO - 1 + tm, c0:c1]
                + taps_ref[1:2, c0:c1] * xbuf[HALO:HALO + tm, c0:c1]
                + taps_ref[2:3, c0:c1] * xbuf[HALO + 1:HALO + 1 + tm, c0:c1])

    r = shifted(0, D)
    k = shifted(D, 2 * D)
    v = shifted(2 * D, 3 * D)
    lora = shifted(3 * D, N_RW_COLS)
    xw = lora[:, 0:RW_W_LORA]
    xa = lora[:, RW_W_LORA:RW_W_LORA + RW_A_LORA]
    xg = lora[:, RW_W_LORA + RW_A_LORA:]
    r_ref[0] = r
    v_ref[0] = v
    kk = k * kkp_ref[...]
    kk = kk * lax.rsqrt(jnp.maximum(_head_sum(kk * kk), 1e-24))
    kk_ref[0] = kk
    g_ref[0] = _mm(_sigmoid(xg), gb_ref[...])
    wh = jnp.tanh(xw)
    kd_sum = jnp.zeros_like(k)
    for d in range(2):
        w = -_softplus(-(w0_ref[d:d + 1, :] + _mm(wh, wb_ref[d]))) - 0.5
        lw_ref[d, 0] = -jnp.exp(w)
        a = _sigmoid(a0_ref[d:d + 1, :] + _mm(xa, ab_ref[d]))
        kd = k * (1.0 + (a - 1.0) * ka_ref[...])
        kd_ref[d, 0] = kd
        be_ref[d, 0] = kk * a
        kd_sum = kd_sum + kd
    bonus_ref[0] = _head_sum(r * kd_sum * rk_ref[...]) * v


def _rw_prep(rw, taps, kkp, ka, rk, w0, wb, a0, ab, gb):
    b, l, _ = rw.shape
    tm = RW_TM
    nt = l // tm
    hb = tm // HALO
    tok = pl.BlockSpec((1, tm, D), lambda i, j: (i, j, 0))
    tokd = pl.BlockSpec((2, 1, tm, D), lambda i, j: (0, i, j, 0))

    def full(*shape):
        return pl.BlockSpec(shape, lambda i, j: (0,) * len(shape))

    in_specs = [
        pl.BlockSpec((1, tm, N_RW_COLS), lambda i, j: (i, j, 0)),
        pl.BlockSpec((1, HALO, N_RW_COLS), lambda i, j: (i, jnp.maximum(j * hb - 1, 0), 0)),
        pl.BlockSpec((1, HALO, N_RW_COLS), lambda i, j: (i, jnp.minimum((j + 1) * hb, l // HALO - 1), 0)),
        full(3, N_RW_COLS), full(1, D), full(1, D), full(1, D),
        full(2, D), full(2, RW_W_LORA, D), full(2, D), full(2, RW_A_LORA, D), full(RW_G_LORA, D),
    ]
    sd = jax.ShapeDtypeStruct
    return pl.pallas_call(
        functools.partial(_rw_prep_kernel, n_tiles=nt),
        grid=(b, nt),
        in_specs=in_specs,
        out_specs=[tok, tok, tok, tokd, tokd, tokd, tok, tok],
        out_shape=[sd((b, l, D), F32)] * 3 + [sd((2, b, l, D), F32)] * 3 + [sd((b, l, D), F32)] * 2,
        scratch_shapes=[pltpu.VMEM((tm + 2 * HALO, N_RW_COLS), F32)],
        compiler_params=_cparams(("arbitrary", "arbitrary")),
        name="rwkv_prep",
    )(rw, rw, rw, taps, kkp.reshape(1, D), ka.reshape(1, D), rk.reshape(1, D), w0, wb, a0, ab, gb)


def _tile_rows(x, n):
    return jnp.concatenate([x] * n, axis=0)


def _rw_scan_kernel(rf_ref, vf_ref, kkf_ref, lwf_ref, kdf_ref, bef_ref, rb_ref, vb_ref, kkb_ref, lwb_ref, kdb_ref,
                    beb_ref, s0_ref, yf_ref, yb_ref, sfin_ref, zt_ref, *, n_tiles):
    c, hw, sw = RW_C, RW_GW, RW_SW
    pk = RW_G * c
    n_groups, n_sub = RW_H // RW_G, RW_GW // RW_SW
    n_chunks = RW_TL // c
    j = pl.program_id(1)

    @pl.when(j == 0)
    def _():
        zt_ref[...] = jnp.zeros_like(zt_ref)
        for bb in range(RW_BB):
            for d in range(2):
                for g in range(RW_H // 2):
                    for jj in range(2):
                        blk = slice(RW_N * jj, RW_N * (jj + 1))
                        zt_ref[bb, d, g, blk, blk] = s0_ref[bb, d, 2 * g + jj]

    ins = ((rf_ref, vf_ref, kkf_ref, lwf_ref, kdf_ref, bef_ref), (rb_ref, vb_ref, kkb_ref, lwb_ref, kdb_ref, beb_ref))
    outs = (yf_ref, yb_ref)
    ri = lax.broadcasted_iota(jnp.int32, (c, pk), 0)
    cs = jnp.bitwise_and(lax.broadcasted_iota(jnp.int32, (c, pk), 1), c - 1)
    eye = (ri == cs).astype(F32)
    strict, incl, lvl = [], [], []
    for d in range(2):
        ti = ri if d == 0 else c - 1 - ri
        ts = cs if d == 0 else c - 1 - cs
        strict.append(ts < ti)
        incl.append(ts <= ti)
        lv_masks = []
        for lv in range(RW_LEVELS):
            same = jnp.right_shift(ti, lv + 1) == jnp.right_shift(ts, lv + 1)
            lower = ((jnp.bitwise_and(jnp.right_shift(ti, lv), 1) == 1)
                     & (jnp.bitwise_and(jnp.right_shift(ts, lv), 1) == 0))
            lv_masks.append(same & lower)
        lvl.append(lv_masks)
    bi = lax.broadcasted_iota(jnp.int32, (pk, pk), 0)
    bj = lax.broadcasted_iota(jnp.int32, (pk, pk), 1)
    bd_p = jnp.right_shift(bi, 5) == jnp.right_shift(bj, 5)
    zi = lax.broadcasted_iota(jnp.int32, (sw, sw), 0)
    zj = lax.broadcasted_iota(jnp.int32, (sw, sw), 1)
    bd_z = jnp.right_shift(zi, 6) == jnp.right_shift(zj, 6)
    wi = lax.broadcasted_iota(jnp.int32, (pk, hw), 0)
    wj = lax.broadcasted_iota(jnp.int32, (pk, hw), 1)
    bd_w = jnp.right_shift(wi, 5) == jnp.right_shift(wj, 6)

    def bdp(x):
        return jnp.where(bd_p, _tile_rows(x, RW_G), 0.0)

    def bdw(x):
        return jnp.where(bd_w, _tile_rows(x, RW_G), 0.0)

    chains = [(bb, d, half) for bb in range(RW_BB) for d in range(2) for half in range(n_groups)]

    def each(fn):
        return {ch: fn(ch) for ch in chains}

    def chunk(i, carry):
        rows = (pl.ds(pl.multiple_of(i * c, c), c), pl.ds(pl.multiple_of((n_chunks - 1 - i) * c, c), c))

        def load(ch):
            bb, d, half = ch
            ln = slice(half * hw, (half + 1) * hw)
            refs = ins[d]
            return (refs[0][bb, rows[d], ln], refs[1][bb, rows[d], ln], refs[2][bb, rows[d], ln],
                    refs[3][0, bb, rows[d], ln], refs[4][0, bb, rows[d], ln], refs[5][0, bb, rows[d], ln])

        x = each(load)
        r, v, kk = each(lambda ch: x[ch][0]), each(lambda ch: x[ch][1]), each(lambda ch: x[ch][2])
        lw, kd, be = each(lambda ch: x[ch][3]), each(lambda ch: x[ch][4]), each(lambda ch: x[ch][5])
        b = each(lambda ch: _scan_rows(lw[ch], ch[1] == 1))
        tot = each(lambda ch: b[ch][0:1] if ch[1] == 1 else b[ch][c - 1:c])
        en = each(lambda ch: jnp.exp(-b[ch]))
        es = each(lambda ch: jnp.exp(tot[ch] - b[ch]))
        lhs = each(lambda ch: jnp.concatenate([kk[ch] * jnp.exp(b[ch] - lw[ch]), r[ch] * jnp.exp(b[ch])], axis=0))
        gb = each(lambda ch: _mm_nt(lhs[ch], bdw(be[ch] * en[ch])))
        gk = each(lambda ch: _mm_nt(lhs[ch], bdw(kd[ch] * en[ch])))
        a_ab = each(lambda ch: jnp.where(strict[ch[1]], gb[ch][0:c], 0.0))
        a_rb = each(lambda ch: jnp.where(incl[ch[1]], gb[ch][c:], 0.0))
        a_ak = each(lambda ch: jnp.where(strict[ch[1]], gk[ch][0:c], 0.0))
        a_rk = each(lambda ch: jnp.where(incl[ch[1]], gk[ch][c:], 0.0))
        t_inv = each(lambda ch: eye - jnp.where(lvl[ch[1]][0], a_ab[ch], 0.0))
        for lv in range(1, RW_LEVELS):
            xm = each(lambda ch: _mm(jnp.where(lvl[ch[1]][lv], a_ab[ch], 0.0), bdp(t_inv[ch])))
            t_inv = each(lambda ch: t_inv[ch] - _mm(t_inv[ch], bdp(xm[ch])))
        zt_old = {(ch, g): zt_ref[ch[0], ch[1], n_sub * ch[2] + g] for ch in chains for g in range(n_sub)}
        zz = each(lambda ch: jnp.concatenate(
            [_mm_nt(lhs[ch][:, g * sw:(g + 1) * sw], zt_old[ch, g]) for g in range(n_sub)], axis=1))
        vbd = each(lambda ch: bdw(v[ch]))
        u = each(lambda ch: _mm(t_inv[ch], bdw(-(zz[ch][0:c] + _mm(a_ak[ch], vbd[ch])))))
        y = each(lambda ch: zz[ch][c:] + _mm(a_rb[ch], bdw(u[ch])) + _mm(a_rk[ch], vbd[ch]))
        uv = each(lambda ch: jnp.concatenate([u[ch], v[ch]], axis=0))
        bk = each(lambda ch: jnp.concatenate([be[ch] * es[ch], kd[ch] * es[ch]], axis=0))
        etot = each(lambda ch: jnp.exp(tot[ch]))
        zt_new = {}
        for ch in chains:
            for g in range(n_sub):
                gl = slice(g * sw, (g + 1) * sw)
                upd = jnp.where(bd_z, _mm_tn(uv[ch][:, gl], bk[ch][:, gl]), 0.0)
                zt_new[ch, g] = zt_old[ch, g] * etot[ch][:, gl] + upd
        for ch in chains:
            bb, d, half = ch
            outs[d][bb, rows[d], half * hw:(half + 1) * hw] = y[ch]
            for g in range(n_sub):
                zt_ref[bb, d, n_sub * half + g] = zt_new[ch, g]
        return carry

    lax.fori_loop(0, n_chunks, chunk, 0)

    @pl.when(j == n_tiles - 1)
    def _():
        for bb in range(RW_BB):
            for d in range(2):
                for g in range(RW_H // 2):
                    for jj in range(2):
                        blk = slice(RW_N * jj, RW_N * (jj + 1))
                        sfin_ref[bb, d, 2 * g + jj] = zt_ref[bb, d, g, blk, blk]


def _rw_scan(r, v, kk, lw, kd, be, s0):
    b, l, _ = r.shape
    tl, nb = RW_TL, RW_BB
    nt = l // tl
    tok_f = pl.BlockSpec((nb, tl, D), lambda i, j: (i, j, 0))
    tok_b = pl.BlockSpec((nb, tl, D), lambda i, j: (i, nt - 1 - j, 0))
    dir_f = pl.BlockSpec((1, nb, tl, D), lambda i, j: (0, i, j, 0))
    dir_b = pl.BlockSpec((1, nb, tl, D), lambda i, j: (1, i, nt - 1 - j, 0))
    st = pl.BlockSpec((nb, 2, RW_H, RW_N, RW_N), lambda i, j: (i, 0, 0, 0, 0))
    sd = jax.ShapeDtypeStruct
    return pl.pallas_call(
        functools.partial(_rw_scan_kernel, n_tiles=nt),
        grid=(b // nb, nt),
        in_specs=[tok_f, tok_f, tok_f, dir_f, dir_f, dir_f, tok_b, tok_b, tok_b, dir_b, dir_b, dir_b, st],
        out_specs=[tok_f, tok_b, st],
        out_shape=[sd((b, l, D), F32), sd((b, l, D), F32), sd((b, 2, RW_H, RW_N, RW_N), F32)],
        scratch_shapes=[pltpu.VMEM((nb, 2, RW_H // 2, RW_SW, RW_SW), F32)],
        compiler_params=_cparams(("arbitrary", "arbitrary")),
        name="rwkv_scan",
    )(r, v, kk, lw, kd, be, r, v, kk, lw, kd, be, s0)


MERGE_TM = 128
ROUTER_LANES = LANES


def _merge_kernel(orf_ref, orb_ref, rg_ref, ogf_ref, ogb_ref, gg_ref, yf_ref, yb_ref, bonus_ref, grw_ref,
                  gates_ref, x_ref, gt_ref, sh_ref, sc_ref, wbr_ref, wout_ref, rgn_ref, ggn_ref, gng_ref, gnb_ref,
                  l1g_ref, l1b_ref, wr_ref, br_ref, x1_ref, u2_ref, lg_ref):
    o_ret = orf_ref[0] + orb_ref[0]
    o_gla = ogf_ref[0] + ogb_ref[0]
    y_ret, y_gla = [], []
    for h in range(RET_H):
        hs = slice(h * RET_DV, (h + 1) * RET_DV)
        y_ret.append(_ln(o_ret[:, hs]))
        og = o_gla[:, hs]
        y_gla.append(og * lax.rsqrt(jnp.mean(og * og, axis=-1, keepdims=True) + EPS))
    z_ret = _silu(rg_ref[0]) * (jnp.concatenate(y_ret, axis=1) * rgn_ref[...])
    z_gla = _silu(gg_ref[0]) * (jnp.concatenate(y_gla, axis=1) * ggn_ref[...])
    y = yf_ref[0] + yb_ref[0]
    mu = _head_sum(y) * (1.0 / RW_N)
    yc = y - mu
    var = _head_sum(yc * yc) * (1.0 / RW_N)
    z_rw = (yc * lax.rsqrt(var + EPS) * gng_ref[...] + gnb_ref[...] + bonus_ref[0]) * grw_ref[0]
    gates = gates_ref[0]
    mixed = (gates[:, 0:D] * _mm(z_ret, wbr_ref[0]) + gates[:, D:2 * D] * _mm(z_gla, wbr_ref[1])
             + gates[:, 2 * D:] * _mm(z_rw, wbr_ref[2]))
    mix = _mm(mixed, wout_ref[...])
    x1 = _ln(ALPHA * x_ref[0] + gt_ref[0] * mix) * l1g_ref[...] + l1b_ref[...]
    x1_ref[0] = x1
    u2 = _ln(x1) * (1.0 + sc_ref[0]) + sh_ref[0]
    u2_ref[0] = u2.astype(u2_ref.dtype)
    lg_ref[0] = _mm_hi(u2, wr_ref[...]) + br_ref[...]


def _merge(orf, orb, pr, ogf, ogb, pg, yf, yb, bonus, grw, gates, x, gt1, sh2, sc2, wbr, wout, rgn, ggn, gng, gnb,
           l1g, l1b, wr, br):
    b, l, _ = x.shape
    tm = MERGE_TM
    tok = pl.BlockSpec((1, tm, D), lambda i, j: (i, j, 0))
    gate_blk = pl.BlockSpec((1, tm, D), lambda i, j: (i, j, 2))
    vec = pl.BlockSpec((1, 1, D), lambda i, j: (i, 0, 0))
    row = pl.BlockSpec((1, D), lambda i, j: (0, 0))
    in_specs = [
        tok, tok, gate_blk, tok, tok, gate_blk, tok, tok, tok, tok,
        pl.BlockSpec((1, tm, 3 * D), lambda i, j: (i, j, 0)),
        tok, vec, vec, vec,
        pl.BlockSpec((3, D, D), lambda i, j: (0, 0, 0)),
        pl.BlockSpec((D, D), lambda i, j: (0, 0)),
        row, row, row, row, row, row,
        pl.BlockSpec((D, ROUTER_LANES), lambda i, j: (0, 0)),
        pl.BlockSpec((1, ROUTER_LANES), lambda i, j: (0, 0)),
    ]
    sd = jax.ShapeDtypeStruct
    r1 = lambda a: a.reshape(1, D)
    return pl.pallas_call(
        _merge_kernel,
        grid=(b, l // tm),
        in_specs=in_specs,
        out_specs=[tok, tok, pl.BlockSpec((1, tm, ROUTER_LANES), lambda i, j: (i, j, 0))],
        out_shape=[sd((b, l, D), F32), sd((b, l, D), BF16), sd((b, l, ROUTER_LANES), F32)],
        compiler_params=_cparams(("arbitrary", "arbitrary")),
        name="merge",
    )(orf, orb, pr, ogf, ogb, pg, yf, yb, bonus, grw, gates, x, gt1, sh2, sc2, wbr, wout,
      r1(rgn), r1(ggn), r1(gng), r1(gnb), r1(l1g), r1(l1b), wr, br)


MOE_TM = 512


def _first_argmax(x, lane):
    m = jnp.max(x, axis=-1, keepdims=True)
    idx = jnp.min(jnp.where(x == m, lane, ROUTER_LANES), axis=-1, keepdims=True)
    return m, idx


def _combine_weights(logits):
    lane = lax.broadcasted_iota(jnp.int32, logits.shape, 1)
    neg = -jnp.inf
    gl = jnp.where(lane < N_GROUPS, logits, neg)
    gmax, gidx = _first_argmax(gl, lane)
    g_w = 1.0 / jnp.sum(jnp.exp(gl - gmax), axis=-1, keepdims=True)
    lo = N_GROUPS + EPG * gidx
    el = jnp.where((lane >= lo) & (lane < lo + EPG), logits, neg)
    m1, i1 = _first_argmax(el, lane)
    m2, i2 = _first_argmax(jnp.where(lane == i1, neg, el), lane)
    e2 = jnp.exp(m2 - m1)
    w1 = 1.0 / (1.0 + e2)
    return g_w * (jnp.where(lane == i1, w1, 0.0) + jnp.where(lane == i2, e2 * w1, 0.0))


def _moe_kernel(u_ref, x_ref, lg_ref, gt_ref, wg_ref, wu_ref, wd_ref, l2g_ref, l2b_ref, o_ref, acc_ref, comb_ref):
    e = pl.program_id(2)

    @pl.when(e == 0)
    def _():
        acc_ref[...] = jnp.zeros_like(acc_ref)
        comb_ref[...] = _combine_weights(lg_ref[0])

    u = u_ref[0]
    hid = _silu(jnp.dot(u, wg_ref[0], preferred_element_type=F32)) * jnp.dot(u, wu_ref[0], preferred_element_type=F32)
    lane = lax.broadcasted_iota(jnp.int32, comb_ref.shape, 1)
    c_e = jnp.sum(jnp.where(lane == e + N_GROUPS, comb_ref[...], 0.0), axis=-1, keepdims=True)
    acc_ref[...] += c_e * _mm(hid, wd_ref[0])

    @pl.when(e == N_EXPERTS - 1)
    def _():
        o_ref[0] = _ln(ALPHA * x_ref[0] + gt_ref[0] * acc_ref[...]) * l2g_ref[...] + l2b_ref[...]


def _moe(u2, x1, logits, gt2, wg, wu, wd, l2g, l2b):
    b, l, _ = x1.shape
    tm = MOE_TM if l % MOE_TM == 0 else 256
    tok = pl.BlockSpec((1, tm, D), lambda i, j, e: (i, j, 0))
    row = pl.BlockSpec((1, D), lambda i, j, e: (0, 0))
    in_specs = [
        tok, tok,
        pl.BlockSpec((1, tm, ROUTER_LANES), lambda i, j, e: (i, j, 0)),
        pl.BlockSpec((1, 1, D), lambda i, j, e: (i, 0, 0)),
        pl.BlockSpec((1, D, EXPERT_FF), lambda i, j, e: (e, 0, 0)),
        pl.BlockSpec((1, D, EXPERT_FF), lambda i, j, e: (e, 0, 0)),
        pl.BlockSpec((1, EXPERT_FF, D), lambda i, j, e: (e, 0, 0)),
        row, row,
    ]
    return pl.pallas_call(
        _moe_kernel,
        grid=(b, l // tm, N_EXPERTS),
        in_specs=in_specs,
        out_specs=tok,
        out_shape=jax.ShapeDtypeStruct((b, l, D), F32),
        scratch_shapes=[pltpu.VMEM((tm, D), F32), pltpu.VMEM((tm, ROUTER_LANES), F32)],
        compiler_params=_cparams(("arbitrary",) * 3),
        name="moe",
    )(u2, x1, logits, gt2, wg, wu, wd, l2g.reshape(1, D), l2b.reshape(1, D))


def _rope_tables(l):
    t = np.arange(l)
    quarter = RET_DK // 4
    freqs = (np.float32(ROPE_BASE) ** (-np.arange(quarter, dtype=np.float32) / quarter)).astype(np.float32)
    rows = (t // GRID_W).astype(np.float32)
    cols = (t % GRID_W).astype(np.float32)
    ang = jnp.asarray(np.concatenate([rows[:, None] * freqs, cols[:, None] * freqs], -1))
    cos, sin = jnp.cos(ang), jnp.sin(ang)
    return jnp.concatenate([cos, cos], -1), jnp.concatenate([-sin, sin], -1)


def _layer_weights(p):
    n_rg = 2 * (RET_H * RET_DK + RET_H * RET_DV)
    w_in = p["w_in"]
    w_lr = jnp.pad(w_in[:, 2 * n_rg:2 * n_rg + GLA_LOWRANK], ((0, 0), (0, LANES - GLA_LOWRANK)))
    w2p, b2p = _gla_gate_params(p["gla_w2"], p["gla_b"])
    pad = ROUTER_LANES - N_GROUPS - N_EXPERTS
    return dict(
        p,
        w_ret=w_in[:, :n_rg].astype(BF16),
        w_gla=jnp.concatenate([w_in[:, n_rg:2 * n_rg], w_lr], axis=1).astype(BF16),
        w_rw=w_in[:, 2 * n_rg + GLA_LOWRANK:].astype(BF16),
        w_merge=p["w_merge"].astype(BF16),
        w_br=p["w_br"].astype(BF16),
        w_out=p["w_out"].astype(BF16),
        gla_w2p=w2p, gla_b2p=b2p,
        w_router=jnp.pad(jnp.concatenate([p["w_rg"], p["w_re"]], axis=1), ((0, 0), (0, pad))),
        b_router=jnp.pad(jnp.concatenate([p["b_rg"], p["b_re"]]), (0, pad)).reshape(1, ROUTER_LANES),
        w_eg=p["w_eg"].astype(BF16), w_eu=p["w_eu"].astype(BF16), w_ed=p["w_ed"].astype(BF16),
    )


def _layer(x, mod, s_ret, s_gla_t, s_rw, rope, p):
    b, l, _ = x.shape
    sh1, sc1, gt1, sh2, sc2, gt2 = [m.reshape(b, 1, D) for m in jnp.split(mod, 6, axis=-1)]
    u = _lnmod(x, sh1, sc1)
    u2d = u.reshape(b * l, D)
    zero = lambda n: jnp.zeros((n,), F32)
    pr = _proj(u2d, p["w_ret"], zero(p["w_ret"].shape[1]), 1024).reshape(b, l, -1)
    pg = _proj(u2d, p["w_gla"], zero(p["w_gla"].shape[1]), 640).reshape(b, l, -1)
    rw = _proj(u2d, p["w_rw"], zero(N_RW_COLS), 1664).reshape(b, l, -1)
    gates = _proj(u2d, p["w_merge"], p["b_merge"], 1024, act="sigmoid").reshape(b, l, -1)

    cos2, sin2 = rope if rope is not None else (jnp.ones((l, RET_DK), F32), jnp.zeros((l, RET_DK), F32))
    orf, orb, ret_fin = _retention(pr, cos2, sin2, s_ret, rope is not None)
    ogf, ogb, gla_fin_t = _gla(pg, p["gla_w2p"], p["gla_b2p"], s_gla_t)
    r, v, kk, lw, kd, be, bonus, grw = _rw_prep(rw, p["rwkv_shift"], p["rwkv_kk"], p["rwkv_ka"],
                                                p["rwkv_rk"].reshape(-1), p["rwkv_w0"], p["rwkv_wb"],
                                                p["rwkv_a0"], p["rwkv_ab"], p["rwkv_gb"])
    yf, yb, rw_fin = _rw_scan(r, v, kk, lw, kd, be, s_rw)
    x1, u2, logits = _merge(orf, orb, pr, ogf, ogb, pg, yf, yb, bonus, grw, gates, x, gt1, sh2, sc2,
                            p["w_br"], p["w_out"], p["ret_gn"], p["gla_gn"], p["rwkv_gn_g"], p["rwkv_gn_b"],
                            p["ln1_g"], p["ln1_b"], p["w_router"], p["b_router"])
    x2 = _moe(u2, x1, logits, gt2, p["w_eg"], p["w_eu"], p["w_ed"], p["ln2_g"], p["ln2_b"])
    return x2, (ret_fin, gla_fin_t, rw_fin)


_PARAM_NAMES = ("w_in", "rwkv_shift", "ret_gn", "gla_w2", "gla_b", "gla_gn", "rwkv_w0", "rwkv_wb", "rwkv_a0",
                "rwkv_ab", "rwkv_gb", "rwkv_kk", "rwkv_ka", "rwkv_rk", "rwkv_gn_g", "rwkv_gn_b", "w_br", "w_merge",
                "b_merge", "w_out", "ln1_g", "ln1_b", "ln2_g", "ln2_b", "w_rg", "b_rg", "w_re", "b_re", "w_eg",
                "w_eu", "w_ed")


def kernel(x_prompt, x_sample, state_ret, state_gla, state_rwkv, c, c_ctx, w_ada, b_ada, w_in, rwkv_shift, ret_gn,
           gla_w2, gla_b, gla_gn, rwkv_w0, rwkv_wb, rwkv_a0, rwkv_ab, rwkv_gb, rwkv_kk, rwkv_ka, rwkv_rk, rwkv_gn_g,
           rwkv_gn_b, w_br, w_merge, b_merge, w_out, ln1_g, ln1_b, ln2_g, ln2_b, w_rg, b_rg, w_re, b_re, w_eg, w_eu,
           w_ed):
    params = dict(zip(_PARAM_NAMES, (w_in, rwkv_shift, ret_gn, gla_w2, gla_b, gla_gn, rwkv_w0, rwkv_wb, rwkv_a0,
                                     rwkv_ab, rwkv_gb, rwkv_kk, rwkv_ka, rwkv_rk, rwkv_gn_g, rwkv_gn_b, w_br, w_merge,
                                     b_merge, w_out, ln1_g, ln1_b, ln2_g, ln2_b, w_rg, b_rg, w_re, b_re, w_eg, w_eu,
                                     w_ed)))
    bc, bl = x_prompt.shape[0], x_sample.shape[0]
    rope = _rope_tables(x_sample.shape[1])
    mod_rows = 2 * SUBLANES
    c_all = jnp.concatenate([c, c_ctx[None, :], jnp.zeros((mod_rows - bl - 1, D), F32)], axis=0)
    z_ret = jnp.zeros((bc, 2, RET_H, RET_DK, RET_DV), F32)
    z_gla_t = jnp.zeros((bc, 2, GLA_H, GLA_DV, GLA_DK), F32)
    z_rw = jnp.zeros((bc, 2, RW_H, RW_N, RW_N), F32)
    h_ctx, h_lat = x_prompt, x_sample
    new_ret, new_gla, new_rw = [], [], []
    for layer in range(DEPTH):
        p = _layer_weights({k: v[layer] for k, v in params.items()})
        mod = _modulation(c_all, w_ada[layer], b_ada[layer])
        mod_ctx = jnp.broadcast_to(mod[bl:bl + 1], (bc, 6 * D))
        h_ctx, (s_ret, s_gla_t, s_rw) = _layer(h_ctx, mod_ctx, z_ret, z_gla_t, z_rw, None, p)
        new_ret.append(s_ret)
        new_gla.append(jnp.swapaxes(s_gla_t, -1, -2))
        new_rw.append(s_rw)
        h_lat, _ = _layer(h_lat, mod[:bl], state_ret[:, layer], jnp.swapaxes(state_gla[:, layer], -1, -2),
                          state_rwkv[:, layer], rope, p)
    return (h_ctx, h_lat, jnp.stack(new_ret, axis=1), jnp.stack(new_gla, axis=1), jnp.stack(new_rw, axis=1))
```

```python
import functools

import jax
import jax.numpy as jnp
import numpy as np
from jax import lax
from jax.experimental import pallas as pl
from jax.experimental.pallas import tpu as pltpu

F32 = jnp.float32
BF16 = jnp.bfloat16
HI = lax.Precision.HIGHEST

D = 1024
DEPTH = 2
GRID_W = 64
RET_H, RET_DK, RET_DV = 4, 128, 256
GLA_H, GLA_DK, GLA_DV = 4, 128, 256
GLA_LOWRANK = 16
GLA_GATE_NORM = 16.0
RW_H, RW_N = 16, 64
RW_W_LORA, RW_A_LORA, RW_G_LORA = 64, 64, 128
N_RW_COLS = 3 * D + RW_W_LORA + RW_A_LORA + RW_G_LORA
N_GROUPS, EPG, N_EXPERTS, EXPERT_FF = 4, 4, 16, 512
ALPHA = (2 * DEPTH) ** 0.25
EPS = 1e-5
ROPE_BASE = 10000.0

LANES = 128
SUBLANES = 8
VMEM_LIMIT = 56 * 1024 * 1024

RET_C = 128
GLA_C = 64
GLA_LEVELS = 6
RW_C = 32
RW_LEVELS = 5
RW_TL = 256
RW_BB = 2
RW_G = 4
RW_GW = RW_G * RW_N
RW_SW = 2 * RW_N


def _cparams(sem):
    return pltpu.CompilerParams(dimension_semantics=sem, vmem_limit_bytes=VMEM_LIMIT)


def _mm(a, b):
    return jnp.dot(a.astype(BF16), b.astype(BF16), preferred_element_type=F32)


def _mm_nt(a, b):
    return lax.dot_general(a.astype(BF16), b.astype(BF16), (((1,), (1,)), ((), ())), preferred_element_type=F32)


def _mm_tn(a, b):
    return lax.dot_general(a.astype(BF16), b.astype(BF16), (((0,), (0,)), ((), ())), preferred_element_type=F32)


def _mm_hi(a, b):
    return jnp.dot(a, b, preferred_element_type=F32, precision=HI)


def _sigmoid(x):
    return 1.0 / (1.0 + jnp.exp(-x))


def _silu(x):
    return x * _sigmoid(x)


def _log_sigmoid(x):
    return jnp.minimum(x, 0.0) - jnp.log(1.0 + jnp.exp(-jnp.abs(x)))


def _softplus(x):
    return jnp.maximum(x, 0.0) + jnp.log(1.0 + jnp.exp(-jnp.abs(x)))


def _ln(x):
    mu = jnp.mean(x, axis=-1, keepdims=True)
    xc = x - mu
    var = jnp.mean(xc * xc, axis=-1, keepdims=True)
    return xc * lax.rsqrt(var + EPS)


def _scan_rows(x, reverse):
    n = x.shape[0]
    row = lax.broadcasted_iota(jnp.int32, x.shape, 0)
    sh = 1
    while sh < n:
        if reverse:
            x = x + jnp.where(row < n - sh, pltpu.roll(x, n - sh, 0), 0.0)
        else:
            x = x + jnp.where(row >= sh, pltpu.roll(x, sh, 0), 0.0)
        sh *= 2
    return x


def _mod_kernel(c_ref, w_ref, b_ref, o_ref):
    o_ref[...] = _mm_hi(_silu(c_ref[...]), w_ref[...]) + b_ref[...]


def _modulation(c_all, w, b):
    m, n, tn = c_all.shape[0], w.shape[1], 512
    return pl.pallas_call(
        _mod_kernel,
        grid=(n // tn,),
        in_specs=[pl.BlockSpec((m, D), lambda j: (0, 0)),
                  pl.BlockSpec((D, tn), lambda j: (0, j)),
                  pl.BlockSpec((1, tn), lambda j: (0, j))],
        out_specs=pl.BlockSpec((m, tn), lambda j: (0, j)),
        out_shape=jax.ShapeDtypeStruct((m, n), F32),
        compiler_params=_cparams(("arbitrary",)),
        name="adaln_mod",
    )(c_all, w, b.reshape(1, n))


def _lnmod_kernel(x_ref, sh_ref, sc_ref, u_ref):
    u_ref[0] = (_ln(x_ref[0]) * (1.0 + sc_ref[0]) + sh_ref[0]).astype(u_ref.dtype)


def _lnmod(x, sh, sc):
    b, l, _ = x.shape
    tm = 512 if l % 512 == 0 else 256
    vec = pl.BlockSpec((1, 1, D), lambda i, j: (i, 0, 0))
    return pl.pallas_call(
        _lnmod_kernel,
        grid=(b, l // tm),
        in_specs=[pl.BlockSpec((1, tm, D), lambda i, j: (i, j, 0)), vec, vec],
        out_specs=pl.BlockSpec((1, tm, D), lambda i, j: (i, j, 0)),
        out_shape=jax.ShapeDtypeStruct((b, l, D), BF16),
        compiler_params=_cparams(("arbitrary", "arbitrary")),
        name="ln_mod",
    )(x, sh, sc)


def _proj_kernel(u_ref, w_ref, b_ref, o_ref, *, act):
    y = jnp.dot(u_ref[...], w_ref[...], preferred_element_type=F32) + b_ref[...]
    if act == "sigmoid":
        y = _sigmoid(y)
    o_ref[...] = y.astype(o_ref.dtype)


def _proj(u2d, w, bias, tn, act=None, out_dtype=BF16):
    t, n = u2d.shape[0], w.shape[1]
    tm = min(t, 2048)
    return pl.pallas_call(
        functools.partial(_proj_kernel, act=act),
        grid=(t // tm, n // tn),
        in_specs=[pl.BlockSpec((tm, D), lambda i, j: (i, 0)),
                  pl.BlockSpec((D, tn), lambda i, j: (0, j)),
                  pl.BlockSpec((1, tn), lambda i, j: (0, j))],
        out_specs=pl.BlockSpec((tm, tn), lambda i, j: (i, j)),
        out_shape=jax.ShapeDtypeStruct((t, n), out_dtype),
        compiler_params=_cparams(("arbitrary", "arbitrary")),
        name="proj",
    )(u2d, w, bias.reshape(1, n))


def _ret_log_gamma(d, h):
    hh = h if d == 0 else RET_H - 1 - h
    return float(np.log(1.0 - 2.0 ** (-5.0 - hh)))


def _ret_tables():
    c = RET_C
    i = np.arange(c, dtype=np.float64)
    dec = np.zeros((2 * RET_H, c, c), np.float32)
    qd = np.zeros((2 * RET_H, c, RET_DK), np.float32)
    kd = np.zeros((2 * RET_H, c, RET_DK), np.float32)
    for d in range(2):
        tau = i if d == 0 else c - 1 - i
        rel = tau[:, None] - tau[None, :]
        for h in range(RET_H):
            lg = _ret_log_gamma(d, h)
            dec[d * RET_H + h] = np.where(rel >= 0, np.exp(np.maximum(rel, 0.0) * lg), 0.0)
            qd[d * RET_H + h] = np.exp((tau + 1.0) * lg)[:, None]
            kd[d * RET_H + h] = np.exp((c - 1.0 - tau) * lg)[:, None]
    return jnp.asarray(dec), jnp.asarray(qd), jnp.asarray(kd)


def _rope(x, cos2, sin2):
    return x * cos2 + pltpu.roll(x, RET_DK // 2, 1) * sin2


def _ret_kernel(qf_ref, kf_ref, vf_ref, qb_ref, kb_ref, vb_ref, cosf_ref, sinf_ref, cosb_ref, sinb_ref,
                dec_ref, qd_ref, kd_ref, s0_ref, of_ref, ob_ref, sfin_ref, s_ref, *, use_rope, n_chunks):
    n = pl.program_id(1)

    @pl.when(n == 0)
    def _():
        s_ref[...] = s0_ref[0]

    ins = ((qf_ref, kf_ref, vf_ref, cosf_ref, sinf_ref), (qb_ref, kb_ref, vb_ref, cosb_ref, sinb_ref))
    outs = (of_ref, ob_ref)
    chains = [(d, h) for d in range(2) for h in range(RET_H)]
    q, k, v = {}, {}, {}
    for d, h in chains:
        q_ref, k_ref, v_ref, cos_ref, sin_ref = ins[d]
        dk = slice(h * RET_DK, (h + 1) * RET_DK)
        q[d, h] = q_ref[0, :, dk].astype(F32)
        k[d, h] = k_ref[0, :, dk].astype(F32) * (RET_DK ** -0.5)
        if use_rope:
            q[d, h] = _rope(q[d, h], cos_ref[...], sin_ref[...])
            k[d, h] = _rope(k[d, h], cos_ref[...], sin_ref[...])
        v[d, h] = v_ref[0, :, h * RET_DV:(h + 1) * RET_DV]
    sc = {ch: _mm_nt(q[ch], k[ch]) * dec_ref[ch[0] * RET_H + ch[1]] for ch in chains}
    s_old = {ch: s_ref[ch[0], ch[1]] for ch in chains}
    o = {ch: _mm(sc[ch], v[ch]) + _mm(q[ch] * qd_ref[ch[0] * RET_H + ch[1]], s_old[ch]) for ch in chains}
    s_new = {ch: s_old[ch] * float(np.exp(RET_C * _ret_log_gamma(*ch)))
             + _mm_tn(k[ch] * kd_ref[ch[0] * RET_H + ch[1]], v[ch]) for ch in chains}
    for d, h in chains:
        outs[d][0, :, h * RET_DV:(h + 1) * RET_DV] = o[d, h].astype(outs[d].dtype)
        s_ref[d, h] = s_new[d, h]

    @pl.when(n == n_chunks - 1)
    def _():
        sfin_ref[0] = s_ref[...]


def _retention(pr, cos2, sin2, s0, use_rope):
    b, l, _ = pr.shape
    c = RET_C
    nc = l // c
    dec, qd, kd = _ret_tables()
    qw, vw = RET_H * RET_DK, RET_H * RET_DV

    def fw(blk):
        return lambda i, n: (i, n, blk)

    def bw(blk):
        return lambda i, n: (i, nc - 1 - n, blk)

    def const(*shape):
        return pl.BlockSpec(shape, lambda i, n: (0,) * len(shape))

    st = pl.BlockSpec((1, 2, RET_H, RET_DK, RET_DV), lambda i, n: (i, 0, 0, 0, 0))
    in_specs = [
        pl.BlockSpec((1, c, qw), fw(0)), pl.BlockSpec((1, c, qw), fw(1)), pl.BlockSpec((1, c, vw), fw(1)),
        pl.BlockSpec((1, c, qw), bw(0)), pl.BlockSpec((1, c, qw), bw(1)), pl.BlockSpec((1, c, vw), bw(1)),
        pl.BlockSpec((c, RET_DK), lambda i, n: (n, 0)), pl.BlockSpec((c, RET_DK), lambda i, n: (n, 0)),
        pl.BlockSpec((c, RET_DK), lambda i, n: (nc - 1 - n, 0)), pl.BlockSpec((c, RET_DK), lambda i, n: (nc - 1 - n, 0)),
        const(2 * RET_H, c, c), const(2 * RET_H, c, RET_DK), const(2 * RET_H, c, RET_DK), st,
    ]
    out_specs = [pl.BlockSpec((1, c, vw), fw(0)), pl.BlockSpec((1, c, vw), bw(0)), st]
    sd = jax.ShapeDtypeStruct
    return pl.pallas_call(
        functools.partial(_ret_kernel, use_rope=use_rope, n_chunks=nc),
        grid=(b, nc),
        in_specs=in_specs,
        out_specs=out_specs,
        out_shape=[sd((b, l, vw), BF16), sd((b, l, vw), BF16), sd((b, 2, RET_H, RET_DK, RET_DV), F32)],
        scratch_shapes=[pltpu.VMEM((2, RET_H, RET_DK, RET_DV), F32)],
        compiler_params=_cparams(("arbitrary", "arbitrary")),
        name="retention",
    )(pr, pr, pr, pr, pr, pr, cos2, sin2, cos2, sin2, dec, qd, kd, s0)


def _gla_boundary(b, lv, reverse, rolls):
    c, w = b.shape
    m = 1 << lv
    if m >= SUBLANES:
        parts = []
        for p0 in range(0, c, 2 * m):
            e = p0 + m if reverse else p0 + m - 1
            parts.append(jnp.broadcast_to(b[e:e + 1, :], (2 * m, w)))
        return jnp.concatenate(parts, axis=0)

    def rolled(s):
        s %= c
        if s not in rolls:
            rolls[s] = b if s == 0 else pltpu.roll(b, s, 0)
        return rolls[s]

    row = lax.broadcasted_iota(jnp.int32, (c, w), 0)
    r = jnp.bitwise_and(row, m - 1)
    upper = jnp.bitwise_and(jnp.right_shift(row, lv), 1) == 1
    out = b
    for t in range(m):
        if reverse:
            out = jnp.where(upper & (r == t), rolled(t), out)
            out = jnp.where(jnp.logical_not(upper) & (r == t), rolled(-(m - t)), out)
        else:
            out = jnp.where(upper & (r == t), rolled(t + 1), out)
            out = jnp.where(jnp.logical_not(upper) & (r == t), rolled(-(m - 1 - t)), out)
    return out


def _gla_kernel(qf_ref, kf_ref, vf_ref, lrf_ref, qb_ref, kb_ref, vb_ref, lrb_ref, w2_ref, b2_ref, s0_ref,
                of_ref, ob_ref, sfin_ref, st_ref, *, n_chunks):
    c = GLA_C
    n = pl.program_id(1)

    @pl.when(n == 0)
    def _():
        st_ref[...] = s0_ref[0]

    ins = ((qf_ref, kf_ref, vf_ref, lrf_ref), (qb_ref, kb_ref, vb_ref, lrb_ref))
    outs = (of_ref, ob_ref)
    hw = GLA_H * GLA_DK
    row = lax.broadcasted_iota(jnp.int32, (c, hw), 0)
    ri = lax.broadcasted_iota(jnp.int32, (c, c), 0)
    ci = lax.broadcasted_iota(jnp.int32, (c, c), 1)
    q, k, v, b_inc, b_rest, tot, ql, kl = {}, {}, {}, {}, {}, {}, {}, {}
    for d in range(2):
        reverse = d == 1
        q_ref, k_ref, v_ref, lr_ref = ins[d]
        q[d] = q_ref[0].astype(F32) * (GLA_DK ** -0.5)
        k[d] = k_ref[0].astype(F32)
        v[d] = v_ref[0]
        gate = _log_sigmoid(_mm_hi(lr_ref[0], w2_ref[d]) + b2_ref[d]) * (1.0 / GLA_GATE_NORM)
        b = _scan_rows(gate, reverse)
        tot[d] = b[0:1] if reverse else b[c - 1:c]
        b_inc[d], b_rest[d] = b, tot[d] - b
        rolls = {}
        for lv in range(GLA_LEVELS):
            be = _gla_boundary(b, lv, reverse, rolls)
            upper = jnp.bitwise_and(jnp.right_shift(row, lv), 1) == 1
            second = jnp.logical_not(upper) if reverse else upper
            e = jnp.exp(jnp.where(second, b - be, be - b))
            ql[d, lv] = jnp.where(second, q[d] * e, 0.0)
            kl[d, lv] = jnp.where(second, 0.0, k[d] * e)
    chains = [(d, h) for d in range(2) for h in range(GLA_H)]

    def dk(h):
        return slice(h * GLA_DK, (h + 1) * GLA_DK)

    def dv(h):
        return slice(h * GLA_DV, (h + 1) * GLA_DV)

    attn = {(d, h): jnp.where(ri == ci, jnp.sum(q[d][:, dk(h)] * k[d][:, dk(h)], axis=-1, keepdims=True), 0.0)
            for d, h in chains}
    for lv in range(GLA_LEVELS):
        same = jnp.right_shift(ri, lv + 1) == jnp.right_shift(ci, lv + 1)
        for d, h in chains:
            attn[d, h] = attn[d, h] + jnp.where(same, _mm_nt(ql[d, lv][:, dk(h)], kl[d, lv][:, dk(h)]), 0.0)
    st_old = {ch: st_ref[ch[0], ch[1]] for ch in chains}
    o = {(d, h): _mm(attn[d, h], v[d][:, dv(h)]) + _mm_nt(q[d][:, dk(h)] * jnp.exp(b_inc[d][:, dk(h)]), st_old[d, h])
         for d, h in chains}
    st_new = {(d, h): st_old[d, h] * jnp.exp(tot[d][:, dk(h)])
              + _mm_tn(v[d][:, dv(h)], k[d][:, dk(h)] * jnp.exp(b_rest[d][:, dk(h)])) for d, h in chains}
    for d, h in chains:
        outs[d][0, :, dv(h)] = o[d, h].astype(outs[d].dtype)
        st_ref[d, h] = st_new[d, h]

    @pl.when(n == n_chunks - 1)
    def _():
        sfin_ref[0] = st_ref[...]


def _gla(pg, plr, w2p, b2, s0t):
    b, l, _ = pg.shape
    c = GLA_C
    nc = l // c
    qw, vw = GLA_H * GLA_DK, GLA_H * GLA_DV

    def fw(blk):
        return lambda i, n: (i, n, blk)

    def bw(blk):
        return lambda i, n: (i, nc - 1 - n, blk)

    st = pl.BlockSpec((1, 2, GLA_H, GLA_DV, GLA_DK), lambda i, n: (i, 0, 0, 0, 0))
    in_specs = [
        pl.BlockSpec((1, c, qw), fw(0)), pl.BlockSpec((1, c, qw), fw(1)), pl.BlockSpec((1, c, vw), fw(1)),
        pl.BlockSpec((1, c, LANES), fw(0)),
        pl.BlockSpec((1, c, qw), bw(0)), pl.BlockSpec((1, c, qw), bw(1)), pl.BlockSpec((1, c, vw), bw(1)),
        pl.BlockSpec((1, c, LANES), bw(0)),
        pl.BlockSpec((2, LANES, qw), lambda i, n: (0, 0, 0)),
        pl.BlockSpec((2, 1, qw), lambda i, n: (0, 0, 0)),
        st,
    ]
    out_specs = [pl.BlockSpec((1, c, vw), fw(0)), pl.BlockSpec((1, c, vw), bw(0)), st]
    sd = jax.ShapeDtypeStruct
    return pl.pallas_call(
        functools.partial(_gla_kernel, n_chunks=nc),
        grid=(b, nc),
        in_specs=in_specs,
        out_specs=out_specs,
        out_shape=[sd((b, l, vw), BF16), sd((b, l, vw), BF16), sd((b, 2, GLA_H, GLA_DV, GLA_DK), F32)],
        scratch_shapes=[pltpu.VMEM((2, GLA_H, GLA_DV, GLA_DK), F32)],
        compiler_params=_cparams(("arbitrary", "arbitrary")),
        name="gla",
    )(pg, pg, pg, plr, pg, pg, pg, plr, w2p, b2, s0t)


def _gla_gate_params(w2, b2):
    w2p = jnp.pad(w2, ((0, 0), (0, LANES - GLA_LOWRANK), (0, 0)))
    return w2p, b2.reshape(2, 1, GLA_H * GLA_DK)


def _split3(x):
    hi = x.astype(BF16)
    r1 = x - hi.astype(F32)
    mid = r1.astype(BF16)
    lo = (r1 - mid.astype(F32)).astype(BF16)
    return hi, mid, lo


def _head_sum(x):
    i = lax.broadcasted_iota(jnp.int32, (LANES, LANES), 0)
    j = lax.broadcasted_iota(jnp.int32, (LANES, LANES), 1)
    ones = jnp.where(jnp.right_shift(i, 6) == jnp.right_shift(j, 6), 1.0, 0.0).astype(BF16)
    out = []
    for t in range(x.shape[1] // LANES):
        parts = _split3(x[:, t * LANES:(t + 1) * LANES])
        out.append(sum(jnp.dot(p, ones, preferred_element_type=F32) for p in parts))
    return jnp.concatenate(out, axis=1)


RW_TM = 256
HALO = 2 * SUBLANES
N_RW_LORA = RW_W_LORA + RW_A_LORA + RW_G_LORA


def _rw_prep_kernel(x_ref, xp_ref, xn_ref, lo_ref, lop_ref, lon_ref, taps_ref, ltaps_ref, kkp_ref, ka_ref, rk_ref,
                    w0_ref, wb_ref, a0_ref, ab_ref, gb_ref, r_ref, v_ref, kk_ref, lw_ref, kd_ref, be_ref, bonus_ref,
                    g_ref, xbuf, lbuf, *, n_tiles):
    tm = RW_TM
    j = pl.program_id(1)
    for buf, cur, prv, nxt in ((xbuf, x_ref, xp_ref, xn_ref), (lbuf, lo_ref, lop_ref, lon_ref)):
        buf[HALO:HALO + tm, :] = cur[0].astype(F32)
        buf[0:HALO, :] = jnp.where(j == 0, 0.0, prv[0].astype(F32))
        buf[HALO + tm:2 * HALO + tm, :] = jnp.where(j == n_tiles - 1, 0.0, nxt[0].astype(F32))

    def shifted(buf, taps, c0, c1):
        return (taps[0:1, c0:c1] * buf[HALO - 1:HALO - 1 + tm, c0:c1]
                + taps[1:2, c0:c1] * buf[HALO:HALO + tm, c0:c1]
                + taps[2:3, c0:c1] * buf[HALO + 1:HALO + 1 + tm, c0:c1])

    r = shifted(xbuf, taps_ref, 0, D)
    k = shifted(xbuf, taps_ref, D, 2 * D)
    v = shifted(xbuf, taps_ref, 2 * D, 3 * D)
    lora = shifted(lbuf, ltaps_ref, 0, N_RW_LORA)
    xw = lora[:, 0:RW_W_LORA]
    xa = lora[:, RW_W_LORA:RW_W_LORA + RW_A_LORA]
    xg = lora[:, RW_W_LORA + RW_A_LORA:]
    r_ref[0] = r.astype(r_ref.dtype)
    v_ref[0] = v.astype(v_ref.dtype)
    kk = k * kkp_ref[...]
    kk = kk * lax.rsqrt(jnp.maximum(_head_sum(kk * kk), 1e-24))
    kk_ref[0] = kk.astype(kk_ref.dtype)
    g_ref[0] = _mm(_sigmoid(xg), gb_ref[...]).astype(g_ref.dtype)
    wh = jnp.tanh(xw)
    kd_sum = jnp.zeros_like(k)
    for d in range(2):
        w = -_softplus(-(w0_ref[d:d + 1, :] + _mm(wh, wb_ref[d]))) - 0.5
        lw_ref[d, 0] = -jnp.exp(w)
        a = _sigmoid(a0_ref[d:d + 1, :] + _mm(xa, ab_ref[d]))
        kd = k * (1.0 + (a - 1.0) * ka_ref[...])
        kd_ref[d, 0] = kd.astype(kd_ref.dtype)
        be_ref[d, 0] = (kk * a).astype(be_ref.dtype)
        kd_sum = kd_sum + kd
    bonus_ref[0] = (_head_sum(r * kd_sum * rk_ref[...]) * v).astype(bonus_ref.dtype)


def _rw_prep(rw, rlo, taps, kkp, ka, rk, w0, wb, a0, ab, gb):
    b, l, _ = rw.shape
    tm = RW_TM
    nt = l // tm
    hb = tm // HALO
    tok = pl.BlockSpec((1, tm, D), lambda i, j: (i, j, 0))
    tokd = pl.BlockSpec((2, 1, tm, D), lambda i, j: (0, i, j, 0))

    def full(*shape):
        return pl.BlockSpec(shape, lambda i, j: (0,) * len(shape))

    def halo_specs(n):
        return [pl.BlockSpec((1, tm, n), lambda i, j: (i, j, 0)),
                pl.BlockSpec((1, HALO, n), lambda i, j: (i, jnp.maximum(j * hb - 1, 0), 0)),
                pl.BlockSpec((1, HALO, n), lambda i, j: (i, jnp.minimum((j + 1) * hb, l // HALO - 1), 0))]

    in_specs = halo_specs(3 * D) + halo_specs(N_RW_LORA) + [
        full(3, 3 * D), full(3, N_RW_LORA), full(1, D), full(1, D), full(1, D),
        full(2, D), full(2, RW_W_LORA, D), full(2, D), full(2, RW_A_LORA, D), full(RW_G_LORA, D),
    ]
    sd = jax.ShapeDtypeStruct
    tok_o, dir_o, dir_f32 = sd((b, l, D), BF16), sd((2, b, l, D), BF16), sd((2, b, l, D), F32)
    return pl.pallas_call(
        functools.partial(_rw_prep_kernel, n_tiles=nt),
        grid=(b, nt),
        in_specs=in_specs,
        out_specs=[tok, tok, tok, tokd, tokd, tokd, tok, tok],
        out_shape=[tok_o, tok_o, tok_o, dir_f32, dir_o, dir_o, tok_o, tok_o],
        scratch_shapes=[pltpu.VMEM((tm + 2 * HALO, 3 * D), F32), pltpu.VMEM((tm + 2 * HALO, N_RW_LORA), F32)],
        compiler_params=_cparams(("arbitrary", "arbitrary")),
        name="rwkv_prep",
    )(rw, rw, rw, rlo, rlo, rlo, taps[:, :3 * D], taps[:, 3 * D:], kkp.reshape(1, D), ka.reshape(1, D),
      rk.reshape(1, D), w0, wb, a0, ab, gb)


def _tile_rows(x, n):
    return jnp.concatenate([x] * n, axis=0)


def _rw_scan_kernel(rf_ref, vf_ref, kkf_ref, lwf_ref, kdf_ref, bef_ref, rb_ref, vb_ref, kkb_ref, lwb_ref, kdb_ref,
                    beb_ref, s0_ref, yf_ref, yb_ref, sfin_ref, zt_ref, *, n_tiles):
    c, hw, sw = RW_C, RW_GW, RW_SW
    pk = RW_G * c
    n_groups, n_sub = RW_H // RW_G, RW_GW // RW_SW
    n_chunks = RW_TL // c
    j = pl.program_id(1)

    @pl.when(j == 0)
    def _():
        zt_ref[...] = jnp.zeros_like(zt_ref)
        for bb in range(RW_BB):
            for d in range(2):
                for g in range(RW_H // 2):
                    for jj in range(2):
                        blk = slice(RW_N * jj, RW_N * (jj + 1))
                        zt_ref[bb, d, g, blk, blk] = s0_ref[bb, d, 2 * g + jj]

    ins = ((rf_ref, vf_ref, kkf_ref, lwf_ref, kdf_ref, bef_ref), (rb_ref, vb_ref, kkb_ref, lwb_ref, kdb_ref, beb_ref))
    outs = (yf_ref, yb_ref)
    ri = lax.broadcasted_iota(jnp.int32, (c, pk), 0)
    cs = jnp.bitwise_and(lax.broadcasted_iota(jnp.int32, (c, pk), 1), c - 1)
    eye = (ri == cs).astype(F32)
    strict, incl, lvl = [], [], []
    for d in range(2):
        ti = ri if d == 0 else c - 1 - ri
        ts = cs if d == 0 else c - 1 - cs
        strict.append(ts < ti)
        incl.append(ts <= ti)
        lv_masks = []
        for lv in range(RW_LEVELS):
            same = jnp.right_shift(ti, lv + 1) == jnp.right_shift(ts, lv + 1)
            lower = ((jnp.bitwise_and(jnp.right_shift(ti, lv), 1) == 1)
                     & (jnp.bitwise_and(jnp.right_shift(ts, lv), 1) == 0))
            lv_masks.append(same & lower)
        lvl.append(lv_masks)
    bi = lax.broadcasted_iota(jnp.int32, (pk, pk), 0)
    bj = lax.broadcasted_iota(jnp.int32, (pk, pk), 1)
    bd_p = jnp.right_shift(bi, 5) == jnp.right_shift(bj, 5)
    zi = lax.broadcasted_iota(jnp.int32, (sw, sw), 0)
    zj = lax.broadcasted_iota(jnp.int32, (sw, sw), 1)
    bd_z = jnp.right_shift(zi, 6) == jnp.right_shift(zj, 6)
    wi = lax.broadcasted_iota(jnp.int32, (pk, hw), 0)
    wj = lax.broadcasted_iota(jnp.int32, (pk, hw), 1)
    bd_w = jnp.right_shift(wi, 5) == jnp.right_shift(wj, 6)

    def bdp(x):
        return jnp.where(bd_p, _tile_rows(x, RW_G), 0.0)

    def bdw(x):
        return jnp.where(bd_w, _tile_rows(x, RW_G), 0.0)

    chains = [(bb, d, half) for bb in range(RW_BB) for d in range(2) for half in range(n_groups)]

    def each(fn):
        return {ch: fn(ch) for ch in chains}

    def chunk(i, carry):
        rows = (pl.ds(pl.multiple_of(i * c, c), c), pl.ds(pl.multiple_of((n_chunks - 1 - i) * c, c), c))

        def load(ch):
            bb, d, half = ch
            ln = slice(half * hw, (half + 1) * hw)
            refs = ins[d]
            return (refs[0][bb, rows[d], ln].astype(F32), refs[1][bb, rows[d], ln].astype(F32),
                    refs[2][bb, rows[d], ln].astype(F32), refs[3][0, bb, rows[d], ln],
                    refs[4][0, bb, rows[d], ln].astype(F32), refs[5][0, bb, rows[d], ln].astype(F32))

        x = each(load)
        r, v, kk = each(lambda ch: x[ch][0]), each(lambda ch: x[ch][1]), each(lambda ch: x[ch][2])
        lw, kd, be = each(lambda ch: x[ch][3]), each(lambda ch: x[ch][4]), each(lambda ch: x[ch][5])
        b = each(lambda ch: _scan_rows(lw[ch], ch[1] == 1))
        tot = each(lambda ch: b[ch][0:1] if ch[1] == 1 else b[ch][c - 1:c])
        en = each(lambda ch: jnp.exp(-b[ch]))
        es = each(lambda ch: jnp.exp(tot[ch] - b[ch]))
        lhs = each(lambda ch: jnp.concatenate([kk[ch] * jnp.exp(b[ch] - lw[ch]), r[ch] * jnp.exp(b[ch])], axis=0))
        gb = each(lambda ch: _mm_nt(lhs[ch], bdw(be[ch] * en[ch])))
        gk = each(lambda ch: _mm_nt(lhs[ch], bdw(kd[ch] * en[ch])))
        a_ab = each(lambda ch: jnp.where(strict[ch[1]], gb[ch][0:c], 0.0))
        a_rb = each(lambda ch: jnp.where(incl[ch[1]], gb[ch][c:], 0.0))
        a_ak = each(lambda ch: jnp.where(strict[ch[1]], gk[ch][0:c], 0.0))
        a_rk = each(lambda ch: jnp.where(incl[ch[1]], gk[ch][c:], 0.0))
        t_inv = each(lambda ch: eye - jnp.where(lvl[ch[1]][0], a_ab[ch], 0.0))
        for lv in range(1, RW_LEVELS):
            xm = each(lambda ch: _mm(jnp.where(lvl[ch[1]][lv], a_ab[ch], 0.0), bdp(t_inv[ch])))
            t_inv = each(lambda ch: t_inv[ch] - _mm(t_inv[ch], bdp(xm[ch])))
        zt_old = {(ch, g): zt_ref[ch[0], ch[1], n_sub * ch[2] + g] for ch in chains for g in range(n_sub)}
        zz = each(lambda ch: jnp.concatenate(
            [_mm_nt(lhs[ch][:, g * sw:(g + 1) * sw], zt_old[ch, g]) for g in range(n_sub)], axis=1))
        vbd = each(lambda ch: bdw(v[ch]))
        u = each(lambda ch: _mm(t_inv[ch], bdw(-(zz[ch][0:c] + _mm(a_ak[ch], vbd[ch])))))
        y = each(lambda ch: zz[ch][c:] + _mm(a_rb[ch], bdw(u[ch])) + _mm(a_rk[ch], vbd[ch]))
        uv = each(lambda ch: jnp.concatenate([u[ch], v[ch]], axis=0))
        bk = each(lambda ch: jnp.concatenate([be[ch] * es[ch], kd[ch] * es[ch]], axis=0))
        etot = each(lambda ch: jnp.exp(tot[ch]))
        zt_new = {}
        for ch in chains:
            for g in range(n_sub):
                gl = slice(g * sw, (g + 1) * sw)
                upd = jnp.where(bd_z, _mm_tn(uv[ch][:, gl], bk[ch][:, gl]), 0.0)
                zt_new[ch, g] = zt_old[ch, g] * etot[ch][:, gl] + upd
        for ch in chains:
            bb, d, half = ch
            outs[d][bb, rows[d], half * hw:(half + 1) * hw] = y[ch].astype(outs[d].dtype)
            for g in range(n_sub):
                zt_ref[bb, d, n_sub * half + g] = zt_new[ch, g]
        return carry

    lax.fori_loop(0, n_chunks, chunk, 0)

    @pl.when(j == n_tiles - 1)
    def _():
        for bb in range(RW_BB):
            for d in range(2):
                for g in range(RW_H // 2):
                    for jj in range(2):
                        blk = slice(RW_N * jj, RW_N * (jj + 1))
                        sfin_ref[bb, d, 2 * g + jj] = zt_ref[bb, d, g, blk, blk]


def _rw_scan(r, v, kk, lw, kd, be, s0):
    b, l, _ = r.shape
    tl, nb = RW_TL, RW_BB
    nt = l // tl
    tok_f = pl.BlockSpec((nb, tl, D), lambda i, j: (i, j, 0))
    tok_b = pl.BlockSpec((nb, tl, D), lambda i, j: (i, nt - 1 - j, 0))
    dir_f = pl.BlockSpec((1, nb, tl, D), lambda i, j: (0, i, j, 0))
    dir_b = pl.BlockSpec((1, nb, tl, D), lambda i, j: (1, i, nt - 1 - j, 0))
    st = pl.BlockSpec((nb, 2, RW_H, RW_N, RW_N), lambda i, j: (i, 0, 0, 0, 0))
    sd = jax.ShapeDtypeStruct
    return pl.pallas_call(
        functools.partial(_rw_scan_kernel, n_tiles=nt),
        grid=(b // nb, nt),
        in_specs=[tok_f, tok_f, tok_f, dir_f, dir_f, dir_f, tok_b, tok_b, tok_b, dir_b, dir_b, dir_b, st],
        out_specs=[tok_f, tok_b, st],
        out_shape=[sd((b, l, D), BF16), sd((b, l, D), BF16), sd((b, 2, RW_H, RW_N, RW_N), F32)],
        scratch_shapes=[pltpu.VMEM((nb, 2, RW_H // 2, RW_SW, RW_SW), F32)],
        compiler_params=_cparams(("arbitrary", "arbitrary")),
        name="rwkv_scan",
    )(r, v, kk, lw, kd, be, r, v, kk, lw, kd, be, s0)


MERGE_TM = 128
ROUTER_LANES = LANES


def _merge_kernel(orf_ref, orb_ref, rg_ref, ogf_ref, ogb_ref, gg_ref, yf_ref, yb_ref, bonus_ref, grw_ref,
                  gates_ref, x_ref, gt_ref, sh_ref, sc_ref, wbr_ref, wout_ref, rgn_ref, ggn_ref, gng_ref, gnb_ref,
                  l1g_ref, l1b_ref, wr_ref, br_ref, x1_ref, u2_ref, lg_ref):
    f32 = lambda ref: ref[0].astype(F32)
    o_ret = f32(orf_ref) + f32(orb_ref)
    o_gla = f32(ogf_ref) + f32(ogb_ref)
    y_ret, y_gla = [], []
    for h in range(RET_H):
        hs = slice(h * RET_DV, (h + 1) * RET_DV)
        y_ret.append(_ln(o_ret[:, hs]))
        og = o_gla[:, hs]
        y_gla.append(og * lax.rsqrt(jnp.mean(og * og, axis=-1, keepdims=True) + EPS))
    z_ret = _silu(f32(rg_ref)) * (jnp.concatenate(y_ret, axis=1) * rgn_ref[...])
    z_gla = _silu(f32(gg_ref)) * (jnp.concatenate(y_gla, axis=1) * ggn_ref[...])
    y = f32(yf_ref) + f32(yb_ref)
    mu = _head_sum(y) * (1.0 / RW_N)
    yc = y - mu
    var = _head_sum(yc * yc) * (1.0 / RW_N)
    z_rw = (yc * lax.rsqrt(var + EPS) * gng_ref[...] + gnb_ref[...] + f32(bonus_ref)) * f32(grw_ref)
    gates = f32(gates_ref)
    mixed = (gates[:, 0:D] * _mm(z_ret, wbr_ref[0]) + gates[:, D:2 * D] * _mm(z_gla, wbr_ref[1])
             + gates[:, 2 * D:] * _mm(z_rw, wbr_ref[2]))
    mix = _mm(mixed, wout_ref[...])
    x1 = _ln(ALPHA * x_ref[0] + gt_ref[0] * mix) * l1g_ref[...] + l1b_ref[...]
    x1_ref[0] = x1
    u2 = _ln(x1) * (1.0 + sc_ref[0]) + sh_ref[0]
    u2_ref[0] = u2.astype(u2_ref.dtype)
    lg_ref[0] = _mm_hi(u2, wr_ref[...]) + br_ref[...]


def _merge(orf, orb, pr, ogf, ogb, pg, yf, yb, bonus, grw, gates, x, gt1, sh2, sc2, wbr, wout, rgn, ggn, gng, gnb,
           l1g, l1b, wr, br):
    b, l, _ = x.shape
    tm = MERGE_TM
    tok = pl.BlockSpec((1, tm, D), lambda i, j: (i, j, 0))
    gate_blk = pl.BlockSpec((1, tm, D), lambda i, j: (i, j, 2))
    vec = pl.BlockSpec((1, 1, D), lambda i, j: (i, 0, 0))
    row = pl.BlockSpec((1, D), lambda i, j: (0, 0))
    in_specs = [
        tok, tok, gate_blk, tok, tok, gate_blk, tok, tok, tok, tok,
        pl.BlockSpec((1, tm, 3 * D), lambda i, j: (i, j, 0)),
        tok, vec, vec, vec,
        pl.BlockSpec((3, D, D), lambda i, j: (0, 0, 0)),
        pl.BlockSpec((D, D), lambda i, j: (0, 0)),
        row, row, row, row, row, row,
        pl.BlockSpec((D, ROUTER_LANES), lambda i, j: (0, 0)),
        pl.BlockSpec((1, ROUTER_LANES), lambda i, j: (0, 0)),
    ]
    sd = jax.ShapeDtypeStruct
    r1 = lambda a: a.reshape(1, D)
    return pl.pallas_call(
        _merge_kernel,
        grid=(b, l // tm),
        in_specs=in_specs,
        out_specs=[tok, tok, pl.BlockSpec((1, tm, ROUTER_LANES), lambda i, j: (i, j, 0))],
        out_shape=[sd((b, l, D), F32), sd((b, l, D), BF16), sd((b, l, ROUTER_LANES), F32)],
        compiler_params=_cparams(("arbitrary", "arbitrary")),
        name="merge",
    )(orf, orb, pr, ogf, ogb, pg, yf, yb, bonus, grw, gates, x, gt1, sh2, sc2, wbr, wout,
      r1(rgn), r1(ggn), r1(gng), r1(gnb), r1(l1g), r1(l1b), wr, br)


MOE_TM = 512


def _first_argmax(x, lane):
    m = jnp.max(x, axis=-1, keepdims=True)
    idx = jnp.min(jnp.where(x == m, lane, ROUTER_LANES), axis=-1, keepdims=True)
    return m, idx


def _combine_weights(logits):
    lane = lax.broadcasted_iota(jnp.int32, logits.shape, 1)
    neg = -jnp.inf
    gl = jnp.where(lane < N_GROUPS, logits, neg)
    gmax, gidx = _first_argmax(gl, lane)
    g_w = 1.0 / jnp.sum(jnp.exp(gl - gmax), axis=-1, keepdims=True)
    lo = N_GROUPS + EPG * gidx
    el = jnp.where((lane >= lo) & (lane < lo + EPG), logits, neg)
    m1, i1 = _first_argmax(el, lane)
    m2, i2 = _first_argmax(jnp.where(lane == i1, neg, el), lane)
    e2 = jnp.exp(m2 - m1)
    w1 = 1.0 / (1.0 + e2)
    return g_w * (jnp.where(lane == i1, w1, 0.0) + jnp.where(lane == i2, e2 * w1, 0.0))


def _moe_kernel(u_ref, x_ref, lg_ref, gt_ref, wg_ref, wu_ref, wd_ref, l2g_ref, l2b_ref, o_ref, acc_ref, comb_ref):
    e = pl.program_id(2)

    @pl.when(e == 0)
    def _():
        acc_ref[...] = jnp.zeros_like(acc_ref)
        comb_ref[...] = _combine_weights(lg_ref[0])

    u = u_ref[0]
    hid = _silu(jnp.dot(u, wg_ref[0], preferred_element_type=F32)) * jnp.dot(u, wu_ref[0], preferred_element_type=F32)
    lane = lax.broadcasted_iota(jnp.int32, comb_ref.shape, 1)
    c_e = jnp.sum(jnp.where(lane == e + N_GROUPS, comb_ref[...], 0.0), axis=-1, keepdims=True)
    acc_ref[...] += c_e * _mm(hid, wd_ref[0])

    @pl.when(e == N_EXPERTS - 1)
    def _():
        o_ref[0] = _ln(ALPHA * x_ref[0] + gt_ref[0] * acc_ref[...]) * l2g_ref[...] + l2b_ref[...]


def _moe(u2, x1, logits, gt2, wg, wu, wd, l2g, l2b):
    b, l, _ = x1.shape
    tm = MOE_TM if l % MOE_TM == 0 else 256
    tok = pl.BlockSpec((1, tm, D), lambda i, j, e: (i, j, 0))
    row = pl.BlockSpec((1, D), lambda i, j, e: (0, 0))
    in_specs = [
        tok, tok,
        pl.BlockSpec((1, tm, ROUTER_LANES), lambda i, j, e: (i, j, 0)),
        pl.BlockSpec((1, 1, D), lambda i, j, e: (i, 0, 0)),
        pl.BlockSpec((1, D, EXPERT_FF), lambda i, j, e: (e, 0, 0)),
        pl.BlockSpec((1, D, EXPERT_FF), lambda i, j, e: (e, 0, 0)),
        pl.BlockSpec((1, EXPERT_FF, D), lambda i, j, e: (e, 0, 0)),
        row, row,
    ]
    return pl.pallas_call(
        _moe_kernel,
        grid=(b, l // tm, N_EXPERTS),
        in_specs=in_specs,
        out_specs=tok,
        out_shape=jax.ShapeDtypeStruct((b, l, D), F32),
        scratch_shapes=[pltpu.VMEM((tm, D), F32), pltpu.VMEM((tm, ROUTER_LANES), F32)],
        compiler_params=_cparams(("arbitrary",) * 3),
        name="moe",
    )(u2, x1, logits, gt2, wg, wu, wd, l2g.reshape(1, D), l2b.reshape(1, D))


def _rope_tables(l):
    t = np.arange(l)
    quarter = RET_DK // 4
    freqs = (np.float32(ROPE_BASE) ** (-np.arange(quarter, dtype=np.float32) / quarter)).astype(np.float32)
    rows = (t // GRID_W).astype(np.float32)
    cols = (t % GRID_W).astype(np.float32)
    ang = jnp.asarray(np.concatenate([rows[:, None] * freqs, cols[:, None] * freqs], -1))
    cos, sin = jnp.cos(ang), jnp.sin(ang)
    return jnp.concatenate([cos, cos], -1), jnp.concatenate([-sin, sin], -1)


def _layer_weights(p):
    n_rg = 2 * (RET_H * RET_DK + RET_H * RET_DV)
    w_in = p["w_in"]
    w_lr = jnp.pad(w_in[:, 2 * n_rg:2 * n_rg + GLA_LOWRANK], ((0, 0), (0, LANES - GLA_LOWRANK)))
    w2p, b2p = _gla_gate_params(p["gla_w2"], p["gla_b"])
    pad = ROUTER_LANES - N_GROUPS - N_EXPERTS
    return dict(
        p,
        w_ret=w_in[:, :n_rg].astype(BF16),
        w_gla=w_in[:, n_rg:2 * n_rg].astype(BF16),
        w_glr=w_lr.astype(BF16),
        w_rw=w_in[:, 2 * n_rg + GLA_LOWRANK:2 * n_rg + GLA_LOWRANK + 3 * D].astype(BF16),
        w_rlo=w_in[:, 2 * n_rg + GLA_LOWRANK + 3 * D:].astype(BF16),
        w_merge=p["w_merge"].astype(BF16),
        w_br=p["w_br"].astype(BF16),
        w_out=p["w_out"].astype(BF16),
        gla_w2p=w2p, gla_b2p=b2p,
        w_router=jnp.pad(jnp.concatenate([p["w_rg"], p["w_re"]], axis=1), ((0, 0), (0, pad))),
        b_router=jnp.pad(jnp.concatenate([p["b_rg"], p["b_re"]]), (0, pad)).reshape(1, ROUTER_LANES),
        w_eg=p["w_eg"].astype(BF16), w_eu=p["w_eu"].astype(BF16), w_ed=p["w_ed"].astype(BF16),
    )


def _layer(x, mod, s_ret, s_gla_t, s_rw, rope, p):
    b, l, _ = x.shape
    sh1, sc1, gt1, sh2, sc2, gt2 = [m.reshape(b, 1, D) for m in jnp.split(mod, 6, axis=-1)]
    u = _lnmod(x, sh1, sc1)
    u2d = u.reshape(b * l, D)
    def proj(w, tn, bias=None, **kw):
        bias = jnp.zeros((w.shape[1],), F32) if bias is None else bias
        return _proj(u2d, w, bias, tn, **kw).reshape(b, l, -1)

    pr = proj(p["w_ret"], 1024)
    pg = proj(p["w_gla"], 1024)
    plr = proj(p["w_glr"], LANES, out_dtype=F32)
    rw = proj(p["w_rw"], 1024)
    rlo = proj(p["w_rlo"], N_RW_LORA, out_dtype=F32)
    gates = proj(p["w_merge"], 1024, bias=p["b_merge"], act="sigmoid")

    cos2, sin2 = rope if rope is not None else (jnp.ones((l, RET_DK), F32), jnp.zeros((l, RET_DK), F32))
    orf, orb, ret_fin = _retention(pr, cos2, sin2, s_ret, rope is not None)
    ogf, ogb, gla_fin_t = _gla(pg, plr, p["gla_w2p"], p["gla_b2p"], s_gla_t)
    r, v, kk, lw, kd, be, bonus, grw = _rw_prep(rw, rlo, p["rwkv_shift"], p["rwkv_kk"], p["rwkv_ka"],
                                                p["rwkv_rk"].reshape(-1), p["rwkv_w0"], p["rwkv_wb"],
                                                p["rwkv_a0"], p["rwkv_ab"], p["rwkv_gb"])
    yf, yb, rw_fin = _rw_scan(r, v, kk, lw, kd, be, s_rw)
    x1, u2, logits = _merge(orf, orb, pr, ogf, ogb, pg, yf, yb, bonus, grw, gates, x, gt1, sh2, sc2,
                            p["w_br"], p["w_out"], p["ret_gn"], p["gla_gn"], p["rwkv_gn_g"], p["rwkv_gn_b"],
                            p["ln1_g"], p["ln1_b"], p["w_router"], p["b_router"])
    x2 = _moe(u2, x1, logits, gt2, p["w_eg"], p["w_eu"], p["w_ed"], p["ln2_g"], p["ln2_b"])
    return x2, (ret_fin, gla_fin_t, rw_fin)


_PARAM_NAMES = ("w_in", "rwkv_shift", "ret_gn", "gla_w2", "gla_b", "gla_gn", "rwkv_w0", "rwkv_wb", "rwkv_a0",
                "rwkv_ab", "rwkv_gb", "rwkv_kk", "rwkv_ka", "rwkv_rk", "rwkv_gn_g", "rwkv_gn_b", "w_br", "w_merge",
                "b_merge", "w_out", "ln1_g", "ln1_b", "ln2_g", "ln2_b", "w_rg", "b_rg", "w_re", "b_re", "w_eg",
                "w_eu", "w_ed")


def kernel(x_prompt, x_sample, state_ret, state_gla, state_rwkv, c, c_ctx, w_ada, b_ada, w_in, rwkv_shift, ret_gn,
           gla_w2, gla_b, gla_gn, rwkv_w0, rwkv_wb, rwkv_a0, rwkv_ab, rwkv_gb, rwkv_kk, rwkv_ka, rwkv_rk, rwkv_gn_g,
           rwkv_gn_b, w_br, w_merge, b_merge, w_out, ln1_g, ln1_b, ln2_g, ln2_b, w_rg, b_rg, w_re, b_re, w_eg, w_eu,
           w_ed):
    params = dict(zip(_PARAM_NAMES, (w_in, rwkv_shift, ret_gn, gla_w2, gla_b, gla_gn, rwkv_w0, rwkv_wb, rwkv_a0,
                                     rwkv_ab, rwkv_gb, rwkv_kk, rwkv_ka, rwkv_rk, rwkv_gn_g, rwkv_gn_b, w_br, w_merge,
                                     b_merge, w_out, ln1_g, ln1_b, ln2_g, ln2_b, w_rg, b_rg, w_re, b_re, w_eg, w_eu,
                                     w_ed)))
    bc, bl = x_prompt.shape[0], x_sample.shape[0]
    rope = _rope_tables(x_sample.shape[1])
    mod_rows = 2 * SUBLANES
    c_all = jnp.concatenate([c, c_ctx[None, :], jnp.zeros((mod_rows - bl - 1, D), F32)], axis=0)
    z_ret = jnp.zeros((bc, 2, RET_H, RET_DK, RET_DV), F32)
    z_gla_t = jnp.zeros((bc, 2, GLA_H, GLA_DV, GLA_DK), F32)
    z_rw = jnp.zeros((bc, 2, RW_H, RW_N, RW_N), F32)
    h_ctx, h_lat = x_prompt, x_sample
    new_ret, new_gla, new_rw = [], [], []
    for layer in range(DEPTH):
        p = _layer_weights({k: v[layer] for k, v in params.items()})
        mod = _modulation(c_all, w_ada[layer], b_ada[layer])
        mod_ctx = jnp.broadcast_to(mod[bl:bl + 1], (bc, 6 * D))
        h_ctx, (s_ret, s_gla_t, s_rw) = _layer(h_ctx, mod_ctx, z_ret, z_gla_t, z_rw, None, p)
        new_ret.append(s_ret)
        new_gla.append(jnp.swapaxes(s_gla_t, -1, -2))
        new_rw.append(s_rw)
        h_lat, _ = _layer(h_lat, mod[:bl], state_ret[:, layer], jnp.swapaxes(state_gla[:, layer], -1, -2),
                          state_rwkv[:, layer], rope, p)
    return (h_ctx, h_lat, jnp.stack(new_ret, axis=1), jnp.stack(new_gla, axis=1), jnp.stack(new_rw, axis=1))
```

```python
import functools

import jax
import jax.numpy as jnp
import numpy as np
from jax import lax
from jax.experimental import pallas as pl
from jax.experimental.pallas import tpu as pltpu

F32 = jnp.float32
BF16 = jnp.bfloat16
HI = lax.Precision.HIGHEST

D = 1024
DEPTH = 2
GRID_W = 64
RET_H, RET_DK, RET_DV = 4, 128, 256
GLA_H, GLA_DK, GLA_DV = 4, 128, 256
GLA_LOWRANK = 16
GLA_GATE_NORM = 16.0
RW_H, RW_N = 16, 64
RW_W_LORA, RW_A_LORA, RW_G_LORA = 64, 64, 128
N_RW_COLS = 3 * D + RW_W_LORA + RW_A_LORA + RW_G_LORA
N_GROUPS, EPG, N_EXPERTS, EXPERT_FF = 4, 4, 16, 512
ALPHA = (2 * DEPTH) ** 0.25
EPS = 1e-5
ROPE_BASE = 10000.0

LANES = 128
SUBLANES = 8
VMEM_LIMIT = 56 * 1024 * 1024

RET_C = 128
GLA_C = 64
GLA_LEVELS = 6
RW_C = 32
RW_LEVELS = 5
RW_TL = 256
RW_BB = 2
RW_G = 4
RW_GW = RW_G * RW_N
RW_SW = 2 * RW_N


def _cparams(sem):
    return pltpu.CompilerParams(dimension_semantics=sem, vmem_limit_bytes=VMEM_LIMIT)


def _mm(a, b):
    return jnp.dot(a.astype(BF16), b.astype(BF16), preferred_element_type=F32)


def _mm_nt(a, b):
    return lax.dot_general(a.astype(BF16), b.astype(BF16), (((1,), (1,)), ((), ())), preferred_element_type=F32)


def _mm_tn(a, b):
    return lax.dot_general(a.astype(BF16), b.astype(BF16), (((0,), (0,)), ((), ())), preferred_element_type=F32)


def _mm_hi(a, b):
    return jnp.dot(a, b, preferred_element_type=F32, precision=HI)


def _sigmoid(x):
    return 1.0 / (1.0 + jnp.exp(-x))


def _silu(x):
    return x * _sigmoid(x)


def _log_sigmoid(x):
    return jnp.minimum(x, 0.0) - jnp.log(1.0 + jnp.exp(-jnp.abs(x)))


def _softplus(x):
    return jnp.maximum(x, 0.0) + jnp.log(1.0 + jnp.exp(-jnp.abs(x)))


def _ln(x):
    mu = jnp.mean(x, axis=-1, keepdims=True)
    xc = x - mu
    var = jnp.mean(xc * xc, axis=-1, keepdims=True)
    return xc * lax.rsqrt(var + EPS)


def _scan_rows(x, reverse):
    n = x.shape[0]
    row = lax.broadcasted_iota(jnp.int32, x.shape, 0)
    sh = 1
    while sh < n:
        if reverse:
            x = x + jnp.where(row < n - sh, pltpu.roll(x, n - sh, 0), 0.0)
        else:
            x = x + jnp.where(row >= sh, pltpu.roll(x, sh, 0), 0.0)
        sh *= 2
    return x


def _mod_kernel(c_ref, w_ref, b_ref, o_ref):
    o_ref[...] = _mm_hi(_silu(c_ref[...]), w_ref[...]) + b_ref[...]


def _modulation(c_all, w, b):
    m, n, tn = c_all.shape[0], w.shape[1], 512
    return pl.pallas_call(
        _mod_kernel,
        grid=(n // tn,),
        in_specs=[pl.BlockSpec((m, D), lambda j: (0, 0)),
                  pl.BlockSpec((D, tn), lambda j: (0, j)),
                  pl.BlockSpec((1, tn), lambda j: (0, j))],
        out_specs=pl.BlockSpec((m, tn), lambda j: (0, j)),
        out_shape=jax.ShapeDtypeStruct((m, n), F32),
        compiler_params=_cparams(("arbitrary",)),
        name="adaln_mod",
    )(c_all, w, b.reshape(1, n))


def _lnmod_kernel(x_ref, sh_ref, sc_ref, u_ref):
    u_ref[0] = (_ln(x_ref[0]) * (1.0 + sc_ref[0]) + sh_ref[0]).astype(u_ref.dtype)


def _lnmod(x, sh, sc):
    b, l, _ = x.shape
    tm = 512 if l % 512 == 0 else 256
    vec = pl.BlockSpec((1, 1, D), lambda i, j: (i, 0, 0))
    return pl.pallas_call(
        _lnmod_kernel,
        grid=(b, l // tm),
        in_specs=[pl.BlockSpec((1, tm, D), lambda i, j: (i, j, 0)), vec, vec],
        out_specs=pl.BlockSpec((1, tm, D), lambda i, j: (i, j, 0)),
        out_shape=jax.ShapeDtypeStruct((b, l, D), BF16),
        compiler_params=_cparams(("arbitrary", "arbitrary")),
        name="ln_mod",
    )(x, sh, sc)


def _proj_kernel(u_ref, w_ref, b_ref, o_ref, *, act):
    y = jnp.dot(u_ref[...], w_ref[...], preferred_element_type=F32) + b_ref[...]
    if act == "sigmoid":
        y = _sigmoid(y)
    o_ref[...] = y.astype(o_ref.dtype)


def _proj(u2d, w, bias, tn, act=None, out_dtype=BF16):
    t, n = u2d.shape[0], w.shape[1]
    tm = min(t, 2048)
    return pl.pallas_call(
        functools.partial(_proj_kernel, act=act),
        grid=(t // tm, n // tn),
        in_specs=[pl.BlockSpec((tm, D), lambda i, j: (i, 0)),
                  pl.BlockSpec((D, tn), lambda i, j: (0, j)),
                  pl.BlockSpec((1, tn), lambda i, j: (0, j))],
        out_specs=pl.BlockSpec((tm, tn), lambda i, j: (i, j)),
        out_shape=jax.ShapeDtypeStruct((t, n), out_dtype),
        compiler_params=_cparams(("arbitrary", "arbitrary")),
        name="proj",
    )(u2d, w, bias.reshape(1, n))


def _ret_log_gamma(d, h):
    hh = h if d == 0 else RET_H - 1 - h
    return float(np.log(1.0 - 2.0 ** (-5.0 - hh)))


def _ret_tables():
    c = RET_C
    i = np.arange(c, dtype=np.float64)
    dec = np.zeros((2 * RET_H, c, c), np.float32)
    qd = np.zeros((2 * RET_H, c, RET_DK), np.float32)
    kd = np.zeros((2 * RET_H, c, RET_DK), np.float32)
    for d in range(2):
        tau = i if d == 0 else c - 1 - i
        rel = tau[:, None] - tau[None, :]
        for h in range(RET_H):
            lg = _ret_log_gamma(d, h)
            dec[d * RET_H + h] = np.where(rel >= 0, np.exp(np.maximum(rel, 0.0) * lg), 0.0)
            qd[d * RET_H + h] = np.exp((tau + 1.0) * lg)[:, None]
            kd[d * RET_H + h] = np.exp((c - 1.0 - tau) * lg)[:, None]
    return jnp.asarray(dec), jnp.asarray(qd), jnp.asarray(kd)


def _rope(x, cos2, sin2):
    return x * cos2 + pltpu.roll(x, RET_DK // 2, 1) * sin2


def _ret_kernel(qf_ref, kf_ref, vf_ref, qb_ref, kb_ref, vb_ref, cosf_ref, sinf_ref, cosb_ref, sinb_ref,
                dec_ref, qd_ref, kd_ref, s0_ref, of_ref, ob_ref, sfin_ref, s_ref, *, use_rope, n_chunks):
    n = pl.program_id(1)

    @pl.when(n == 0)
    def _():
        s_ref[...] = s0_ref[0]

    ins = ((qf_ref, kf_ref, vf_ref, cosf_ref, sinf_ref), (qb_ref, kb_ref, vb_ref, cosb_ref, sinb_ref))
    outs = (of_ref, ob_ref)
    chains = [(d, h) for d in range(2) for h in range(RET_H)]
    q, k, v = {}, {}, {}
    for d, h in chains:
        q_ref, k_ref, v_ref, cos_ref, sin_ref = ins[d]
        dk = slice(h * RET_DK, (h + 1) * RET_DK)
        q[d, h] = q_ref[0, :, dk].astype(F32)
        k[d, h] = k_ref[0, :, dk].astype(F32) * (RET_DK ** -0.5)
        if use_rope:
            q[d, h] = _rope(q[d, h], cos_ref[...], sin_ref[...])
            k[d, h] = _rope(k[d, h], cos_ref[...], sin_ref[...])
        v[d, h] = v_ref[0, :, h * RET_DV:(h + 1) * RET_DV]
    sc = {ch: _mm_nt(q[ch], k[ch]) * dec_ref[ch[0] * RET_H + ch[1]] for ch in chains}
    s_old = {ch: s_ref[ch[0], ch[1]] for ch in chains}
    o = {ch: _mm(sc[ch], v[ch]) + _mm(q[ch] * qd_ref[ch[0] * RET_H + ch[1]], s_old[ch]) for ch in chains}
    s_new = {ch: s_old[ch] * float(np.exp(RET_C * _ret_log_gamma(*ch)))
             + _mm_tn(k[ch] * kd_ref[ch[0] * RET_H + ch[1]], v[ch]) for ch in chains}
    for d, h in chains:
        outs[d][0, :, h * RET_DV:(h + 1) * RET_DV] = o[d, h].astype(outs[d].dtype)
        s_ref[d, h] = s_new[d, h]

    @pl.when(n == n_chunks - 1)
    def _():
        sfin_ref[0] = s_ref[...]


def _retention(pr, cos2, sin2, s0, use_rope):
    b, l, _ = pr.shape
    c = RET_C
    nc = l // c
    dec, qd, kd = _ret_tables()
    qw, vw = RET_H * RET_DK, RET_H * RET_DV

    def fw(blk):
        return lambda i, n: (i, n, blk)

    def bw(blk):
        return lambda i, n: (i, nc - 1 - n, blk)

    def const(*shape):
        return pl.BlockSpec(shape, lambda i, n: (0,) * len(shape))

    st = pl.BlockSpec((1, 2, RET_H, RET_DK, RET_DV), lambda i, n: (i, 0, 0, 0, 0))
    in_specs = [
        pl.BlockSpec((1, c, qw), fw(0)), pl.BlockSpec((1, c, qw), fw(1)), pl.BlockSpec((1, c, vw), fw(1)),
        pl.BlockSpec((1, c, qw), bw(0)), pl.BlockSpec((1, c, qw), bw(1)), pl.BlockSpec((1, c, vw), bw(1)),
        pl.BlockSpec((c, RET_DK), lambda i, n: (n, 0)), pl.BlockSpec((c, RET_DK), lambda i, n: (n, 0)),
        pl.BlockSpec((c, RET_DK), lambda i, n: (nc - 1 - n, 0)), pl.BlockSpec((c, RET_DK), lambda i, n: (nc - 1 - n, 0)),
        const(2 * RET_H, c, c), const(2 * RET_H, c, RET_DK), const(2 * RET_H, c, RET_DK), st,
    ]
    out_specs = [pl.BlockSpec((1, c, vw), fw(0)), pl.BlockSpec((1, c, vw), bw(0)), st]
    sd = jax.ShapeDtypeStruct
    return pl.pallas_call(
        functools.partial(_ret_kernel, use_rope=use_rope, n_chunks=nc),
        grid=(b, nc),
        in_specs=in_specs,
        out_specs=out_specs,
        out_shape=[sd((b, l, vw), BF16), sd((b, l, vw), BF16), sd((b, 2, RET_H, RET_DK, RET_DV), F32)],
        scratch_shapes=[pltpu.VMEM((2, RET_H, RET_DK, RET_DV), F32)],
        compiler_params=_cparams(("arbitrary", "arbitrary")),
        name="retention",
    )(pr, pr, pr, pr, pr, pr, cos2, sin2, cos2, sin2, dec, qd, kd, s0)


def _gla_boundary(b, lv, reverse, rolls):
    c, w = b.shape
    m = 1 << lv
    if m >= SUBLANES:
        parts = []
        for p0 in range(0, c, 2 * m):
            e = p0 + m if reverse else p0 + m - 1
            parts.append(jnp.broadcast_to(b[e:e + 1, :], (2 * m, w)))
        return jnp.concatenate(parts, axis=0)

    def rolled(s):
        s %= c
        if s not in rolls:
            rolls[s] = b if s == 0 else pltpu.roll(b, s, 0)
        return rolls[s]

    row = lax.broadcasted_iota(jnp.int32, (c, w), 0)
    r = jnp.bitwise_and(row, m - 1)
    upper = jnp.bitwise_and(jnp.right_shift(row, lv), 1) == 1
    out = b
    for t in range(m):
        if reverse:
            out = jnp.where(upper & (r == t), rolled(t), out)
            out = jnp.where(jnp.logical_not(upper) & (r == t), rolled(-(m - t)), out)
        else:
            out = jnp.where(upper & (r == t), rolled(t + 1), out)
            out = jnp.where(jnp.logical_not(upper) & (r == t), rolled(-(m - 1 - t)), out)
    return out


def _gla_kernel(qf_ref, kf_ref, vf_ref, lrf_ref, qb_ref, kb_ref, vb_ref, lrb_ref, w2_ref, b2_ref, s0_ref,
                of_ref, ob_ref, sfin_ref, st_ref, *, n_chunks):
    c = GLA_C
    n = pl.program_id(1)

    @pl.when(n == 0)
    def _():
        st_ref[...] = s0_ref[0]

    ins = ((qf_ref, kf_ref, vf_ref, lrf_ref), (qb_ref, kb_ref, vb_ref, lrb_ref))
    outs = (of_ref, ob_ref)
    hw = GLA_H * GLA_DK
    row = lax.broadcasted_iota(jnp.int32, (c, hw), 0)
    ri = lax.broadcasted_iota(jnp.int32, (c, c), 0)
    ci = lax.broadcasted_iota(jnp.int32, (c, c), 1)
    q, k, v, b_inc, b_rest, tot, ql, kl = {}, {}, {}, {}, {}, {}, {}, {}
    for d in range(2):
        reverse = d == 1
        q_ref, k_ref, v_ref, lr_ref = ins[d]
        q[d] = q_ref[0].astype(F32) * (GLA_DK ** -0.5)
        k[d] = k_ref[0].astype(F32)
        v[d] = v_ref[0]
        gate = _log_sigmoid(_mm_hi(lr_ref[0], w2_ref[d]) + b2_ref[d]) * (1.0 / GLA_GATE_NORM)
        b = _scan_rows(gate, reverse)
        tot[d] = b[0:1] if reverse else b[c - 1:c]
        b_inc[d], b_rest[d] = b, tot[d] - b
        rolls = {}
        for lv in range(GLA_LEVELS):
            be = _gla_boundary(b, lv, reverse, rolls)
            upper = jnp.bitwise_and(jnp.right_shift(row, lv), 1) == 1
            second = jnp.logical_not(upper) if reverse else upper
            e = jnp.exp(jnp.where(second, b - be, be - b))
            ql[d, lv] = jnp.where(second, q[d] * e, 0.0)
            kl[d, lv] = jnp.where(second, 0.0, k[d] * e)
    chains = [(d, h) for d in range(2) for h in range(GLA_H)]

    def dk(h):
        return slice(h * GLA_DK, (h + 1) * GLA_DK)

    def dv(h):
        return slice(h * GLA_DV, (h + 1) * GLA_DV)

    attn = {(d, h): jnp.where(ri == ci, jnp.sum(q[d][:, dk(h)] * k[d][:, dk(h)], axis=-1, keepdims=True), 0.0)
            for d, h in chains}
    for lv in range(GLA_LEVELS):
        same = jnp.right_shift(ri, lv + 1) == jnp.right_shift(ci, lv + 1)
        for d, h in chains:
            attn[d, h] = attn[d, h] + jnp.where(same, _mm_nt(ql[d, lv][:, dk(h)], kl[d, lv][:, dk(h)]), 0.0)
    st_old = {ch: st_ref[ch[0], ch[1]] for ch in chains}
    o = {(d, h): _mm(attn[d, h], v[d][:, dv(h)]) + _mm_nt(q[d][:, dk(h)] * jnp.exp(b_inc[d][:, dk(h)]), st_old[d, h])
         for d, h in chains}
    st_new = {(d, h): st_old[d, h] * jnp.exp(tot[d][:, dk(h)])
              + _mm_tn(v[d][:, dv(h)], k[d][:, dk(h)] * jnp.exp(b_rest[d][:, dk(h)])) for d, h in chains}
    for d, h in chains:
        outs[d][0, :, dv(h)] = o[d, h].astype(outs[d].dtype)
        st_ref[d, h] = st_new[d, h]

    @pl.when(n == n_chunks - 1)
    def _():
        sfin_ref[0] = st_ref[...]


def _gla(pg, plr, w2p, b2, s0t):
    b, l, _ = pg.shape
    c = GLA_C
    nc = l // c
    qw, vw = GLA_H * GLA_DK, GLA_H * GLA_DV

    def fw(blk):
        return lambda i, n: (i, n, blk)

    def bw(blk):
        return lambda i, n: (i, nc - 1 - n, blk)

    st = pl.BlockSpec((1, 2, GLA_H, GLA_DV, GLA_DK), lambda i, n: (i, 0, 0, 0, 0))
    in_specs = [
        pl.BlockSpec((1, c, qw), fw(0)), pl.BlockSpec((1, c, qw), fw(1)), pl.BlockSpec((1, c, vw), fw(1)),
        pl.BlockSpec((1, c, LANES), fw(0)),
        pl.BlockSpec((1, c, qw), bw(0)), pl.BlockSpec((1, c, qw), bw(1)), pl.BlockSpec((1, c, vw), bw(1)),
        pl.BlockSpec((1, c, LANES), bw(0)),
        pl.BlockSpec((2, LANES, qw), lambda i, n: (0, 0, 0)),
        pl.BlockSpec((2, 1, qw), lambda i, n: (0, 0, 0)),
        st,
    ]
    out_specs = [pl.BlockSpec((1, c, vw), fw(0)), pl.BlockSpec((1, c, vw), bw(0)), st]
    sd = jax.ShapeDtypeStruct
    return pl.pallas_call(
        functools.partial(_gla_kernel, n_chunks=nc),
        grid=(b, nc),
        in_specs=in_specs,
        out_specs=out_specs,
        out_shape=[sd((b, l, vw), BF16), sd((b, l, vw), BF16), sd((b, 2, GLA_H, GLA_DV, GLA_DK), F32)],
        scratch_shapes=[pltpu.VMEM((2, GLA_H, GLA_DV, GLA_DK), F32)],
        compiler_params=_cparams(("arbitrary", "arbitrary")),
        name="gla",
    )(pg, pg, pg, plr, pg, pg, pg, plr, w2p, b2, s0t)


def _gla_gate_params(w2, b2):
    w2p = jnp.pad(w2, ((0, 0), (0, LANES - GLA_LOWRANK), (0, 0)))
    return w2p, b2.reshape(2, 1, GLA_H * GLA_DK)


def _split3(x):
    hi = x.astype(BF16)
    r1 = x - hi.astype(F32)
    mid = r1.astype(BF16)
    lo = (r1 - mid.astype(F32)).astype(BF16)
    return hi, mid, lo


def _head_sum(x):
    i = lax.broadcasted_iota(jnp.int32, (LANES, LANES), 0)
    j = lax.broadcasted_iota(jnp.int32, (LANES, LANES), 1)
    ones = jnp.where(jnp.right_shift(i, 6) == jnp.right_shift(j, 6), 1.0, 0.0).astype(BF16)
    out = []
    for t in range(x.shape[1] // LANES):
        parts = _split3(x[:, t * LANES:(t + 1) * LANES])
        out.append(sum(jnp.dot(p, ones, preferred_element_type=F32) for p in parts))
    return jnp.concatenate(out, axis=1)


RW_TM = 256
HALO = 2 * SUBLANES
N_RW_LORA = RW_W_LORA + RW_A_LORA + RW_G_LORA


def _rw_prep_kernel(x_ref, xp_ref, xn_ref, lo_ref, lop_ref, lon_ref, taps_ref, ltaps_ref, kkp_ref, ka_ref, rk_ref,
                    w0_ref, wb_ref, a0_ref, ab_ref, gb_ref, r_ref, v_ref, kk_ref, lw_ref, kd_ref, be_ref, bonus_ref,
                    g_ref, xbuf, lbuf, *, n_tiles):
    tm = RW_TM
    j = pl.program_id(1)
    for buf, cur, prv, nxt in ((xbuf, x_ref, xp_ref, xn_ref), (lbuf, lo_ref, lop_ref, lon_ref)):
        buf[HALO:HALO + tm, :] = cur[0].astype(F32)
        buf[0:HALO, :] = jnp.where(j == 0, 0.0, prv[0].astype(F32))
        buf[HALO + tm:2 * HALO + tm, :] = jnp.where(j == n_tiles - 1, 0.0, nxt[0].astype(F32))

    def shifted(buf, taps, c0, c1):
        return (taps[0:1, c0:c1] * buf[HALO - 1:HALO - 1 + tm, c0:c1]
                + taps[1:2, c0:c1] * buf[HALO:HALO + tm, c0:c1]
                + taps[2:3, c0:c1] * buf[HALO + 1:HALO + 1 + tm, c0:c1])

    r = shifted(xbuf, taps_ref, 0, D)
    k = shifted(xbuf, taps_ref, D, 2 * D)
    v = shifted(xbuf, taps_ref, 2 * D, 3 * D)
    lora = shifted(lbuf, ltaps_ref, 0, N_RW_LORA)
    xw = lora[:, 0:RW_W_LORA]
    xa = lora[:, RW_W_LORA:RW_W_LORA + RW_A_LORA]
    xg = lora[:, RW_W_LORA + RW_A_LORA:]
    r_ref[0] = r.astype(r_ref.dtype)
    v_ref[0] = v.astype(v_ref.dtype)
    kk = k * kkp_ref[...]
    kk = kk * lax.rsqrt(jnp.maximum(_head_sum(kk * kk), 1e-24))
    kk_ref[0] = kk.astype(kk_ref.dtype)
    g_ref[0] = _mm(_sigmoid(xg), gb_ref[...]).astype(g_ref.dtype)
    wh = jnp.tanh(xw)
    kd_sum = jnp.zeros_like(k)
    for d in range(2):
        w = -_softplus(-(w0_ref[d:d + 1, :] + _mm(wh, wb_ref[d]))) - 0.5
        lw_ref[d, 0] = -jnp.exp(w)
        a = _sigmoid(a0_ref[d:d + 1, :] + _mm(xa, ab_ref[d]))
        kd = k * (1.0 + (a - 1.0) * ka_ref[...])
        kd_ref[d, 0] = kd.astype(kd_ref.dtype)
        be_ref[d, 0] = (kk * a).astype(be_ref.dtype)
        kd_sum = kd_sum + kd
    bonus_ref[0] = (_head_sum(r * kd_sum * rk_ref[...]) * v).astype(bonus_ref.dtype)


def _rw_prep(rw, rlo, taps, kkp, ka, rk, w0, wb, a0, ab, gb):
    b, l, _ = rw.shape
    tm = RW_TM
    nt = l // tm
    hb = tm // HALO
    tok = pl.BlockSpec((1, tm, D), lambda i, j: (i, j, 0))
    tokd = pl.BlockSpec((2, 1, tm, D), lambda i, j: (0, i, j, 0))

    def full(*shape):
        return pl.BlockSpec(shape, lambda i, j: (0,) * len(shape))

    def halo_specs(n):
        return [pl.BlockSpec((1, tm, n), lambda i, j: (i, j, 0)),
                pl.BlockSpec((1, HALO, n), lambda i, j: (i, jnp.maximum(j * hb - 1, 0), 0)),
                pl.BlockSpec((1, HALO, n), lambda i, j: (i, jnp.minimum((j + 1) * hb, l // HALO - 1), 0))]

    in_specs = halo_specs(3 * D) + halo_specs(N_RW_LORA) + [
        full(3, 3 * D), full(3, N_RW_LORA), full(1, D), full(1, D), full(1, D),
        full(2, D), full(2, RW_W_LORA, D), full(2, D), full(2, RW_A_LORA, D), full(RW_G_LORA, D),
    ]
    sd = jax.ShapeDtypeStruct
    tok_o, dir_o, dir_f32 = sd((b, l, D), BF16), sd((2, b, l, D), BF16), sd((2, b, l, D), F32)
    return pl.pallas_call(
        functools.partial(_rw_prep_kernel, n_tiles=nt),
        grid=(b, nt),
        in_specs=in_specs,
        out_specs=[tok, tok, tok, tokd, tokd, tokd, tok, tok],
        out_shape=[tok_o, tok_o, tok_o, dir_f32, dir_o, dir_o, tok_o, tok_o],
        scratch_shapes=[pltpu.VMEM((tm + 2 * HALO, 3 * D), F32), pltpu.VMEM((tm + 2 * HALO, N_RW_LORA), F32)],
        compiler_params=_cparams(("arbitrary", "arbitrary")),
        name="rwkv_prep",
    )(rw, rw, rw, rlo, rlo, rlo, taps[:, :3 * D], taps[:, 3 * D:], kkp.reshape(1, D), ka.reshape(1, D),
      rk.reshape(1, D), w0, wb, a0, ab, gb)


def _tile_rows(x, n):
    return jnp.concatenate([x] * n, axis=0)


def _rw_scan_kernel(rf_ref, vf_ref, kkf_ref, lwf_ref, kdf_ref, bef_ref, rb_ref, vb_ref, kkb_ref, lwb_ref, kdb_ref,
                    beb_ref, s0_ref, yf_ref, yb_ref, sfin_ref, zt_ref, *, n_tiles):
    c, hw, sw = RW_C, RW_GW, RW_SW
    pk = RW_G * c
    n_groups, n_sub = RW_H // RW_G, RW_GW // RW_SW
    n_chunks = RW_TL // c
    j = pl.program_id(1)

    @pl.when(j == 0)
    def _():
        zt_ref[...] = jnp.zeros_like(zt_ref)
        for bb in range(RW_BB):
            for d in range(2):
                for g in range(RW_H // 2):
                    for jj in range(2):
                        blk = slice(RW_N * jj, RW_N * (jj + 1))
                        zt_ref[bb, d, g, blk, blk] = s0_ref[bb, d, 2 * g + jj]

    ins = ((rf_ref, vf_ref, kkf_ref, lwf_ref, kdf_ref, bef_ref), (rb_ref, vb_ref, kkb_ref, lwb_ref, kdb_ref, beb_ref))
    outs = (yf_ref, yb_ref)
    ri = lax.broadcasted_iota(jnp.int32, (c, pk), 0)
    cs = jnp.bitwise_and(lax.broadcasted_iota(jnp.int32, (c, pk), 1), c - 1)
    eye = (ri == cs).astype(F32)
    strict, incl, lvl = [], [], []
    for d in range(2):
        ti = ri if d == 0 else c - 1 - ri
        ts = cs if d == 0 else c - 1 - cs
        strict.append(ts < ti)
        incl.append(ts <= ti)
        lv_masks = []
        for lv in range(RW_LEVELS):
            same = jnp.right_shift(ti, lv + 1) == jnp.right_shift(ts, lv + 1)
            lower = ((jnp.bitwise_and(jnp.right_shift(ti, lv), 1) == 1)
                     & (jnp.bitwise_and(jnp.right_shift(ts, lv), 1) == 0))
            lv_masks.append(same & lower)
        lvl.append(lv_masks)
    bi = lax.broadcasted_iota(jnp.int32, (pk, pk), 0)
    bj = lax.broadcasted_iota(jnp.int32, (pk, pk), 1)
    bd_p = jnp.right_shift(bi, 5) == jnp.right_shift(bj, 5)
    zi = lax.broadcasted_iota(jnp.int32, (sw, sw), 0)
    zj = lax.broadcasted_iota(jnp.int32, (sw, sw), 1)
    bd_z = jnp.right_shift(zi, 6) == jnp.right_shift(zj, 6)
    wi = lax.broadcasted_iota(jnp.int32, (pk, hw), 0)
    wj = lax.broadcasted_iota(jnp.int32, (pk, hw), 1)
    bd_w = jnp.right_shift(wi, 5) == jnp.right_shift(wj, 6)

    def bdp(x):
        return jnp.where(bd_p, _tile_rows(x, RW_G), 0.0)

    def bdw(x):
        return jnp.where(bd_w, _tile_rows(x, RW_G), 0.0)

    chains = [(bb, d, half) for bb in range(RW_BB) for d in range(2) for half in range(n_groups)]

    def each(fn):
        return {ch: fn(ch) for ch in chains}

    def chunk(i, carry):
        rows = (pl.ds(pl.multiple_of(i * c, c), c), pl.ds(pl.multiple_of((n_chunks - 1 - i) * c, c), c))

        def load(ch):
            bb, d, half = ch
            ln = slice(half * hw, (half + 1) * hw)
            refs = ins[d]
            return (refs[0][bb, rows[d], ln].astype(F32), refs[1][bb, rows[d], ln].astype(F32),
                    refs[2][bb, rows[d], ln].astype(F32), refs[3][0, bb, rows[d], ln],
                    refs[4][0, bb, rows[d], ln].astype(F32), refs[5][0, bb, rows[d], ln].astype(F32))

        x = each(load)
        r, v, kk = each(lambda ch: x[ch][0]), each(lambda ch: x[ch][1]), each(lambda ch: x[ch][2])
        lw, kd, be = each(lambda ch: x[ch][3]), each(lambda ch: x[ch][4]), each(lambda ch: x[ch][5])
        b = each(lambda ch: _scan_rows(lw[ch], ch[1] == 1))
        tot = each(lambda ch: b[ch][0:1] if ch[1] == 1 else b[ch][c - 1:c])
        en = each(lambda ch: jnp.exp(-b[ch]))
        es = each(lambda ch: jnp.exp(tot[ch] - b[ch]))
        lhs = each(lambda ch: jnp.concatenate([kk[ch] * jnp.exp(b[ch] - lw[ch]), r[ch] * jnp.exp(b[ch])], axis=0))
        gb = each(lambda ch: _mm_nt(lhs[ch], bdw(be[ch] * en[ch])))
        gk = each(lambda ch: _mm_nt(lhs[ch], bdw(kd[ch] * en[ch])))
        a_ab = each(lambda ch: jnp.where(strict[ch[1]], gb[ch][0:c], 0.0))
        a_rb = each(lambda ch: jnp.where(incl[ch[1]], gb[ch][c:], 0.0))
        a_ak = each(lambda ch: jnp.where(strict[ch[1]], gk[ch][0:c], 0.0))
        a_rk = each(lambda ch: jnp.where(incl[ch[1]], gk[ch][c:], 0.0))
        t_inv = each(lambda ch: eye - jnp.where(lvl[ch[1]][0], a_ab[ch], 0.0))
        for lv in range(1, RW_LEVELS):
            xm = each(lambda ch: _mm(jnp.where(lvl[ch[1]][lv], a_ab[ch], 0.0), bdp(t_inv[ch])))
            t_inv = each(lambda ch: t_inv[ch] - _mm(t_inv[ch], bdp(xm[ch])))
        zt_old = {(ch, g): zt_ref[ch[0], ch[1], n_sub * ch[2] + g] for ch in chains for g in range(n_sub)}
        zz = each(lambda ch: jnp.concatenate(
            [_mm_nt(lhs[ch][:, g * sw:(g + 1) * sw], zt_old[ch, g]) for g in range(n_sub)], axis=1))
        vbd = each(lambda ch: bdw(v[ch]))
        u = each(lambda ch: _mm(t_inv[ch], bdw(-(zz[ch][0:c] + _mm(a_ak[ch], vbd[ch])))))
        y = each(lambda ch: zz[ch][c:] + _mm(a_rb[ch], bdw(u[ch])) + _mm(a_rk[ch], vbd[ch]))
        uv = each(lambda ch: jnp.concatenate([u[ch], v[ch]], axis=0))
        bk = each(lambda ch: jnp.concatenate([be[ch] * es[ch], kd[ch] * es[ch]], axis=0))
        etot = each(lambda ch: jnp.exp(tot[ch]))
        zt_new = {}
        for ch in chains:
            for g in range(n_sub):
                gl = slice(g * sw, (g + 1) * sw)
                upd = jnp.where(bd_z, _mm_tn(uv[ch][:, gl], bk[ch][:, gl]), 0.0)
                zt_new[ch, g] = zt_old[ch, g] * etot[ch][:, gl] + upd
        for ch in chains:
            bb, d, half = ch
            outs[d][bb, rows[d], half * hw:(half + 1) * hw] = y[ch].astype(outs[d].dtype)
            for g in range(n_sub):
                zt_ref[bb, d, n_sub * half + g] = zt_new[ch, g]
        return carry

    lax.fori_loop(0, n_chunks, chunk, 0)

    @pl.when(j == n_tiles - 1)
    def _():
        for bb in range(RW_BB):
            for d in range(2):
                for g in range(RW_H // 2):
                    for jj in range(2):
                        blk = slice(RW_N * jj, RW_N * (jj + 1))
                        sfin_ref[bb, d, 2 * g + jj] = zt_ref[bb, d, g, blk, blk]


def _rw_scan(r, v, kk, lw, kd, be, s0):
    b, l, _ = r.shape
    tl, nb = RW_TL, RW_BB
    nt = l // tl
    tok_f = pl.BlockSpec((nb, tl, D), lambda i, j: (i, j, 0))
    tok_b = pl.BlockSpec((nb, tl, D), lambda i, j: (i, nt - 1 - j, 0))
    dir_f = pl.BlockSpec((1, nb, tl, D), lambda i, j: (0, i, j, 0))
    dir_b = pl.BlockSpec((1, nb, tl, D), lambda i, j: (1, i, nt - 1 - j, 0))
    st = pl.BlockSpec((nb, 2, RW_H, RW_N, RW_N), lambda i, j: (i, 0, 0, 0, 0))
    sd = jax.ShapeDtypeStruct
    return pl.pallas_call(
        functools.partial(_rw_scan_kernel, n_tiles=nt),
        grid=(b // nb, nt),
        in_specs=[tok_f, tok_f, tok_f, dir_f, dir_f, dir_f, tok_b, tok_b, tok_b, dir_b, dir_b, dir_b, st],
        out_specs=[tok_f, tok_b, st],
        out_shape=[sd((b, l, D), BF16), sd((b, l, D), BF16), sd((b, 2, RW_H, RW_N, RW_N), F32)],
        scratch_shapes=[pltpu.VMEM((nb, 2, RW_H // 2, RW_SW, RW_SW), F32)],
        compiler_params=_cparams(("arbitrary", "arbitrary")),
        name="rwkv_scan",
    )(r, v, kk, lw, kd, be, r, v, kk, lw, kd, be, s0)


MERGE_TM = 128
ROUTER_LANES = LANES


def _merge_kernel(orf_ref, orb_ref, rg_ref, ogf_ref, ogb_ref, gg_ref, yf_ref, yb_ref, bonus_ref, grw_ref,
                  gates_ref, x_ref, gt_ref, sh_ref, sc_ref, wbr_ref, wout_ref, rgn_ref, ggn_ref, gng_ref, gnb_ref,
                  l1g_ref, l1b_ref, wr_ref, br_ref, x1_ref, u2_ref, lg_ref):
    f32 = lambda ref: ref[0].astype(F32)
    o_ret = f32(orf_ref) + f32(orb_ref)
    o_gla = f32(ogf_ref) + f32(ogb_ref)
    y_ret, y_gla = [], []
    for h in range(RET_H):
        hs = slice(h * RET_DV, (h + 1) * RET_DV)
        y_ret.append(_ln(o_ret[:, hs]))
        og = o_gla[:, hs]
        y_gla.append(og * lax.rsqrt(jnp.mean(og * og, axis=-1, keepdims=True) + EPS))
    z_ret = _silu(f32(rg_ref)) * (jnp.concatenate(y_ret, axis=1) * rgn_ref[...])
    z_gla = _silu(f32(gg_ref)) * (jnp.concatenate(y_gla, axis=1) * ggn_ref[...])
    y = f32(yf_ref) + f32(yb_ref)
    mu = _head_sum(y) * (1.0 / RW_N)
    yc = y - mu
    var = _head_sum(yc * yc) * (1.0 / RW_N)
    z_rw = (yc * lax.rsqrt(var + EPS) * gng_ref[...] + gnb_ref[...] + f32(bonus_ref)) * f32(grw_ref)
    gates = f32(gates_ref)
    mixed = (gates[:, 0:D] * _mm(z_ret, wbr_ref[0]) + gates[:, D:2 * D] * _mm(z_gla, wbr_ref[1])
             + gates[:, 2 * D:] * _mm(z_rw, wbr_ref[2]))
    mix = _mm(mixed, wout_ref[...])
    x1 = _ln(ALPHA * x_ref[0] + gt_ref[0] * mix) * l1g_ref[...] + l1b_ref[...]
    x1_ref[0] = x1
    u2 = _ln(x1) * (1.0 + sc_ref[0]) + sh_ref[0]
    u2_ref[0] = u2.astype(u2_ref.dtype)
    lg_ref[0] = (_mm_hi(u2, wr_ref[...]) + br_ref[...]).T


def _merge(orf, orb, pr, ogf, ogb, pg, yf, yb, bonus, grw, gates, x, gt1, sh2, sc2, wbr, wout, rgn, ggn, gng, gnb,
           l1g, l1b, wr, br):
    b, l, _ = x.shape
    tm = MERGE_TM
    tok = pl.BlockSpec((1, tm, D), lambda i, j: (i, j, 0))
    gate_blk = pl.BlockSpec((1, tm, D), lambda i, j: (i, j, 2))
    vec = pl.BlockSpec((1, 1, D), lambda i, j: (i, 0, 0))
    row = pl.BlockSpec((1, D), lambda i, j: (0, 0))
    in_specs = [
        tok, tok, gate_blk, tok, tok, gate_blk, tok, tok, tok, tok,
        pl.BlockSpec((1, tm, 3 * D), lambda i, j: (i, j, 0)),
        tok, vec, vec, vec,
        pl.BlockSpec((3, D, D), lambda i, j: (0, 0, 0)),
        pl.BlockSpec((D, D), lambda i, j: (0, 0)),
        row, row, row, row, row, row,
        pl.BlockSpec((D, ROUTER_LANES), lambda i, j: (0, 0)),
        pl.BlockSpec((1, ROUTER_LANES), lambda i, j: (0, 0)),
    ]
    sd = jax.ShapeDtypeStruct
    r1 = lambda a: a.reshape(1, D)
    return pl.pallas_call(
        _merge_kernel,
        grid=(b, l // tm),
        in_specs=in_specs,
        out_specs=[tok, tok, pl.BlockSpec((1, ROUTER_LANES, tm), lambda i, j: (i, 0, j))],
        out_shape=[sd((b, l, D), F32), sd((b, l, D), BF16), sd((b, ROUTER_LANES, l), F32)],
        compiler_params=_cparams(("arbitrary", "arbitrary")),
        name="merge",
    )(orf, orb, pr, ogf, ogb, pg, yf, yb, bonus, grw, gates, x, gt1, sh2, sc2, wbr, wout,
      r1(rgn), r1(ggn), r1(gng), r1(gnb), r1(l1g), r1(l1b), wr, br)


MOE_TM = 1024
MOE_BLK = 256


def _first_argmax_rows(x, row):
    m = jnp.max(x, axis=0, keepdims=True)
    idx = jnp.min(jnp.where(x == m, row, x.shape[0]), axis=0, keepdims=True)
    return m, idx


def _routing(lt):
    tm = lt.shape[1]
    gl = lt[0:N_GROUPS]
    gmax, gidx = _first_argmax_rows(gl, lax.broadcasted_iota(jnp.int32, (N_GROUPS, tm), 0))
    g_w = 1.0 / jnp.sum(jnp.exp(gl - gmax), axis=0, keepdims=True)
    row = lax.broadcasted_iota(jnp.int32, (N_EXPERTS, tm), 0)
    neg = -jnp.inf
    el = jnp.where(jnp.right_shift(row, 2) == gidx, lt[N_GROUPS:N_GROUPS + N_EXPERTS], neg)
    m1, i1 = _first_argmax_rows(el, row)
    m2, i2 = _first_argmax_rows(jnp.where(row == i1, neg, el), row)
    e2 = jnp.exp(m2 - m1)
    w1 = 1.0 / (1.0 + e2)
    return gidx, g_w * (jnp.where(row == i1, w1, 0.0) + jnp.where(row == i2, e2 * w1, 0.0))


def _moe_kernel(u_ref, lt_ref, wg_ref, wu_ref, wd_ref, o_ref, acc_ref):
    g = pl.program_id(2)
    tm, blk = acc_ref.shape[0], MOE_BLK

    @pl.when(g == 0)
    def _():
        acc_ref[...] = jnp.zeros_like(acc_ref)

    gidx, comb = _routing(lt_ref[0])
    member = jnp.broadcast_to((gidx == g).astype(F32), (SUBLANES, tm))
    lane = lax.broadcasted_iota(jnp.int32, (SUBLANES, tm), 1)
    count = member
    sh = 1
    while sh < tm:
        count = count + jnp.where(lane >= sh, pltpu.roll(count, sh, 1), 0.0)
        sh *= 2
    pos = jnp.where(member > 0.0, count - 1.0, -1.0).astype(jnp.int32)[0:1]
    n_tok = jnp.max(count).astype(jnp.int32)
    comb_pad = jnp.concatenate([comb, jnp.zeros((LANES - N_EXPERTS, tm), F32)], axis=0)
    comb_parts = _split3(comb_pad)
    u = u_ref[0]
    sel_lane = lax.broadcasted_iota(jnp.int32, (blk, LANES), 1)

    def body(i, carry):
        slot = lax.broadcasted_iota(jnp.int32, (blk, tm), 0) + i * blk
        onehot = jnp.where(slot == pos, 1.0, 0.0).astype(BF16)
        xg = jnp.dot(onehot, u, preferred_element_type=F32).astype(BF16)
        cw = sum(lax.dot_general(onehot, c, (((1,), (1,)), ((), ())), preferred_element_type=F32)
                 for c in comb_parts)
        y = jnp.zeros((blk, D), F32)
        for e in range(EPG):
            hid = (_silu(jnp.dot(xg, wg_ref[e], preferred_element_type=F32))
                   * jnp.dot(xg, wu_ref[e], preferred_element_type=F32))
            c_e = jnp.sum(jnp.where(sel_lane == g * EPG + e, cw, 0.0), axis=-1, keepdims=True)
            y = y + c_e * _mm(hid, wd_ref[e])
        acc_ref[...] += _mm_tn(onehot, y)
        return carry

    lax.fori_loop(0, (n_tok + blk - 1) // blk, body, 0)

    @pl.when(g == N_GROUPS - 1)
    def _():
        o_ref[0] = acc_ref[...].astype(o_ref.dtype)


def _moe(u2, logits_t, wg, wu, wd):
    b, l, _ = u2.shape
    tm = min(MOE_TM, l)
    tok = pl.BlockSpec((1, tm, D), lambda i, j, g: (i, j, 0))
    in_specs = [
        tok,
        pl.BlockSpec((1, ROUTER_LANES, tm), lambda i, j, g: (i, 0, j)),
        pl.BlockSpec((EPG, D, EXPERT_FF), lambda i, j, g: (g, 0, 0)),
        pl.BlockSpec((EPG, D, EXPERT_FF), lambda i, j, g: (g, 0, 0)),
        pl.BlockSpec((EPG, EXPERT_FF, D), lambda i, j, g: (g, 0, 0)),
    ]
    return pl.pallas_call(
        _moe_kernel,
        grid=(b, l // tm, N_GROUPS),
        in_specs=in_specs,
        out_specs=tok,
        out_shape=jax.ShapeDtypeStruct((b, l, D), BF16),
        scratch_shapes=[pltpu.VMEM((tm, D), F32)],
        compiler_params=_cparams(("arbitrary",) * 3),
        name="moe",
    )(u2, logits_t, wg, wu, wd)


def _post_kernel(x_ref, m_ref, gt_ref, l2g_ref, l2b_ref, sh_ref, sc_ref, o_ref, u_ref):
    x2 = _ln(ALPHA * x_ref[0] + gt_ref[0] * m_ref[0].astype(F32)) * l2g_ref[...] + l2b_ref[...]
    o_ref[0] = x2
    u_ref[0] = (_ln(x2) * (1.0 + sc_ref[0]) + sh_ref[0]).astype(u_ref.dtype)


def _post(x1, moe_out, gt2, l2g, l2b, sh_next, sc_next):
    b, l, _ = x1.shape
    tm = 512 if l % 512 == 0 else 256
    tok = pl.BlockSpec((1, tm, D), lambda i, j: (i, j, 0))
    vec = pl.BlockSpec((1, 1, D), lambda i, j: (i, 0, 0))
    row = pl.BlockSpec((1, D), lambda i, j: (0, 0))
    sd = jax.ShapeDtypeStruct
    return pl.pallas_call(
        _post_kernel,
        grid=(b, l // tm),
        in_specs=[tok, tok, vec, row, row, vec, vec],
        out_specs=[tok, tok],
        out_shape=[sd((b, l, D), F32), sd((b, l, D), BF16)],
        compiler_params=_cparams(("arbitrary", "arbitrary")),
        name="post_ln",
    )(x1, moe_out, gt2, l2g.reshape(1, D), l2b.reshape(1, D), sh_next, sc_next)


def _rope_tables(l):
    t = np.arange(l)
    quarter = RET_DK // 4
    freqs = (np.float32(ROPE_BASE) ** (-np.arange(quarter, dtype=np.float32) / quarter)).astype(np.float32)
    rows = (t // GRID_W).astype(np.float32)
    cols = (t % GRID_W).astype(np.float32)
    ang = jnp.asarray(np.concatenate([rows[:, None] * freqs, cols[:, None] * freqs], -1))
    cos, sin = jnp.cos(ang), jnp.sin(ang)
    return jnp.concatenate([cos, cos], -1), jnp.concatenate([-sin, sin], -1)


def _layer_weights(p):
    n_rg = 2 * (RET_H * RET_DK + RET_H * RET_DV)
    w_in = p["w_in"]
    w_lr = jnp.pad(w_in[:, 2 * n_rg:2 * n_rg + GLA_LOWRANK], ((0, 0), (0, LANES - GLA_LOWRANK)))
    w2p, b2p = _gla_gate_params(p["gla_w2"], p["gla_b"])
    pad = ROUTER_LANES - N_GROUPS - N_EXPERTS
    return dict(
        p,
        w_ret=w_in[:, :n_rg].astype(BF16),
        w_gla=w_in[:, n_rg:2 * n_rg].astype(BF16),
        w_glr=w_lr.astype(BF16),
        w_rw=w_in[:, 2 * n_rg + GLA_LOWRANK:2 * n_rg + GLA_LOWRANK + 3 * D].astype(BF16),
        w_rlo=w_in[:, 2 * n_rg + GLA_LOWRANK + 3 * D:].astype(BF16),
        w_merge=p["w_merge"].astype(BF16),
        w_br=p["w_br"].astype(BF16),
        w_out=p["w_out"].astype(BF16),
        gla_w2p=w2p, gla_b2p=b2p,
        w_router=jnp.pad(jnp.concatenate([p["w_rg"], p["w_re"]], axis=1), ((0, 0), (0, pad))),
        b_router=jnp.pad(jnp.concatenate([p["b_rg"], p["b_re"]]), (0, pad)).reshape(1, ROUTER_LANES),
        w_eg=p["w_eg"].astype(BF16), w_eu=p["w_eu"].astype(BF16), w_ed=p["w_ed"].astype(BF16),
    )


def _split_mod(mod):
    return [m.reshape(mod.shape[0], 1, D) for m in jnp.split(mod, 6, axis=-1)]


def _layer(x, u, mod, mod_next, s_ret, s_gla_t, s_rw, rope, p):
    b, l, _ = x.shape
    _, _, gt1, sh2, sc2, gt2 = _split_mod(mod)
    sh_next, sc_next = _split_mod(mod_next)[:2]
    u2d = u.reshape(b * l, D)

    def proj(w, tn, bias=None, **kw):
        bias = jnp.zeros((w.shape[1],), F32) if bias is None else bias
        return _proj(u2d, w, bias, tn, **kw).reshape(b, l, -1)

    pr = proj(p["w_ret"], 1024)
    pg = proj(p["w_gla"], 1024)
    plr = proj(p["w_glr"], LANES, out_dtype=F32)
    rw = proj(p["w_rw"], 1024)
    rlo = proj(p["w_rlo"], N_RW_LORA, out_dtype=F32)
    gates = proj(p["w_merge"], 1024, bias=p["b_merge"], act="sigmoid")

    cos2, sin2 = rope if rope is not None else (jnp.ones((l, RET_DK), F32), jnp.zeros((l, RET_DK), F32))
    orf, orb, ret_fin = _retention(pr, cos2, sin2, s_ret, rope is not None)
    ogf, ogb, gla_fin_t = _gla(pg, plr, p["gla_w2p"], p["gla_b2p"], s_gla_t)
    r, v, kk, lw, kd, be, bonus, grw = _rw_prep(rw, rlo, p["rwkv_shift"], p["rwkv_kk"], p["rwkv_ka"],
                                                p["rwkv_rk"].reshape(-1), p["rwkv_w0"], p["rwkv_wb"],
                                                p["rwkv_a0"], p["rwkv_ab"], p["rwkv_gb"])
    yf, yb, rw_fin = _rw_scan(r, v, kk, lw, kd, be, s_rw)
    x1, u2, logits_t = _merge(orf, orb, pr, ogf, ogb, pg, yf, yb, bonus, grw, gates, x, gt1, sh2, sc2,
                              p["w_br"], p["w_out"], p["ret_gn"], p["gla_gn"], p["rwkv_gn_g"], p["rwkv_gn_b"],
                              p["ln1_g"], p["ln1_b"], p["w_router"], p["b_router"])
    moe_out = _moe(u2, logits_t, p["w_eg"], p["w_eu"], p["w_ed"])
    x2, u_next = _post(x1, moe_out, gt2, p["ln2_g"], p["ln2_b"], sh_next, sc_next)
    return x2, u_next, (ret_fin, gla_fin_t, rw_fin)


_PARAM_NAMES = ("w_in", "rwkv_shift", "ret_gn", "gla_w2", "gla_b", "gla_gn", "rwkv_w0", "rwkv_wb", "rwkv_a0",
                "rwkv_ab", "rwkv_gb", "rwkv_kk", "rwkv_ka", "rwkv_rk", "rwkv_gn_g", "rwkv_gn_b", "w_br", "w_merge",
                "b_merge", "w_out", "ln1_g", "ln1_b", "ln2_g", "ln2_b", "w_rg", "b_rg", "w_re", "b_re", "w_eg",
                "w_eu", "w_ed")


def kernel(x_prompt, x_sample, state_ret, state_gla, state_rwkv, c, c_ctx, w_ada, b_ada, w_in, rwkv_shift, ret_gn,
           gla_w2, gla_b, gla_gn, rwkv_w0, rwkv_wb, rwkv_a0, rwkv_ab, rwkv_gb, rwkv_kk, rwkv_ka, rwkv_rk, rwkv_gn_g,
           rwkv_gn_b, w_br, w_merge, b_merge, w_out, ln1_g, ln1_b, ln2_g, ln2_b, w_rg, b_rg, w_re, b_re, w_eg, w_eu,
           w_ed):
    params = dict(zip(_PARAM_NAMES, (w_in, rwkv_shift, ret_gn, gla_w2, gla_b, gla_gn, rwkv_w0, rwkv_wb, rwkv_a0,
                                     rwkv_ab, rwkv_gb, rwkv_kk, rwkv_ka, rwkv_rk, rwkv_gn_g, rwkv_gn_b, w_br, w_merge,
                                     b_merge, w_out, ln1_g, ln1_b, ln2_g, ln2_b, w_rg, b_rg, w_re, b_re, w_eg, w_eu,
                                     w_ed)))
    bc, bl = x_prompt.shape[0], x_sample.shape[0]
    rope = _rope_tables(x_sample.shape[1])
    mod_rows = 2 * SUBLANES
    c_all = jnp.concatenate([c, c_ctx[None, :], jnp.zeros((mod_rows - bl - 1, D), F32)], axis=0)
    z_ret = jnp.zeros((bc, 2, RET_H, RET_DK, RET_DV), F32)
    z_gla_t = jnp.zeros((bc, 2, GLA_H, GLA_DV, GLA_DK), F32)
    z_rw = jnp.zeros((bc, 2, RW_H, RW_N, RW_N), F32)
    mods = [_modulation(c_all, w_ada[layer], b_ada[layer]) for layer in range(DEPTH)]
    mods.append(jnp.zeros_like(mods[0]))
    mods_lat = [m[:bl] for m in mods]
    mods_ctx = [jnp.broadcast_to(m[bl:bl + 1], (bc, 6 * D)) for m in mods]
    h_ctx, h_lat = x_prompt, x_sample
    sh, sc = _split_mod(mods_ctx[0])[:2]
    u_ctx = _lnmod(h_ctx, sh, sc)
    sh, sc = _split_mod(mods_lat[0])[:2]
    u_lat = _lnmod(h_lat, sh, sc)
    new_ret, new_gla, new_rw = [], [], []
    for layer in range(DEPTH):
        p = _layer_weights({k: v[layer] for k, v in params.items()})
        h_ctx, u_ctx, (s_ret, s_gla_t, s_rw) = _layer(h_ctx, u_ctx, mods_ctx[layer], mods_ctx[layer + 1],
                                                      z_ret, z_gla_t, z_rw, None, p)
        new_ret.append(s_ret)
        new_gla.append(jnp.swapaxes(s_gla_t, -1, -2))
        new_rw.append(s_rw)
        h_lat, u_lat, _ = _layer(h_lat, u_lat, mods_lat[layer], mods_lat[layer + 1], state_ret[:, layer],
                                 jnp.swapaxes(state_gla[:, layer], -1, -2), state_rwkv[:, layer], rope, p)
    return (h_ctx, h_lat, jnp.stack(new_ret, axis=1), jnp.stack(new_gla, axis=1), jnp.stack(new_rw, axis=1))
```

```python
import functools

import jax
import jax.numpy as jnp
import numpy as np
from jax import lax
from jax.experimental import pallas as pl
from jax.experimental.pallas import tpu as pltpu

F32 = jnp.float32
BF16 = jnp.bfloat16
HI = lax.Precision.HIGHEST

D = 1024
DEPTH = 2
GRID_W = 64
RET_H, RET_DK, RET_DV = 4, 128, 256
GLA_H, GLA_DK, GLA_DV = 4, 128, 256
GLA_LOWRANK = 16
GLA_GATE_NORM = 16.0
RW_H, RW_N = 16, 64
RW_W_LORA, RW_A_LORA, RW_G_LORA = 64, 64, 128
N_RW_COLS = 3 * D + RW_W_LORA + RW_A_LORA + RW_G_LORA
N_GROUPS, EPG, N_EXPERTS, EXPERT_FF = 4, 4, 16, 512
ALPHA = (2 * DEPTH) ** 0.25
EPS = 1e-5
ROPE_BASE = 10000.0

LANES = 128
SUBLANES = 8
VMEM_LIMIT = 56 * 1024 * 1024

RET_C = 128
GLA_C = 64
GLA_LEVELS = 6
RW_C = 32
RW_LEVELS = 5
RW_TL = 256
RW_BB = 2
RW_G = 4
RW_GW = RW_G * RW_N
RW_SW = 2 * RW_N


def _cparams(sem):
    return pltpu.CompilerParams(dimension_semantics=sem, vmem_limit_bytes=VMEM_LIMIT)


def _mm(a, b):
    return jnp.dot(a.astype(BF16), b.astype(BF16), preferred_element_type=F32)


def _mm_nt(a, b):
    return lax.dot_general(a.astype(BF16), b.astype(BF16), (((1,), (1,)), ((), ())), preferred_element_type=F32)


def _mm_tn(a, b):
    return lax.dot_general(a.astype(BF16), b.astype(BF16), (((0,), (0,)), ((), ())), preferred_element_type=F32)


def _mm_hi(a, b):
    return jnp.dot(a, b, preferred_element_type=F32, precision=HI)


def _split2(x):
    hi = x.astype(BF16)
    return hi, (x - hi.astype(F32)).astype(BF16)


def _mm_split(a, b_hi, b_lo):
    a_hi, a_lo = _split2(a)
    dot = functools.partial(jnp.dot, preferred_element_type=F32)
    return dot(a_hi, b_hi) + (dot(a_hi, b_lo) + dot(a_lo, b_hi))


def _sigmoid(x):
    return 1.0 / (1.0 + jnp.exp(-x))


def _silu(x):
    return x * _sigmoid(x)


def _log_sigmoid(x):
    return jnp.minimum(x, 0.0) - jnp.log(1.0 + jnp.exp(-jnp.abs(x)))


def _softplus(x):
    return jnp.maximum(x, 0.0) + jnp.log(1.0 + jnp.exp(-jnp.abs(x)))


def _ln(x):
    mu = jnp.mean(x, axis=-1, keepdims=True)
    xc = x - mu
    var = jnp.mean(xc * xc, axis=-1, keepdims=True)
    return xc * lax.rsqrt(var + EPS)


def _scan_rows(x, reverse):
    n = x.shape[0]
    row = lax.broadcasted_iota(jnp.int32, x.shape, 0)
    sh = 1
    while sh < n:
        if reverse:
            x = x + jnp.where(row < n - sh, pltpu.roll(x, n - sh, 0), 0.0)
        else:
            x = x + jnp.where(row >= sh, pltpu.roll(x, sh, 0), 0.0)
        sh *= 2
    return x


def _mod_kernel(c_ref, w_ref, b_ref, o_ref):
    o_ref[...] = _mm_hi(_silu(c_ref[...]), w_ref[...]) + b_ref[...]


def _modulation(c_all, w, b):
    m, n, tn = c_all.shape[0], w.shape[1], 512
    return pl.pallas_call(
        _mod_kernel,
        grid=(n // tn,),
        in_specs=[pl.BlockSpec((m, D), lambda j: (0, 0)),
                  pl.BlockSpec((D, tn), lambda j: (0, j)),
                  pl.BlockSpec((1, tn), lambda j: (0, j))],
        out_specs=pl.BlockSpec((m, tn), lambda j: (0, j)),
        out_shape=jax.ShapeDtypeStruct((m, n), F32),
        compiler_params=_cparams(("arbitrary",)),
        name="adaln_mod",
    )(c_all, w, b.reshape(1, n))


def _lnmod_kernel(x_ref, sh_ref, sc_ref, u_ref):
    u_ref[0] = (_ln(x_ref[0]) * (1.0 + sc_ref[0]) + sh_ref[0]).astype(u_ref.dtype)


def _lnmod(x, sh, sc):
    b, l, _ = x.shape
    tm = 512 if l % 512 == 0 else 256
    vec = pl.BlockSpec((1, 1, D), lambda i, j: (i, 0, 0))
    return pl.pallas_call(
        _lnmod_kernel,
        grid=(b, l // tm),
        in_specs=[pl.BlockSpec((1, tm, D), lambda i, j: (i, j, 0)), vec, vec],
        out_specs=pl.BlockSpec((1, tm, D), lambda i, j: (i, j, 0)),
        out_shape=jax.ShapeDtypeStruct((b, l, D), BF16),
        compiler_params=_cparams(("arbitrary", "arbitrary")),
        name="ln_mod",
    )(x, sh, sc)


def _proj_kernel(u_ref, w_ref, b_ref, o_ref, *, act):
    y = jnp.dot(u_ref[...], w_ref[...], preferred_element_type=F32) + b_ref[...]
    if act == "sigmoid":
        y = _sigmoid(y)
    o_ref[...] = y.astype(o_ref.dtype)


def _proj(u2d, w, bias, tn, act=None, out_dtype=BF16):
    t, n = u2d.shape[0], w.shape[1]
    tm = min(t, 2048)
    return pl.pallas_call(
        functools.partial(_proj_kernel, act=act),
        grid=(t // tm, n // tn),
        in_specs=[pl.BlockSpec((tm, D), lambda i, j: (i, 0)),
                  pl.BlockSpec((D, tn), lambda i, j: (0, j)),
                  pl.BlockSpec((1, tn), lambda i, j: (0, j))],
        out_specs=pl.BlockSpec((tm, tn), lambda i, j: (i, j)),
        out_shape=jax.ShapeDtypeStruct((t, n), out_dtype),
        compiler_params=_cparams(("arbitrary", "arbitrary")),
        name="proj",
    )(u2d, w, bias.reshape(1, n))


def _ret_log_gamma(d, h):
    hh = h if d == 0 else RET_H - 1 - h
    return float(np.log(1.0 - 2.0 ** (-5.0 - hh)))


def _ret_tables():
    c = RET_C
    i = np.arange(c, dtype=np.float64)
    dec = np.zeros((2 * RET_H, c, c), np.float32)
    qd = np.zeros((2 * RET_H, c, RET_DK), np.float32)
    kd = np.zeros((2 * RET_H, c, RET_DK), np.float32)
    for d in range(2):
        tau = i if d == 0 else c - 1 - i
        rel = tau[:, None] - tau[None, :]
        for h in range(RET_H):
            lg = _ret_log_gamma(d, h)
            dec[d * RET_H + h] = np.where(rel >= 0, np.exp(np.maximum(rel, 0.0) * lg), 0.0)
            qd[d * RET_H + h] = np.exp((tau + 1.0) * lg)[:, None]
            kd[d * RET_H + h] = np.exp((c - 1.0 - tau) * lg)[:, None]
    return jnp.asarray(dec), jnp.asarray(qd), jnp.asarray(kd)


def _rope(x, cos2, sin2):
    return x * cos2 + pltpu.roll(x, RET_DK // 2, 1) * sin2


def _ret_kernel(qf_ref, kf_ref, vf_ref, qb_ref, kb_ref, vb_ref, cosf_ref, sinf_ref, cosb_ref, sinb_ref,
                dec_ref, qd_ref, kd_ref, s0_ref, of_ref, ob_ref, sfin_ref, s_ref, *, use_rope, n_chunks):
    n = pl.program_id(1)

    @pl.when(n == 0)
    def _():
        s_ref[...] = s0_ref[0]

    ins = ((qf_ref, kf_ref, vf_ref, cosf_ref, sinf_ref), (qb_ref, kb_ref, vb_ref, cosb_ref, sinb_ref))
    outs = (of_ref, ob_ref)
    chains = [(d, h) for d in range(2) for h in range(RET_H)]
    q, k, v = {}, {}, {}
    for d, h in chains:
        q_ref, k_ref, v_ref, cos_ref, sin_ref = ins[d]
        dk = slice(h * RET_DK, (h + 1) * RET_DK)
        q[d, h] = q_ref[0, :, dk].astype(F32)
        k[d, h] = k_ref[0, :, dk].astype(F32) * (RET_DK ** -0.5)
        if use_rope:
            q[d, h] = _rope(q[d, h], cos_ref[...], sin_ref[...])
            k[d, h] = _rope(k[d, h], cos_ref[...], sin_ref[...])
        v[d, h] = v_ref[0, :, h * RET_DV:(h + 1) * RET_DV]
    sc = {ch: _mm_nt(q[ch], k[ch]) * dec_ref[ch[0] * RET_H + ch[1]] for ch in chains}
    s_old = {ch: s_ref[ch[0], ch[1]] for ch in chains}
    o = {ch: _mm(sc[ch], v[ch]) + _mm(q[ch] * qd_ref[ch[0] * RET_H + ch[1]], s_old[ch]) for ch in chains}
    s_new = {ch: s_old[ch] * float(np.exp(RET_C * _ret_log_gamma(*ch)))
             + _mm_tn(k[ch] * kd_ref[ch[0] * RET_H + ch[1]], v[ch]) for ch in chains}
    for d, h in chains:
        outs[d][0, :, h * RET_DV:(h + 1) * RET_DV] = o[d, h].astype(outs[d].dtype)
        s_ref[d, h] = s_new[d, h]

    @pl.when(n == n_chunks - 1)
    def _():
        sfin_ref[0] = s_ref[...]


def _retention(pr, cos2, sin2, s0, use_rope):
    b, l, _ = pr.shape
    c = RET_C
    nc = l // c
    dec, qd, kd = _ret_tables()
    qw, vw = RET_H * RET_DK, RET_H * RET_DV

    def fw(blk):
        return lambda i, n: (i, n, blk)

    def bw(blk):
        return lambda i, n: (i, nc - 1 - n, blk)

    def const(*shape):
        return pl.BlockSpec(shape, lambda i, n: (0,) * len(shape))

    st = pl.BlockSpec((1, 2, RET_H, RET_DK, RET_DV), lambda i, n: (i, 0, 0, 0, 0))
    in_specs = [
        pl.BlockSpec((1, c, qw), fw(0)), pl.BlockSpec((1, c, qw), fw(1)), pl.BlockSpec((1, c, vw), fw(1)),
        pl.BlockSpec((1, c, qw), bw(0)), pl.BlockSpec((1, c, qw), bw(1)), pl.BlockSpec((1, c, vw), bw(1)),
        pl.BlockSpec((c, RET_DK), lambda i, n: (n, 0)), pl.BlockSpec((c, RET_DK), lambda i, n: (n, 0)),
        pl.BlockSpec((c, RET_DK), lambda i, n: (nc - 1 - n, 0)), pl.BlockSpec((c, RET_DK), lambda i, n: (nc - 1 - n, 0)),
        const(2 * RET_H, c, c), const(2 * RET_H, c, RET_DK), const(2 * RET_H, c, RET_DK), st,
    ]
    out_specs = [pl.BlockSpec((1, c, vw), fw(0)), pl.BlockSpec((1, c, vw), bw(0)), st]
    sd = jax.ShapeDtypeStruct
    return pl.pallas_call(
        functools.partial(_ret_kernel, use_rope=use_rope, n_chunks=nc),
        grid=(b, nc),
        in_specs=in_specs,
        out_specs=out_specs,
        out_shape=[sd((b, l, vw), BF16), sd((b, l, vw), BF16), sd((b, 2, RET_H, RET_DK, RET_DV), F32)],
        scratch_shapes=[pltpu.VMEM((2, RET_H, RET_DK, RET_DV), F32)],
        compiler_params=_cparams(("arbitrary", "arbitrary")),
        name="retention",
    )(pr, pr, pr, pr, pr, pr, cos2, sin2, cos2, sin2, dec, qd, kd, s0)


def _gla_boundary(b, lv, reverse, rolls):
    c, w = b.shape
    m = 1 << lv
    if m >= SUBLANES:
        parts = []
        for p0 in range(0, c, 2 * m):
            e = p0 + m if reverse else p0 + m - 1
            parts.append(jnp.broadcast_to(b[e:e + 1, :], (2 * m, w)))
        return jnp.concatenate(parts, axis=0)

    def rolled(s):
        s %= c
        if s not in rolls:
            rolls[s] = b if s == 0 else pltpu.roll(b, s, 0)
        return rolls[s]

    row = lax.broadcasted_iota(jnp.int32, (c, w), 0)
    r = jnp.bitwise_and(row, m - 1)
    upper = jnp.bitwise_and(jnp.right_shift(row, lv), 1) == 1
    out = b
    for t in range(m):
        if reverse:
            out = jnp.where(upper & (r == t), rolled(t), out)
            out = jnp.where(jnp.logical_not(upper) & (r == t), rolled(-(m - t)), out)
        else:
            out = jnp.where(upper & (r == t), rolled(t + 1), out)
            out = jnp.where(jnp.logical_not(upper) & (r == t), rolled(-(m - 1 - t)), out)
    return out


def _gla_kernel(qf_ref, kf_ref, vf_ref, lrf_ref, qb_ref, kb_ref, vb_ref, lrb_ref, w2_ref, b2_ref, s0_ref,
                of_ref, ob_ref, sfin_ref, st_ref, *, n_chunks):
    c = GLA_C
    n = pl.program_id(1)

    @pl.when(n == 0)
    def _():
        st_ref[...] = s0_ref[0]

    ins = ((qf_ref, kf_ref, vf_ref, lrf_ref), (qb_ref, kb_ref, vb_ref, lrb_ref))
    outs = (of_ref, ob_ref)
    hw = GLA_H * GLA_DK
    row = lax.broadcasted_iota(jnp.int32, (c, hw), 0)
    ri = lax.broadcasted_iota(jnp.int32, (c, c), 0)
    ci = lax.broadcasted_iota(jnp.int32, (c, c), 1)
    q, k, v, b_inc, b_rest, tot, ql, kl = {}, {}, {}, {}, {}, {}, {}, {}
    for d in range(2):
        reverse = d == 1
        q_ref, k_ref, v_ref, lr_ref = ins[d]
        q[d] = q_ref[0].astype(F32) * (GLA_DK ** -0.5)
        k[d] = k_ref[0].astype(F32)
        v[d] = v_ref[0]
        gate = _log_sigmoid(_mm_split(lr_ref[0], w2_ref[d, 0], w2_ref[d, 1]) + b2_ref[d])
        gate = gate * (1.0 / GLA_GATE_NORM)
        b = _scan_rows(gate, reverse)
        tot[d] = b[0:1] if reverse else b[c - 1:c]
        b_inc[d], b_rest[d] = b, tot[d] - b
        rolls = {}
        for lv in range(GLA_LEVELS):
            be = _gla_boundary(b, lv, reverse, rolls)
            upper = jnp.bitwise_and(jnp.right_shift(row, lv), 1) == 1
            second = jnp.logical_not(upper) if reverse else upper
            e = jnp.exp(jnp.where(second, b - be, be - b))
            ql[d, lv] = jnp.where(second, q[d] * e, 0.0)
            kl[d, lv] = jnp.where(second, 0.0, k[d] * e)
    chains = [(d, h) for d in range(2) for h in range(GLA_H)]

    def dk(h):
        return slice(h * GLA_DK, (h + 1) * GLA_DK)

    def dv(h):
        return slice(h * GLA_DV, (h + 1) * GLA_DV)

    attn = {(d, h): jnp.where(ri == ci, jnp.sum(q[d][:, dk(h)] * k[d][:, dk(h)], axis=-1, keepdims=True), 0.0)
            for d, h in chains}
    for lv in range(GLA_LEVELS):
        same = jnp.right_shift(ri, lv + 1) == jnp.right_shift(ci, lv + 1)
        for d, h in chains:
            attn[d, h] = attn[d, h] + jnp.where(same, _mm_nt(ql[d, lv][:, dk(h)], kl[d, lv][:, dk(h)]), 0.0)
    st_old = {ch: st_ref[ch[0], ch[1]] for ch in chains}
    o = {(d, h): _mm(attn[d, h], v[d][:, dv(h)]) + _mm_nt(q[d][:, dk(h)] * jnp.exp(b_inc[d][:, dk(h)]), st_old[d, h])
         for d, h in chains}
    st_new = {(d, h): st_old[d, h] * jnp.exp(tot[d][:, dk(h)])
              + _mm_tn(v[d][:, dv(h)], k[d][:, dk(h)] * jnp.exp(b_rest[d][:, dk(h)])) for d, h in chains}
    for d, h in chains:
        outs[d][0, :, dv(h)] = o[d, h].astype(outs[d].dtype)
        st_ref[d, h] = st_new[d, h]

    @pl.when(n == n_chunks - 1)
    def _():
        sfin_ref[0] = st_ref[...]


def _gla(pg, plr, w2p, b2, s0t):
    b, l, _ = pg.shape
    c = GLA_C
    nc = l // c
    qw, vw = GLA_H * GLA_DK, GLA_H * GLA_DV

    def fw(blk):
        return lambda i, n: (i, n, blk)

    def bw(blk):
        return lambda i, n: (i, nc - 1 - n, blk)

    st = pl.BlockSpec((1, 2, GLA_H, GLA_DV, GLA_DK), lambda i, n: (i, 0, 0, 0, 0))
    in_specs = [
        pl.BlockSpec((1, c, qw), fw(0)), pl.BlockSpec((1, c, qw), fw(1)), pl.BlockSpec((1, c, vw), fw(1)),
        pl.BlockSpec((1, c, LANES), fw(0)),
        pl.BlockSpec((1, c, qw), bw(0)), pl.BlockSpec((1, c, qw), bw(1)), pl.BlockSpec((1, c, vw), bw(1)),
        pl.BlockSpec((1, c, LANES), bw(0)),
        pl.BlockSpec((2, 2, LANES, qw), lambda i, n: (0, 0, 0, 0)),
        pl.BlockSpec((2, 1, qw), lambda i, n: (0, 0, 0)),
        st,
    ]
    out_specs = [pl.BlockSpec((1, c, vw), fw(0)), pl.BlockSpec((1, c, vw), bw(0)), st]
    sd = jax.ShapeDtypeStruct
    return pl.pallas_call(
        functools.partial(_gla_kernel, n_chunks=nc),
        grid=(b, nc),
        in_specs=in_specs,
        out_specs=out_specs,
        out_shape=[sd((b, l, vw), BF16), sd((b, l, vw), BF16), sd((b, 2, GLA_H, GLA_DV, GLA_DK), F32)],
        scratch_shapes=[pltpu.VMEM((2, GLA_H, GLA_DV, GLA_DK), F32)],
        compiler_params=_cparams(("arbitrary", "arbitrary")),
        name="gla",
    )(pg, pg, pg, plr, pg, pg, pg, plr, w2p, b2, s0t)


def _gla_gate_params(w2, b2):
    w2p = jnp.pad(w2, ((0, 0), (0, LANES - GLA_LOWRANK), (0, 0)))
    return jnp.stack(_split2(w2p), axis=1), b2.reshape(2, 1, GLA_H * GLA_DK)


def _head_sum(x):
    i = lax.broadcasted_iota(jnp.int32, (LANES, LANES), 0)
    j = lax.broadcasted_iota(jnp.int32, (LANES, LANES), 1)
    ones = jnp.where(jnp.right_shift(i, 6) == jnp.right_shift(j, 6), 1.0, 0.0).astype(BF16)
    out = []
    for t in range(x.shape[1] // LANES):
        parts = _split2(x[:, t * LANES:(t + 1) * LANES])
        out.append(sum(jnp.dot(p, ones, preferred_element_type=F32) for p in parts))
    return jnp.concatenate(out, axis=1)


RW_TM = 256
HALO = 2 * SUBLANES
N_RW_LORA = RW_W_LORA + RW_A_LORA + RW_G_LORA


def _rw_prep_kernel(x_ref, xp_ref, xn_ref, lo_ref, lop_ref, lon_ref, taps_ref, ltaps_ref, kkp_ref, ka_ref, rk_ref,
                    w0_ref, wb_ref, a0_ref, ab_ref, gb_ref, r_ref, v_ref, kk_ref, lw_ref, kd_ref, be_ref, bonus_ref,
                    g_ref, xbuf, lbuf, *, n_tiles):
    tm = RW_TM
    j = pl.program_id(1)
    first, last = j == 0, j == n_tiles - 1
    xbuf[0:HALO, :] = jnp.where(first, jnp.zeros_like(xp_ref[0]), xp_ref[0])
    xbuf[HALO:HALO + tm, :] = x_ref[0]
    xbuf[HALO + tm:2 * HALO + tm, :] = jnp.where(last, jnp.zeros_like(xn_ref[0]), xn_ref[0])
    ri = lax.broadcasted_iota(jnp.int32, (tm, tm + 2 * HALO), 0)
    ci = lax.broadcasted_iota(jnp.int32, (tm, tm + 2 * HALO), 1)
    pick_prev = jnp.where(ci == ri + (HALO - 1), 1.0, 0.0).astype(BF16)
    pick_next = jnp.where(ci == ri + (HALO + 1), 1.0, 0.0).astype(BF16)

    def shifted_main(c0, c1):
        x_all = xbuf[:, c0:c1]
        return (taps_ref[0:1, c0:c1] * jnp.dot(pick_prev, x_all, preferred_element_type=F32)
                + taps_ref[1:2, c0:c1] * x_ref[0, :, c0:c1].astype(F32)
                + taps_ref[2:3, c0:c1] * jnp.dot(pick_next, x_all, preferred_element_type=F32))

    lbuf[HALO:HALO + tm, :] = lo_ref[0]
    lbuf[0:HALO, :] = jnp.where(first, 0.0, lop_ref[0])
    lbuf[HALO + tm:2 * HALO + tm, :] = jnp.where(last, 0.0, lon_ref[0])
    lora = (ltaps_ref[0:1, :] * lbuf[HALO - 1:HALO - 1 + tm, :] + ltaps_ref[1:2, :] * lbuf[HALO:HALO + tm, :]
            + ltaps_ref[2:3, :] * lbuf[HALO + 1:HALO + 1 + tm, :])
    r = shifted_main(0, D)
    k = shifted_main(D, 2 * D)
    v = shifted_main(2 * D, 3 * D)
    xw = lora[:, 0:RW_W_LORA]
    xa = lora[:, RW_W_LORA:RW_W_LORA + RW_A_LORA]
    xg = lora[:, RW_W_LORA + RW_A_LORA:]
    r_ref[0] = r.astype(r_ref.dtype)
    v_ref[0] = v.astype(v_ref.dtype)
    kk = k * kkp_ref[...]
    kk = kk * lax.rsqrt(jnp.maximum(_head_sum(kk * kk), 1e-24))
    kk_ref[0] = kk.astype(kk_ref.dtype)
    g_ref[0] = _mm(_sigmoid(xg), gb_ref[...]).astype(g_ref.dtype)
    wh = jnp.tanh(xw)
    kd_sum = jnp.zeros_like(k)
    for d in range(2):
        w = -_softplus(-(w0_ref[d:d + 1, :] + _mm(wh, wb_ref[d]))) - 0.5
        lw_ref[d, 0] = -jnp.exp(w)
        a = _sigmoid(a0_ref[d:d + 1, :] + _mm(xa, ab_ref[d]))
        kd = k * (1.0 + (a - 1.0) * ka_ref[...])
        kd_ref[d, 0] = kd.astype(kd_ref.dtype)
        be_ref[d, 0] = (kk * a).astype(be_ref.dtype)
        kd_sum = kd_sum + kd
    bonus_ref[0] = (_head_sum(r * kd_sum * rk_ref[...]) * v).astype(bonus_ref.dtype)


def _rw_prep(rw, rlo, taps, kkp, ka, rk, w0, wb, a0, ab, gb):
    b, l, _ = rw.shape
    tm = RW_TM
    nt = l // tm
    hb = tm // HALO
    tok = pl.BlockSpec((1, tm, D), lambda i, j: (i, j, 0))
    tokd = pl.BlockSpec((2, 1, tm, D), lambda i, j: (0, i, j, 0))

    def full(*shape):
        return pl.BlockSpec(shape, lambda i, j: (0,) * len(shape))

    def halo_specs(n):
        return [pl.BlockSpec((1, tm, n), lambda i, j: (i, j, 0)),
                pl.BlockSpec((1, HALO, n), lambda i, j: (i, jnp.maximum(j * hb - 1, 0), 0)),
                pl.BlockSpec((1, HALO, n), lambda i, j: (i, jnp.minimum((j + 1) * hb, l // HALO - 1), 0))]

    in_specs = halo_specs(3 * D) + halo_specs(N_RW_LORA) + [
        full(3, 3 * D), full(3, N_RW_LORA), full(1, D), full(1, D), full(1, D),
        full(2, D), full(2, RW_W_LORA, D), full(2, D), full(2, RW_A_LORA, D), full(RW_G_LORA, D),
    ]
    sd = jax.ShapeDtypeStruct
    tok_o, dir_o, dir_f32 = sd((b, l, D), BF16), sd((2, b, l, D), BF16), sd((2, b, l, D), F32)
    return pl.pallas_call(
        functools.partial(_rw_prep_kernel, n_tiles=nt),
        grid=(b, nt),
        in_specs=in_specs,
        out_specs=[tok, tok, tok, tokd, tokd, tokd, tok, tok],
        out_shape=[tok_o, tok_o, tok_o, dir_f32, dir_o, dir_o, tok_o, tok_o],
        scratch_shapes=[pltpu.VMEM((tm + 2 * HALO, 3 * D), BF16), pltpu.VMEM((tm + 2 * HALO, N_RW_LORA), F32)],
        compiler_params=_cparams(("arbitrary", "arbitrary")),
        name="rwkv_prep",
    )(rw, rw, rw, rlo, rlo, rlo, taps[:, :3 * D], taps[:, 3 * D:], kkp.reshape(1, D), ka.reshape(1, D),
      rk.reshape(1, D), w0, wb, a0, ab, gb)


def _tile_rows(x, n):
    return jnp.concatenate([x] * n, axis=0)


def _rw_scan_kernel(rf_ref, vf_ref, kkf_ref, lwf_ref, kdf_ref, bef_ref, rb_ref, vb_ref, kkb_ref, lwb_ref, kdb_ref,
                    beb_ref, s0_ref, yf_ref, yb_ref, sfin_ref, zt_ref, *, n_tiles):
    c, hw, sw = RW_C, RW_GW, RW_SW
    pk = RW_G * c
    n_groups, n_sub = RW_H // RW_G, RW_GW // RW_SW
    n_chunks = RW_TL // c
    j = pl.program_id(1)

    @pl.when(j == 0)
    def _():
        zt_ref[...] = jnp.zeros_like(zt_ref)
        for bb in range(RW_BB):
            for d in range(2):
                for g in range(RW_H // 2):
                    for jj in range(2):
                        blk = slice(RW_N * jj, RW_N * (jj + 1))
                        zt_ref[bb, d, g, blk, blk] = s0_ref[bb, d, 2 * g + jj]

    ins = ((rf_ref, vf_ref, kkf_ref, lwf_ref, kdf_ref, bef_ref), (rb_ref, vb_ref, kkb_ref, lwb_ref, kdb_ref, beb_ref))
    outs = (yf_ref, yb_ref)
    ri = lax.broadcasted_iota(jnp.int32, (c, pk), 0)
    cs = jnp.bitwise_and(lax.broadcasted_iota(jnp.int32, (c, pk), 1), c - 1)
    eye = (ri == cs).astype(F32)
    strict, incl, lvl = [], [], []
    for d in range(2):
        ti = ri if d == 0 else c - 1 - ri
        ts = cs if d == 0 else c - 1 - cs
        strict.append(ts < ti)
        incl.append(ts <= ti)
        lv_masks = []
        for lv in range(RW_LEVELS):
            same = jnp.right_shift(ti, lv + 1) == jnp.right_shift(ts, lv + 1)
            lower = ((jnp.bitwise_and(jnp.right_shift(ti, lv), 1) == 1)
                     & (jnp.bitwise_and(jnp.right_shift(ts, lv), 1) == 0))
            lv_masks.append(same & lower)
        lvl.append(lv_masks)
    bi = lax.broadcasted_iota(jnp.int32, (pk, pk), 0)
    bj = lax.broadcasted_iota(jnp.int32, (pk, pk), 1)
    bd_p = jnp.right_shift(bi, 5) == jnp.right_shift(bj, 5)
    zi = lax.broadcasted_iota(jnp.int32, (sw, sw), 0)
    zj = lax.broadcasted_iota(jnp.int32, (sw, sw), 1)
    bd_z = jnp.right_shift(zi, 6) == jnp.right_shift(zj, 6)
    wi = lax.broadcasted_iota(jnp.int32, (pk, hw), 0)
    wj = lax.broadcasted_iota(jnp.int32, (pk, hw), 1)
    bd_w = jnp.right_shift(wi, 5) == jnp.right_shift(wj, 6)

    def bdp(x):
        return jnp.where(bd_p, _tile_rows(x, RW_G), 0.0)

    def bdw(x):
        return jnp.where(bd_w, _tile_rows(x, RW_G), 0.0)

    chains = [(bb, d, half) for bb in range(RW_BB) for d in range(2) for half in range(n_groups)]

    def each(fn):
        return {ch: fn(ch) for ch in chains}

    def chunk(i, carry):
        rows = (pl.ds(pl.multiple_of(i * c, c), c), pl.ds(pl.multiple_of((n_chunks - 1 - i) * c, c), c))

        def load(ch):
            bb, d, half = ch
            ln = slice(half * hw, (half + 1) * hw)
            refs = ins[d]
            return (refs[0][bb, rows[d], ln].astype(F32), refs[1][bb, rows[d], ln].astype(F32),
                    refs[2][bb, rows[d], ln].astype(F32), refs[3][0, bb, rows[d], ln],
                    refs[4][0, bb, rows[d], ln].astype(F32), refs[5][0, bb, rows[d], ln].astype(F32))

        x = each(load)
        r, v, kk = each(lambda ch: x[ch][0]), each(lambda ch: x[ch][1]), each(lambda ch: x[ch][2])
        lw, kd, be = each(lambda ch: x[ch][3]), each(lambda ch: x[ch][4]), each(lambda ch: x[ch][5])
        b = each(lambda ch: _scan_rows(lw[ch], ch[1] == 1))
        tot = each(lambda ch: b[ch][0:1] if ch[1] == 1 else b[ch][c - 1:c])
        en = each(lambda ch: jnp.exp(-b[ch]))
        es = each(lambda ch: jnp.exp(tot[ch] - b[ch]))
        lhs = each(lambda ch: jnp.concatenate([kk[ch] * jnp.exp(b[ch] - lw[ch]), r[ch] * jnp.exp(b[ch])], axis=0))
        gb = each(lambda ch: _mm_nt(lhs[ch], bdw(be[ch] * en[ch])))
        gk = each(lambda ch: _mm_nt(lhs[ch], bdw(kd[ch] * en[ch])))
        a_ab = each(lambda ch: jnp.where(strict[ch[1]], gb[ch][0:c], 0.0))
        a_rb = each(lambda ch: jnp.where(incl[ch[1]], gb[ch][c:], 0.0))
        a_ak = each(lambda ch: jnp.where(strict[ch[1]], gk[ch][0:c], 0.0))
        a_rk = each(lambda ch: jnp.where(incl[ch[1]], gk[ch][c:], 0.0))
        t_inv = each(lambda ch: eye - jnp.where(lvl[ch[1]][0], a_ab[ch], 0.0))
        for lv in range(1, RW_LEVELS):
            xm = each(lambda ch: _mm(jnp.where(lvl[ch[1]][lv], a_ab[ch], 0.0), bdp(t_inv[ch])))
            t_inv = each(lambda ch: t_inv[ch] - _mm(t_inv[ch], bdp(xm[ch])))
        zt_old = {(ch, g): zt_ref[ch[0], ch[1], n_sub * ch[2] + g] for ch in chains for g in range(n_sub)}
        zz = each(lambda ch: jnp.concatenate(
            [_mm_nt(lhs[ch][:, g * sw:(g + 1) * sw], zt_old[ch, g]) for g in range(n_sub)], axis=1))
        vbd = each(lambda ch: bdw(v[ch]))
        u = each(lambda ch: _mm(t_inv[ch], bdw(-(zz[ch][0:c] + _mm(a_ak[ch], vbd[ch])))))
        y = each(lambda ch: zz[ch][c:] + _mm(a_rb[ch], bdw(u[ch])) + _mm(a_rk[ch], vbd[ch]))
        uv = each(lambda ch: jnp.concatenate([u[ch], v[ch]], axis=0))
        bk = each(lambda ch: jnp.concatenate([be[ch] * es[ch], kd[ch] * es[ch]], axis=0))
        etot = each(lambda ch: jnp.exp(tot[ch]))
        zt_new = {}
        for ch in chains:
            for g in range(n_sub):
                gl = slice(g * sw, (g + 1) * sw)
                upd = jnp.where(bd_z, _mm_tn(uv[ch][:, gl], bk[ch][:, gl]), 0.0)
                zt_new[ch, g] = zt_old[ch, g] * etot[ch][:, gl] + upd
        for ch in chains:
            bb, d, half = ch
            outs[d][bb, rows[d], half * hw:(half + 1) * hw] = y[ch].astype(outs[d].dtype)
            for g in range(n_sub):
                zt_ref[bb, d, n_sub * half + g] = zt_new[ch, g]
        return carry

    lax.fori_loop(0, n_chunks, chunk, 0)

    @pl.when(j == n_tiles - 1)
    def _():
        for bb in range(RW_BB):
            for d in range(2):
                for g in range(RW_H // 2):
                    for jj in range(2):
                        blk = slice(RW_N * jj, RW_N * (jj + 1))
                        sfin_ref[bb, d, 2 * g + jj] = zt_ref[bb, d, g, blk, blk]


def _rw_scan(r, v, kk, lw, kd, be, s0):
    b, l, _ = r.shape
    tl, nb = RW_TL, RW_BB
    nt = l // tl
    tok_f = pl.BlockSpec((nb, tl, D), lambda i, j: (i, j, 0))
    tok_b = pl.BlockSpec((nb, tl, D), lambda i, j: (i, nt - 1 - j, 0))
    dir_f = pl.BlockSpec((1, nb, tl, D), lambda i, j: (0, i, j, 0))
    dir_b = pl.BlockSpec((1, nb, tl, D), lambda i, j: (1, i, nt - 1 - j, 0))
    st = pl.BlockSpec((nb, 2, RW_H, RW_N, RW_N), lambda i, j: (i, 0, 0, 0, 0))
    sd = jax.ShapeDtypeStruct
    return pl.pallas_call(
        functools.partial(_rw_scan_kernel, n_tiles=nt),
        grid=(b // nb, nt),
        in_specs=[tok_f, tok_f, tok_f, dir_f, dir_f, dir_f, tok_b, tok_b, tok_b, dir_b, dir_b, dir_b, st],
        out_specs=[tok_f, tok_b, st],
        out_shape=[sd((b, l, D), BF16), sd((b, l, D), BF16), sd((b, 2, RW_H, RW_N, RW_N), F32)],
        scratch_shapes=[pltpu.VMEM((nb, 2, RW_H // 2, RW_SW, RW_SW), F32)],
        compiler_params=_cparams(("arbitrary", "arbitrary")),
        name="rwkv_scan",
    )(r, v, kk, lw, kd, be, r, v, kk, lw, kd, be, s0)


MERGE_TM = 128
ROUTER_LANES = LANES


def _merge_kernel(orf_ref, orb_ref, rg_ref, ogf_ref, ogb_ref, gg_ref, yf_ref, yb_ref, bonus_ref, grw_ref,
                  gates_ref, x_ref, gt_ref, sh_ref, sc_ref, wbr_ref, wout_ref, rgn_ref, ggn_ref, gng_ref, gnb_ref,
                  l1g_ref, l1b_ref, wr_ref, br_ref, x1_ref, u2_ref, lg_ref):
    f32 = lambda ref: ref[0].astype(F32)
    o_ret = f32(orf_ref) + f32(orb_ref)
    o_gla = f32(ogf_ref) + f32(ogb_ref)
    y_ret, y_gla = [], []
    for h in range(RET_H):
        hs = slice(h * RET_DV, (h + 1) * RET_DV)
        y_ret.append(_ln(o_ret[:, hs]))
        og = o_gla[:, hs]
        y_gla.append(og * lax.rsqrt(jnp.mean(og * og, axis=-1, keepdims=True) + EPS))
    z_ret = _silu(f32(rg_ref)) * (jnp.concatenate(y_ret, axis=1) * rgn_ref[...])
    z_gla = _silu(f32(gg_ref)) * (jnp.concatenate(y_gla, axis=1) * ggn_ref[...])
    y = f32(yf_ref) + f32(yb_ref)
    mu = _head_sum(y) * (1.0 / RW_N)
    yc = y - mu
    var = _head_sum(yc * yc) * (1.0 / RW_N)
    z_rw = (yc * lax.rsqrt(var + EPS) * gng_ref[...] + gnb_ref[...] + f32(bonus_ref)) * f32(grw_ref)
    gates = f32(gates_ref)
    mixed = (gates[:, 0:D] * _mm(z_ret, wbr_ref[0]) + gates[:, D:2 * D] * _mm(z_gla, wbr_ref[1])
             + gates[:, 2 * D:] * _mm(z_rw, wbr_ref[2]))
    mix = _mm(mixed, wout_ref[...])
    x1 = _ln(ALPHA * x_ref[0] + gt_ref[0] * mix) * l1g_ref[...] + l1b_ref[...]
    x1_ref[0] = x1
    u2 = _ln(x1) * (1.0 + sc_ref[0]) + sh_ref[0]
    u2_ref[0] = u2.astype(u2_ref.dtype)
    lg_ref[0] = (_mm_split(u2, wr_ref[0], wr_ref[1]) + br_ref[...]).T


def _merge(orf, orb, pr, ogf, ogb, pg, yf, yb, bonus, grw, gates, x, gt1, sh2, sc2, wbr, wout, rgn, ggn, gng, gnb,
           l1g, l1b, wr, br):
    b, l, _ = x.shape
    tm = MERGE_TM
    tok = pl.BlockSpec((1, tm, D), lambda i, j: (i, j, 0))
    gate_blk = pl.BlockSpec((1, tm, D), lambda i, j: (i, j, 2))
    vec = pl.BlockSpec((1, 1, D), lambda i, j: (i, 0, 0))
    row = pl.BlockSpec((1, D), lambda i, j: (0, 0))
    in_specs = [
        tok, tok, gate_blk, tok, tok, gate_blk, tok, tok, tok, tok,
        pl.BlockSpec((1, tm, 3 * D), lambda i, j: (i, j, 0)),
        tok, vec, vec, vec,
        pl.BlockSpec((3, D, D), lambda i, j: (0, 0, 0)),
        pl.BlockSpec((D, D), lambda i, j: (0, 0)),
        row, row, row, row, row, row,
        pl.BlockSpec((2, D, ROUTER_LANES), lambda i, j: (0, 0, 0)),
        pl.BlockSpec((1, ROUTER_LANES), lambda i, j: (0, 0)),
    ]
    sd = jax.ShapeDtypeStruct
    r1 = lambda a: a.reshape(1, D)
    return pl.pallas_call(
        _merge_kernel,
        grid=(b, l // tm),
        in_specs=in_specs,
        out_specs=[tok, tok, pl.BlockSpec((1, ROUTER_LANES, tm), lambda i, j: (i, 0, j))],
        out_shape=[sd((b, l, D), F32), sd((b, l, D), BF16), sd((b, ROUTER_LANES, l), F32)],
        compiler_params=_cparams(("arbitrary", "arbitrary")),
        name="merge",
    )(orf, orb, pr, ogf, ogb, pg, yf, yb, bonus, grw, gates, x, gt1, sh2, sc2, wbr, wout,
      r1(rgn), r1(ggn), r1(gng), r1(gnb), r1(l1g), r1(l1b), wr, br)


MOE_TM = 1024


def _moe_block_rows(tm):
    rows = tm / N_GROUPS + 4.0 * np.sqrt(tm * (N_GROUPS - 1.0)) / N_GROUPS
    bf16_tile = 2 * SUBLANES
    return int(min(tm, -(-rows // bf16_tile) * bf16_tile))


def _first_argmax_rows(x, row):
    m = jnp.max(x, axis=0, keepdims=True)
    idx = jnp.min(jnp.where(x == m, row, x.shape[0]), axis=0, keepdims=True)
    return m, idx


def _routing(lt):
    tm = lt.shape[1]
    gl = lt[0:N_GROUPS]
    gmax, gidx = _first_argmax_rows(gl, lax.broadcasted_iota(jnp.int32, (N_GROUPS, tm), 0))
    g_w = 1.0 / jnp.sum(jnp.exp(gl - gmax), axis=0, keepdims=True)
    row = lax.broadcasted_iota(jnp.int32, (N_EXPERTS, tm), 0)
    neg = -jnp.inf
    el = jnp.where(jnp.right_shift(row, 2) == gidx, lt[N_GROUPS:N_GROUPS + N_EXPERTS], neg)
    m1, i1 = _first_argmax_rows(el, row)
    m2, i2 = _first_argmax_rows(jnp.where(row == i1, neg, el), row)
    e2 = jnp.exp(m2 - m1)
    w1 = 1.0 / (1.0 + e2)
    return gidx, g_w * (jnp.where(row == i1, w1, 0.0) + jnp.where(row == i2, e2 * w1, 0.0))


def _moe_kernel(u_ref, lt_ref, wg_ref, wu_ref, wd_ref, o_ref, acc_ref):
    g = pl.program_id(2)
    tm = acc_ref.shape[0]
    blk = _moe_block_rows(tm)

    @pl.when(g == 0)
    def _():
        acc_ref[...] = jnp.zeros_like(acc_ref)

    gidx, comb = _routing(lt_ref[0])
    member = jnp.broadcast_to((gidx == g).astype(F32), (SUBLANES, tm))
    lane = lax.broadcasted_iota(jnp.int32, (SUBLANES, tm), 1)
    count = member
    sh = 1
    while sh < tm:
        count = count + jnp.where(lane >= sh, pltpu.roll(count, sh, 1), 0.0)
        sh *= 2
    pos = jnp.where(member > 0.0, count - 1.0, -1.0).astype(jnp.int32)[0:1]
    n_tok = jnp.max(count).astype(jnp.int32)
    comb_pad = jnp.concatenate([comb, jnp.zeros((LANES - N_EXPERTS, tm), F32)], axis=0)
    comb_parts = _split2(comb_pad)
    u = u_ref[0]
    sel_lane = lax.broadcasted_iota(jnp.int32, (blk, LANES), 1)

    def body(i, carry):
        slot = lax.broadcasted_iota(jnp.int32, (blk, tm), 0) + i * blk
        onehot = jnp.where(slot == pos, 1.0, 0.0).astype(BF16)
        xg = jnp.dot(onehot, u, preferred_element_type=F32).astype(BF16)
        cw = sum(lax.dot_general(onehot, c, (((1,), (1,)), ((), ())), preferred_element_type=F32)
                 for c in comb_parts)
        y = jnp.zeros((blk, D), F32)
        for e in range(EPG):
            hid = (_silu(jnp.dot(xg, wg_ref[e], preferred_element_type=F32))
                   * jnp.dot(xg, wu_ref[e], preferred_element_type=F32))
            c_e = jnp.sum(jnp.where(sel_lane == g * EPG + e, cw, 0.0), axis=-1, keepdims=True)
            y = y + c_e * _mm(hid, wd_ref[e])
        acc_ref[...] += _mm_tn(onehot, y)
        return carry

    lax.fori_loop(0, (n_tok + blk - 1) // blk, body, 0)

    @pl.when(g == N_GROUPS - 1)
    def _():
        o_ref[0] = acc_ref[...].astype(o_ref.dtype)


def _moe(u2, logits_t, wg, wu, wd):
    b, l, _ = u2.shape
    tm = min(MOE_TM, l)
    tok = pl.BlockSpec((1, tm, D), lambda i, j, g: (i, j, 0))
    in_specs = [
        tok,
        pl.BlockSpec((1, ROUTER_LANES, tm), lambda i, j, g: (i, 0, j)),
        pl.BlockSpec((EPG, D, EXPERT_FF), lambda i, j, g: (g, 0, 0)),
        pl.BlockSpec((EPG, D, EXPERT_FF), lambda i, j, g: (g, 0, 0)),
        pl.BlockSpec((EPG, EXPERT_FF, D), lambda i, j, g: (g, 0, 0)),
    ]
    return pl.pallas_call(
        _moe_kernel,
        grid=(b, l // tm, N_GROUPS),
        in_specs=in_specs,
        out_specs=tok,
        out_shape=jax.ShapeDtypeStruct((b, l, D), BF16),
        scratch_shapes=[pltpu.VMEM((tm, D), F32)],
        compiler_params=_cparams(("arbitrary",) * 3),
        name="moe",
    )(u2, logits_t, wg, wu, wd)


def _post_kernel(x_ref, m_ref, gt_ref, l2g_ref, l2b_ref, sh_ref, sc_ref, o_ref, u_ref):
    x2 = _ln(ALPHA * x_ref[0] + gt_ref[0] * m_ref[0].astype(F32)) * l2g_ref[...] + l2b_ref[...]
    o_ref[0] = x2
    u_ref[0] = (_ln(x2) * (1.0 + sc_ref[0]) + sh_ref[0]).astype(u_ref.dtype)


def _post(x1, moe_out, gt2, l2g, l2b, sh_next, sc_next):
    b, l, _ = x1.shape
    tm = 512 if l % 512 == 0 else 256
    tok = pl.BlockSpec((1, tm, D), lambda i, j: (i, j, 0))
    vec = pl.BlockSpec((1, 1, D), lambda i, j: (i, 0, 0))
    row = pl.BlockSpec((1, D), lambda i, j: (0, 0))
    sd = jax.ShapeDtypeStruct
    return pl.pallas_call(
        _post_kernel,
        grid=(b, l // tm),
        in_specs=[tok, tok, vec, row, row, vec, vec],
        out_specs=[tok, tok],
        out_shape=[sd((b, l, D), F32), sd((b, l, D), BF16)],
        compiler_params=_cparams(("arbitrary", "arbitrary")),
        name="post_ln",
    )(x1, moe_out, gt2, l2g.reshape(1, D), l2b.reshape(1, D), sh_next, sc_next)


def _rope_tables(l):
    t = np.arange(l)
    quarter = RET_DK // 4
    freqs = (np.float32(ROPE_BASE) ** (-np.arange(quarter, dtype=np.float32) / quarter)).astype(np.float32)
    rows = (t // GRID_W).astype(np.float32)
    cols = (t % GRID_W).astype(np.float32)
    ang = jnp.asarray(np.concatenate([rows[:, None] * freqs, cols[:, None] * freqs], -1))
    cos, sin = jnp.cos(ang), jnp.sin(ang)
    return jnp.concatenate([cos, cos], -1), jnp.concatenate([-sin, sin], -1)


def _layer_weights(p):
    n_rg = 2 * (RET_H * RET_DK + RET_H * RET_DV)
    w_in = p["w_in"]
    w_lr = jnp.pad(w_in[:, 2 * n_rg:2 * n_rg + GLA_LOWRANK], ((0, 0), (0, LANES - GLA_LOWRANK)))
    w2p, b2p = _gla_gate_params(p["gla_w2"], p["gla_b"])
    pad = ROUTER_LANES - N_GROUPS - N_EXPERTS
    return dict(
        p,
        w_ret=w_in[:, :n_rg].astype(BF16),
        w_gla=w_in[:, n_rg:2 * n_rg].astype(BF16),
        w_glr=w_lr.astype(BF16),
        w_rw=w_in[:, 2 * n_rg + GLA_LOWRANK:2 * n_rg + GLA_LOWRANK + 3 * D].astype(BF16),
        w_rlo=w_in[:, 2 * n_rg + GLA_LOWRANK + 3 * D:].astype(BF16),
        w_merge=p["w_merge"].astype(BF16),
        w_br=p["w_br"].astype(BF16),
        w_out=p["w_out"].astype(BF16),
        gla_w2p=w2p, gla_b2p=b2p,
        w_router=jnp.stack(_split2(jnp.pad(jnp.concatenate([p["w_rg"], p["w_re"]], axis=1), ((0, 0), (0, pad))))),
        b_router=jnp.pad(jnp.concatenate([p["b_rg"], p["b_re"]]), (0, pad)).reshape(1, ROUTER_LANES),
        w_eg=p["w_eg"].astype(BF16), w_eu=p["w_eu"].astype(BF16), w_ed=p["w_ed"].astype(BF16),
    )


def _split_mod(mod):
    return [m.reshape(mod.shape[0], 1, D) for m in jnp.split(mod, 6, axis=-1)]


def _layer(x, u, mod, mod_next, s_ret, s_gla_t, s_rw, rope, p):
    b, l, _ = x.shape
    _, _, gt1, sh2, sc2, gt2 = _split_mod(mod)
    sh_next, sc_next = _split_mod(mod_next)[:2]
    u2d = u.reshape(b * l, D)

    def proj(w, tn, bias=None, **kw):
        bias = jnp.zeros((w.shape[1],), F32) if bias is None else bias
        return _proj(u2d, w, bias, tn, **kw).reshape(b, l, -1)

    pr = proj(p["w_ret"], 1024)
    pg = proj(p["w_gla"], 1024)
    plr = proj(p["w_glr"], LANES, out_dtype=F32)
    rw = proj(p["w_rw"], 1024)
    rlo = proj(p["w_rlo"], N_RW_LORA, out_dtype=F32)
    gates = proj(p["w_merge"], 1024, bias=p["b_merge"], act="sigmoid")

    cos2, sin2 = rope if rope is not None else (jnp.ones((l, RET_DK), F32), jnp.zeros((l, RET_DK), F32))
    orf, orb, ret_fin = _retention(pr, cos2, sin2, s_ret, rope is not None)
    ogf, ogb, gla_fin_t = _gla(pg, plr, p["gla_w2p"], p["gla_b2p"], s_gla_t)
    r, v, kk, lw, kd, be, bonus, grw = _rw_prep(rw, rlo, p["rwkv_shift"], p["rwkv_kk"], p["rwkv_ka"],
                                                p["rwkv_rk"].reshape(-1), p["rwkv_w0"], p["rwkv_wb"],
                                                p["rwkv_a0"], p["rwkv_ab"], p["rwkv_gb"])
    yf, yb, rw_fin = _rw_scan(r, v, kk, lw, kd, be, s_rw)
    x1, u2, logits_t = _merge(orf, orb, pr, ogf, ogb, pg, yf, yb, bonus, grw, gates, x, gt1, sh2, sc2,
                              p["w_br"], p["w_out"], p["ret_gn"], p["gla_gn"], p["rwkv_gn_g"], p["rwkv_gn_b"],
                              p["ln1_g"], p["ln1_b"], p["w_router"], p["b_router"])
    moe_out = _moe(u2, logits_t, p["w_eg"], p["w_eu"], p["w_ed"])
    x2, u_next = _post(x1, moe_out, gt2, p["ln2_g"], p["ln2_b"], sh_next, sc_next)
    return x2, u_next, (ret_fin, gla_fin_t, rw_fin)


_PARAM_NAMES = ("w_in", "rwkv_shift", "ret_gn", "gla_w2", "gla_b", "gla_gn", "rwkv_w0", "rwkv_wb", "rwkv_a0",
                "rwkv_ab", "rwkv_gb", "rwkv_kk", "rwkv_ka", "rwkv_rk", "rwkv_gn_g", "rwkv_gn_b", "w_br", "w_merge",
                "b_merge", "w_out", "ln1_g", "ln1_b", "ln2_g", "ln2_b", "w_rg", "b_rg", "w_re", "b_re", "w_eg",
                "w_eu", "w_ed")


def kernel(x_prompt, x_sample, state_ret, state_gla, state_rwkv, c, c_ctx, w_ada, b_ada, w_in, rwkv_shift, ret_gn,
           gla_w2, gla_b, gla_gn, rwkv_w0, rwkv_wb, rwkv_a0, rwkv_ab, rwkv_gb, rwkv_kk, rwkv_ka, rwkv_rk, rwkv_gn_g,
           rwkv_gn_b, w_br, w_merge, b_merge, w_out, ln1_g, ln1_b, ln2_g, ln2_b, w_rg, b_rg, w_re, b_re, w_eg, w_eu,
           w_ed):
    params = dict(zip(_PARAM_NAMES, (w_in, rwkv_shift, ret_gn, gla_w2, gla_b, gla_gn, rwkv_w0, rwkv_wb, rwkv_a0,
                                     rwkv_ab, rwkv_gb, rwkv_kk, rwkv_ka, rwkv_rk, rwkv_gn_g, rwkv_gn_b, w_br, w_merge,
                                     b_merge, w_out, ln1_g, ln1_b, ln2_g, ln2_b, w_rg, b_rg, w_re, b_re, w_eg, w_eu,
                                     w_ed)))
    bc, bl = x_prompt.shape[0], x_sample.shape[0]
    rope = _rope_tables(x_sample.shape[1])
    mod_rows = 2 * SUBLANES
    c_all = jnp.concatenate([c, c_ctx[None, :], jnp.zeros((mod_rows - bl - 1, D), F32)], axis=0)
    z_ret = jnp.zeros((bc, 2, RET_H, RET_DK, RET_DV), F32)
    z_gla_t = jnp.zeros((bc, 2, GLA_H, GLA_DV, GLA_DK), F32)
    z_rw = jnp.zeros((bc, 2, RW_H, RW_N, RW_N), F32)
    mods = [_modulation(c_all, w_ada[layer], b_ada[layer]) for layer in range(DEPTH)]
    mods.append(jnp.zeros_like(mods[0]))
    mods_lat = [m[:bl] for m in mods]
    mods_ctx = [jnp.broadcast_to(m[bl:bl + 1], (bc, 6 * D)) for m in mods]
    h_ctx, h_lat = x_prompt, x_sample
    sh, sc = _split_mod(mods_ctx[0])[:2]
    u_ctx = _lnmod(h_ctx, sh, sc)
    sh, sc = _split_mod(mods_lat[0])[:2]
    u_lat = _lnmod(h_lat, sh, sc)
    new_ret, new_gla, new_rw = [], [], []
    for layer in range(DEPTH):
        p = _layer_weights({k: v[layer] for k, v in params.items()})
        h_ctx, u_ctx, (s_ret, s_gla_t, s_rw) = _layer(h_ctx, u_ctx, mods_ctx[layer], mods_ctx[layer + 1],
                                                      z_ret, z_gla_t, z_rw, None, p)
        new_ret.append(s_ret)
        new_gla.append(jnp.swapaxes(s_gla_t, -1, -2))
        new_rw.append(s_rw)
        h_lat, u_lat, _ = _layer(h_lat, u_lat, mods_lat[layer], mods_lat[layer + 1], state_ret[:, layer],
                                 jnp.swapaxes(state_gla[:, layer], -1, -2), state_rwkv[:, layer], rope, p)
    return (h_ctx, h_lat, jnp.stack(new_ret, axis=1), jnp.stack(new_gla, axis=1), jnp.stack(new_rw, axis=1))
```

```python
import functools

import jax
import jax.numpy as jnp
import numpy as np
from jax import lax
from jax.experimental import pallas as pl
from jax.experimental.pallas import tpu as pltpu

F32 = jnp.float32
BF16 = jnp.bfloat16
HI = lax.Precision.HIGHEST

D = 1024
DEPTH = 2
GRID_W = 64
RET_H, RET_DK, RET_DV = 4, 128, 256
GLA_H, GLA_DK, GLA_DV = 4, 128, 256
GLA_LOWRANK = 16
GLA_GATE_NORM = 16.0
RW_H, RW_N = 16, 64
RW_W_LORA, RW_A_LORA, RW_G_LORA = 64, 64, 128
N_RW_COLS = 3 * D + RW_W_LORA + RW_A_LORA + RW_G_LORA
N_GROUPS, EPG, N_EXPERTS, EXPERT_FF = 4, 4, 16, 512
ALPHA = (2 * DEPTH) ** 0.25
EPS = 1e-5
ROPE_BASE = 10000.0

LANES = 128
SUBLANES = 8
VMEM_LIMIT = 56 * 1024 * 1024

RET_C = 128
GLA_C = 64
GLA_LEVELS = 6
RW_C = 32
RW_LEVELS = 5
RW_TL = 256
RW_BB = 2
RW_G = 4
RW_GW = RW_G * RW_N
RW_SW = 2 * RW_N


def _cparams(sem):
    return pltpu.CompilerParams(dimension_semantics=sem, vmem_limit_bytes=VMEM_LIMIT)


def _mm(a, b):
    return jnp.dot(a.astype(BF16), b.astype(BF16), preferred_element_type=F32)


def _mm_nt(a, b):
    return lax.dot_general(a.astype(BF16), b.astype(BF16), (((1,), (1,)), ((), ())), preferred_element_type=F32)


def _mm_tn(a, b):
    return lax.dot_general(a.astype(BF16), b.astype(BF16), (((0,), (0,)), ((), ())), preferred_element_type=F32)


def _mm_hi(a, b):
    return jnp.dot(a, b, preferred_element_type=F32, precision=HI)


def _split2(x):
    hi = x.astype(BF16)
    return hi, (x - hi.astype(F32)).astype(BF16)


def _mm_split(a, b_hi, b_lo):
    a_hi, a_lo = _split2(a)
    dot = functools.partial(jnp.dot, preferred_element_type=F32)
    return dot(a_hi, b_hi) + (dot(a_hi, b_lo) + dot(a_lo, b_hi))


def _sigmoid(x):
    return 1.0 / (1.0 + jnp.exp(-x))


def _silu(x):
    return x * _sigmoid(x)


def _log_sigmoid(x):
    return jnp.minimum(x, 0.0) - jnp.log(1.0 + jnp.exp(-jnp.abs(x)))


def _softplus(x):
    return jnp.maximum(x, 0.0) + jnp.log(1.0 + jnp.exp(-jnp.abs(x)))


def _ln(x):
    mu = jnp.mean(x, axis=-1, keepdims=True)
    xc = x - mu
    var = jnp.mean(xc * xc, axis=-1, keepdims=True)
    return xc * lax.rsqrt(var + EPS)


def _scan_rows(x, reverse):
    n = x.shape[0]
    row = lax.broadcasted_iota(jnp.int32, x.shape, 0)
    sh = 1
    while sh < n:
        if reverse:
            x = x + jnp.where(row < n - sh, pltpu.roll(x, n - sh, 0), 0.0)
        else:
            x = x + jnp.where(row >= sh, pltpu.roll(x, sh, 0), 0.0)
        sh *= 2
    return x


def _mod_kernel(c_ref, w_ref, b_ref, o_ref):
    o_ref[...] = _mm_hi(_silu(c_ref[...]), w_ref[...]) + b_ref[...]


def _modulation(c_all, w, b):
    m, n, tn = c_all.shape[0], w.shape[1], 512
    return pl.pallas_call(
        _mod_kernel,
        grid=(n // tn,),
        in_specs=[pl.BlockSpec((m, D), lambda j: (0, 0)),
                  pl.BlockSpec((D, tn), lambda j: (0, j)),
                  pl.BlockSpec((1, tn), lambda j: (0, j))],
        out_specs=pl.BlockSpec((m, tn), lambda j: (0, j)),
        out_shape=jax.ShapeDtypeStruct((m, n), F32),
        compiler_params=_cparams(("arbitrary",)),
        name="adaln_mod",
    )(c_all, w, b.reshape(1, n))


def _lnmod_kernel(x_ref, sh_ref, sc_ref, u_ref):
    u_ref[0] = (_ln(x_ref[0]) * (1.0 + sc_ref[0]) + sh_ref[0]).astype(u_ref.dtype)


def _lnmod(x, sh, sc):
    b, l, _ = x.shape
    tm = 512 if l % 512 == 0 else 256
    vec = pl.BlockSpec((1, 1, D), lambda i, j: (i, 0, 0))
    return pl.pallas_call(
        _lnmod_kernel,
        grid=(b, l // tm),
        in_specs=[pl.BlockSpec((1, tm, D), lambda i, j: (i, j, 0)), vec, vec],
        out_specs=pl.BlockSpec((1, tm, D), lambda i, j: (i, j, 0)),
        out_shape=jax.ShapeDtypeStruct((b, l, D), BF16),
        compiler_params=_cparams(("arbitrary", "arbitrary")),
        name="ln_mod",
    )(x, sh, sc)


def _proj_kernel(u_ref, w_ref, b_ref, o_ref, *, act):
    y = jnp.dot(u_ref[...], w_ref[...], preferred_element_type=F32) + b_ref[...]
    if act == "sigmoid":
        y = _sigmoid(y)
    o_ref[...] = y.astype(o_ref.dtype)


def _proj(u2d, w, bias, tn, act=None, out_dtype=BF16):
    t, n = u2d.shape[0], w.shape[1]
    tm = min(t, 2048)
    return pl.pallas_call(
        functools.partial(_proj_kernel, act=act),
        grid=(t // tm, n // tn),
        in_specs=[pl.BlockSpec((tm, D), lambda i, j: (i, 0)),
                  pl.BlockSpec((D, tn), lambda i, j: (0, j)),
                  pl.BlockSpec((1, tn), lambda i, j: (0, j))],
        out_specs=pl.BlockSpec((tm, tn), lambda i, j: (i, j)),
        out_shape=jax.ShapeDtypeStruct((t, n), out_dtype),
        compiler_params=_cparams(("arbitrary", "arbitrary")),
        name="proj",
    )(u2d, w, bias.reshape(1, n))


def _ret_log_gamma(d, h):
    hh = h if d == 0 else RET_H - 1 - h
    return float(np.log(1.0 - 2.0 ** (-5.0 - hh)))


def _ret_tables():
    c = RET_C
    i = np.arange(c, dtype=np.float64)
    dec = np.zeros((2 * RET_H, c, c), np.float32)
    qd = np.zeros((2 * RET_H, c, RET_DK), np.float32)
    kd = np.zeros((2 * RET_H, c, RET_DK), np.float32)
    for d in range(2):
        tau = i if d == 0 else c - 1 - i
        rel = tau[:, None] - tau[None, :]
        for h in range(RET_H):
            lg = _ret_log_gamma(d, h)
            dec[d * RET_H + h] = np.where(rel >= 0, np.exp(np.maximum(rel, 0.0) * lg), 0.0)
            qd[d * RET_H + h] = np.exp((tau + 1.0) * lg)[:, None]
            kd[d * RET_H + h] = np.exp((c - 1.0 - tau) * lg)[:, None]
    return jnp.asarray(dec), jnp.asarray(qd), jnp.asarray(kd)


def _rope(x, cos2, sin2):
    return x * cos2 + pltpu.roll(x, RET_DK // 2, 1) * sin2


def _ret_kernel(qf_ref, kf_ref, vf_ref, qb_ref, kb_ref, vb_ref, cosf_ref, sinf_ref, cosb_ref, sinb_ref,
                dec_ref, qd_ref, kd_ref, s0_ref, of_ref, ob_ref, sfin_ref, s_ref, *, use_rope, n_chunks):
    n = pl.program_id(1)

    @pl.when(n == 0)
    def _():
        s_ref[...] = s0_ref[0]

    ins = ((qf_ref, kf_ref, vf_ref, cosf_ref, sinf_ref), (qb_ref, kb_ref, vb_ref, cosb_ref, sinb_ref))
    outs = (of_ref, ob_ref)
    chains = [(d, h) for d in range(2) for h in range(RET_H)]
    q, k, v = {}, {}, {}
    for d, h in chains:
        q_ref, k_ref, v_ref, cos_ref, sin_ref = ins[d]
        dk = slice(h * RET_DK, (h + 1) * RET_DK)
        q[d, h] = q_ref[0, :, dk].astype(F32)
        k[d, h] = k_ref[0, :, dk].astype(F32) * (RET_DK ** -0.5)
        if use_rope:
            q[d, h] = _rope(q[d, h], cos_ref[...], sin_ref[...])
            k[d, h] = _rope(k[d, h], cos_ref[...], sin_ref[...])
        v[d, h] = v_ref[0, :, h * RET_DV:(h + 1) * RET_DV]
    sc = {ch: _mm_nt(q[ch], k[ch]) * dec_ref[ch[0] * RET_H + ch[1]] for ch in chains}
    s_old = {ch: s_ref[ch[0], ch[1]] for ch in chains}
    o = {ch: _mm(sc[ch], v[ch]) + _mm(q[ch] * qd_ref[ch[0] * RET_H + ch[1]], s_old[ch]) for ch in chains}
    s_new = {ch: s_old[ch] * float(np.exp(RET_C * _ret_log_gamma(*ch)))
             + _mm_tn(k[ch] * kd_ref[ch[0] * RET_H + ch[1]], v[ch]) for ch in chains}
    for d, h in chains:
        outs[d][0, :, h * RET_DV:(h + 1) * RET_DV] = o[d, h].astype(outs[d].dtype)
        s_ref[d, h] = s_new[d, h]

    @pl.when(n == n_chunks - 1)
    def _():
        sfin_ref[0] = s_ref[...]


def _retention(pr, cos2, sin2, s0, use_rope):
    b, l, _ = pr.shape
    c = RET_C
    nc = l // c
    dec, qd, kd = _ret_tables()
    qw, vw = RET_H * RET_DK, RET_H * RET_DV

    def fw(blk):
        return lambda i, n: (i, n, blk)

    def bw(blk):
        return lambda i, n: (i, nc - 1 - n, blk)

    def const(*shape):
        return pl.BlockSpec(shape, lambda i, n: (0,) * len(shape))

    st = pl.BlockSpec((1, 2, RET_H, RET_DK, RET_DV), lambda i, n: (i, 0, 0, 0, 0))
    in_specs = [
        pl.BlockSpec((1, c, qw), fw(0)), pl.BlockSpec((1, c, qw), fw(1)), pl.BlockSpec((1, c, vw), fw(1)),
        pl.BlockSpec((1, c, qw), bw(0)), pl.BlockSpec((1, c, qw), bw(1)), pl.BlockSpec((1, c, vw), bw(1)),
        pl.BlockSpec((c, RET_DK), lambda i, n: (n, 0)), pl.BlockSpec((c, RET_DK), lambda i, n: (n, 0)),
        pl.BlockSpec((c, RET_DK), lambda i, n: (nc - 1 - n, 0)), pl.BlockSpec((c, RET_DK), lambda i, n: (nc - 1 - n, 0)),
        const(2 * RET_H, c, c), const(2 * RET_H, c, RET_DK), const(2 * RET_H, c, RET_DK), st,
    ]
    out_specs = [pl.BlockSpec((1, c, vw), fw(0)), pl.BlockSpec((1, c, vw), bw(0)), st]
    sd = jax.ShapeDtypeStruct
    return pl.pallas_call(
        functools.partial(_ret_kernel, use_rope=use_rope, n_chunks=nc),
        grid=(b, nc),
        in_specs=in_specs,
        out_specs=out_specs,
        out_shape=[sd((b, l, vw), BF16), sd((b, l, vw), BF16), sd((b, 2, RET_H, RET_DK, RET_DV), F32)],
        scratch_shapes=[pltpu.VMEM((2, RET_H, RET_DK, RET_DV), F32)],
        compiler_params=_cparams(("arbitrary", "arbitrary")),
        name="retention",
    )(pr, pr, pr, pr, pr, pr, cos2, sin2, cos2, sin2, dec, qd, kd, s0)


def _gla_boundary(b, lv, reverse, rolls):
    c, w = b.shape
    m = 1 << lv
    if m >= SUBLANES:
        parts = []
        for p0 in range(0, c, 2 * m):
            e = p0 + m if reverse else p0 + m - 1
            parts.append(jnp.broadcast_to(b[e:e + 1, :], (2 * m, w)))
        return jnp.concatenate(parts, axis=0)

    def rolled(s):
        s %= c
        if s not in rolls:
            rolls[s] = b if s == 0 else pltpu.roll(b, s, 0)
        return rolls[s]

    row = lax.broadcasted_iota(jnp.int32, (c, w), 0)
    r = jnp.bitwise_and(row, m - 1)
    upper = jnp.bitwise_and(jnp.right_shift(row, lv), 1) == 1
    out = b
    for t in range(m):
        if reverse:
            out = jnp.where(upper & (r == t), rolled(t), out)
            out = jnp.where(jnp.logical_not(upper) & (r == t), rolled(-(m - t)), out)
        else:
            out = jnp.where(upper & (r == t), rolled(t + 1), out)
            out = jnp.where(jnp.logical_not(upper) & (r == t), rolled(-(m - 1 - t)), out)
    return out


def _gla_kernel(qf_ref, kf_ref, vf_ref, lrf_ref, qb_ref, kb_ref, vb_ref, lrb_ref, w2_ref, b2_ref, s0_ref,
                of_ref, ob_ref, sfin_ref, st_ref, *, n_chunks):
    c = GLA_C
    n = pl.program_id(1)

    @pl.when(n == 0)
    def _():
        st_ref[...] = s0_ref[0]

    ins = ((qf_ref, kf_ref, vf_ref, lrf_ref), (qb_ref, kb_ref, vb_ref, lrb_ref))
    outs = (of_ref, ob_ref)
    hw = GLA_H * GLA_DK
    row = lax.broadcasted_iota(jnp.int32, (c, hw), 0)
    ri = lax.broadcasted_iota(jnp.int32, (c, c), 0)
    ci = lax.broadcasted_iota(jnp.int32, (c, c), 1)
    q, k, v, b_inc, b_rest, tot, ql, kl = {}, {}, {}, {}, {}, {}, {}, {}
    for d in range(2):
        reverse = d == 1
        q_ref, k_ref, v_ref, lr_ref = ins[d]
        q[d] = q_ref[0].astype(F32) * (GLA_DK ** -0.5)
        k[d] = k_ref[0].astype(F32)
        v[d] = v_ref[0]
        gate = _log_sigmoid(_mm_split(lr_ref[0], w2_ref[d, 0], w2_ref[d, 1]) + b2_ref[d])
        gate = gate * (1.0 / GLA_GATE_NORM)
        b = _scan_rows(gate, reverse)
        tot[d] = b[0:1] if reverse else b[c - 1:c]
        b_inc[d], b_rest[d] = b, tot[d] - b
        rolls = {}
        for lv in range(GLA_LEVELS):
            be = _gla_boundary(b, lv, reverse, rolls)
            upper = jnp.bitwise_and(jnp.right_shift(row, lv), 1) == 1
            second = jnp.logical_not(upper) if reverse else upper
            e = jnp.exp(jnp.where(second, b - be, be - b))
            ql[d, lv] = jnp.where(second, q[d] * e, 0.0)
            kl[d, lv] = jnp.where(second, 0.0, k[d] * e)
    chains = [(d, h) for d in range(2) for h in range(GLA_H)]

    def dk(h):
        return slice(h * GLA_DK, (h + 1) * GLA_DK)

    def dv(h):
        return slice(h * GLA_DV, (h + 1) * GLA_DV)

    attn = {(d, h): jnp.where(ri == ci, jnp.sum(q[d][:, dk(h)] * k[d][:, dk(h)], axis=-1, keepdims=True), 0.0)
            for d, h in chains}
    for lv in range(GLA_LEVELS):
        same = jnp.right_shift(ri, lv + 1) == jnp.right_shift(ci, lv + 1)
        for d, h in chains:
            attn[d, h] = attn[d, h] + jnp.where(same, _mm_nt(ql[d, lv][:, dk(h)], kl[d, lv][:, dk(h)]), 0.0)
    st_old = {ch: st_ref[ch[0], ch[1]] for ch in chains}
    o = {(d, h): _mm(attn[d, h], v[d][:, dv(h)]) + _mm_nt(q[d][:, dk(h)] * jnp.exp(b_inc[d][:, dk(h)]), st_old[d, h])
         for d, h in chains}
    st_new = {(d, h): st_old[d, h] * jnp.exp(tot[d][:, dk(h)])
              + _mm_tn(v[d][:, dv(h)], k[d][:, dk(h)] * jnp.exp(b_rest[d][:, dk(h)])) for d, h in chains}
    for d, h in chains:
        outs[d][0, :, dv(h)] = o[d, h].astype(outs[d].dtype)
        st_ref[d, h] = st_new[d, h]

    @pl.when(n == n_chunks - 1)
    def _():
        sfin_ref[0] = st_ref[...]


def _gla(pg, plr, w2p, b2, s0t):
    b, l, _ = pg.shape
    c = GLA_C
    nc = l // c
    qw, vw = GLA_H * GLA_DK, GLA_H * GLA_DV

    def fw(blk):
        return lambda i, n: (i, n, blk)

    def bw(blk):
        return lambda i, n: (i, nc - 1 - n, blk)

    st = pl.BlockSpec((1, 2, GLA_H, GLA_DV, GLA_DK), lambda i, n: (i, 0, 0, 0, 0))
    in_specs = [
        pl.BlockSpec((1, c, qw), fw(0)), pl.BlockSpec((1, c, qw), fw(1)), pl.BlockSpec((1, c, vw), fw(1)),
        pl.BlockSpec((1, c, LANES), fw(0)),
        pl.BlockSpec((1, c, qw), bw(0)), pl.BlockSpec((1, c, qw), bw(1)), pl.BlockSpec((1, c, vw), bw(1)),
        pl.BlockSpec((1, c, LANES), bw(0)),
        pl.BlockSpec((2, 2, LANES, qw), lambda i, n: (0, 0, 0, 0)),
        pl.BlockSpec((2, 1, qw), lambda i, n: (0, 0, 0)),
        st,
    ]
    out_specs = [pl.BlockSpec((1, c, vw), fw(0)), pl.BlockSpec((1, c, vw), bw(0)), st]
    sd = jax.ShapeDtypeStruct
    return pl.pallas_call(
        functools.partial(_gla_kernel, n_chunks=nc),
        grid=(b, nc),
        in_specs=in_specs,
        out_specs=out_specs,
        out_shape=[sd((b, l, vw), BF16), sd((b, l, vw), BF16), sd((b, 2, GLA_H, GLA_DV, GLA_DK), F32)],
        scratch_shapes=[pltpu.VMEM((2, GLA_H, GLA_DV, GLA_DK), F32)],
        compiler_params=_cparams(("arbitrary", "arbitrary")),
        name="gla",
    )(pg, pg, pg, plr, pg, pg, pg, plr, w2p, b2, s0t)


def _gla_gate_params(w2, b2):
    w2p = jnp.pad(w2, ((0, 0), (0, LANES - GLA_LOWRANK), (0, 0)))
    return jnp.stack(_split2(w2p), axis=1), b2.reshape(2, 1, GLA_H * GLA_DK)


def _head_sum(x):
    i = lax.broadcasted_iota(jnp.int32, (LANES, LANES), 0)
    j = lax.broadcasted_iota(jnp.int32, (LANES, LANES), 1)
    ones = jnp.where(jnp.right_shift(i, 6) == jnp.right_shift(j, 6), 1.0, 0.0).astype(BF16)
    out = []
    for t in range(x.shape[1] // LANES):
        parts = _split2(x[:, t * LANES:(t + 1) * LANES])
        out.append(sum(jnp.dot(p, ones, preferred_element_type=F32) for p in parts))
    return jnp.concatenate(out, axis=1)


RW_TM = 256
HALO = 2 * SUBLANES
N_RW_LORA = RW_W_LORA + RW_A_LORA + RW_G_LORA


def _rw_prep_kernel(x_ref, xp_ref, xn_ref, lo_ref, lop_ref, lon_ref, taps_ref, ltaps_ref, kkp_ref, ka_ref, rk_ref,
                    w0_ref, wb_ref, a0_ref, ab_ref, gb_ref, r_ref, v_ref, kk_ref, lw_ref, kd_ref, be_ref, bonus_ref,
                    g_ref, xbuf, lbuf, *, n_tiles):
    tm = RW_TM
    j = pl.program_id(1)
    first, last = j == 0, j == n_tiles - 1
    xbuf[0:HALO, :] = jnp.where(first, jnp.zeros_like(xp_ref[0]), xp_ref[0])
    xbuf[HALO:HALO + tm, :] = x_ref[0]
    xbuf[HALO + tm:2 * HALO + tm, :] = jnp.where(last, jnp.zeros_like(xn_ref[0]), xn_ref[0])
    ri = lax.broadcasted_iota(jnp.int32, (tm, tm + 2 * HALO), 0)
    ci = lax.broadcasted_iota(jnp.int32, (tm, tm + 2 * HALO), 1)
    pick_prev = jnp.where(ci == ri + (HALO - 1), 1.0, 0.0).astype(BF16)
    pick_next = jnp.where(ci == ri + (HALO + 1), 1.0, 0.0).astype(BF16)

    def shifted_main(c0, c1):
        x_all = xbuf[:, c0:c1]
        return (taps_ref[0:1, c0:c1] * jnp.dot(pick_prev, x_all, preferred_element_type=F32)
                + taps_ref[1:2, c0:c1] * x_ref[0, :, c0:c1].astype(F32)
                + taps_ref[2:3, c0:c1] * jnp.dot(pick_next, x_all, preferred_element_type=F32))

    lbuf[HALO:HALO + tm, :] = lo_ref[0]
    lbuf[0:HALO, :] = jnp.where(first, 0.0, lop_ref[0])
    lbuf[HALO + tm:2 * HALO + tm, :] = jnp.where(last, 0.0, lon_ref[0])
    lora = (ltaps_ref[0:1, :] * lbuf[HALO - 1:HALO - 1 + tm, :] + ltaps_ref[1:2, :] * lbuf[HALO:HALO + tm, :]
            + ltaps_ref[2:3, :] * lbuf[HALO + 1:HALO + 1 + tm, :])
    r = shifted_main(0, D)
    k = shifted_main(D, 2 * D)
    v = shifted_main(2 * D, 3 * D)
    xw = lora[:, 0:RW_W_LORA]
    xa = lora[:, RW_W_LORA:RW_W_LORA + RW_A_LORA]
    xg = lora[:, RW_W_LORA + RW_A_LORA:]
    r_ref[0] = r.astype(r_ref.dtype)
    v_ref[0] = v.astype(v_ref.dtype)
    kk = k * kkp_ref[...]
    kk = kk * lax.rsqrt(jnp.maximum(_head_sum(kk * kk), 1e-24))
    kk_ref[0] = kk.astype(kk_ref.dtype)
    g_ref[0] = _mm(_sigmoid(xg), gb_ref[...]).astype(g_ref.dtype)
    wh = jnp.tanh(xw)
    kd_sum = jnp.zeros_like(k)
    for d in range(2):
        w = -_softplus(-(w0_ref[d:d + 1, :] + _mm(wh, wb_ref[d]))) - 0.5
        lw_ref[d, 0] = -jnp.exp(w)
        a = _sigmoid(a0_ref[d:d + 1, :] + _mm(xa, ab_ref[d]))
        kd = k * (1.0 + (a - 1.0) * ka_ref[...])
        kd_ref[d, 0] = kd.astype(kd_ref.dtype)
        be_ref[d, 0] = (kk * a).astype(be_ref.dtype)
        kd_sum = kd_sum + kd
    bonus_ref[0] = (_head_sum(r * kd_sum * rk_ref[...]) * v).astype(bonus_ref.dtype)


def _rw_prep(rw, rlo, taps, kkp, ka, rk, w0, wb, a0, ab, gb):
    b, l, _ = rw.shape
    tm = RW_TM
    nt = l // tm
    hb = tm // HALO
    tok = pl.BlockSpec((1, tm, D), lambda i, j: (i, j, 0))
    tokd = pl.BlockSpec((2, 1, tm, D), lambda i, j: (0, i, j, 0))

    def full(*shape):
        return pl.BlockSpec(shape, lambda i, j: (0,) * len(shape))

    def halo_specs(n):
        return [pl.BlockSpec((1, tm, n), lambda i, j: (i, j, 0)),
                pl.BlockSpec((1, HALO, n), lambda i, j: (i, jnp.maximum(j * hb - 1, 0), 0)),
                pl.BlockSpec((1, HALO, n), lambda i, j: (i, jnp.minimum((j + 1) * hb, l // HALO - 1), 0))]

    in_specs = halo_specs(3 * D) + halo_specs(N_RW_LORA) + [
        full(3, 3 * D), full(3, N_RW_LORA), full(1, D), full(1, D), full(1, D),
        full(2, D), full(2, RW_W_LORA, D), full(2, D), full(2, RW_A_LORA, D), full(RW_G_LORA, D),
    ]
    sd = jax.ShapeDtypeStruct
    tok_o, dir_o, dir_f32 = sd((b, l, D), BF16), sd((2, b, l, D), BF16), sd((2, b, l, D), F32)
    return pl.pallas_call(
        functools.partial(_rw_prep_kernel, n_tiles=nt),
        grid=(b, nt),
        in_specs=in_specs,
        out_specs=[tok, tok, tok, tokd, tokd, tokd, tok, tok],
        out_shape=[tok_o, tok_o, tok_o, dir_f32, dir_o, dir_o, tok_o, tok_o],
        scratch_shapes=[pltpu.VMEM((tm + 2 * HALO, 3 * D), BF16), pltpu.VMEM((tm + 2 * HALO, N_RW_LORA), F32)],
        compiler_params=_cparams(("arbitrary", "arbitrary")),
        name="rwkv_prep",
    )(rw, rw, rw, rlo, rlo, rlo, taps[:, :3 * D], taps[:, 3 * D:], kkp.reshape(1, D), ka.reshape(1, D),
      rk.reshape(1, D), w0, wb, a0, ab, gb)


def _tile_rows(x, n):
    return jnp.concatenate([x] * n, axis=0)


def _rw_scan_kernel(rf_ref, vf_ref, kkf_ref, lwf_ref, kdf_ref, bef_ref, rb_ref, vb_ref, kkb_ref, lwb_ref, kdb_ref,
                    beb_ref, s0_ref, yf_ref, yb_ref, sfin_ref, zt_ref, *, n_tiles):
    c, hw, sw = RW_C, RW_GW, RW_SW
    pk = RW_G * c
    n_groups, n_sub = RW_H // RW_G, RW_GW // RW_SW
    n_chunks = rf_ref.shape[1] // c
    j = pl.program_id(1)

    @pl.when(j == 0)
    def _():
        zt_ref[...] = jnp.zeros_like(zt_ref)
        for bb in range(RW_BB):
            for d in range(2):
                for g in range(RW_H // 2):
                    for jj in range(2):
                        blk = slice(RW_N * jj, RW_N * (jj + 1))
                        zt_ref[bb, d, g, blk, blk] = s0_ref[bb, d, 2 * g + jj]

    ins = ((rf_ref, vf_ref, kkf_ref, lwf_ref, kdf_ref, bef_ref), (rb_ref, vb_ref, kkb_ref, lwb_ref, kdb_ref, beb_ref))
    outs = (yf_ref, yb_ref)
    ri = lax.broadcasted_iota(jnp.int32, (c, pk), 0)
    cs = jnp.bitwise_and(lax.broadcasted_iota(jnp.int32, (c, pk), 1), c - 1)
    eye = (ri == cs).astype(F32)
    strict, incl, lvl = [], [], []
    for d in range(2):
        ti = ri if d == 0 else c - 1 - ri
        ts = cs if d == 0 else c - 1 - cs
        strict.append(ts < ti)
        incl.append(ts <= ti)
        lv_masks = []
        for lv in range(RW_LEVELS):
            same = jnp.right_shift(ti, lv + 1) == jnp.right_shift(ts, lv + 1)
            lower = ((jnp.bitwise_and(jnp.right_shift(ti, lv), 1) == 1)
                     & (jnp.bitwise_and(jnp.right_shift(ts, lv), 1) == 0))
            lv_masks.append(same & lower)
        lvl.append(lv_masks)
    bi = lax.broadcasted_iota(jnp.int32, (pk, pk), 0)
    bj = lax.broadcasted_iota(jnp.int32, (pk, pk), 1)
    bd_p = jnp.right_shift(bi, RW_LEVELS) == jnp.right_shift(bj, RW_LEVELS)
    zi = lax.broadcasted_iota(jnp.int32, (sw, sw), 0)
    zj = lax.broadcasted_iota(jnp.int32, (sw, sw), 1)
    bd_z = jnp.right_shift(zi, 6) == jnp.right_shift(zj, 6)
    wi = lax.broadcasted_iota(jnp.int32, (pk, hw), 0)
    wj = lax.broadcasted_iota(jnp.int32, (pk, hw), 1)
    bd_w = jnp.right_shift(wi, RW_LEVELS) == jnp.right_shift(wj, 6)

    def bdp(x):
        return jnp.where(bd_p, _tile_rows(x, RW_G), 0.0)

    def bdw(x):
        return jnp.where(bd_w, _tile_rows(x, RW_G), 0.0)

    chains = [(bb, d, half) for bb in range(RW_BB) for d in range(2) for half in range(n_groups)]

    def each(fn):
        return {ch: fn(ch) for ch in chains}

    def chunk(i, carry):
        rows = (pl.ds(pl.multiple_of(i * c, c), c), pl.ds(pl.multiple_of((n_chunks - 1 - i) * c, c), c))

        def load(ch):
            bb, d, half = ch
            ln = slice(half * hw, (half + 1) * hw)
            refs = ins[d]
            return (refs[0][bb, rows[d], ln].astype(F32), refs[1][bb, rows[d], ln].astype(F32),
                    refs[2][bb, rows[d], ln].astype(F32), refs[3][0, bb, rows[d], ln],
                    refs[4][0, bb, rows[d], ln].astype(F32), refs[5][0, bb, rows[d], ln].astype(F32))

        x = each(load)
        r, v, kk = each(lambda ch: x[ch][0]), each(lambda ch: x[ch][1]), each(lambda ch: x[ch][2])
        lw, kd, be = each(lambda ch: x[ch][3]), each(lambda ch: x[ch][4]), each(lambda ch: x[ch][5])
        b = each(lambda ch: _scan_rows(lw[ch], ch[1] == 1))
        tot = each(lambda ch: b[ch][0:1] if ch[1] == 1 else b[ch][c - 1:c])
        en = each(lambda ch: jnp.exp(-b[ch]))
        es = each(lambda ch: jnp.exp(tot[ch] - b[ch]))
        lhs = each(lambda ch: jnp.concatenate([kk[ch] * jnp.exp(b[ch] - lw[ch]), r[ch] * jnp.exp(b[ch])], axis=0))
        gb = each(lambda ch: _mm_nt(lhs[ch], bdw(be[ch] * en[ch])))
        gk = each(lambda ch: _mm_nt(lhs[ch], bdw(kd[ch] * en[ch])))
        a_ab = each(lambda ch: jnp.where(strict[ch[1]], gb[ch][0:c], 0.0))
        a_rb = each(lambda ch: jnp.where(incl[ch[1]], gb[ch][c:], 0.0))
        a_ak = each(lambda ch: jnp.where(strict[ch[1]], gk[ch][0:c], 0.0))
        a_rk = each(lambda ch: jnp.where(incl[ch[1]], gk[ch][c:], 0.0))
        t_inv = each(lambda ch: eye - jnp.where(lvl[ch[1]][0], a_ab[ch], 0.0))
        for lv in range(1, RW_LEVELS):
            xm = each(lambda ch: _mm(jnp.where(lvl[ch[1]][lv], a_ab[ch], 0.0), bdp(t_inv[ch])))
            t_inv = each(lambda ch: t_inv[ch] - _mm(t_inv[ch], bdp(xm[ch])))
        zt_old = {(ch, g): zt_ref[ch[0], ch[1], n_sub * ch[2] + g] for ch in chains for g in range(n_sub)}
        zz = each(lambda ch: jnp.concatenate(
            [_mm_nt(lhs[ch][:, g * sw:(g + 1) * sw], zt_old[ch, g]) for g in range(n_sub)], axis=1))
        vbd = each(lambda ch: bdw(v[ch]))
        u = each(lambda ch: _mm(t_inv[ch], bdw(-(zz[ch][0:c] + _mm(a_ak[ch], vbd[ch])))))
        y = each(lambda ch: zz[ch][c:] + _mm(a_rb[ch], bdw(u[ch])) + _mm(a_rk[ch], vbd[ch]))
        uv = each(lambda ch: jnp.concatenate([u[ch], v[ch]], axis=0))
        bk = each(lambda ch: jnp.concatenate([be[ch] * es[ch], kd[ch] * es[ch]], axis=0))
        etot = each(lambda ch: jnp.exp(tot[ch]))
        zt_new = {}
        for ch in chains:
            for g in range(n_sub):
                gl = slice(g * sw, (g + 1) * sw)
                upd = jnp.where(bd_z, _mm_tn(uv[ch][:, gl], bk[ch][:, gl]), 0.0)
                zt_new[ch, g] = zt_old[ch, g] * etot[ch][:, gl] + upd
        for ch in chains:
            bb, d, half = ch
            outs[d][bb, rows[d], half * hw:(half + 1) * hw] = y[ch].astype(outs[d].dtype)
            for g in range(n_sub):
                zt_ref[bb, d, n_sub * half + g] = zt_new[ch, g]
        return carry

    lax.fori_loop(0, n_chunks, chunk, 0)

    @pl.when(j == n_tiles - 1)
    def _():
        for bb in range(RW_BB):
            for d in range(2):
                for g in range(RW_H // 2):
                    for jj in range(2):
                        blk = slice(RW_N * jj, RW_N * (jj + 1))
                        sfin_ref[bb, d, 2 * g + jj] = zt_ref[bb, d, g, blk, blk]


def _rw_scan(r, v, kk, lw, kd, be, s0):
    b, l, _ = r.shape
    tl, nb = min(RW_TL, l), RW_BB
    nt = l // tl
    tok_f = pl.BlockSpec((nb, tl, D), lambda i, j: (i, j, 0))
    tok_b = pl.BlockSpec((nb, tl, D), lambda i, j: (i, nt - 1 - j, 0))
    dir_f = pl.BlockSpec((1, nb, tl, D), lambda i, j: (0, i, j, 0))
    dir_b = pl.BlockSpec((1, nb, tl, D), lambda i, j: (1, i, nt - 1 - j, 0))
    st = pl.BlockSpec((nb, 2, RW_H, RW_N, RW_N), lambda i, j: (i, 0, 0, 0, 0))
    sd = jax.ShapeDtypeStruct
    return pl.pallas_call(
        functools.partial(_rw_scan_kernel, n_tiles=nt),
        grid=(b // nb, nt),
        in_specs=[tok_f, tok_f, tok_f, dir_f, dir_f, dir_f, tok_b, tok_b, tok_b, dir_b, dir_b, dir_b, st],
        out_specs=[tok_f, tok_b, st],
        out_shape=[sd((b, l, D), BF16), sd((b, l, D), BF16), sd((b, 2, RW_H, RW_N, RW_N), F32)],
        scratch_shapes=[pltpu.VMEM((nb, 2, RW_H // 2, RW_SW, RW_SW), F32)],
        compiler_params=_cparams(("arbitrary", "arbitrary")),
        name="rwkv_scan",
    )(r, v, kk, lw, kd, be, r, v, kk, lw, kd, be, s0)


MERGE_TM = 128
ROUTER_LANES = LANES


def _merge_kernel(orf_ref, orb_ref, rg_ref, ogf_ref, ogb_ref, gg_ref, yf_ref, yb_ref, bonus_ref, grw_ref,
                  gates_ref, x_ref, gt_ref, sh_ref, sc_ref, wbr_ref, wout_ref, rgn_ref, ggn_ref, gng_ref, gnb_ref,
                  l1g_ref, l1b_ref, wr_ref, br_ref, x1_ref, u2_ref, lg_ref):
    f32 = lambda ref: ref[0].astype(F32)
    o_ret = f32(orf_ref) + f32(orb_ref)
    o_gla = f32(ogf_ref) + f32(ogb_ref)
    y_ret, y_gla = [], []
    for h in range(RET_H):
        hs = slice(h * RET_DV, (h + 1) * RET_DV)
        y_ret.append(_ln(o_ret[:, hs]))
        og = o_gla[:, hs]
        y_gla.append(og * lax.rsqrt(jnp.mean(og * og, axis=-1, keepdims=True) + EPS))
    z_ret = _silu(f32(rg_ref)) * (jnp.concatenate(y_ret, axis=1) * rgn_ref[...])
    z_gla = _silu(f32(gg_ref)) * (jnp.concatenate(y_gla, axis=1) * ggn_ref[...])
    y = f32(yf_ref) + f32(yb_ref)
    mu = _head_sum(y) * (1.0 / RW_N)
    yc = y - mu
    var = _head_sum(yc * yc) * (1.0 / RW_N)
    z_rw = (yc * lax.rsqrt(var + EPS) * gng_ref[...] + gnb_ref[...] + f32(bonus_ref)) * f32(grw_ref)
    gates = f32(gates_ref)
    mixed = (gates[:, 0:D] * _mm(z_ret, wbr_ref[0]) + gates[:, D:2 * D] * _mm(z_gla, wbr_ref[1])
             + gates[:, 2 * D:] * _mm(z_rw, wbr_ref[2]))
    mix = _mm(mixed, wout_ref[...])
    x1 = _ln(ALPHA * x_ref[0] + gt_ref[0] * mix) * l1g_ref[...] + l1b_ref[...]
    x1_ref[0] = x1
    u2 = _ln(x1) * (1.0 + sc_ref[0]) + sh_ref[0]
    u2_ref[0] = u2.astype(u2_ref.dtype)
    lg_ref[...] = (_mm_split(u2, wr_ref[0], wr_ref[1]) + br_ref[...]).T


def _merge(orf, orb, pr, ogf, ogb, pg, yf, yb, bonus, grw, gates, x, gt1, sh2, sc2, wbr, wout, rgn, ggn, gng, gnb,
           l1g, l1b, wr, br):
    b, l, _ = x.shape
    tm = MERGE_TM
    tok = pl.BlockSpec((1, tm, D), lambda i, j: (i, j, 0))
    gate_blk = pl.BlockSpec((1, tm, D), lambda i, j: (i, j, 2))
    vec = pl.BlockSpec((1, 1, D), lambda i, j: (i, 0, 0))
    row = pl.BlockSpec((1, D), lambda i, j: (0, 0))
    in_specs = [
        tok, tok, gate_blk, tok, tok, gate_blk, tok, tok, tok, tok,
        pl.BlockSpec((1, tm, 3 * D), lambda i, j: (i, j, 0)),
        tok, vec, vec, vec,
        pl.BlockSpec((3, D, D), lambda i, j: (0, 0, 0)),
        pl.BlockSpec((D, D), lambda i, j: (0, 0)),
        row, row, row, row, row, row,
        pl.BlockSpec((2, D, ROUTER_LANES), lambda i, j: (0, 0, 0)),
        pl.BlockSpec((1, ROUTER_LANES), lambda i, j: (0, 0)),
    ]
    sd = jax.ShapeDtypeStruct
    r1 = lambda a: a.reshape(1, D)
    return pl.pallas_call(
        _merge_kernel,
        grid=(b, l // tm),
        in_specs=in_specs,
        out_specs=[tok, tok, pl.BlockSpec((ROUTER_LANES, tm), lambda i, j: (0, i * (l // tm) + j))],
        out_shape=[sd((b, l, D), F32), sd((b, l, D), BF16), sd((ROUTER_LANES, b * l), F32)],
        compiler_params=_cparams(("arbitrary", "arbitrary")),
        name="merge",
    )(orf, orb, pr, ogf, ogb, pg, yf, yb, bonus, grw, gates, x, gt1, sh2, sc2, wbr, wout,
      r1(rgn), r1(ggn), r1(gng), r1(gnb), r1(l1g), r1(l1b), wr, br)


MOE_TM = 1024


def _moe_block_rows(tm):
    rows = tm / N_GROUPS + 4.0 * np.sqrt(tm * (N_GROUPS - 1.0)) / N_GROUPS
    bf16_tile = 2 * SUBLANES
    return int(min(tm, -(-rows // bf16_tile) * bf16_tile))


def _first_argmax_rows(x, row):
    m = jnp.max(x, axis=0, keepdims=True)
    idx = jnp.min(jnp.where(x == m, row, x.shape[0]), axis=0, keepdims=True)
    return m, idx


def _routing(lt):
    tm = lt.shape[1]
    gl = lt[0:N_GROUPS]
    gmax, gidx = _first_argmax_rows(gl, lax.broadcasted_iota(jnp.int32, (N_GROUPS, tm), 0))
    g_w = 1.0 / jnp.sum(jnp.exp(gl - gmax), axis=0, keepdims=True)
    row = lax.broadcasted_iota(jnp.int32, (N_EXPERTS, tm), 0)
    neg = -jnp.inf
    el = jnp.where(jnp.right_shift(row, 2) == gidx, lt[N_GROUPS:N_GROUPS + N_EXPERTS], neg)
    m1, i1 = _first_argmax_rows(el, row)
    m2, i2 = _first_argmax_rows(jnp.where(row == i1, neg, el), row)
    e2 = jnp.exp(m2 - m1)
    w1 = 1.0 / (1.0 + e2)
    return gidx, g_w * (jnp.where(row == i1, w1, 0.0) + jnp.where(row == i2, e2 * w1, 0.0))


def _moe_kernel(u_ref, lt_ref, wg_ref, wu_ref, wd_ref, o_ref, acc_ref):
    g = pl.program_id(1)
    tm = acc_ref.shape[0]
    blk = _moe_block_rows(tm)

    @pl.when(g == 0)
    def _():
        acc_ref[...] = jnp.zeros_like(acc_ref)

    gidx, comb = _routing(lt_ref[...])
    member = jnp.broadcast_to((gidx == g).astype(F32), (SUBLANES, tm))
    lane = lax.broadcasted_iota(jnp.int32, (SUBLANES, tm), 1)
    count = member
    sh = 1
    while sh < tm:
        count = count + jnp.where(lane >= sh, pltpu.roll(count, sh, 1), 0.0)
        sh *= 2
    pos = jnp.where(member > 0.0, count - 1.0, -1.0).astype(jnp.int32)[0:1]
    n_tok = jnp.max(count).astype(jnp.int32)
    comb_pad = jnp.concatenate([comb, jnp.zeros((LANES - N_EXPERTS, tm), F32)], axis=0)
    comb_parts = _split2(comb_pad)
    u = u_ref[...]
    sel_lane = lax.broadcasted_iota(jnp.int32, (blk, LANES), 1)

    def body(i, carry):
        slot = lax.broadcasted_iota(jnp.int32, (blk, tm), 0) + i * blk
        onehot = jnp.where(slot == pos, 1.0, 0.0).astype(BF16)
        xg = jnp.dot(onehot, u, preferred_element_type=F32).astype(BF16)
        cw = sum(lax.dot_general(onehot, c, (((1,), (1,)), ((), ())), preferred_element_type=F32)
                 for c in comb_parts)
        y = jnp.zeros((blk, D), F32)
        for e in range(EPG):
            hid = (_silu(jnp.dot(xg, wg_ref[e], preferred_element_type=F32))
                   * jnp.dot(xg, wu_ref[e], preferred_element_type=F32))
            c_e = jnp.sum(jnp.where(sel_lane == g * EPG + e, cw, 0.0), axis=-1, keepdims=True)
            y = y + c_e * _mm(hid, wd_ref[e])
        acc_ref[...] += _mm_tn(onehot, y)
        return carry

    lax.fori_loop(0, (n_tok + blk - 1) // blk, body, 0)

    @pl.when(g == N_GROUPS - 1)
    def _():
        o_ref[...] = acc_ref[...].astype(o_ref.dtype)


def _moe(u2, logits_t, wg, wu, wd):
    b, l, _ = u2.shape
    t = b * l
    tm = min(MOE_TM, t)
    tok = pl.BlockSpec((tm, D), lambda i, g: (i, 0))
    in_specs = [
        tok,
        pl.BlockSpec((ROUTER_LANES, tm), lambda i, g: (0, i)),
        pl.BlockSpec((EPG, D, EXPERT_FF), lambda i, g: (g, 0, 0)),
        pl.BlockSpec((EPG, D, EXPERT_FF), lambda i, g: (g, 0, 0)),
        pl.BlockSpec((EPG, EXPERT_FF, D), lambda i, g: (g, 0, 0)),
    ]
    return pl.pallas_call(
        _moe_kernel,
        grid=(t // tm, N_GROUPS),
        in_specs=in_specs,
        out_specs=tok,
        out_shape=jax.ShapeDtypeStruct((t, D), BF16),
        scratch_shapes=[pltpu.VMEM((tm, D), F32)],
        compiler_params=_cparams(("arbitrary",) * 2),
        name="moe",
    )(u2.reshape(t, D), logits_t, wg, wu, wd).reshape(b, l, D)


def _post_kernel(x_ref, m_ref, gt_ref, l2g_ref, l2b_ref, sh_ref, sc_ref, o_ref, u_ref):
    x2 = _ln(ALPHA * x_ref[0] + gt_ref[0] * m_ref[0].astype(F32)) * l2g_ref[...] + l2b_ref[...]
    o_ref[0] = x2
    u_ref[0] = (_ln(x2) * (1.0 + sc_ref[0]) + sh_ref[0]).astype(u_ref.dtype)


def _post(x1, moe_out, gt2, l2g, l2b, sh_next, sc_next):
    b, l, _ = x1.shape
    tm = 512 if l % 512 == 0 else 256
    tok = pl.BlockSpec((1, tm, D), lambda i, j: (i, j, 0))
    vec = pl.BlockSpec((1, 1, D), lambda i, j: (i, 0, 0))
    row = pl.BlockSpec((1, D), lambda i, j: (0, 0))
    sd = jax.ShapeDtypeStruct
    return pl.pallas_call(
        _post_kernel,
        grid=(b, l // tm),
        in_specs=[tok, tok, vec, row, row, vec, vec],
        out_specs=[tok, tok],
        out_shape=[sd((b, l, D), F32), sd((b, l, D), BF16)],
        compiler_params=_cparams(("arbitrary", "arbitrary")),
        name="post_ln",
    )(x1, moe_out, gt2, l2g.reshape(1, D), l2b.reshape(1, D), sh_next, sc_next)


def _rope_tables(l):
    t = np.arange(l)
    quarter = RET_DK // 4
    freqs = (np.float32(ROPE_BASE) ** (-np.arange(quarter, dtype=np.float32) / quarter)).astype(np.float32)
    rows = (t // GRID_W).astype(np.float32)
    cols = (t % GRID_W).astype(np.float32)
    ang = jnp.asarray(np.concatenate([rows[:, None] * freqs, cols[:, None] * freqs], -1))
    cos, sin = jnp.cos(ang), jnp.sin(ang)
    return jnp.concatenate([cos, cos], -1), jnp.concatenate([-sin, sin], -1)


def _layer_weights(p):
    n_rg = 2 * (RET_H * RET_DK + RET_H * RET_DV)
    w_in = p["w_in"]
    w_lr = jnp.pad(w_in[:, 2 * n_rg:2 * n_rg + GLA_LOWRANK], ((0, 0), (0, LANES - GLA_LOWRANK)))
    w2p, b2p = _gla_gate_params(p["gla_w2"], p["gla_b"])
    pad = ROUTER_LANES - N_GROUPS - N_EXPERTS
    return dict(
        p,
        w_ret=w_in[:, :n_rg].astype(BF16),
        w_gla=w_in[:, n_rg:2 * n_rg].astype(BF16),
        w_glr=w_lr.astype(BF16),
        w_rw=w_in[:, 2 * n_rg + GLA_LOWRANK:2 * n_rg + GLA_LOWRANK + 3 * D].astype(BF16),
        w_rlo=w_in[:, 2 * n_rg + GLA_LOWRANK + 3 * D:].astype(BF16),
        w_merge=p["w_merge"].astype(BF16),
        w_br=p["w_br"].astype(BF16),
        w_out=p["w_out"].astype(BF16),
        gla_w2p=w2p, gla_b2p=b2p,
        w_router=jnp.stack(_split2(jnp.pad(jnp.concatenate([p["w_rg"], p["w_re"]], axis=1), ((0, 0), (0, pad))))),
        b_router=jnp.pad(jnp.concatenate([p["b_rg"], p["b_re"]]), (0, pad)).reshape(1, ROUTER_LANES),
        w_eg=p["w_eg"].astype(BF16), w_eu=p["w_eu"].astype(BF16), w_ed=p["w_ed"].astype(BF16),
    )


def _split_mod(mod):
    return [m.reshape(mod.shape[0], 1, D) for m in jnp.split(mod, 6, axis=-1)]


def _layer(x, u, mod, mod_next, s_ret, s_gla_t, s_rw, rope, p):
    b, l, _ = x.shape
    _, _, gt1, sh2, sc2, gt2 = _split_mod(mod)
    sh_next, sc_next = _split_mod(mod_next)[:2]
    u2d = u.reshape(b * l, D)

    def proj(w, tn, bias=None, **kw):
        bias = jnp.zeros((w.shape[1],), F32) if bias is None else bias
        return _proj(u2d, w, bias, tn, **kw).reshape(b, l, -1)

    pr = proj(p["w_ret"], 1024)
    pg = proj(p["w_gla"], 1024)
    plr = proj(p["w_glr"], LANES, out_dtype=F32)
    rw = proj(p["w_rw"], 1024)
    rlo = proj(p["w_rlo"], N_RW_LORA, out_dtype=F32)
    gates = proj(p["w_merge"], 1024, bias=p["b_merge"], act="sigmoid")

    cos2, sin2 = rope if rope is not None else (jnp.ones((l, RET_DK), F32), jnp.zeros((l, RET_DK), F32))
    orf, orb, ret_fin = _retention(pr, cos2, sin2, s_ret, rope is not None)
    ogf, ogb, gla_fin_t = _gla(pg, plr, p["gla_w2p"], p["gla_b2p"], s_gla_t)
    r, v, kk, lw, kd, be, bonus, grw = _rw_prep(rw, rlo, p["rwkv_shift"], p["rwkv_kk"], p["rwkv_ka"],
                                                p["rwkv_rk"].reshape(-1), p["rwkv_w0"], p["rwkv_wb"],
                                                p["rwkv_a0"], p["rwkv_ab"], p["rwkv_gb"])
    yf, yb, rw_fin = _rw_scan(r, v, kk, lw, kd, be, s_rw)
    x1, u2, logits_t = _merge(orf, orb, pr, ogf, ogb, pg, yf, yb, bonus, grw, gates, x, gt1, sh2, sc2,
                              p["w_br"], p["w_out"], p["ret_gn"], p["gla_gn"], p["rwkv_gn_g"], p["rwkv_gn_b"],
                              p["ln1_g"], p["ln1_b"], p["w_router"], p["b_router"])
    moe_out = _moe(u2, logits_t, p["w_eg"], p["w_eu"], p["w_ed"])
    x2, u_next = _post(x1, moe_out, gt2, p["ln2_g"], p["ln2_b"], sh_next, sc_next)
    return x2, u_next, (ret_fin, gla_fin_t, rw_fin)


_PARAM_NAMES = ("w_in", "rwkv_shift", "ret_gn", "gla_w2", "gla_b", "gla_gn", "rwkv_w0", "rwkv_wb", "rwkv_a0",
                "rwkv_ab", "rwkv_gb", "rwkv_kk", "rwkv_ka", "rwkv_rk", "rwkv_gn_g", "rwkv_gn_b", "w_br", "w_merge",
                "b_merge", "w_out", "ln1_g", "ln1_b", "ln2_g", "ln2_b", "w_rg", "b_rg", "w_re", "b_re", "w_eg",
                "w_eu", "w_ed")


def kernel(x_prompt, x_sample, state_ret, state_gla, state_rwkv, c, c_ctx, w_ada, b_ada, w_in, rwkv_shift, ret_gn,
           gla_w2, gla_b, gla_gn, rwkv_w0, rwkv_wb, rwkv_a0, rwkv_ab, rwkv_gb, rwkv_kk, rwkv_ka, rwkv_rk, rwkv_gn_g,
           rwkv_gn_b, w_br, w_merge, b_merge, w_out, ln1_g, ln1_b, ln2_g, ln2_b, w_rg, b_rg, w_re, b_re, w_eg, w_eu,
           w_ed):
    params = dict(zip(_PARAM_NAMES, (w_in, rwkv_shift, ret_gn, gla_w2, gla_b, gla_gn, rwkv_w0, rwkv_wb, rwkv_a0,
                                     rwkv_ab, rwkv_gb, rwkv_kk, rwkv_ka, rwkv_rk, rwkv_gn_g, rwkv_gn_b, w_br, w_merge,
                                     b_merge, w_out, ln1_g, ln1_b, ln2_g, ln2_b, w_rg, b_rg, w_re, b_re, w_eg, w_eu,
                                     w_ed)))
    bc, bl = x_prompt.shape[0], x_sample.shape[0]
    rope = _rope_tables(x_sample.shape[1])
    mod_rows = 2 * SUBLANES
    c_all = jnp.concatenate([c, c_ctx[None, :], jnp.zeros((mod_rows - bl - 1, D), F32)], axis=0)
    z_ret = jnp.zeros((bc, 2, RET_H, RET_DK, RET_DV), F32)
    z_gla_t = jnp.zeros((bc, 2, GLA_H, GLA_DV, GLA_DK), F32)
    z_rw = jnp.zeros((bc, 2, RW_H, RW_N, RW_N), F32)
    mods = [_modulation(c_all, w_ada[layer], b_ada[layer]) for layer in range(DEPTH)]
    mods.append(jnp.zeros_like(mods[0]))
    mods_lat = [m[:bl] for m in mods]
    mods_ctx = [jnp.broadcast_to(m[bl:bl + 1], (bc, 6 * D)) for m in mods]
    h_ctx, h_lat = x_prompt, x_sample
    sh, sc = _split_mod(mods_ctx[0])[:2]
    u_ctx = _lnmod(h_ctx, sh, sc)
    sh, sc = _split_mod(mods_lat[0])[:2]
    u_lat = _lnmod(h_lat, sh, sc)
    new_ret, new_gla, new_rw = [], [], []
    for layer in range(DEPTH):
        p = _layer_weights({k: v[layer] for k, v in params.items()})
        h_ctx, u_ctx, (s_ret, s_gla_t, s_rw) = _layer(h_ctx, u_ctx, mods_ctx[layer], mods_ctx[layer + 1],
                                                      z_ret, z_gla_t, z_rw, None, p)
        new_ret.append(s_ret)
        new_gla.append(jnp.swapaxes(s_gla_t, -1, -2))
        new_rw.append(s_rw)
        h_lat, u_lat, _ = _layer(h_lat, u_lat, mods_lat[layer], mods_lat[layer + 1], state_ret[:, layer],
                                 jnp.swapaxes(state_gla[:, layer], -1, -2), state_rwkv[:, layer], rope, p)
    return (h_ctx, h_lat, jnp.stack(new_ret, axis=1), jnp.stack(new_gla, axis=1), jnp.stack(new_rw, axis=1))
```

```python
import functools

import jax
import jax.numpy as jnp
import numpy as np
from jax import lax
from jax.experimental import pallas as pl
from jax.experimental.pallas import tpu as pltpu

F32 = jnp.float32
BF16 = jnp.bfloat16
HI = lax.Precision.HIGHEST

D = 1024
DEPTH = 2
GRID_W = 64
RET_H, RET_DK, RET_DV = 4, 128, 256
GLA_H, GLA_DK, GLA_DV = 4, 128, 256
GLA_LOWRANK = 16
GLA_GATE_NORM = 16.0
RW_H, RW_N = 16, 64
RW_W_LORA, RW_A_LORA, RW_G_LORA = 64, 64, 128
N_RW_COLS = 3 * D + RW_W_LORA + RW_A_LORA + RW_G_LORA
N_GROUPS, EPG, N_EXPERTS, EXPERT_FF = 4, 4, 16, 512
ALPHA = (2 * DEPTH) ** 0.25
EPS = 1e-5
ROPE_BASE = 10000.0

LANES = 128
SUBLANES = 8
VMEM_LIMIT = 56 * 1024 * 1024

RET_C = 128
GLA_C = 64
GLA_LEVELS = 6
RW_C = 32
RW_LEVELS = 5
RW_TL = 256
RW_BB = 2
RW_G = 4
RW_GW = RW_G * RW_N
RW_SW = 2 * RW_N


def _cparams(sem):
    return pltpu.CompilerParams(dimension_semantics=sem, vmem_limit_bytes=VMEM_LIMIT)


def _mm(a, b):
    return jnp.dot(a.astype(BF16), b.astype(BF16), preferred_element_type=F32)


def _mm_nt(a, b):
    return lax.dot_general(a.astype(BF16), b.astype(BF16), (((1,), (1,)), ((), ())), preferred_element_type=F32)


def _mm_tn(a, b):
    return lax.dot_general(a.astype(BF16), b.astype(BF16), (((0,), (0,)), ((), ())), preferred_element_type=F32)


def _mm_hi(a, b):
    return jnp.dot(a, b, preferred_element_type=F32, precision=HI)


def _split2(x):
    hi = x.astype(BF16)
    return hi, (x - hi.astype(F32)).astype(BF16)


def _mm_split(a, b_hi, b_lo):
    a_hi, a_lo = _split2(a)
    dot = functools.partial(jnp.dot, preferred_element_type=F32)
    return dot(a_hi, b_hi) + (dot(a_hi, b_lo) + dot(a_lo, b_hi))


def _sigmoid(x):
    return 1.0 / (1.0 + jnp.exp(-x))


def _silu(x):
    return x * _sigmoid(x)


def _log_sigmoid(x):
    return jnp.minimum(x, 0.0) - jnp.log(1.0 + jnp.exp(-jnp.abs(x)))


def _softplus(x):
    return jnp.maximum(x, 0.0) + jnp.log(1.0 + jnp.exp(-jnp.abs(x)))


def _ln(x):
    mu = jnp.mean(x, axis=-1, keepdims=True)
    xc = x - mu
    var = jnp.mean(xc * xc, axis=-1, keepdims=True)
    return xc * lax.rsqrt(var + EPS)


def _scan_rows(x, reverse):
    n = x.shape[0]
    row = lax.broadcasted_iota(jnp.int32, x.shape, 0)
    sh = 1
    while sh < n:
        if reverse:
            x = x + jnp.where(row < n - sh, pltpu.roll(x, n - sh, 0), 0.0)
        else:
            x = x + jnp.where(row >= sh, pltpu.roll(x, sh, 0), 0.0)
        sh *= 2
    return x


def _mod_kernel(c_ref, w_ref, b_ref, o_ref):
    o_ref[...] = _mm_hi(_silu(c_ref[...]), w_ref[...]) + b_ref[...]


def _modulation(c_all, w, b):
    m, n, tn = c_all.shape[0], w.shape[1], 512
    return pl.pallas_call(
        _mod_kernel,
        grid=(n // tn,),
        in_specs=[pl.BlockSpec((m, D), lambda j: (0, 0)),
                  pl.BlockSpec((D, tn), lambda j: (0, j)),
                  pl.BlockSpec((1, tn), lambda j: (0, j))],
        out_specs=pl.BlockSpec((m, tn), lambda j: (0, j)),
        out_shape=jax.ShapeDtypeStruct((m, n), F32),
        compiler_params=_cparams(("arbitrary",)),
        name="adaln_mod",
    )(c_all, w, b.reshape(1, n))


def _lnmod_kernel(x_ref, sh_ref, sc_ref, u_ref):
    u_ref[0] = (_ln(x_ref[0]) * (1.0 + sc_ref[0]) + sh_ref[0]).astype(u_ref.dtype)


def _lnmod(x, sh, sc):
    b, l, _ = x.shape
    tm = 512 if l % 512 == 0 else 256
    vec = pl.BlockSpec((1, 1, D), lambda i, j: (i, 0, 0))
    return pl.pallas_call(
        _lnmod_kernel,
        grid=(b, l // tm),
        in_specs=[pl.BlockSpec((1, tm, D), lambda i, j: (i, j, 0)), vec, vec],
        out_specs=pl.BlockSpec((1, tm, D), lambda i, j: (i, j, 0)),
        out_shape=jax.ShapeDtypeStruct((b, l, D), BF16),
        compiler_params=_cparams(("arbitrary", "arbitrary")),
        name="ln_mod",
    )(x, sh, sc)


def _proj_kernel(u_ref, w_ref, b_ref, o_ref, *, act):
    y = jnp.dot(u_ref[...], w_ref[...], preferred_element_type=F32) + b_ref[...]
    if act == "sigmoid":
        y = _sigmoid(y)
    o_ref[...] = y.astype(o_ref.dtype)


def _proj(u2d, w, bias, tn, act=None, out_dtype=BF16):
    t, n = u2d.shape[0], w.shape[1]
    tm = min(t, 2048)
    return pl.pallas_call(
        functools.partial(_proj_kernel, act=act),
        grid=(t // tm, n // tn),
        in_specs=[pl.BlockSpec((tm, D), lambda i, j: (i, 0)),
                  pl.BlockSpec((D, tn), lambda i, j: (0, j)),
                  pl.BlockSpec((1, tn), lambda i, j: (0, j))],
        out_specs=pl.BlockSpec((tm, tn), lambda i, j: (i, j)),
        out_shape=jax.ShapeDtypeStruct((t, n), out_dtype),
        compiler_params=_cparams(("arbitrary", "arbitrary")),
        name="proj",
    )(u2d, w, bias.reshape(1, n))


def _ret_log_gamma(d, h):
    hh = h if d == 0 else RET_H - 1 - h
    return float(np.log(1.0 - 2.0 ** (-5.0 - hh)))


def _ret_tables():
    c = RET_C
    i = np.arange(c, dtype=np.float64)
    dec = np.zeros((2 * RET_H, c, c), np.float32)
    qd = np.zeros((2 * RET_H, c, RET_DK), np.float32)
    kd = np.zeros((2 * RET_H, c, RET_DK), np.float32)
    for d in range(2):
        tau = i if d == 0 else c - 1 - i
        rel = tau[:, None] - tau[None, :]
        for h in range(RET_H):
            lg = _ret_log_gamma(d, h)
            dec[d * RET_H + h] = np.where(rel >= 0, np.exp(np.maximum(rel, 0.0) * lg), 0.0)
            qd[d * RET_H + h] = np.exp((tau + 1.0) * lg)[:, None]
            kd[d * RET_H + h] = np.exp((c - 1.0 - tau) * lg)[:, None]
    return jnp.asarray(dec), jnp.asarray(qd), jnp.asarray(kd)


def _rope(x, cos2, sin2):
    return x * cos2 + pltpu.roll(x, RET_DK // 2, 1) * sin2


def _ret_kernel(qf_ref, kf_ref, vf_ref, qb_ref, kb_ref, vb_ref, cosf_ref, sinf_ref, cosb_ref, sinb_ref,
                dec_ref, qd_ref, kd_ref, s0_ref, of_ref, ob_ref, sfin_ref, s_ref, *, use_rope, n_chunks):
    n = pl.program_id(1)

    @pl.when(n == 0)
    def _():
        s_ref[...] = s0_ref[0]

    ins = ((qf_ref, kf_ref, vf_ref, cosf_ref, sinf_ref), (qb_ref, kb_ref, vb_ref, cosb_ref, sinb_ref))
    outs = (of_ref, ob_ref)
    chains = [(d, h) for d in range(2) for h in range(RET_H)]
    q, k, v = {}, {}, {}
    for d, h in chains:
        q_ref, k_ref, v_ref, cos_ref, sin_ref = ins[d]
        dk = slice(h * RET_DK, (h + 1) * RET_DK)
        q[d, h] = q_ref[0, :, dk].astype(F32)
        k[d, h] = k_ref[0, :, dk].astype(F32) * (RET_DK ** -0.5)
        if use_rope:
            q[d, h] = _rope(q[d, h], cos_ref[...], sin_ref[...])
            k[d, h] = _rope(k[d, h], cos_ref[...], sin_ref[...])
        v[d, h] = v_ref[0, :, h * RET_DV:(h + 1) * RET_DV]
    sc = {ch: _mm_nt(q[ch], k[ch]) * dec_ref[ch[0] * RET_H + ch[1]] for ch in chains}
    s_old = {ch: s_ref[ch[0], ch[1]] for ch in chains}
    o = {ch: _mm(sc[ch], v[ch]) + _mm(q[ch] * qd_ref[ch[0] * RET_H + ch[1]], s_old[ch]) for ch in chains}
    s_new = {ch: s_old[ch] * float(np.exp(RET_C * _ret_log_gamma(*ch)))
             + _mm_tn(k[ch] * kd_ref[ch[0] * RET_H + ch[1]], v[ch]) for ch in chains}
    for d, h in chains:
        outs[d][0, :, h * RET_DV:(h + 1) * RET_DV] = o[d, h].astype(outs[d].dtype)
        s_ref[d, h] = s_new[d, h]

    @pl.when(n == n_chunks - 1)
    def _():
        sfin_ref[0] = s_ref[...]


def _retention(pr, cos2, sin2, s0, use_rope):
    b, l, _ = pr.shape
    c = RET_C
    nc = l // c
    dec, qd, kd = _ret_tables()
    qw, vw = RET_H * RET_DK, RET_H * RET_DV

    def fw(blk):
        return lambda i, n: (i, n, blk)

    def bw(blk):
        return lambda i, n: (i, nc - 1 - n, blk)

    def const(*shape):
        return pl.BlockSpec(shape, lambda i, n: (0,) * len(shape))

    st = pl.BlockSpec((1, 2, RET_H, RET_DK, RET_DV), lambda i, n: (i, 0, 0, 0, 0))
    in_specs = [
        pl.BlockSpec((1, c, qw), fw(0)), pl.BlockSpec((1, c, qw), fw(1)), pl.BlockSpec((1, c, vw), fw(1)),
        pl.BlockSpec((1, c, qw), bw(0)), pl.BlockSpec((1, c, qw), bw(1)), pl.BlockSpec((1, c, vw), bw(1)),
        pl.BlockSpec((c, RET_DK), lambda i, n: (n, 0)), pl.BlockSpec((c, RET_DK), lambda i, n: (n, 0)),
        pl.BlockSpec((c, RET_DK), lambda i, n: (nc - 1 - n, 0)), pl.BlockSpec((c, RET_DK), lambda i, n: (nc - 1 - n, 0)),
        const(2 * RET_H, c, c), const(2 * RET_H, c, RET_DK), const(2 * RET_H, c, RET_DK), st,
    ]
    out_specs = [pl.BlockSpec((1, c, vw), fw(0)), pl.BlockSpec((1, c, vw), bw(0)), st]
    sd = jax.ShapeDtypeStruct
    return pl.pallas_call(
        functools.partial(_ret_kernel, use_rope=use_rope, n_chunks=nc),
        grid=(b, nc),
        in_specs=in_specs,
        out_specs=out_specs,
        out_shape=[sd((b, l, vw), BF16), sd((b, l, vw), BF16), sd((b, 2, RET_H, RET_DK, RET_DV), F32)],
        scratch_shapes=[pltpu.VMEM((2, RET_H, RET_DK, RET_DV), F32)],
        compiler_params=_cparams(("arbitrary", "arbitrary")),
        name="retention",
    )(pr, pr, pr, pr, pr, pr, cos2, sin2, cos2, sin2, dec, qd, kd, s0)


def _gla_boundary(b, lv, reverse, rolls):
    c, w = b.shape
    m = 1 << lv
    if m >= SUBLANES:
        parts = []
        for p0 in range(0, c, 2 * m):
            e = p0 + m if reverse else p0 + m - 1
            parts.append(jnp.broadcast_to(b[e:e + 1, :], (2 * m, w)))
        return jnp.concatenate(parts, axis=0)

    def rolled(s):
        s %= c
        if s not in rolls:
            rolls[s] = b if s == 0 else pltpu.roll(b, s, 0)
        return rolls[s]

    row = lax.broadcasted_iota(jnp.int32, (c, w), 0)
    r = jnp.bitwise_and(row, m - 1)
    upper = jnp.bitwise_and(jnp.right_shift(row, lv), 1) == 1
    out = b
    for t in range(m):
        if reverse:
            out = jnp.where(upper & (r == t), rolled(t), out)
            out = jnp.where(jnp.logical_not(upper) & (r == t), rolled(-(m - t)), out)
        else:
            out = jnp.where(upper & (r == t), rolled(t + 1), out)
            out = jnp.where(jnp.logical_not(upper) & (r == t), rolled(-(m - 1 - t)), out)
    return out


def _gla_kernel(qf_ref, kf_ref, vf_ref, lrf_ref, qb_ref, kb_ref, vb_ref, lrb_ref, w2_ref, b2_ref, s0_ref,
                of_ref, ob_ref, sfin_ref, st_ref, *, n_chunks):
    c = GLA_C
    n = pl.program_id(1)

    @pl.when(n == 0)
    def _():
        st_ref[...] = s0_ref[0]

    ins = ((qf_ref, kf_ref, vf_ref, lrf_ref), (qb_ref, kb_ref, vb_ref, lrb_ref))
    outs = (of_ref, ob_ref)
    hw = GLA_H * GLA_DK
    row = lax.broadcasted_iota(jnp.int32, (c, hw), 0)
    ri = lax.broadcasted_iota(jnp.int32, (c, c), 0)
    ci = lax.broadcasted_iota(jnp.int32, (c, c), 1)
    q, k, v, b_inc, b_rest, tot, ql, kl = {}, {}, {}, {}, {}, {}, {}, {}
    for d in range(2):
        reverse = d == 1
        q_ref, k_ref, v_ref, lr_ref = ins[d]
        q[d] = q_ref[0].astype(F32) * (GLA_DK ** -0.5)
        k[d] = k_ref[0].astype(F32)
        v[d] = v_ref[0]
        gate = _log_sigmoid(_mm_split(lr_ref[0], w2_ref[d, 0], w2_ref[d, 1]) + b2_ref[d])
        gate = gate * (1.0 / GLA_GATE_NORM)
        b = _scan_rows(gate, reverse)
        tot[d] = b[0:1] if reverse else b[c - 1:c]
        b_inc[d], b_rest[d] = b, tot[d] - b
        rolls = {}
        for lv in range(GLA_LEVELS):
            be = _gla_boundary(b, lv, reverse, rolls)
            upper = jnp.bitwise_and(jnp.right_shift(row, lv), 1) == 1
            second = jnp.logical_not(upper) if reverse else upper
            e = jnp.exp(jnp.where(second, b - be, be - b))
            ql[d, lv] = jnp.where(second, q[d] * e, 0.0)
            kl[d, lv] = jnp.where(second, 0.0, k[d] * e)
    chains = [(d, h) for d in range(2) for h in range(GLA_H)]

    def dk(h):
        return slice(h * GLA_DK, (h + 1) * GLA_DK)

    def dv(h):
        return slice(h * GLA_DV, (h + 1) * GLA_DV)

    attn = {(d, h): jnp.where(ri == ci, jnp.sum(q[d][:, dk(h)] * k[d][:, dk(h)], axis=-1, keepdims=True), 0.0)
            for d, h in chains}
    for lv in range(GLA_LEVELS):
        same = jnp.right_shift(ri, lv + 1) == jnp.right_shift(ci, lv + 1)
        for d, h in chains:
            attn[d, h] = attn[d, h] + jnp.where(same, _mm_nt(ql[d, lv][:, dk(h)], kl[d, lv][:, dk(h)]), 0.0)
    st_old = {ch: st_ref[ch[0], ch[1]] for ch in chains}
    o = {(d, h): _mm(attn[d, h], v[d][:, dv(h)]) + _mm_nt(q[d][:, dk(h)] * jnp.exp(b_inc[d][:, dk(h)]), st_old[d, h])
         for d, h in chains}
    st_new = {(d, h): st_old[d, h] * jnp.exp(tot[d][:, dk(h)])
              + _mm_tn(v[d][:, dv(h)], k[d][:, dk(h)] * jnp.exp(b_rest[d][:, dk(h)])) for d, h in chains}
    for d, h in chains:
        outs[d][0, :, dv(h)] = o[d, h].astype(outs[d].dtype)
        st_ref[d, h] = st_new[d, h]

    @pl.when(n == n_chunks - 1)
    def _():
        sfin_ref[0] = st_ref[...]


def _gla(pg, plr, w2p, b2, s0t):
    b, l, _ = pg.shape
    c = GLA_C
    nc = l // c
    qw, vw = GLA_H * GLA_DK, GLA_H * GLA_DV

    def fw(blk):
        return lambda i, n: (i, n, blk)

    def bw(blk):
        return lambda i, n: (i, nc - 1 - n, blk)

    st = pl.BlockSpec((1, 2, GLA_H, GLA_DV, GLA_DK), lambda i, n: (i, 0, 0, 0, 0))
    in_specs = [
        pl.BlockSpec((1, c, qw), fw(0)), pl.BlockSpec((1, c, qw), fw(1)), pl.BlockSpec((1, c, vw), fw(1)),
        pl.BlockSpec((1, c, LANES), fw(0)),
        pl.BlockSpec((1, c, qw), bw(0)), pl.BlockSpec((1, c, qw), bw(1)), pl.BlockSpec((1, c, vw), bw(1)),
        pl.BlockSpec((1, c, LANES), bw(0)),
        pl.BlockSpec((2, 2, LANES, qw), lambda i, n: (0, 0, 0, 0)),
        pl.BlockSpec((2, 1, qw), lambda i, n: (0, 0, 0)),
        st,
    ]
    out_specs = [pl.BlockSpec((1, c, vw), fw(0)), pl.BlockSpec((1, c, vw), bw(0)), st]
    sd = jax.ShapeDtypeStruct
    return pl.pallas_call(
        functools.partial(_gla_kernel, n_chunks=nc),
        grid=(b, nc),
        in_specs=in_specs,
        out_specs=out_specs,
        out_shape=[sd((b, l, vw), BF16), sd((b, l, vw), BF16), sd((b, 2, GLA_H, GLA_DV, GLA_DK), F32)],
        scratch_shapes=[pltpu.VMEM((2, GLA_H, GLA_DV, GLA_DK), F32)],
        compiler_params=_cparams(("arbitrary", "arbitrary")),
        name="gla",
    )(pg, pg, pg, plr, pg, pg, pg, plr, w2p, b2, s0t)


def _gla_gate_params(w2, b2):
    w2p = jnp.pad(w2, ((0, 0), (0, LANES - GLA_LOWRANK), (0, 0)))
    return jnp.stack(_split2(w2p), axis=1), b2.reshape(2, 1, GLA_H * GLA_DK)


def _head_sum(x):
    i = lax.broadcasted_iota(jnp.int32, (LANES, LANES), 0)
    j = lax.broadcasted_iota(jnp.int32, (LANES, LANES), 1)
    ones = jnp.where(jnp.right_shift(i, 6) == jnp.right_shift(j, 6), 1.0, 0.0).astype(BF16)
    out = []
    for t in range(x.shape[1] // LANES):
        parts = _split2(x[:, t * LANES:(t + 1) * LANES])
        out.append(sum(jnp.dot(p, ones, preferred_element_type=F32) for p in parts))
    return jnp.concatenate(out, axis=1)


RW_TM = 256
HALO = 2 * SUBLANES
N_RW_LORA = RW_W_LORA + RW_A_LORA + RW_G_LORA


def _rw_prep_kernel(x_ref, xp_ref, xn_ref, lo_ref, lop_ref, lon_ref, taps_ref, ltaps_ref, kkp_ref, ka_ref, rk_ref,
                    w0_ref, wb_ref, a0_ref, ab_ref, gb_ref, r_ref, v_ref, kk_ref, lw_ref, kd_ref, be_ref, bonus_ref,
                    g_ref, xbuf, lbuf, *, n_tiles):
    tm = RW_TM
    j = pl.program_id(1)
    first, last = j == 0, j == n_tiles - 1
    xbuf[0:HALO, :] = jnp.where(first, jnp.zeros_like(xp_ref[0]), xp_ref[0])
    xbuf[HALO:HALO + tm, :] = x_ref[0]
    xbuf[HALO + tm:2 * HALO + tm, :] = jnp.where(last, jnp.zeros_like(xn_ref[0]), xn_ref[0])
    ri = lax.broadcasted_iota(jnp.int32, (tm, tm + 2 * HALO), 0)
    ci = lax.broadcasted_iota(jnp.int32, (tm, tm + 2 * HALO), 1)
    pick_prev = jnp.where(ci == ri + (HALO - 1), 1.0, 0.0).astype(BF16)
    pick_next = jnp.where(ci == ri + (HALO + 1), 1.0, 0.0).astype(BF16)

    def shifted_main(c0, c1):
        x_all = xbuf[:, c0:c1]
        return (taps_ref[0:1, c0:c1] * jnp.dot(pick_prev, x_all, preferred_element_type=F32)
                + taps_ref[1:2, c0:c1] * x_ref[0, :, c0:c1].astype(F32)
                + taps_ref[2:3, c0:c1] * jnp.dot(pick_next, x_all, preferred_element_type=F32))

    lbuf[HALO:HALO + tm, :] = lo_ref[0]
    lbuf[0:HALO, :] = jnp.where(first, 0.0, lop_ref[0])
    lbuf[HALO + tm:2 * HALO + tm, :] = jnp.where(last, 0.0, lon_ref[0])
    lora = (ltaps_ref[0:1, :] * lbuf[HALO - 1:HALO - 1 + tm, :] + ltaps_ref[1:2, :] * lbuf[HALO:HALO + tm, :]
            + ltaps_ref[2:3, :] * lbuf[HALO + 1:HALO + 1 + tm, :])
    r = shifted_main(0, D)
    k = shifted_main(D, 2 * D)
    v = shifted_main(2 * D, 3 * D)
    xw = lora[:, 0:RW_W_LORA]
    xa = lora[:, RW_W_LORA:RW_W_LORA + RW_A_LORA]
    xg = lora[:, RW_W_LORA + RW_A_LORA:]
    r_ref[0] = r.astype(r_ref.dtype)
    v_ref[0] = v.astype(v_ref.dtype)
    kk = k * kkp_ref[...]
    kk = kk * lax.rsqrt(jnp.maximum(_head_sum(kk * kk), 1e-24))
    kk_ref[0] = kk.astype(kk_ref.dtype)
    g_ref[0] = _mm(_sigmoid(xg), gb_ref[...]).astype(g_ref.dtype)
    wh = jnp.tanh(xw)
    kd_sum = jnp.zeros_like(k)
    for d in range(2):
        w = -_softplus(-(w0_ref[d:d + 1, :] + _mm(wh, wb_ref[d]))) - 0.5
        lw_ref[d, 0] = -jnp.exp(w)
        a = _sigmoid(a0_ref[d:d + 1, :] + _mm(xa, ab_ref[d]))
        kd = k * (1.0 + (a - 1.0) * ka_ref[...])
        kd_ref[d, 0] = kd.astype(kd_ref.dtype)
        be_ref[d, 0] = (kk * a).astype(be_ref.dtype)
        kd_sum = kd_sum + kd
    bonus_ref[0] = (_head_sum(r * kd_sum * rk_ref[...]) * v).astype(bonus_ref.dtype)


def _rw_prep(rw, rlo, taps, kkp, ka, rk, w0, wb, a0, ab, gb):
    b, l, _ = rw.shape
    tm = RW_TM
    nt = l // tm
    hb = tm // HALO
    tok = pl.BlockSpec((1, tm, D), lambda i, j: (i, j, 0))
    tokd = pl.BlockSpec((2, 1, tm, D), lambda i, j: (0, i, j, 0))

    def full(*shape):
        return pl.BlockSpec(shape, lambda i, j: (0,) * len(shape))

    def halo_specs(n):
        return [pl.BlockSpec((1, tm, n), lambda i, j: (i, j, 0)),
                pl.BlockSpec((1, HALO, n), lambda i, j: (i, jnp.maximum(j * hb - 1, 0), 0)),
                pl.BlockSpec((1, HALO, n), lambda i, j: (i, jnp.minimum((j + 1) * hb, l // HALO - 1), 0))]

    in_specs = halo_specs(3 * D) + halo_specs(N_RW_LORA) + [
        full(3, 3 * D), full(3, N_RW_LORA), full(1, D), full(1, D), full(1, D),
        full(2, D), full(2, RW_W_LORA, D), full(2, D), full(2, RW_A_LORA, D), full(RW_G_LORA, D),
    ]
    sd = jax.ShapeDtypeStruct
    tok_o, dir_o, dir_f32 = sd((b, l, D), BF16), sd((2, b, l, D), BF16), sd((2, b, l, D), F32)
    return pl.pallas_call(
        functools.partial(_rw_prep_kernel, n_tiles=nt),
        grid=(b, nt),
        in_specs=in_specs,
        out_specs=[tok, tok, tok, tokd, tokd, tokd, tok, tok],
        out_shape=[tok_o, tok_o, tok_o, dir_f32, dir_o, dir_o, tok_o, tok_o],
        scratch_shapes=[pltpu.VMEM((tm + 2 * HALO, 3 * D), BF16), pltpu.VMEM((tm + 2 * HALO, N_RW_LORA), F32)],
        compiler_params=_cparams(("arbitrary", "arbitrary")),
        name="rwkv_prep",
    )(rw, rw, rw, rlo, rlo, rlo, taps[:, :3 * D], taps[:, 3 * D:], kkp.reshape(1, D), ka.reshape(1, D),
      rk.reshape(1, D), w0, wb, a0, ab, gb)


def _tile_rows(x, n):
    return jnp.concatenate([x] * n, axis=0)


def _rw_scan_kernel(rf_ref, vf_ref, kkf_ref, lwf_ref, kdf_ref, bef_ref, rb_ref, vb_ref, kkb_ref, lwb_ref, kdb_ref,
                    beb_ref, s0_ref, yf_ref, yb_ref, sfin_ref, zt_ref, sv_lhs, sv_w, sv_bk, sv_tot, *, n_tiles):
    c, hw, sw = RW_C, RW_GW, RW_SW
    pk = RW_G * c
    n_groups, n_sub = RW_H // RW_G, RW_GW // RW_SW
    n_chunks = rf_ref.shape[1] // c
    j = pl.program_id(1)

    @pl.when(j == 0)
    def _():
        zt_ref[...] = jnp.zeros_like(zt_ref)
        for bb in range(RW_BB):
            for d in range(2):
                for g in range(RW_H // 2):
                    for jj in range(2):
                        blk = slice(RW_N * jj, RW_N * (jj + 1))
                        zt_ref[bb, d, g, blk, blk] = s0_ref[bb, d, 2 * g + jj]

    ins = ((rf_ref, vf_ref, kkf_ref, lwf_ref, kdf_ref, bef_ref), (rb_ref, vb_ref, kkb_ref, lwb_ref, kdb_ref, beb_ref))
    outs = (yf_ref, yb_ref)
    ri = lax.broadcasted_iota(jnp.int32, (c, pk), 0)
    cs = jnp.bitwise_and(lax.broadcasted_iota(jnp.int32, (c, pk), 1), c - 1)
    eye = (ri == cs).astype(F32)
    strict, incl, lvl = [], [], []
    for d in range(2):
        ti = ri if d == 0 else c - 1 - ri
        ts = cs if d == 0 else c - 1 - cs
        strict.append(ts < ti)
        incl.append(ts <= ti)
        lv_masks = []
        for lv in range(RW_LEVELS):
            same = jnp.right_shift(ti, lv + 1) == jnp.right_shift(ts, lv + 1)
            lower = ((jnp.bitwise_and(jnp.right_shift(ti, lv), 1) == 1)
                     & (jnp.bitwise_and(jnp.right_shift(ts, lv), 1) == 0))
            lv_masks.append(same & lower)
        lvl.append(lv_masks)
    bi = lax.broadcasted_iota(jnp.int32, (pk, pk), 0)
    bj = lax.broadcasted_iota(jnp.int32, (pk, pk), 1)
    bd_p = jnp.right_shift(bi, RW_LEVELS) == jnp.right_shift(bj, RW_LEVELS)
    zi = lax.broadcasted_iota(jnp.int32, (sw, sw), 0)
    zj = lax.broadcasted_iota(jnp.int32, (sw, sw), 1)
    bd_z = jnp.right_shift(zi, 6) == jnp.right_shift(zj, 6)
    term_row = lax.broadcasted_iota(jnp.int32, (2 * SUBLANES, sw), 0)
    wi = lax.broadcasted_iota(jnp.int32, (pk, hw), 0)
    wj = lax.broadcasted_iota(jnp.int32, (pk, hw), 1)
    bd_w = jnp.right_shift(wi, RW_LEVELS) == jnp.right_shift(wj, 6)

    def bdp(x):
        return jnp.where(bd_p, _tile_rows(x, RW_G), 0.0)

    def bdw(x):
        return jnp.where(bd_w, _tile_rows(x, RW_G), 0.0)

    chains = [(bb, d, half) for bb in range(RW_BB) for d in range(2) for half in range(n_groups)]

    def each(fn):
        return {ch: fn(ch) for ch in chains}

    def chunk_rows(i):
        return (pl.ds(pl.multiple_of(i * c, c), c), pl.ds(pl.multiple_of((n_chunks - 1 - i) * c, c), c))

    def scale(i, slot):
        rows = chunk_rows(i)

        def load(ch):
            bb, d, half = ch
            ln = slice(half * hw, (half + 1) * hw)
            refs = ins[d]
            return (refs[0][bb, rows[d], ln].astype(F32), refs[2][bb, rows[d], ln].astype(F32),
                    refs[3][0, bb, rows[d], ln], refs[4][0, bb, rows[d], ln].astype(F32),
                    refs[5][0, bb, rows[d], ln].astype(F32))

        x = each(load)
        r, kk, lw = each(lambda ch: x[ch][0]), each(lambda ch: x[ch][1]), each(lambda ch: x[ch][2])
        kd, be = each(lambda ch: x[ch][3]), each(lambda ch: x[ch][4])
        b = each(lambda ch: _scan_rows(lw[ch], ch[1] == 1))
        tot = each(lambda ch: b[ch][0:1] if ch[1] == 1 else b[ch][c - 1:c])
        en = each(lambda ch: jnp.exp(-b[ch]))
        es = each(lambda ch: jnp.exp(tot[ch] - b[ch]))
        for n, ch in enumerate(chains):
            sv_lhs[slot, n] = jnp.concatenate([kk[ch] * jnp.exp(b[ch] - lw[ch]), r[ch] * jnp.exp(b[ch])],
                                              axis=0).astype(BF16)
            sv_w[slot, n] = jnp.concatenate([be[ch] * en[ch], kd[ch] * en[ch]], axis=0).astype(BF16)
            sv_bk[slot, n] = jnp.concatenate([be[ch] * es[ch], kd[ch] * es[ch]], axis=0).astype(BF16)
            sv_tot[slot, n] = jnp.broadcast_to(tot[ch], (SUBLANES, hw))

    scale(0, 0)

    def chunk(i, carry):
        rows = chunk_rows(i)
        slot = lax.rem(i, 2)
        index = {ch: n for n, ch in enumerate(chains)}
        lhs = each(lambda ch: sv_lhs[slot, index[ch]])
        w_en = each(lambda ch: sv_w[slot, index[ch]])
        bk = each(lambda ch: sv_bk[slot, index[ch]])
        tot = each(lambda ch: sv_tot[slot, index[ch]][0:1])
        v = each(lambda ch: ins[ch[1]][1][ch[0], rows[ch[1]], ch[2] * hw:(ch[2] + 1) * hw].astype(F32))
        gb = each(lambda ch: _mm_nt(lhs[ch], bdw(w_en[ch][0:c])))
        gk = each(lambda ch: _mm_nt(lhs[ch], bdw(w_en[ch][c:])))
        a_ab = each(lambda ch: jnp.where(strict[ch[1]], gb[ch][0:c], 0.0))
        a_rb = each(lambda ch: jnp.where(incl[ch[1]], gb[ch][c:], 0.0))
        a_ak = each(lambda ch: jnp.where(strict[ch[1]], gk[ch][0:c], 0.0))
        a_rk = each(lambda ch: jnp.where(incl[ch[1]], gk[ch][c:], 0.0))
        t_inv = each(lambda ch: eye - jnp.where(lvl[ch[1]][0], a_ab[ch], 0.0))
        for lv in range(1, RW_LEVELS):
            xm = each(lambda ch: _mm(jnp.where(lvl[ch[1]][lv], a_ab[ch], 0.0), bdp(t_inv[ch])))
            t_inv = each(lambda ch: t_inv[ch] - _mm(t_inv[ch], bdp(xm[ch])))
            if lv == 1:
                scale(jnp.minimum(i + 1, n_chunks - 1), 1 - slot)
        zt_old = {(ch, g): zt_ref[ch[0], ch[1], n_sub * ch[2] + g] for ch in chains for g in range(n_sub)}
        zz = each(lambda ch: jnp.concatenate(
            [_mm(lhs[ch][:, g * sw:(g + 1) * sw], zt_old[ch, g]) for g in range(n_sub)], axis=1))
        av = each(lambda ch: _mm(jnp.concatenate([a_ak[ch], a_rk[ch]], axis=0), bdw(v[ch])))
        u = each(lambda ch: _mm(t_inv[ch], bdw(-(zz[ch][0:c] + av[ch][0:c]))))
        y = each(lambda ch: zz[ch][c:] + _mm(a_rb[ch], bdw(u[ch])) + av[ch][c:])
        uv = each(lambda ch: jnp.concatenate([u[ch], v[ch]], axis=0))

        def decay_cols(ch, gl):
            x = tot[ch][:, gl]
            hi = x.astype(BF16).astype(F32)
            mid = (x - hi).astype(BF16).astype(F32)
            lo = x - hi - mid
            terms = jnp.where(term_row == 0, hi, jnp.where(term_row == 1, mid, jnp.where(term_row == 2, lo, 0.0)))
            return jnp.exp(_mm_tn(terms, jnp.ones((2 * SUBLANES, sw), BF16)))

        zt_new = {}
        for ch in chains:
            for g in range(n_sub):
                gl = slice(g * sw, (g + 1) * sw)
                upd = jnp.where(bd_z, _mm_tn(bk[ch][:, gl], uv[ch][:, gl]), 0.0)
                zt_new[ch, g] = zt_old[ch, g] * decay_cols(ch, gl) + upd
        for ch in chains:
            bb, d, half = ch
            outs[d][bb, rows[d], half * hw:(half + 1) * hw] = y[ch].astype(outs[d].dtype)
            for g in range(n_sub):
                zt_ref[bb, d, n_sub * half + g] = zt_new[ch, g]
        return carry

    lax.fori_loop(0, n_chunks, chunk, 0)

    @pl.when(j == n_tiles - 1)
    def _():
        for bb in range(RW_BB):
            for d in range(2):
                for g in range(RW_H // 2):
                    for jj in range(2):
                        blk = slice(RW_N * jj, RW_N * (jj + 1))
                        sfin_ref[bb, d, 2 * g + jj] = zt_ref[bb, d, g, blk, blk]


def _rw_scan(r, v, kk, lw, kd, be, s0):
    b, l, _ = r.shape
    tl, nb = min(RW_TL, l), RW_BB
    nt = l // tl
    n_ch = nb * 2 * (RW_H // RW_G)
    tok_f = pl.BlockSpec((nb, tl, D), lambda i, j: (i, j, 0))
    tok_b = pl.BlockSpec((nb, tl, D), lambda i, j: (i, nt - 1 - j, 0))
    dir_f = pl.BlockSpec((1, nb, tl, D), lambda i, j: (0, i, j, 0))
    dir_b = pl.BlockSpec((1, nb, tl, D), lambda i, j: (1, i, nt - 1 - j, 0))
    st = pl.BlockSpec((nb, 2, RW_H, RW_N, RW_N), lambda i, j: (i, 0, 0, 0, 0))
    sd = jax.ShapeDtypeStruct
    return pl.pallas_call(
        functools.partial(_rw_scan_kernel, n_tiles=nt),
        grid=(b // nb, nt),
        in_specs=[tok_f, tok_f, tok_f, dir_f, dir_f, dir_f, tok_b, tok_b, tok_b, dir_b, dir_b, dir_b, st],
        out_specs=[tok_f, tok_b, st],
        out_shape=[sd((b, l, D), BF16), sd((b, l, D), BF16), sd((b, 2, RW_H, RW_N, RW_N), F32)],
        scratch_shapes=[pltpu.VMEM((nb, 2, RW_H // 2, RW_SW, RW_SW), F32),
                        pltpu.VMEM((2, n_ch, 2 * RW_C, RW_GW), BF16),
                        pltpu.VMEM((2, n_ch, 2 * RW_C, RW_GW), BF16),
                        pltpu.VMEM((2, n_ch, 2 * RW_C, RW_GW), BF16),
                        pltpu.VMEM((2, n_ch, SUBLANES, RW_GW), F32)],
        compiler_params=_cparams(("arbitrary", "arbitrary")),
        name="rwkv_scan",
    )(r, v, kk, lw, kd, be, r, v, kk, lw, kd, be, s0)


MERGE_TM = 128
ROUTER_LANES = LANES


def _merge_kernel(orf_ref, orb_ref, rg_ref, ogf_ref, ogb_ref, gg_ref, yf_ref, yb_ref, bonus_ref, grw_ref,
                  gates_ref, x_ref, gt_ref, sh_ref, sc_ref, wbr_ref, wout_ref, rgn_ref, ggn_ref, gng_ref, gnb_ref,
                  l1g_ref, l1b_ref, wr_ref, br_ref, x1_ref, u2_ref, lg_ref):
    f32 = lambda ref: ref[0].astype(F32)
    o_ret = f32(orf_ref) + f32(orb_ref)
    o_gla = f32(ogf_ref) + f32(ogb_ref)
    y_ret, y_gla = [], []
    for h in range(RET_H):
        hs = slice(h * RET_DV, (h + 1) * RET_DV)
        y_ret.append(_ln(o_ret[:, hs]))
        og = o_gla[:, hs]
        y_gla.append(og * lax.rsqrt(jnp.mean(og * og, axis=-1, keepdims=True) + EPS))
    z_ret = _silu(f32(rg_ref)) * (jnp.concatenate(y_ret, axis=1) * rgn_ref[...])
    z_gla = _silu(f32(gg_ref)) * (jnp.concatenate(y_gla, axis=1) * ggn_ref[...])
    y = f32(yf_ref) + f32(yb_ref)
    mu = _head_sum(y) * (1.0 / RW_N)
    yc = y - mu
    var = _head_sum(yc * yc) * (1.0 / RW_N)
    z_rw = (yc * lax.rsqrt(var + EPS) * gng_ref[...] + gnb_ref[...] + f32(bonus_ref)) * f32(grw_ref)
    gates = f32(gates_ref)
    mixed = (gates[:, 0:D] * _mm(z_ret, wbr_ref[0]) + gates[:, D:2 * D] * _mm(z_gla, wbr_ref[1])
             + gates[:, 2 * D:] * _mm(z_rw, wbr_ref[2]))
    mix = _mm(mixed, wout_ref[...])
    x1 = _ln(ALPHA * x_ref[0] + gt_ref[0] * mix) * l1g_ref[...] + l1b_ref[...]
    x1_ref[0] = x1
    u2 = _ln(x1) * (1.0 + sc_ref[0]) + sh_ref[0]
    u2_ref[0] = u2.astype(u2_ref.dtype)
    lg_ref[...] = (_mm_split(u2, wr_ref[0], wr_ref[1]) + br_ref[...]).T


def _merge(orf, orb, pr, ogf, ogb, pg, yf, yb, bonus, grw, gates, x, gt1, sh2, sc2, wbr, wout, rgn, ggn, gng, gnb,
           l1g, l1b, wr, br):
    b, l, _ = x.shape
    tm = MERGE_TM
    tok = pl.BlockSpec((1, tm, D), lambda i, j: (i, j, 0))
    gate_blk = pl.BlockSpec((1, tm, D), lambda i, j: (i, j, 2))
    vec = pl.BlockSpec((1, 1, D), lambda i, j: (i, 0, 0))
    row = pl.BlockSpec((1, D), lambda i, j: (0, 0))
    in_specs = [
        tok, tok, gate_blk, tok, tok, gate_blk, tok, tok, tok, tok,
        pl.BlockSpec((1, tm, 3 * D), lambda i, j: (i, j, 0)),
        tok, vec, vec, vec,
        pl.BlockSpec((3, D, D), lambda i, j: (0, 0, 0)),
        pl.BlockSpec((D, D), lambda i, j: (0, 0)),
        row, row, row, row, row, row,
        pl.BlockSpec((2, D, ROUTER_LANES), lambda i, j: (0, 0, 0)),
        pl.BlockSpec((1, ROUTER_LANES), lambda i, j: (0, 0)),
    ]
    sd = jax.ShapeDtypeStruct
    r1 = lambda a: a.reshape(1, D)
    return pl.pallas_call(
        _merge_kernel,
        grid=(b, l // tm),
        in_specs=in_specs,
        out_specs=[tok, tok, pl.BlockSpec((ROUTER_LANES, tm), lambda i, j: (0, i * (l // tm) + j))],
        out_shape=[sd((b, l, D), F32), sd((b, l, D), BF16), sd((ROUTER_LANES, b * l), F32)],
        compiler_params=_cparams(("arbitrary", "arbitrary")),
        name="merge",
    )(orf, orb, pr, ogf, ogb, pg, yf, yb, bonus, grw, gates, x, gt1, sh2, sc2, wbr, wout,
      r1(rgn), r1(ggn), r1(gng), r1(gnb), r1(l1g), r1(l1b), wr, br)


MOE_TM = 1024


def _moe_block_rows(tm):
    rows = tm / N_GROUPS + 4.0 * np.sqrt(tm * (N_GROUPS - 1.0)) / N_GROUPS
    bf16_tile = 2 * SUBLANES
    return int(min(tm, -(-rows // bf16_tile) * bf16_tile))


def _first_argmax_rows(x, row):
    m = jnp.max(x, axis=0, keepdims=True)
    idx = jnp.min(jnp.where(x == m, row, x.shape[0]), axis=0, keepdims=True)
    return m, idx


def _routing(lt):
    tm = lt.shape[1]
    gl = lt[0:N_GROUPS]
    gmax, gidx = _first_argmax_rows(gl, lax.broadcasted_iota(jnp.int32, (N_GROUPS, tm), 0))
    g_w = 1.0 / jnp.sum(jnp.exp(gl - gmax), axis=0, keepdims=True)
    row = lax.broadcasted_iota(jnp.int32, (N_EXPERTS, tm), 0)
    neg = -jnp.inf
    el = jnp.where(jnp.right_shift(row, 2) == gidx, lt[N_GROUPS:N_GROUPS + N_EXPERTS], neg)
    m1, i1 = _first_argmax_rows(el, row)
    m2, i2 = _first_argmax_rows(jnp.where(row == i1, neg, el), row)
    e2 = jnp.exp(m2 - m1)
    w1 = 1.0 / (1.0 + e2)
    return gidx, g_w * (jnp.where(row == i1, w1, 0.0) + jnp.where(row == i2, e2 * w1, 0.0))


def _moe_kernel(u_ref, lt_ref, wg_ref, wu_ref, wd_ref, o_ref, acc_ref):
    g = pl.program_id(1)
    tm = acc_ref.shape[0]
    blk = _moe_block_rows(tm)

    @pl.when(g == 0)
    def _():
        acc_ref[...] = jnp.zeros_like(acc_ref)

    gidx, comb = _routing(lt_ref[...])
    member = jnp.broadcast_to((gidx == g).astype(F32), (SUBLANES, tm))
    lane = lax.broadcasted_iota(jnp.int32, (SUBLANES, tm), 1)
    count = member
    sh = 1
    while sh < tm:
        count = count + jnp.where(lane >= sh, pltpu.roll(count, sh, 1), 0.0)
        sh *= 2
    pos = jnp.where(member > 0.0, count - 1.0, -1.0).astype(jnp.int32)[0:1]
    n_tok = jnp.max(count).astype(jnp.int32)
    comb_pad = jnp.concatenate([comb, jnp.zeros((LANES - N_EXPERTS, tm), F32)], axis=0)
    comb_parts = _split2(comb_pad)
    u = u_ref[...]
    sel_lane = lax.broadcasted_iota(jnp.int32, (blk, LANES), 1)

    def body(i, carry):
        slot = lax.broadcasted_iota(jnp.int32, (blk, tm), 0) + i * blk
        onehot = jnp.where(slot == pos, 1.0, 0.0).astype(BF16)
        xg = jnp.dot(onehot, u, preferred_element_type=F32).astype(BF16)
        cw = sum(lax.dot_general(onehot, c, (((1,), (1,)), ((), ())), preferred_element_type=F32)
                 for c in comb_parts)
        y = jnp.zeros((blk, D), F32)
        for e in range(EPG):
            hid = (_silu(jnp.dot(xg, wg_ref[e], preferred_element_type=F32))
                   * jnp.dot(xg, wu_ref[e], preferred_element_type=F32))
            c_e = jnp.sum(jnp.where(sel_lane == g * EPG + e, cw, 0.0), axis=-1, keepdims=True)
            y = y + c_e * _mm(hid, wd_ref[e])
        acc_ref[...] += _mm_tn(onehot, y)
        return carry

    lax.fori_loop(0, (n_tok + blk - 1) // blk, body, 0)

    @pl.when(g == N_GROUPS - 1)
    def _():
        o_ref[...] = acc_ref[...].astype(o_ref.dtype)


def _moe(u2, logits_t, wg, wu, wd):
    b, l, _ = u2.shape
    t = b * l
    tm = min(MOE_TM, t)
    tok = pl.BlockSpec((tm, D), lambda i, g: (i, 0))
    in_specs = [
        tok,
        pl.BlockSpec((ROUTER_LANES, tm), lambda i, g: (0, i)),
        pl.BlockSpec((EPG, D, EXPERT_FF), lambda i, g: (g, 0, 0)),
        pl.BlockSpec((EPG, D, EXPERT_FF), lambda i, g: (g, 0, 0)),
        pl.BlockSpec((EPG, EXPERT_FF, D), lambda i, g: (g, 0, 0)),
    ]
    return pl.pallas_call(
        _moe_kernel,
        grid=(t // tm, N_GROUPS),
        in_specs=in_specs,
        out_specs=tok,
        out_shape=jax.ShapeDtypeStruct((t, D), BF16),
        scratch_shapes=[pltpu.VMEM((tm, D), F32)],
        compiler_params=_cparams(("arbitrary",) * 2),
        name="moe",
    )(u2.reshape(t, D), logits_t, wg, wu, wd).reshape(b, l, D)


def _post_kernel(x_ref, m_ref, gt_ref, l2g_ref, l2b_ref, sh_ref, sc_ref, o_ref, u_ref):
    x2 = _ln(ALPHA * x_ref[0] + gt_ref[0] * m_ref[0].astype(F32)) * l2g_ref[...] + l2b_ref[...]
    o_ref[0] = x2
    u_ref[0] = (_ln(x2) * (1.0 + sc_ref[0]) + sh_ref[0]).astype(u_ref.dtype)


def _post(x1, moe_out, gt2, l2g, l2b, sh_next, sc_next):
    b, l, _ = x1.shape
    tm = 512 if l % 512 == 0 else 256
    tok = pl.BlockSpec((1, tm, D), lambda i, j: (i, j, 0))
    vec = pl.BlockSpec((1, 1, D), lambda i, j: (i, 0, 0))
    row = pl.BlockSpec((1, D), lambda i, j: (0, 0))
    sd = jax.ShapeDtypeStruct
    return pl.pallas_call(
        _post_kernel,
        grid=(b, l // tm),
        in_specs=[tok, tok, vec, row, row, vec, vec],
        out_specs=[tok, tok],
        out_shape=[sd((b, l, D), F32), sd((b, l, D), BF16)],
        compiler_params=_cparams(("arbitrary", "arbitrary")),
        name="post_ln",
    )(x1, moe_out, gt2, l2g.reshape(1, D), l2b.reshape(1, D), sh_next, sc_next)


def _rope_tables(l):
    t = np.arange(l)
    quarter = RET_DK // 4
    freqs = (np.float32(ROPE_BASE) ** (-np.arange(quarter, dtype=np.float32) / quarter)).astype(np.float32)
    rows = (t // GRID_W).astype(np.float32)
    cols = (t % GRID_W).astype(np.float32)
    ang = jnp.asarray(np.concatenate([rows[:, None] * freqs, cols[:, None] * freqs], -1))
    cos, sin = jnp.cos(ang), jnp.sin(ang)
    return jnp.concatenate([cos, cos], -1), jnp.concatenate([-sin, sin], -1)


def _layer_weights(p):
    n_rg = 2 * (RET_H * RET_DK + RET_H * RET_DV)
    w_in = p["w_in"]
    w_lr = jnp.pad(w_in[:, 2 * n_rg:2 * n_rg + GLA_LOWRANK], ((0, 0), (0, LANES - GLA_LOWRANK)))
    w2p, b2p = _gla_gate_params(p["gla_w2"], p["gla_b"])
    pad = ROUTER_LANES - N_GROUPS - N_EXPERTS
    return dict(
        p,
        w_ret=w_in[:, :n_rg].astype(BF16),
        w_gla=w_in[:, n_rg:2 * n_rg].astype(BF16),
        w_glr=w_lr.astype(BF16),
        w_rw=w_in[:, 2 * n_rg + GLA_LOWRANK:2 * n_rg + GLA_LOWRANK + 3 * D].astype(BF16),
        w_rlo=w_in[:, 2 * n_rg + GLA_LOWRANK + 3 * D:].astype(BF16),
        w_merge=p["w_merge"].astype(BF16),
        w_br=p["w_br"].astype(BF16),
        w_out=p["w_out"].astype(BF16),
        gla_w2p=w2p, gla_b2p=b2p,
        w_router=jnp.stack(_split2(jnp.pad(jnp.concatenate([p["w_rg"], p["w_re"]], axis=1), ((0, 0), (0, pad))))),
        b_router=jnp.pad(jnp.concatenate([p["b_rg"], p["b_re"]]), (0, pad)).reshape(1, ROUTER_LANES),
        w_eg=p["w_eg"].astype(BF16), w_eu=p["w_eu"].astype(BF16), w_ed=p["w_ed"].astype(BF16),
    )


def _split_mod(mod):
    return [m.reshape(mod.shape[0], 1, D) for m in jnp.split(mod, 6, axis=-1)]


def _layer(x, u, mod, mod_next, s_ret, s_gla_t, s_rw, rope, p):
    b, l, _ = x.shape
    _, _, gt1, sh2, sc2, gt2 = _split_mod(mod)
    sh_next, sc_next = _split_mod(mod_next)[:2]
    u2d = u.reshape(b * l, D)

    def proj(w, tn, bias=None, **kw):
        bias = jnp.zeros((w.shape[1],), F32) if bias is None else bias
        return _proj(u2d, w, bias, tn, **kw).reshape(b, l, -1)

    pr = proj(p["w_ret"], 1024)
    pg = proj(p["w_gla"], 1024)
    plr = proj(p["w_glr"], LANES, out_dtype=F32)
    rw = proj(p["w_rw"], 1024)
    rlo = proj(p["w_rlo"], N_RW_LORA, out_dtype=F32)
    gates = proj(p["w_merge"], 1024, bias=p["b_merge"], act="sigmoid")

    cos2, sin2 = rope if rope is not None else (jnp.ones((l, RET_DK), F32), jnp.zeros((l, RET_DK), F32))
    orf, orb, ret_fin = _retention(pr, cos2, sin2, s_ret, rope is not None)
    ogf, ogb, gla_fin_t = _gla(pg, plr, p["gla_w2p"], p["gla_b2p"], s_gla_t)
    r, v, kk, lw, kd, be, bonus, grw = _rw_prep(rw, rlo, p["rwkv_shift"], p["rwkv_kk"], p["rwkv_ka"],
                                                p["rwkv_rk"].reshape(-1), p["rwkv_w0"], p["rwkv_wb"],
                                                p["rwkv_a0"], p["rwkv_ab"], p["rwkv_gb"])
    yf, yb, rw_fin = _rw_scan(r, v, kk, lw, kd, be, s_rw)
    x1, u2, logits_t = _merge(orf, orb, pr, ogf, ogb, pg, yf, yb, bonus, grw, gates, x, gt1, sh2, sc2,
                              p["w_br"], p["w_out"], p["ret_gn"], p["gla_gn"], p["rwkv_gn_g"], p["rwkv_gn_b"],
                              p["ln1_g"], p["ln1_b"], p["w_router"], p["b_router"])
    moe_out = _moe(u2, logits_t, p["w_eg"], p["w_eu"], p["w_ed"])
    x2, u_next = _post(x1, moe_out, gt2, p["ln2_g"], p["ln2_b"], sh_next, sc_next)
    return x2, u_next, (ret_fin, gla_fin_t, rw_fin)


_PARAM_NAMES = ("w_in", "rwkv_shift", "ret_gn", "gla_w2", "gla_b", "gla_gn", "rwkv_w0", "rwkv_wb", "rwkv_a0",
                "rwkv_ab", "rwkv_gb", "rwkv_kk", "rwkv_ka", "rwkv_rk", "rwkv_gn_g", "rwkv_gn_b", "w_br", "w_merge",
                "b_merge", "w_out", "ln1_g", "ln1_b", "ln2_g", "ln2_b", "w_rg", "b_rg", "w_re", "b_re", "w_eg",
                "w_eu", "w_ed")


def kernel(x_prompt, x_sample, state_ret, state_gla, state_rwkv, c, c_ctx, w_ada, b_ada, w_in, rwkv_shift, ret_gn,
           gla_w2, gla_b, gla_gn, rwkv_w0, rwkv_wb, rwkv_a0, rwkv_ab, rwkv_gb, rwkv_kk, rwkv_ka, rwkv_rk, rwkv_gn_g,
           rwkv_gn_b, w_br, w_merge, b_merge, w_out, ln1_g, ln1_b, ln2_g, ln2_b, w_rg, b_rg, w_re, b_re, w_eg, w_eu,
           w_ed):
    params = dict(zip(_PARAM_NAMES, (w_in, rwkv_shift, ret_gn, gla_w2, gla_b, gla_gn, rwkv_w0, rwkv_wb, rwkv_a0,
                                     rwkv_ab, rwkv_gb, rwkv_kk, rwkv_ka, rwkv_rk, rwkv_gn_g, rwkv_gn_b, w_br, w_merge,
                                     b_merge, w_out, ln1_g, ln1_b, ln2_g, ln2_b, w_rg, b_rg, w_re, b_re, w_eg, w_eu,
                                     w_ed)))
    bc, bl = x_prompt.shape[0], x_sample.shape[0]
    rope = _rope_tables(x_sample.shape[1])
    mod_rows = 2 * SUBLANES
    c_all = jnp.concatenate([c, c_ctx[None, :], jnp.zeros((mod_rows - bl - 1, D), F32)], axis=0)
    z_ret = jnp.zeros((bc, 2, RET_H, RET_DK, RET_DV), F32)
    z_gla_t = jnp.zeros((bc, 2, GLA_H, GLA_DV, GLA_DK), F32)
    z_rw = jnp.zeros((bc, 2, RW_H, RW_N, RW_N), F32)
    mods = [_modulation(c_all, w_ada[layer], b_ada[layer]) for layer in range(DEPTH)]
    mods.append(jnp.zeros_like(mods[0]))
    mods_lat = [m[:bl] for m in mods]
    mods_ctx = [jnp.broadcast_to(m[bl:bl + 1], (bc, 6 * D)) for m in mods]
    h_ctx, h_lat = x_prompt, x_sample
    sh, sc = _split_mod(mods_ctx[0])[:2]
    u_ctx = _lnmod(h_ctx, sh, sc)
    sh, sc = _split_mod(mods_lat[0])[:2]
    u_lat = _lnmod(h_lat, sh, sc)
    new_ret, new_gla, new_rw = [], [], []
    for layer in range(DEPTH):
        p = _layer_weights({k: v[layer] for k, v in params.items()})
        h_ctx, u_ctx, (s_ret, s_gla_t, s_rw) = _layer(h_ctx, u_ctx, mods_ctx[layer], mods_ctx[layer + 1],
                                                      z_ret, z_gla_t, z_rw, None, p)
        new_ret.append(s_ret)
        new_gla.append(jnp.swapaxes(s_gla_t, -1, -2))
        new_rw.append(jnp.swapaxes(s_rw, -1, -2))
        h_lat, u_lat, _ = _layer(h_lat, u_lat, mods_lat[layer], mods_lat[layer + 1], state_ret[:, layer],
                                 jnp.swapaxes(state_gla[:, layer], -1, -2),
                                 jnp.swapaxes(state_rwkv[:, layer], -1, -2), rope, p)
    return (h_ctx, h_lat, jnp.stack(new_ret, axis=1), jnp.stack(new_gla, axis=1), jnp.stack(new_rw, axis=1))
```

```python
import functools

import jax
import jax.numpy as jnp
import numpy as np
from jax import lax
from jax.experimental import pallas as pl
from jax.experimental.pallas import tpu as pltpu

F32 = jnp.float32
BF16 = jnp.bfloat16
HI = lax.Precision.HIGHEST

D = 1024
DEPTH = 2
GRID_W = 64
RET_H, RET_DK, RET_DV = 4, 128, 256
GLA_H, GLA_DK, GLA_DV = 4, 128, 256
GLA_LOWRANK = 16
GLA_GATE_NORM = 16.0
RW_H, RW_N = 16, 64
RW_W_LORA, RW_A_LORA, RW_G_LORA = 64, 64, 128
N_RW_COLS = 3 * D + RW_W_LORA + RW_A_LORA + RW_G_LORA
N_GROUPS, EPG, N_EXPERTS, EXPERT_FF = 4, 4, 16, 512
ALPHA = (2 * DEPTH) ** 0.25
EPS = 1e-5
ROPE_BASE = 10000.0

LANES = 128
SUBLANES = 8
VMEM_LIMIT = 56 * 1024 * 1024

RET_C = 128
GLA_C = 64
GLA_LEVELS = 6
RW_C = 32
RW_LEVELS = 5
RW_TL = 256
RW_BB = 2
RW_G = 4
RW_GW = RW_G * RW_N
RW_SW = 2 * RW_N


def _cparams(sem):
    return pltpu.CompilerParams(dimension_semantics=sem, vmem_limit_bytes=VMEM_LIMIT)


def _mm(a, b):
    return jnp.dot(a.astype(BF16), b.astype(BF16), preferred_element_type=F32)


def _mm_nt(a, b):
    return lax.dot_general(a.astype(BF16), b.astype(BF16), (((1,), (1,)), ((), ())), preferred_element_type=F32)


def _mm_tn(a, b):
    return lax.dot_general(a.astype(BF16), b.astype(BF16), (((0,), (0,)), ((), ())), preferred_element_type=F32)


def _mm_hi(a, b):
    return jnp.dot(a, b, preferred_element_type=F32, precision=HI)


def _split2(x):
    hi = x.astype(BF16)
    return hi, (x - hi.astype(F32)).astype(BF16)


def _mm_split(a, b_hi, b_lo):
    a_hi, a_lo = _split2(a)
    dot = functools.partial(jnp.dot, preferred_element_type=F32)
    return dot(a_hi, b_hi) + (dot(a_hi, b_lo) + dot(a_lo, b_hi))


def _sigmoid(x):
    return 1.0 / (1.0 + jnp.exp(-x))


def _silu(x):
    return x * _sigmoid(x)


def _log_sigmoid(x):
    return jnp.minimum(x, 0.0) - jnp.log(1.0 + jnp.exp(-jnp.abs(x)))


def _softplus(x):
    return jnp.maximum(x, 0.0) + jnp.log(1.0 + jnp.exp(-jnp.abs(x)))


def _ln(x):
    mu = jnp.mean(x, axis=-1, keepdims=True)
    xc = x - mu
    var = jnp.mean(xc * xc, axis=-1, keepdims=True)
    return xc * lax.rsqrt(var + EPS)


def _scan_rows(x, reverse):
    n = x.shape[0]
    row = lax.broadcasted_iota(jnp.int32, x.shape, 0)
    sh = 1
    while sh < n:
        if reverse:
            x = x + jnp.where(row < n - sh, pltpu.roll(x, n - sh, 0), 0.0)
        else:
            x = x + jnp.where(row >= sh, pltpu.roll(x, sh, 0), 0.0)
        sh *= 2
    return x


def _mod_kernel(c_ref, w_ref, b_ref, o_ref):
    o_ref[...] = _mm_hi(_silu(c_ref[...]), w_ref[...]) + b_ref[...]


def _modulation(c_all, w, b):
    m, n, tn = c_all.shape[0], w.shape[1], 512
    return pl.pallas_call(
        _mod_kernel,
        grid=(n // tn,),
        in_specs=[pl.BlockSpec((m, D), lambda j: (0, 0)),
                  pl.BlockSpec((D, tn), lambda j: (0, j)),
                  pl.BlockSpec((1, tn), lambda j: (0, j))],
        out_specs=pl.BlockSpec((m, tn), lambda j: (0, j)),
        out_shape=jax.ShapeDtypeStruct((m, n), F32),
        compiler_params=_cparams(("arbitrary",)),
        name="adaln_mod",
    )(c_all, w, b.reshape(1, n))


def _lnmod_kernel(x_ref, sh_ref, sc_ref, u_ref):
    u_ref[0] = (_ln(x_ref[0]) * (1.0 + sc_ref[0]) + sh_ref[0]).astype(u_ref.dtype)


def _lnmod(x, sh, sc):
    b, l, _ = x.shape
    tm = 512 if l % 512 == 0 else 256
    vec = pl.BlockSpec((1, 1, D), lambda i, j: (i, 0, 0))
    return pl.pallas_call(
        _lnmod_kernel,
        grid=(b, l // tm),
        in_specs=[pl.BlockSpec((1, tm, D), lambda i, j: (i, j, 0)), vec, vec],
        out_specs=pl.BlockSpec((1, tm, D), lambda i, j: (i, j, 0)),
        out_shape=jax.ShapeDtypeStruct((b, l, D), BF16),
        compiler_params=_cparams(("arbitrary", "arbitrary")),
        name="ln_mod",
    )(x, sh, sc)


def _proj_kernel(u_ref, w_ref, b_ref, o_ref, *, act):
    y = jnp.dot(u_ref[...], w_ref[...], preferred_element_type=F32) + b_ref[...]
    if act == "sigmoid":
        y = _sigmoid(y)
    o_ref[...] = y.astype(o_ref.dtype)


def _proj(u2d, w, bias, tn, act=None, out_dtype=BF16):
    t, n = u2d.shape[0], w.shape[1]
    tm = min(t, 2048)
    return pl.pallas_call(
        functools.partial(_proj_kernel, act=act),
        grid=(t // tm, n // tn),
        in_specs=[pl.BlockSpec((tm, D), lambda i, j: (i, 0)),
                  pl.BlockSpec((D, tn), lambda i, j: (0, j)),
                  pl.BlockSpec((1, tn), lambda i, j: (0, j))],
        out_specs=pl.BlockSpec((tm, tn), lambda i, j: (i, j)),
        out_shape=jax.ShapeDtypeStruct((t, n), out_dtype),
        compiler_params=_cparams(("arbitrary", "arbitrary")),
        name="proj",
    )(u2d, w, bias.reshape(1, n))


def _ret_log_gamma(d, h):
    hh = h if d == 0 else RET_H - 1 - h
    return float(np.log(1.0 - 2.0 ** (-5.0 - hh)))


def _ret_tables():
    c = RET_C
    i = np.arange(c, dtype=np.float64)
    dec = np.zeros((2 * RET_H, c, c), np.float32)
    qd = np.zeros((2 * RET_H, c, RET_DK), np.float32)
    kd = np.zeros((2 * RET_H, c, RET_DK), np.float32)
    for d in range(2):
        tau = i if d == 0 else c - 1 - i
        rel = tau[:, None] - tau[None, :]
        for h in range(RET_H):
            lg = _ret_log_gamma(d, h)
            dec[d * RET_H + h] = np.where(rel >= 0, np.exp(np.maximum(rel, 0.0) * lg), 0.0)
            qd[d * RET_H + h] = np.exp((tau + 1.0) * lg)[:, None]
            kd[d * RET_H + h] = np.exp((c - 1.0 - tau) * lg)[:, None]
    return jnp.asarray(dec), jnp.asarray(qd), jnp.asarray(kd)


def _rope(x, cos2, sin2):
    return x * cos2 + pltpu.roll(x, RET_DK // 2, 1) * sin2


def _ret_kernel(qf_ref, kf_ref, vf_ref, qb_ref, kb_ref, vb_ref, cosf_ref, sinf_ref, cosb_ref, sinb_ref,
                dec_ref, qd_ref, kd_ref, s0_ref, of_ref, ob_ref, sfin_ref, s_ref, *, use_rope, n_chunks):
    n = pl.program_id(1)

    @pl.when(n == 0)
    def _():
        s_ref[...] = s0_ref[0]

    ins = ((qf_ref, kf_ref, vf_ref, cosf_ref, sinf_ref), (qb_ref, kb_ref, vb_ref, cosb_ref, sinb_ref))
    outs = (of_ref, ob_ref)
    chains = [(d, h) for d in range(2) for h in range(RET_H)]
    q, k, v = {}, {}, {}
    for d, h in chains:
        q_ref, k_ref, v_ref, cos_ref, sin_ref = ins[d]
        dk = slice(h * RET_DK, (h + 1) * RET_DK)
        q[d, h] = q_ref[0, :, dk].astype(F32)
        k[d, h] = k_ref[0, :, dk].astype(F32) * (RET_DK ** -0.5)
        if use_rope:
            q[d, h] = _rope(q[d, h], cos_ref[...], sin_ref[...])
            k[d, h] = _rope(k[d, h], cos_ref[...], sin_ref[...])
        v[d, h] = v_ref[0, :, h * RET_DV:(h + 1) * RET_DV]
    sc = {ch: _mm_nt(q[ch], k[ch]) * dec_ref[ch[0] * RET_H + ch[1]] for ch in chains}
    s_old = {ch: s_ref[ch[0], ch[1]] for ch in chains}
    o = {ch: _mm(sc[ch], v[ch]) + _mm(q[ch] * qd_ref[ch[0] * RET_H + ch[1]], s_old[ch]) for ch in chains}
    s_new = {ch: s_old[ch] * float(np.exp(RET_C * _ret_log_gamma(*ch)))
             + _mm_tn(k[ch] * kd_ref[ch[0] * RET_H + ch[1]], v[ch]) for ch in chains}
    for d, h in chains:
        outs[d][0, :, h * RET_DV:(h + 1) * RET_DV] = o[d, h].astype(outs[d].dtype)
        s_ref[d, h] = s_new[d, h]

    @pl.when(n == n_chunks - 1)
    def _():
        sfin_ref[0] = s_ref[...]


def _retention(pr, cos2, sin2, s0, use_rope):
    b, l, _ = pr.shape
    c = RET_C
    nc = l // c
    dec, qd, kd = _ret_tables()
    qw, vw = RET_H * RET_DK, RET_H * RET_DV

    def fw(blk):
        return lambda i, n: (i, n, blk)

    def bw(blk):
        return lambda i, n: (i, nc - 1 - n, blk)

    def const(*shape):
        return pl.BlockSpec(shape, lambda i, n: (0,) * len(shape))

    st = pl.BlockSpec((1, 2, RET_H, RET_DK, RET_DV), lambda i, n: (i, 0, 0, 0, 0))
    in_specs = [
        pl.BlockSpec((1, c, qw), fw(0)), pl.BlockSpec((1, c, qw), fw(1)), pl.BlockSpec((1, c, vw), fw(1)),
        pl.BlockSpec((1, c, qw), bw(0)), pl.BlockSpec((1, c, qw), bw(1)), pl.BlockSpec((1, c, vw), bw(1)),
        pl.BlockSpec((c, RET_DK), lambda i, n: (n, 0)), pl.BlockSpec((c, RET_DK), lambda i, n: (n, 0)),
        pl.BlockSpec((c, RET_DK), lambda i, n: (nc - 1 - n, 0)), pl.BlockSpec((c, RET_DK), lambda i, n: (nc - 1 - n, 0)),
        const(2 * RET_H, c, c), const(2 * RET_H, c, RET_DK), const(2 * RET_H, c, RET_DK), st,
    ]
    out_specs = [pl.BlockSpec((1, c, vw), fw(0)), pl.BlockSpec((1, c, vw), bw(0)), st]
    sd = jax.ShapeDtypeStruct
    return pl.pallas_call(
        functools.partial(_ret_kernel, use_rope=use_rope, n_chunks=nc),
        grid=(b, nc),
        in_specs=in_specs,
        out_specs=out_specs,
        out_shape=[sd((b, l, vw), BF16), sd((b, l, vw), BF16), sd((b, 2, RET_H, RET_DK, RET_DV), F32)],
        scratch_shapes=[pltpu.VMEM((2, RET_H, RET_DK, RET_DV), F32)],
        compiler_params=_cparams(("arbitrary", "arbitrary")),
        name="retention",
    )(pr, pr, pr, pr, pr, pr, cos2, sin2, cos2, sin2, dec, qd, kd, s0)


def _gla_boundary(b, lv, reverse, rolls):
    c, w = b.shape
    m = 1 << lv
    if m >= SUBLANES:
        parts = []
        for p0 in range(0, c, 2 * m):
            e = p0 + m if reverse else p0 + m - 1
            parts.append(jnp.broadcast_to(b[e:e + 1, :], (2 * m, w)))
        return jnp.concatenate(parts, axis=0)

    def rolled(s):
        s %= c
        if s not in rolls:
            rolls[s] = b if s == 0 else pltpu.roll(b, s, 0)
        return rolls[s]

    row = lax.broadcasted_iota(jnp.int32, (c, w), 0)
    r = jnp.bitwise_and(row, m - 1)
    upper = jnp.bitwise_and(jnp.right_shift(row, lv), 1) == 1
    out = b
    for t in range(m):
        if reverse:
            out = jnp.where(upper & (r == t), rolled(t), out)
            out = jnp.where(jnp.logical_not(upper) & (r == t), rolled(-(m - t)), out)
        else:
            out = jnp.where(upper & (r == t), rolled(t + 1), out)
            out = jnp.where(jnp.logical_not(upper) & (r == t), rolled(-(m - 1 - t)), out)
    return out


def _gla_kernel(qf_ref, kf_ref, vf_ref, lrf_ref, qb_ref, kb_ref, vb_ref, lrb_ref, w2_ref, b2_ref, s0_ref,
                of_ref, ob_ref, sfin_ref, st_ref, *, n_chunks):
    c = GLA_C
    n = pl.program_id(1)

    @pl.when(n == 0)
    def _():
        st_ref[...] = s0_ref[0]

    ins = ((qf_ref, kf_ref, vf_ref, lrf_ref), (qb_ref, kb_ref, vb_ref, lrb_ref))
    outs = (of_ref, ob_ref)
    hw = GLA_H * GLA_DK
    row = lax.broadcasted_iota(jnp.int32, (c, hw), 0)
    ri = lax.broadcasted_iota(jnp.int32, (c, c), 0)
    ci = lax.broadcasted_iota(jnp.int32, (c, c), 1)
    q, k, v, b_inc, b_rest, tot, ql, kl = {}, {}, {}, {}, {}, {}, {}, {}
    for d in range(2):
        reverse = d == 1
        q_ref, k_ref, v_ref, lr_ref = ins[d]
        q[d] = q_ref[0].astype(F32) * (GLA_DK ** -0.5)
        k[d] = k_ref[0].astype(F32)
        v[d] = v_ref[0]
        gate = _log_sigmoid(_mm_split(lr_ref[0], w2_ref[d, 0], w2_ref[d, 1]) + b2_ref[d])
        gate = gate * (1.0 / GLA_GATE_NORM)
        b = _scan_rows(gate, reverse)
        tot[d] = b[0:1] if reverse else b[c - 1:c]
        b_inc[d], b_rest[d] = b, tot[d] - b
        rolls = {}
        for lv in range(GLA_LEVELS):
            be = _gla_boundary(b, lv, reverse, rolls)
            upper = jnp.bitwise_and(jnp.right_shift(row, lv), 1) == 1
            second = jnp.logical_not(upper) if reverse else upper
            e = jnp.exp(jnp.where(second, b - be, be - b))
            ql[d, lv] = jnp.where(second, q[d] * e, 0.0)
            kl[d, lv] = jnp.where(second, 0.0, k[d] * e)
    chains = [(d, h) for d in range(2) for h in range(GLA_H)]

    def dk(h):
        return slice(h * GLA_DK, (h + 1) * GLA_DK)

    def dv(h):
        return slice(h * GLA_DV, (h + 1) * GLA_DV)

    attn = {(d, h): jnp.where(ri == ci, jnp.sum(q[d][:, dk(h)] * k[d][:, dk(h)], axis=-1, keepdims=True), 0.0)
            for d, h in chains}
    for lv in range(GLA_LEVELS):
        same = jnp.right_shift(ri, lv + 1) == jnp.right_shift(ci, lv + 1)
        for d, h in chains:
            attn[d, h] = attn[d, h] + jnp.where(same, _mm_nt(ql[d, lv][:, dk(h)], kl[d, lv][:, dk(h)]), 0.0)
    st_old = {ch: st_ref[ch[0], ch[1]] for ch in chains}
    o = {(d, h): _mm(attn[d, h], v[d][:, dv(h)]) + _mm_nt(q[d][:, dk(h)] * jnp.exp(b_inc[d][:, dk(h)]), st_old[d, h])
         for d, h in chains}
    st_new = {(d, h): st_old[d, h] * jnp.exp(tot[d][:, dk(h)])
              + _mm_tn(v[d][:, dv(h)], k[d][:, dk(h)] * jnp.exp(b_rest[d][:, dk(h)])) for d, h in chains}
    for d, h in chains:
        outs[d][0, :, dv(h)] = o[d, h].astype(outs[d].dtype)
        st_ref[d, h] = st_new[d, h]

    @pl.when(n == n_chunks - 1)
    def _():
        sfin_ref[0] = st_ref[...]


def _gla(pg, plr, w2p, b2, s0t):
    b, l, _ = pg.shape
    c = GLA_C
    nc = l // c
    qw, vw = GLA_H * GLA_DK, GLA_H * GLA_DV

    def fw(blk):
        return lambda i, n: (i, n, blk)

    def bw(blk):
        return lambda i, n: (i, nc - 1 - n, blk)

    st = pl.BlockSpec((1, 2, GLA_H, GLA_DV, GLA_DK), lambda i, n: (i, 0, 0, 0, 0))
    in_specs = [
        pl.BlockSpec((1, c, qw), fw(0)), pl.BlockSpec((1, c, qw), fw(1)), pl.BlockSpec((1, c, vw), fw(1)),
        pl.BlockSpec((1, c, LANES), fw(0)),
        pl.BlockSpec((1, c, qw), bw(0)), pl.BlockSpec((1, c, qw), bw(1)), pl.BlockSpec((1, c, vw), bw(1)),
        pl.BlockSpec((1, c, LANES), bw(0)),
        pl.BlockSpec((2, 2, LANES, qw), lambda i, n: (0, 0, 0, 0)),
        pl.BlockSpec((2, 1, qw), lambda i, n: (0, 0, 0)),
        st,
    ]
    out_specs = [pl.BlockSpec((1, c, vw), fw(0)), pl.BlockSpec((1, c, vw), bw(0)), st]
    sd = jax.ShapeDtypeStruct
    return pl.pallas_call(
        functools.partial(_gla_kernel, n_chunks=nc),
        grid=(b, nc),
        in_specs=in_specs,
        out_specs=out_specs,
        out_shape=[sd((b, l, vw), BF16), sd((b, l, vw), BF16), sd((b, 2, GLA_H, GLA_DV, GLA_DK), F32)],
        scratch_shapes=[pltpu.VMEM((2, GLA_H, GLA_DV, GLA_DK), F32)],
        compiler_params=_cparams(("arbitrary", "arbitrary")),
        name="gla",
    )(pg, pg, pg, plr, pg, pg, pg, plr, w2p, b2, s0t)


def _gla_gate_params(w2, b2):
    w2p = jnp.pad(w2, ((0, 0), (0, LANES - GLA_LOWRANK), (0, 0)))
    return jnp.stack(_split2(w2p), axis=1), b2.reshape(2, 1, GLA_H * GLA_DK)


def _head_sum(x):
    i = lax.broadcasted_iota(jnp.int32, (LANES, LANES), 0)
    j = lax.broadcasted_iota(jnp.int32, (LANES, LANES), 1)
    ones = jnp.where(jnp.right_shift(i, 6) == jnp.right_shift(j, 6), 1.0, 0.0).astype(BF16)
    out = []
    for t in range(x.shape[1] // LANES):
        parts = _split2(x[:, t * LANES:(t + 1) * LANES])
        out.append(sum(jnp.dot(p, ones, preferred_element_type=F32) for p in parts))
    return jnp.concatenate(out, axis=1)


RW_TM = 256
HALO = 2 * SUBLANES
N_RW_LORA = RW_W_LORA + RW_A_LORA + RW_G_LORA


def _rw_prep_kernel(x_ref, xp_ref, xn_ref, lo_ref, lop_ref, lon_ref, taps_ref, ltaps_ref, kkp_ref, ka_ref, rk_ref,
                    w0_ref, wb_ref, a0_ref, ab_ref, gb_ref, r_ref, v_ref, kk_ref, lw_ref, kd_ref, be_ref, bonus_ref,
                    g_ref, xbuf, lbuf, *, n_tiles):
    tm = RW_TM
    j = pl.program_id(1)
    first, last = j == 0, j == n_tiles - 1
    xbuf[0:HALO, :] = jnp.where(first, jnp.zeros_like(xp_ref[0]), xp_ref[0])
    xbuf[HALO:HALO + tm, :] = x_ref[0]
    xbuf[HALO + tm:2 * HALO + tm, :] = jnp.where(last, jnp.zeros_like(xn_ref[0]), xn_ref[0])
    ri = lax.broadcasted_iota(jnp.int32, (tm, tm + 2 * HALO), 0)
    ci = lax.broadcasted_iota(jnp.int32, (tm, tm + 2 * HALO), 1)
    pick_prev = jnp.where(ci == ri + (HALO - 1), 1.0, 0.0).astype(BF16)
    pick_next = jnp.where(ci == ri + (HALO + 1), 1.0, 0.0).astype(BF16)

    def shifted_main(c0, c1):
        x_all = xbuf[:, c0:c1]
        return (taps_ref[0:1, c0:c1] * jnp.dot(pick_prev, x_all, preferred_element_type=F32)
                + taps_ref[1:2, c0:c1] * x_ref[0, :, c0:c1].astype(F32)
                + taps_ref[2:3, c0:c1] * jnp.dot(pick_next, x_all, preferred_element_type=F32))

    lbuf[HALO:HALO + tm, :] = lo_ref[0]
    lbuf[0:HALO, :] = jnp.where(first, 0.0, lop_ref[0])
    lbuf[HALO + tm:2 * HALO + tm, :] = jnp.where(last, 0.0, lon_ref[0])
    lora = (ltaps_ref[0:1, :] * lbuf[HALO - 1:HALO - 1 + tm, :] + ltaps_ref[1:2, :] * lbuf[HALO:HALO + tm, :]
            + ltaps_ref[2:3, :] * lbuf[HALO + 1:HALO + 1 + tm, :])
    r = shifted_main(0, D)
    k = shifted_main(D, 2 * D)
    v = shifted_main(2 * D, 3 * D)
    xw = lora[:, 0:RW_W_LORA]
    xa = lora[:, RW_W_LORA:RW_W_LORA + RW_A_LORA]
    xg = lora[:, RW_W_LORA + RW_A_LORA:]
    r_ref[0] = r.astype(r_ref.dtype)
    v_ref[0] = v.astype(v_ref.dtype)
    kk = k * kkp_ref[...]
    kk = kk * lax.rsqrt(jnp.maximum(_head_sum(kk * kk), 1e-24))
    kk_ref[0] = kk.astype(kk_ref.dtype)
    g_ref[0] = _mm(_sigmoid(xg), gb_ref[...]).astype(g_ref.dtype)
    wh = jnp.tanh(xw)
    kd_sum = jnp.zeros_like(k)
    for d in range(2):
        w = -_softplus(-(w0_ref[d:d + 1, :] + _mm(wh, wb_ref[d]))) - 0.5
        lw_ref[d, 0] = -jnp.exp(w)
        a = _sigmoid(a0_ref[d:d + 1, :] + _mm(xa, ab_ref[d]))
        kd = k * (1.0 + (a - 1.0) * ka_ref[...])
        kd_ref[d, 0] = kd.astype(kd_ref.dtype)
        be_ref[d, 0] = (kk * a).astype(be_ref.dtype)
        kd_sum = kd_sum + kd
    bonus_ref[0] = (_head_sum(r * kd_sum * rk_ref[...]) * v).astype(bonus_ref.dtype)


def _rw_prep(rw, rlo, taps, kkp, ka, rk, w0, wb, a0, ab, gb):
    b, l, _ = rw.shape
    tm = RW_TM
    nt = l // tm
    hb = tm // HALO
    tok = pl.BlockSpec((1, tm, D), lambda i, j: (i, j, 0))
    tokd = pl.BlockSpec((2, 1, tm, D), lambda i, j: (0, i, j, 0))

    def full(*shape):
        return pl.BlockSpec(shape, lambda i, j: (0,) * len(shape))

    def halo_specs(n):
        return [pl.BlockSpec((1, tm, n), lambda i, j: (i, j, 0)),
                pl.BlockSpec((1, HALO, n), lambda i, j: (i, jnp.maximum(j * hb - 1, 0), 0)),
                pl.BlockSpec((1, HALO, n), lambda i, j: (i, jnp.minimum((j + 1) * hb, l // HALO - 1), 0))]

    in_specs = halo_specs(3 * D) + halo_specs(N_RW_LORA) + [
        full(3, 3 * D), full(3, N_RW_LORA), full(1, D), full(1, D), full(1, D),
        full(2, D), full(2, RW_W_LORA, D), full(2, D), full(2, RW_A_LORA, D), full(RW_G_LORA, D),
    ]
    sd = jax.ShapeDtypeStruct
    tok_o, dir_o, dir_f32 = sd((b, l, D), BF16), sd((2, b, l, D), BF16), sd((2, b, l, D), F32)
    return pl.pallas_call(
        functools.partial(_rw_prep_kernel, n_tiles=nt),
        grid=(b, nt),
        in_specs=in_specs,
        out_specs=[tok, tok, tok, tokd, tokd, tokd, tok, tok],
        out_shape=[tok_o, tok_o, tok_o, dir_f32, dir_o, dir_o, tok_o, tok_o],
        scratch_shapes=[pltpu.VMEM((tm + 2 * HALO, 3 * D), BF16), pltpu.VMEM((tm + 2 * HALO, N_RW_LORA), F32)],
        compiler_params=_cparams(("arbitrary", "arbitrary")),
        name="rwkv_prep",
    )(rw, rw, rw, rlo, rlo, rlo, taps[:, :3 * D], taps[:, 3 * D:], kkp.reshape(1, D), ka.reshape(1, D),
      rk.reshape(1, D), w0, wb, a0, ab, gb)


def _tile_rows(x, n):
    return jnp.concatenate([x] * n, axis=0)


def _rw_scan_kernel(rf_ref, vf_ref, kkf_ref, lwf_ref, kdf_ref, bef_ref, rb_ref, vb_ref, kkb_ref, lwb_ref, kdb_ref,
                    beb_ref, s0_ref, yf_ref, yb_ref, sfin_ref, zt_ref, sv_lhs, sv_w, sv_bk, sv_tot, *, n_tiles):
    c, hw, sw = RW_C, RW_GW, RW_SW
    pk = RW_G * c
    n_groups, n_sub = RW_H // RW_G, RW_GW // RW_SW
    n_chunks = rf_ref.shape[1] // c
    j = pl.program_id(1)

    @pl.when(j == 0)
    def _():
        zt_ref[...] = jnp.zeros_like(zt_ref)
        for bb in range(RW_BB):
            for d in range(2):
                for g in range(RW_H // 2):
                    for jj in range(2):
                        blk = slice(RW_N * jj, RW_N * (jj + 1))
                        zt_ref[bb, d, g, blk, blk] = s0_ref[bb, d, 2 * g + jj]

    ins = ((rf_ref, vf_ref, kkf_ref, lwf_ref, kdf_ref, bef_ref), (rb_ref, vb_ref, kkb_ref, lwb_ref, kdb_ref, beb_ref))
    outs = (yf_ref, yb_ref)
    ri = lax.broadcasted_iota(jnp.int32, (c, pk), 0)
    cs = jnp.bitwise_and(lax.broadcasted_iota(jnp.int32, (c, pk), 1), c - 1)
    eye = (ri == cs).astype(F32)
    strict, incl, lvl, quad = [], [], [], []
    for d in range(2):
        ti = ri if d == 0 else c - 1 - ri
        ts = cs if d == 0 else c - 1 - cs
        strict.append(ts < ti)
        incl.append(ts <= ti)
        same4 = jnp.right_shift(ti, 2) == jnp.right_shift(ts, 2)
        below = jnp.bitwise_and(ti, 3) - jnp.bitwise_and(ts, 3)
        quad.append(tuple(same4 & (below == k) for k in (1, 2, 3)))
        lv_masks = []
        for lv in range(RW_LEVELS):
            same = jnp.right_shift(ti, lv + 1) == jnp.right_shift(ts, lv + 1)
            lower = ((jnp.bitwise_and(jnp.right_shift(ti, lv), 1) == 1)
                     & (jnp.bitwise_and(jnp.right_shift(ts, lv), 1) == 0))
            lv_masks.append(same & lower)
        lvl.append(lv_masks)
    bi = lax.broadcasted_iota(jnp.int32, (pk, pk), 0)
    bj = lax.broadcasted_iota(jnp.int32, (pk, pk), 1)
    bd_p = jnp.right_shift(bi, RW_LEVELS) == jnp.right_shift(bj, RW_LEVELS)
    zi = lax.broadcasted_iota(jnp.int32, (sw, sw), 0)
    zj = lax.broadcasted_iota(jnp.int32, (sw, sw), 1)
    bd_z = jnp.right_shift(zi, 6) == jnp.right_shift(zj, 6)
    term_row = lax.broadcasted_iota(jnp.int32, (2 * SUBLANES, sw), 0)
    wi = lax.broadcasted_iota(jnp.int32, (pk, hw), 0)
    wj = lax.broadcasted_iota(jnp.int32, (pk, hw), 1)
    bd_w = jnp.right_shift(wi, RW_LEVELS) == jnp.right_shift(wj, 6)

    def bdp(x):
        return jnp.where(bd_p, _tile_rows(x, RW_G), 0.0)

    def bdw(x):
        return jnp.where(bd_w, _tile_rows(x, RW_G), 0.0)

    chains = [(bb, d, half) for bb in range(RW_BB) for d in range(2) for half in range(n_groups)]

    def each(fn):
        return {ch: fn(ch) for ch in chains}

    def chunk_rows(i):
        return (pl.ds(pl.multiple_of(i * c, c), c), pl.ds(pl.multiple_of((n_chunks - 1 - i) * c, c), c))

    def scale(i, slot):
        rows = chunk_rows(i)

        def load(ch):
            bb, d, half = ch
            ln = slice(half * hw, (half + 1) * hw)
            refs = ins[d]
            return (refs[0][bb, rows[d], ln].astype(F32), refs[2][bb, rows[d], ln].astype(F32),
                    refs[3][0, bb, rows[d], ln], refs[4][0, bb, rows[d], ln].astype(F32),
                    refs[5][0, bb, rows[d], ln].astype(F32))

        x = each(load)
        r, kk, lw = each(lambda ch: x[ch][0]), each(lambda ch: x[ch][1]), each(lambda ch: x[ch][2])
        kd, be = each(lambda ch: x[ch][3]), each(lambda ch: x[ch][4])
        b = each(lambda ch: _scan_rows(lw[ch], ch[1] == 1))
        tot = each(lambda ch: b[ch][0:1] if ch[1] == 1 else b[ch][c - 1:c])
        en = each(lambda ch: jnp.exp(-b[ch]))
        es = each(lambda ch: jnp.exp(tot[ch] - b[ch]))
        for n, ch in enumerate(chains):
            sv_lhs[slot, n] = jnp.concatenate([kk[ch] * jnp.exp(b[ch] - lw[ch]), r[ch] * jnp.exp(b[ch])],
                                              axis=0).astype(BF16)
            sv_w[slot, n] = jnp.concatenate([be[ch] * en[ch], kd[ch] * en[ch]], axis=0).astype(BF16)
            sv_bk[slot, n] = jnp.concatenate([be[ch] * es[ch], kd[ch] * es[ch]], axis=0).astype(BF16)
            sv_tot[slot, n] = jnp.broadcast_to(tot[ch], (SUBLANES, hw))

    scale(0, 0)

    def chunk(i, carry):
        rows = chunk_rows(i)
        slot = lax.rem(i, 2)
        index = {ch: n for n, ch in enumerate(chains)}
        lhs = each(lambda ch: sv_lhs[slot, index[ch]])
        w_en = each(lambda ch: sv_w[slot, index[ch]])
        bk = each(lambda ch: sv_bk[slot, index[ch]])
        tot = each(lambda ch: sv_tot[slot, index[ch]][0:1])
        v = each(lambda ch: ins[ch[1]][1][ch[0], rows[ch[1]], ch[2] * hw:(ch[2] + 1) * hw].astype(F32))
        gb = each(lambda ch: _mm_nt(lhs[ch], bdw(w_en[ch][0:c])))
        gk = each(lambda ch: _mm_nt(lhs[ch], bdw(w_en[ch][c:])))
        a_ab = each(lambda ch: jnp.where(strict[ch[1]], gb[ch][0:c], 0.0))
        a_rb = each(lambda ch: jnp.where(incl[ch[1]], gb[ch][c:], 0.0))
        a_ak = each(lambda ch: jnp.where(strict[ch[1]], gk[ch][0:c], 0.0))
        a_rk = each(lambda ch: jnp.where(incl[ch[1]], gk[ch][c:], 0.0))
        def inverse4(ch):
            d = ch[1]
            q1, q2, q3 = quad[d]
            a = jnp.where(q1 | q2 | q3, a_ab[ch], 0.0)
            col = (lambda x, k: pltpu.roll(x, pk - k, 1)) if d == 0 else (lambda x, k: pltpu.roll(x, k, 1))
            row = (lambda x, k: pltpu.roll(x, k, 0)) if d == 0 else (lambda x, k: pltpu.roll(x, c - k, 0))
            c1, c2, r1, r2 = col(a, 1), col(a, 2), row(a, 1), row(a, 2)
            two = jnp.where(q2, c1 * r1, 0.0)
            three = jnp.where(q3, c1 * r2 + c2 * r1 - c2 * row(c1, 1) * r2, 0.0)
            return eye - a + two + three

        t_inv = each(inverse4)
        for lv in range(2, RW_LEVELS):
            xm = each(lambda ch: _mm(jnp.where(lvl[ch[1]][lv], a_ab[ch], 0.0), bdp(t_inv[ch])))
            t_inv = each(lambda ch: t_inv[ch] - _mm(t_inv[ch], bdp(xm[ch])))
            if lv == 2:
                scale(jnp.minimum(i + 1, n_chunks - 1), 1 - slot)
        zt_old = {(ch, g): zt_ref[ch[0], ch[1], n_sub * ch[2] + g] for ch in chains for g in range(n_sub)}
        zz = each(lambda ch: jnp.concatenate(
            [_mm(lhs[ch][:, g * sw:(g + 1) * sw], zt_old[ch, g]) for g in range(n_sub)], axis=1))
        av = each(lambda ch: _mm(jnp.concatenate([a_ak[ch], a_rk[ch]], axis=0), bdw(v[ch])))
        u = each(lambda ch: _mm(t_inv[ch], bdw(-(zz[ch][0:c] + av[ch][0:c]))))
        y = each(lambda ch: zz[ch][c:] + _mm(a_rb[ch], bdw(u[ch])) + av[ch][c:])
        uv = each(lambda ch: jnp.concatenate([u[ch], v[ch]], axis=0))

        def decay_cols(ch, gl):
            x = tot[ch][:, gl]
            hi = x.astype(BF16).astype(F32)
            mid = (x - hi).astype(BF16).astype(F32)
            lo = x - hi - mid
            terms = jnp.where(term_row == 0, hi, jnp.where(term_row == 1, mid, jnp.where(term_row == 2, lo, 0.0)))
            return jnp.exp(_mm_tn(terms, jnp.ones((2 * SUBLANES, sw), BF16)))

        zt_new = {}
        for ch in chains:
            for g in range(n_sub):
                gl = slice(g * sw, (g + 1) * sw)
                upd = jnp.where(bd_z, _mm_tn(bk[ch][:, gl], uv[ch][:, gl]), 0.0)
                zt_new[ch, g] = zt_old[ch, g] * decay_cols(ch, gl) + upd
        for ch in chains:
            bb, d, half = ch
            outs[d][bb, rows[d], half * hw:(half + 1) * hw] = y[ch].astype(outs[d].dtype)
            for g in range(n_sub):
                zt_ref[bb, d, n_sub * half + g] = zt_new[ch, g]
        return carry

    lax.fori_loop(0, n_chunks, chunk, 0)

    @pl.when(j == n_tiles - 1)
    def _():
        for bb in range(RW_BB):
            for d in range(2):
                for g in range(RW_H // 2):
                    for jj in range(2):
                        blk = slice(RW_N * jj, RW_N * (jj + 1))
                        sfin_ref[bb, d, 2 * g + jj] = zt_ref[bb, d, g, blk, blk]


def _rw_scan(r, v, kk, lw, kd, be, s0):
    b, l, _ = r.shape
    tl, nb = min(RW_TL, l), RW_BB
    nt = l // tl
    n_ch = nb * 2 * (RW_H // RW_G)
    tok_f = pl.BlockSpec((nb, tl, D), lambda i, j: (i, j, 0))
    tok_b = pl.BlockSpec((nb, tl, D), lambda i, j: (i, nt - 1 - j, 0))
    dir_f = pl.BlockSpec((1, nb, tl, D), lambda i, j: (0, i, j, 0))
    dir_b = pl.BlockSpec((1, nb, tl, D), lambda i, j: (1, i, nt - 1 - j, 0))
    st = pl.BlockSpec((nb, 2, RW_H, RW_N, RW_N), lambda i, j: (i, 0, 0, 0, 0))
    sd = jax.ShapeDtypeStruct
    return pl.pallas_call(
        functools.partial(_rw_scan_kernel, n_tiles=nt),
        grid=(b // nb, nt),
        in_specs=[tok_f, tok_f, tok_f, dir_f, dir_f, dir_f, tok_b, tok_b, tok_b, dir_b, dir_b, dir_b, st],
        out_specs=[tok_f, tok_b, st],
        out_shape=[sd((b, l, D), BF16), sd((b, l, D), BF16), sd((b, 2, RW_H, RW_N, RW_N), F32)],
        scratch_shapes=[pltpu.VMEM((nb, 2, RW_H // 2, RW_SW, RW_SW), F32),
                        pltpu.VMEM((2, n_ch, 2 * RW_C, RW_GW), BF16),
                        pltpu.VMEM((2, n_ch, 2 * RW_C, RW_GW), BF16),
                        pltpu.VMEM((2, n_ch, 2 * RW_C, RW_GW), BF16),
                        pltpu.VMEM((2, n_ch, SUBLANES, RW_GW), F32)],
        compiler_params=_cparams(("arbitrary", "arbitrary")),
        name="rwkv_scan",
    )(r, v, kk, lw, kd, be, r, v, kk, lw, kd, be, s0)


MERGE_TM = 128
ROUTER_LANES = LANES


def _merge_kernel(orf_ref, orb_ref, rg_ref, ogf_ref, ogb_ref, gg_ref, yf_ref, yb_ref, bonus_ref, grw_ref,
                  gates_ref, x_ref, gt_ref, sh_ref, sc_ref, wbr_ref, wout_ref, rgn_ref, ggn_ref, gng_ref, gnb_ref,
                  l1g_ref, l1b_ref, wr_ref, br_ref, x1_ref, u2_ref, lg_ref):
    f32 = lambda ref: ref[0].astype(F32)
    o_ret = f32(orf_ref) + f32(orb_ref)
    o_gla = f32(ogf_ref) + f32(ogb_ref)
    y_ret, y_gla = [], []
    for h in range(RET_H):
        hs = slice(h * RET_DV, (h + 1) * RET_DV)
        y_ret.append(_ln(o_ret[:, hs]))
        og = o_gla[:, hs]
        y_gla.append(og * lax.rsqrt(jnp.mean(og * og, axis=-1, keepdims=True) + EPS))
    z_ret = _silu(f32(rg_ref)) * (jnp.concatenate(y_ret, axis=1) * rgn_ref[...])
    z_gla = _silu(f32(gg_ref)) * (jnp.concatenate(y_gla, axis=1) * ggn_ref[...])
    y = f32(yf_ref) + f32(yb_ref)
    mu = _head_sum(y) * (1.0 / RW_N)
    yc = y - mu
    var = _head_sum(yc * yc) * (1.0 / RW_N)
    z_rw = (yc * lax.rsqrt(var + EPS) * gng_ref[...] + gnb_ref[...] + f32(bonus_ref)) * f32(grw_ref)
    gates = f32(gates_ref)
    mixed = (gates[:, 0:D] * _mm(z_ret, wbr_ref[0]) + gates[:, D:2 * D] * _mm(z_gla, wbr_ref[1])
             + gates[:, 2 * D:] * _mm(z_rw, wbr_ref[2]))
    mix = _mm(mixed, wout_ref[...])
    x1 = _ln(ALPHA * x_ref[0] + gt_ref[0] * mix) * l1g_ref[...] + l1b_ref[...]
    x1_ref[0] = x1
    u2 = _ln(x1) * (1.0 + sc_ref[0]) + sh_ref[0]
    u2_ref[0] = u2.astype(u2_ref.dtype)
    lg_ref[...] = (_mm_split(u2, wr_ref[0], wr_ref[1]) + br_ref[...]).T


def _merge(orf, orb, pr, ogf, ogb, pg, yf, yb, bonus, grw, gates, x, gt1, sh2, sc2, wbr, wout, rgn, ggn, gng, gnb,
           l1g, l1b, wr, br):
    b, l, _ = x.shape
    tm = MERGE_TM
    tok = pl.BlockSpec((1, tm, D), lambda i, j: (i, j, 0))
    gate_blk = pl.BlockSpec((1, tm, D), lambda i, j: (i, j, 2))
    vec = pl.BlockSpec((1, 1, D), lambda i, j: (i, 0, 0))
    row = pl.BlockSpec((1, D), lambda i, j: (0, 0))
    in_specs = [
        tok, tok, gate_blk, tok, tok, gate_blk, tok, tok, tok, tok,
        pl.BlockSpec((1, tm, 3 * D), lambda i, j: (i, j, 0)),
        tok, vec, vec, vec,
        pl.BlockSpec((3, D, D), lambda i, j: (0, 0, 0)),
        pl.BlockSpec((D, D), lambda i, j: (0, 0)),
        row, row, row, row, row, row,
        pl.BlockSpec((2, D, ROUTER_LANES), lambda i, j: (0, 0, 0)),
        pl.BlockSpec((1, ROUTER_LANES), lambda i, j: (0, 0)),
    ]
    sd = jax.ShapeDtypeStruct
    r1 = lambda a: a.reshape(1, D)
    return pl.pallas_call(
        _merge_kernel,
        grid=(b, l // tm),
        in_specs=in_specs,
        out_specs=[tok, tok, pl.BlockSpec((ROUTER_LANES, tm), lambda i, j: (0, i * (l // tm) + j))],
        out_shape=[sd((b, l, D), F32), sd((b, l, D), BF16), sd((ROUTER_LANES, b * l), F32)],
        compiler_params=_cparams(("arbitrary", "arbitrary")),
        name="merge",
    )(orf, orb, pr, ogf, ogb, pg, yf, yb, bonus, grw, gates, x, gt1, sh2, sc2, wbr, wout,
      r1(rgn), r1(ggn), r1(gng), r1(gnb), r1(l1g), r1(l1b), wr, br)


MOE_TM = 1024


def _moe_block_rows(tm):
    rows = tm / N_GROUPS + 4.0 * np.sqrt(tm * (N_GROUPS - 1.0)) / N_GROUPS
    bf16_tile = 2 * SUBLANES
    return int(min(tm, -(-rows // bf16_tile) * bf16_tile))


def _first_argmax_rows(x, row):
    m = jnp.max(x, axis=0, keepdims=True)
    idx = jnp.min(jnp.where(x == m, row, x.shape[0]), axis=0, keepdims=True)
    return m, idx


def _routing(lt):
    tm = lt.shape[1]
    gl = lt[0:N_GROUPS]
    gmax, gidx = _first_argmax_rows(gl, lax.broadcasted_iota(jnp.int32, (N_GROUPS, tm), 0))
    g_w = 1.0 / jnp.sum(jnp.exp(gl - gmax), axis=0, keepdims=True)
    row = lax.broadcasted_iota(jnp.int32, (N_EXPERTS, tm), 0)
    neg = -jnp.inf
    el = jnp.where(jnp.right_shift(row, 2) == gidx, lt[N_GROUPS:N_GROUPS + N_EXPERTS], neg)
    m1, i1 = _first_argmax_rows(el, row)
    m2, i2 = _first_argmax_rows(jnp.where(row == i1, neg, el), row)
    e2 = jnp.exp(m2 - m1)
    w1 = 1.0 / (1.0 + e2)
    return gidx, g_w * (jnp.where(row == i1, w1, 0.0) + jnp.where(row == i2, e2 * w1, 0.0))


def _moe_kernel(u_ref, lt_ref, wg_ref, wu_ref, wd_ref, o_ref, acc_ref):
    g = pl.program_id(1)
    tm = acc_ref.shape[0]
    blk = _moe_block_rows(tm)

    @pl.when(g == 0)
    def _():
        acc_ref[...] = jnp.zeros_like(acc_ref)

    gidx, comb = _routing(lt_ref[...])
    member = jnp.broadcast_to((gidx == g).astype(F32), (SUBLANES, tm))
    lane = lax.broadcasted_iota(jnp.int32, (SUBLANES, tm), 1)
    count = member
    sh = 1
    while sh < tm:
        count = count + jnp.where(lane >= sh, pltpu.roll(count, sh, 1), 0.0)
        sh *= 2
    pos = jnp.where(member > 0.0, count - 1.0, -1.0).astype(jnp.int32)[0:1]
    n_tok = jnp.max(count).astype(jnp.int32)
    comb_pad = jnp.concatenate([comb, jnp.zeros((LANES - N_EXPERTS, tm), F32)], axis=0)
    comb_parts = _split2(comb_pad)
    u = u_ref[...]
    sel_lane = lax.broadcasted_iota(jnp.int32, (blk, LANES), 1)

    def body(i, carry):
        slot = lax.broadcasted_iota(jnp.int32, (blk, tm), 0) + i * blk
        onehot = jnp.where(slot == pos, 1.0, 0.0).astype(BF16)
        xg = jnp.dot(onehot, u, preferred_element_type=F32).astype(BF16)
        cw = sum(lax.dot_general(onehot, c, (((1,), (1,)), ((), ())), preferred_element_type=F32)
                 for c in comb_parts)
        y = jnp.zeros((blk, D), F32)
        for e in range(EPG):
            hid = (_silu(jnp.dot(xg, wg_ref[e], preferred_element_type=F32))
                   * jnp.dot(xg, wu_ref[e], preferred_element_type=F32))
            c_e = jnp.sum(jnp.where(sel_lane == g * EPG + e, cw, 0.0), axis=-1, keepdims=True)
            y = y + c_e * _mm(hid, wd_ref[e])
        acc_ref[...] += _mm_tn(onehot, y)
        return carry

    lax.fori_loop(0, (n_tok + blk - 1) // blk, body, 0)

    @pl.when(g == N_GROUPS - 1)
    def _():
        o_ref[...] = acc_ref[...].astype(o_ref.dtype)


def _moe(u2, logits_t, wg, wu, wd):
    b, l, _ = u2.shape
    t = b * l
    tm = min(MOE_TM, t)
    tok = pl.BlockSpec((tm, D), lambda i, g: (i, 0))
    in_specs = [
        tok,
        pl.BlockSpec((ROUTER_LANES, tm), lambda i, g: (0, i)),
        pl.BlockSpec((EPG, D, EXPERT_FF), lambda i, g: (g, 0, 0)),
        pl.BlockSpec((EPG, D, EXPERT_FF), lambda i, g: (g, 0, 0)),
        pl.BlockSpec((EPG, EXPERT_FF, D), lambda i, g: (g, 0, 0)),
    ]
    return pl.pallas_call(
        _moe_kernel,
        grid=(t // tm, N_GROUPS),
        in_specs=in_specs,
        out_specs=tok,
        out_shape=jax.ShapeDtypeStruct((t, D), BF16),
        scratch_shapes=[pltpu.VMEM((tm, D), F32)],
        compiler_params=_cparams(("arbitrary",) * 2),
        name="moe",
    )(u2.reshape(t, D), logits_t, wg, wu, wd).reshape(b, l, D)


def _post_kernel(x_ref, m_ref, gt_ref, l2g_ref, l2b_ref, sh_ref, sc_ref, o_ref, u_ref):
    x2 = _ln(ALPHA * x_ref[0] + gt_ref[0] * m_ref[0].astype(F32)) * l2g_ref[...] + l2b_ref[...]
    o_ref[0] = x2
    u_ref[0] = (_ln(x2) * (1.0 + sc_ref[0]) + sh_ref[0]).astype(u_ref.dtype)


def _post(x1, moe_out, gt2, l2g, l2b, sh_next, sc_next):
    b, l, _ = x1.shape
    tm = 512 if l % 512 == 0 else 256
    tok = pl.BlockSpec((1, tm, D), lambda i, j: (i, j, 0))
    vec = pl.BlockSpec((1, 1, D), lambda i, j: (i, 0, 0))
    row = pl.BlockSpec((1, D), lambda i, j: (0, 0))
    sd = jax.ShapeDtypeStruct
    return pl.pallas_call(
        _post_kernel,
        grid=(b, l // tm),
        in_specs=[tok, tok, vec, row, row, vec, vec],
        out_specs=[tok, tok],
        out_shape=[sd((b, l, D), F32), sd((b, l, D), BF16)],
        compiler_params=_cparams(("arbitrary", "arbitrary")),
        name="post_ln",
    )(x1, moe_out, gt2, l2g.reshape(1, D), l2b.reshape(1, D), sh_next, sc_next)


def _rope_tables(l):
    t = np.arange(l)
    quarter = RET_DK // 4
    freqs = (np.float32(ROPE_BASE) ** (-np.arange(quarter, dtype=np.float32) / quarter)).astype(np.float32)
    rows = (t // GRID_W).astype(np.float32)
    cols = (t % GRID_W).astype(np.float32)
    ang = jnp.asarray(np.concatenate([rows[:, None] * freqs, cols[:, None] * freqs], -1))
    cos, sin = jnp.cos(ang), jnp.sin(ang)
    return jnp.concatenate([cos, cos], -1), jnp.concatenate([-sin, sin], -1)


def _layer_weights(p):
    n_rg = 2 * (RET_H * RET_DK + RET_H * RET_DV)
    w_in = p["w_in"]
    w_lr = jnp.pad(w_in[:, 2 * n_rg:2 * n_rg + GLA_LOWRANK], ((0, 0), (0, LANES - GLA_LOWRANK)))
    w2p, b2p = _gla_gate_params(p["gla_w2"], p["gla_b"])
    pad = ROUTER_LANES - N_GROUPS - N_EXPERTS
    return dict(
        p,
        w_ret=w_in[:, :n_rg].astype(BF16),
        w_gla=w_in[:, n_rg:2 * n_rg].astype(BF16),
        w_glr=w_lr.astype(BF16),
        w_rw=w_in[:, 2 * n_rg + GLA_LOWRANK:2 * n_rg + GLA_LOWRANK + 3 * D].astype(BF16),
        w_rlo=w_in[:, 2 * n_rg + GLA_LOWRANK + 3 * D:].astype(BF16),
        w_merge=p["w_merge"].astype(BF16),
        w_br=p["w_br"].astype(BF16),
        w_out=p["w_out"].astype(BF16),
        gla_w2p=w2p, gla_b2p=b2p,
        w_router=jnp.stack(_split2(jnp.pad(jnp.concatenate([p["w_rg"], p["w_re"]], axis=1), ((0, 0), (0, pad))))),
        b_router=jnp.pad(jnp.concatenate([p["b_rg"], p["b_re"]]), (0, pad)).reshape(1, ROUTER_LANES),
        w_eg=p["w_eg"].astype(BF16), w_eu=p["w_eu"].astype(BF16), w_ed=p["w_ed"].astype(BF16),
    )


def _split_mod(mod):
    return [m.reshape(mod.shape[0], 1, D) for m in jnp.split(mod, 6, axis=-1)]


def _layer(x, u, mod, mod_next, s_ret, s_gla_t, s_rw, rope, p):
    b, l, _ = x.shape
    _, _, gt1, sh2, sc2, gt2 = _split_mod(mod)
    sh_next, sc_next = _split_mod(mod_next)[:2]
    u2d = u.reshape(b * l, D)

    def proj(w, tn, bias=None, **kw):
        bias = jnp.zeros((w.shape[1],), F32) if bias is None else bias
        return _proj(u2d, w, bias, tn, **kw).reshape(b, l, -1)

    pr = proj(p["w_ret"], 1024)
    pg = proj(p["w_gla"], 1024)
    plr = proj(p["w_glr"], LANES, out_dtype=F32)
    rw = proj(p["w_rw"], 1024)
    rlo = proj(p["w_rlo"], N_RW_LORA, out_dtype=F32)
    gates = proj(p["w_merge"], 1024, bias=p["b_merge"], act="sigmoid")

    cos2, sin2 = rope if rope is not None else (jnp.ones((l, RET_DK), F32), jnp.zeros((l, RET_DK), F32))
    orf, orb, ret_fin = _retention(pr, cos2, sin2, s_ret, rope is not None)
    ogf, ogb, gla_fin_t = _gla(pg, plr, p["gla_w2p"], p["gla_b2p"], s_gla_t)
    r, v, kk, lw, kd, be, bonus, grw = _rw_prep(rw, rlo, p["rwkv_shift"], p["rwkv_kk"], p["rwkv_ka"],
                                                p["rwkv_rk"].reshape(-1), p["rwkv_w0"], p["rwkv_wb"],
                                                p["rwkv_a0"], p["rwkv_ab"], p["rwkv_gb"])
    yf, yb, rw_fin = _rw_scan(r, v, kk, lw, kd, be, s_rw)
    x1, u2, logits_t = _merge(orf, orb, pr, ogf, ogb, pg, yf, yb, bonus, grw, gates, x, gt1, sh2, sc2,
                              p["w_br"], p["w_out"], p["ret_gn"], p["gla_gn"], p["rwkv_gn_g"], p["rwkv_gn_b"],
                              p["ln1_g"], p["ln1_b"], p["w_router"], p["b_router"])
    moe_out = _moe(u2, logits_t, p["w_eg"], p["w_eu"], p["w_ed"])
    x2, u_next = _post(x1, moe_out, gt2, p["ln2_g"], p["ln2_b"], sh_next, sc_next)
    return x2, u_next, (ret_fin, gla_fin_t, rw_fin)


_PARAM_NAMES = ("w_in", "rwkv_shift", "ret_gn", "gla_w2", "gla_b", "gla_gn", "rwkv_w0", "rwkv_wb", "rwkv_a0",
                "rwkv_ab", "rwkv_gb", "rwkv_kk", "rwkv_ka", "rwkv_rk", "rwkv_gn_g", "rwkv_gn_b", "w_br", "w_merge",
                "b_merge", "w_out", "ln1_g", "ln1_b", "ln2_g", "ln2_b", "w_rg", "b_rg", "w_re", "b_re", "w_eg",
                "w_eu", "w_ed")


def kernel(x_prompt, x_sample, state_ret, state_gla, state_rwkv, c, c_ctx, w_ada, b_ada, w_in, rwkv_shift, ret_gn,
           gla_w2, gla_b, gla_gn, rwkv_w0, rwkv_wb, rwkv_a0, rwkv_ab, rwkv_gb, rwkv_kk, rwkv_ka, rwkv_rk, rwkv_gn_g,
           rwkv_gn_b, w_br, w_merge, b_merge, w_out, ln1_g, ln1_b, ln2_g, ln2_b, w_rg, b_rg, w_re, b_re, w_eg, w_eu,
           w_ed):
    params = dict(zip(_PARAM_NAMES, (w_in, rwkv_shift, ret_gn, gla_w2, gla_b, gla_gn, rwkv_w0, rwkv_wb, rwkv_a0,
                                     rwkv_ab, rwkv_gb, rwkv_kk, rwkv_ka, rwkv_rk, rwkv_gn_g, rwkv_gn_b, w_br, w_merge,
                                     b_merge, w_out, ln1_g, ln1_b, ln2_g, ln2_b, w_rg, b_rg, w_re, b_re, w_eg, w_eu,
                                     w_ed)))
    bc, bl = x_prompt.shape[0], x_sample.shape[0]
    rope = _rope_tables(x_sample.shape[1])
    mod_rows = 2 * SUBLANES
    c_all = jnp.concatenate([c, c_ctx[None, :], jnp.zeros((mod_rows - bl - 1, D), F32)], axis=0)
    z_ret = jnp.zeros((bc, 2, RET_H, RET_DK, RET_DV), F32)
    z_gla_t = jnp.zeros((bc, 2, GLA_H, GLA_DV, GLA_DK), F32)
    z_rw = jnp.zeros((bc, 2, RW_H, RW_N, RW_N), F32)
    mods = [_modulation(c_all, w_ada[layer], b_ada[layer]) for layer in range(DEPTH)]
    mods.append(jnp.zeros_like(mods[0]))
    mods_lat = [m[:bl] for m in mods]
    mods_ctx = [jnp.broadcast_to(m[bl:bl + 1], (bc, 6 * D)) for m in mods]
    h_ctx, h_lat = x_prompt, x_sample
    sh, sc = _split_mod(mods_ctx[0])[:2]
    u_ctx = _lnmod(h_ctx, sh, sc)
    sh, sc = _split_mod(mods_lat[0])[:2]
    u_lat = _lnmod(h_lat, sh, sc)
    new_ret, new_gla, new_rw = [], [], []
    for layer in range(DEPTH):
        p = _layer_weights({k: v[layer] for k, v in params.items()})
        h_ctx, u_ctx, (s_ret, s_gla_t, s_rw) = _layer(h_ctx, u_ctx, mods_ctx[layer], mods_ctx[layer + 1],
                                                      z_ret, z_gla_t, z_rw, None, p)
        new_ret.append(s_ret)
        new_gla.append(jnp.swapaxes(s_gla_t, -1, -2))
        new_rw.append(jnp.swapaxes(s_rw, -1, -2))
        h_lat, u_lat, _ = _layer(h_lat, u_lat, mods_lat[layer], mods_lat[layer + 1], state_ret[:, layer],
                                 jnp.swapaxes(state_gla[:, layer], -1, -2),
                                 jnp.swapaxes(state_rwkv[:, layer], -1, -2), rope, p)
    return (h_ctx, h_lat, jnp.stack(new_ret, axis=1), jnp.stack(new_gla, axis=1), jnp.stack(new_rw, axis=1))
```

```python
import functools

import jax
import jax.numpy as jnp
import numpy as np
from jax import lax
from jax.experimental import pallas as pl
from jax.experimental.pallas import tpu as pltpu

F32 = jnp.float32
BF16 = jnp.bfloat16
HI = lax.Precision.HIGHEST

D = 1024
DEPTH = 2
GRID_W = 64
RET_H, RET_DK, RET_DV = 4, 128, 256
GLA_H, GLA_DK, GLA_DV = 4, 128, 256
GLA_LOWRANK = 16
GLA_GATE_NORM = 16.0
RW_H, RW_N = 16, 64
RW_W_LORA, RW_A_LORA, RW_G_LORA = 64, 64, 128
N_RW_COLS = 3 * D + RW_W_LORA + RW_A_LORA + RW_G_LORA
N_GROUPS, EPG, N_EXPERTS, EXPERT_FF = 4, 4, 16, 512
ALPHA = (2 * DEPTH) ** 0.25
EPS = 1e-5
ROPE_BASE = 10000.0

LANES = 128
SUBLANES = 8
VMEM_LIMIT = 56 * 1024 * 1024

RET_C = 128
GLA_C = 64
GLA_LEVELS = 6
RW_C = 32
RW_LEVELS = 5
RW_TL = 256
RW_BB = 2
RW_G = 4
RW_GW = RW_G * RW_N
RW_SW = 2 * RW_N


def _cparams(sem):
    return pltpu.CompilerParams(dimension_semantics=sem, vmem_limit_bytes=VMEM_LIMIT)


def _mm(a, b):
    return jnp.dot(a.astype(BF16), b.astype(BF16), preferred_element_type=F32)


def _mm_nt(a, b):
    return lax.dot_general(a.astype(BF16), b.astype(BF16), (((1,), (1,)), ((), ())), preferred_element_type=F32)


def _mm_tn(a, b):
    return lax.dot_general(a.astype(BF16), b.astype(BF16), (((0,), (0,)), ((), ())), preferred_element_type=F32)


def _mm_hi(a, b):
    return jnp.dot(a, b, preferred_element_type=F32, precision=HI)


def _split2(x):
    hi = x.astype(BF16)
    return hi, (x - hi.astype(F32)).astype(BF16)


def _mm_split(a, b_hi, b_lo):
    a_hi, a_lo = _split2(a)
    dot = functools.partial(jnp.dot, preferred_element_type=F32)
    return dot(a_hi, b_hi) + (dot(a_hi, b_lo) + dot(a_lo, b_hi))


def _sigmoid(x):
    return 1.0 / (1.0 + jnp.exp(-x))


def _silu(x):
    return x * _sigmoid(x)


def _log_sigmoid(x):
    return jnp.minimum(x, 0.0) - jnp.log(1.0 + jnp.exp(-jnp.abs(x)))


def _softplus(x):
    return jnp.maximum(x, 0.0) + jnp.log(1.0 + jnp.exp(-jnp.abs(x)))


def _ln(x):
    mu = jnp.mean(x, axis=-1, keepdims=True)
    xc = x - mu
    var = jnp.mean(xc * xc, axis=-1, keepdims=True)
    return xc * lax.rsqrt(var + EPS)


def _scan_rows(x, reverse):
    n = x.shape[0]
    row = lax.broadcasted_iota(jnp.int32, x.shape, 0)
    sh = 1
    while sh < n:
        if reverse:
            x = x + jnp.where(row < n - sh, pltpu.roll(x, n - sh, 0), 0.0)
        else:
            x = x + jnp.where(row >= sh, pltpu.roll(x, sh, 0), 0.0)
        sh *= 2
    return x


def _mod_kernel(c_ref, w_ref, b_ref, o_ref):
    o_ref[...] = _mm_hi(_silu(c_ref[...]), w_ref[...]) + b_ref[...]


def _modulation(c_all, w, b):
    m, n, tn = c_all.shape[0], w.shape[1], 512
    return pl.pallas_call(
        _mod_kernel,
        grid=(n // tn,),
        in_specs=[pl.BlockSpec((m, D), lambda j: (0, 0)),
                  pl.BlockSpec((D, tn), lambda j: (0, j)),
                  pl.BlockSpec((1, tn), lambda j: (0, j))],
        out_specs=pl.BlockSpec((m, tn), lambda j: (0, j)),
        out_shape=jax.ShapeDtypeStruct((m, n), F32),
        compiler_params=_cparams(("arbitrary",)),
        name="adaln_mod",
    )(c_all, w, b.reshape(1, n))


def _lnmod_kernel(x_ref, sh_ref, sc_ref, u_ref):
    u_ref[0] = (_ln(x_ref[0]) * (1.0 + sc_ref[0]) + sh_ref[0]).astype(u_ref.dtype)


def _lnmod(x, sh, sc):
    b, l, _ = x.shape
    tm = 512 if l % 512 == 0 else 256
    vec = pl.BlockSpec((1, 1, D), lambda i, j: (i, 0, 0))
    return pl.pallas_call(
        _lnmod_kernel,
        grid=(b, l // tm),
        in_specs=[pl.BlockSpec((1, tm, D), lambda i, j: (i, j, 0)), vec, vec],
        out_specs=pl.BlockSpec((1, tm, D), lambda i, j: (i, j, 0)),
        out_shape=jax.ShapeDtypeStruct((b, l, D), BF16),
        compiler_params=_cparams(("arbitrary", "arbitrary")),
        name="ln_mod",
    )(x, sh, sc)


def _proj_kernel(u_ref, w_ref, b_ref, o_ref, *, act):
    y = jnp.dot(u_ref[...], w_ref[...], preferred_element_type=F32) + b_ref[...]
    if act == "sigmoid":
        y = _sigmoid(y)
    o_ref[...] = y.astype(o_ref.dtype)


def _proj(u2d, w, bias, tn, act=None, out_dtype=BF16):
    t, n = u2d.shape[0], w.shape[1]
    tm = min(t, 2048)
    return pl.pallas_call(
        functools.partial(_proj_kernel, act=act),
        grid=(t // tm, n // tn),
        in_specs=[pl.BlockSpec((tm, D), lambda i, j: (i, 0)),
                  pl.BlockSpec((D, tn), lambda i, j: (0, j)),
                  pl.BlockSpec((1, tn), lambda i, j: (0, j))],
        out_specs=pl.BlockSpec((tm, tn), lambda i, j: (i, j)),
        out_shape=jax.ShapeDtypeStruct((t, n), out_dtype),
        compiler_params=_cparams(("arbitrary", "arbitrary")),
        name="proj",
    )(u2d, w, bias.reshape(1, n))


def _ret_log_gamma(d, h):
    hh = h if d == 0 else RET_H - 1 - h
    return float(np.log(1.0 - 2.0 ** (-5.0 - hh)))


def _ret_tables():
    c = RET_C
    i = np.arange(c, dtype=np.float64)
    dec = np.zeros((2 * RET_H, c, c), np.float32)
    qd = np.zeros((2 * RET_H, c, RET_DK), np.float32)
    kd = np.zeros((2 * RET_H, c, RET_DK), np.float32)
    for d in range(2):
        tau = i if d == 0 else c - 1 - i
        rel = tau[:, None] - tau[None, :]
        for h in range(RET_H):
            lg = _ret_log_gamma(d, h)
            dec[d * RET_H + h] = np.where(rel >= 0, np.exp(np.maximum(rel, 0.0) * lg), 0.0)
            qd[d * RET_H + h] = np.exp((tau + 1.0) * lg)[:, None]
            kd[d * RET_H + h] = np.exp((c - 1.0 - tau) * lg)[:, None]
    return jnp.asarray(dec), jnp.asarray(qd), jnp.asarray(kd)


def _rope(x, cos2, sin2):
    return x * cos2 + pltpu.roll(x, RET_DK // 2, 1) * sin2


def _ret_kernel(qf_ref, kf_ref, vf_ref, qb_ref, kb_ref, vb_ref, cosf_ref, sinf_ref, cosb_ref, sinb_ref,
                dec_ref, qd_ref, kd_ref, s0_ref, of_ref, ob_ref, sfin_ref, s_ref, *, use_rope, n_chunks):
    n = pl.program_id(1)

    @pl.when(n == 0)
    def _():
        s_ref[...] = s0_ref[0]

    ins = ((qf_ref, kf_ref, vf_ref, cosf_ref, sinf_ref), (qb_ref, kb_ref, vb_ref, cosb_ref, sinb_ref))
    outs = (of_ref, ob_ref)
    chains = [(d, h) for d in range(2) for h in range(RET_H)]
    q, k, v = {}, {}, {}
    for d, h in chains:
        q_ref, k_ref, v_ref, cos_ref, sin_ref = ins[d]
        dk = slice(h * RET_DK, (h + 1) * RET_DK)
        q[d, h] = q_ref[0, :, dk].astype(F32)
        k[d, h] = k_ref[0, :, dk].astype(F32) * (RET_DK ** -0.5)
        if use_rope:
            q[d, h] = _rope(q[d, h], cos_ref[...], sin_ref[...])
            k[d, h] = _rope(k[d, h], cos_ref[...], sin_ref[...])
        v[d, h] = v_ref[0, :, h * RET_DV:(h + 1) * RET_DV]
    sc = {ch: _mm_nt(q[ch], k[ch]) * dec_ref[ch[0] * RET_H + ch[1]] for ch in chains}
    s_old = {ch: s_ref[ch[0], ch[1]] for ch in chains}
    o = {ch: _mm(sc[ch], v[ch]) + _mm(q[ch] * qd_ref[ch[0] * RET_H + ch[1]], s_old[ch]) for ch in chains}
    s_new = {ch: s_old[ch] * float(np.exp(RET_C * _ret_log_gamma(*ch)))
             + _mm_tn(k[ch] * kd_ref[ch[0] * RET_H + ch[1]], v[ch]) for ch in chains}
    for d, h in chains:
        outs[d][0, :, h * RET_DV:(h + 1) * RET_DV] = o[d, h].astype(outs[d].dtype)
        s_ref[d, h] = s_new[d, h]

    @pl.when(n == n_chunks - 1)
    def _():
        sfin_ref[0] = s_ref[...]


def _retention(pr, cos2, sin2, s0, use_rope):
    b, l, _ = pr.shape
    c = RET_C
    nc = l // c
    dec, qd, kd = _ret_tables()
    qw, vw = RET_H * RET_DK, RET_H * RET_DV

    def fw(blk):
        return lambda i, n: (i, n, blk)

    def bw(blk):
        return lambda i, n: (i, nc - 1 - n, blk)

    def const(*shape):
        return pl.BlockSpec(shape, lambda i, n: (0,) * len(shape))

    st = pl.BlockSpec((1, 2, RET_H, RET_DK, RET_DV), lambda i, n: (i, 0, 0, 0, 0))
    in_specs = [
        pl.BlockSpec((1, c, qw), fw(0)), pl.BlockSpec((1, c, qw), fw(1)), pl.BlockSpec((1, c, vw), fw(1)),
        pl.BlockSpec((1, c, qw), bw(0)), pl.BlockSpec((1, c, qw), bw(1)), pl.BlockSpec((1, c, vw), bw(1)),
        pl.BlockSpec((c, RET_DK), lambda i, n: (n, 0)), pl.BlockSpec((c, RET_DK), lambda i, n: (n, 0)),
        pl.BlockSpec((c, RET_DK), lambda i, n: (nc - 1 - n, 0)), pl.BlockSpec((c, RET_DK), lambda i, n: (nc - 1 - n, 0)),
        const(2 * RET_H, c, c), const(2 * RET_H, c, RET_DK), const(2 * RET_H, c, RET_DK), st,
    ]
    out_specs = [pl.BlockSpec((1, c, vw), fw(0)), pl.BlockSpec((1, c, vw), bw(0)), st]
    sd = jax.ShapeDtypeStruct
    return pl.pallas_call(
        functools.partial(_ret_kernel, use_rope=use_rope, n_chunks=nc),
        grid=(b, nc),
        in_specs=in_specs,
        out_specs=out_specs,
        out_shape=[sd((b, l, vw), BF16), sd((b, l, vw), BF16), sd((b, 2, RET_H, RET_DK, RET_DV), F32)],
        scratch_shapes=[pltpu.VMEM((2, RET_H, RET_DK, RET_DV), F32)],
        compiler_params=_cparams(("arbitrary", "arbitrary")),
        name="retention",
    )(pr, pr, pr, pr, pr, pr, cos2, sin2, cos2, sin2, dec, qd, kd, s0)


def _gla_boundary(b, lv, reverse, rolls):
    c, w = b.shape
    m = 1 << lv
    if m >= SUBLANES:
        parts = []
        for p0 in range(0, c, 2 * m):
            e = p0 + m if reverse else p0 + m - 1
            parts.append(jnp.broadcast_to(b[e:e + 1, :], (2 * m, w)))
        return jnp.concatenate(parts, axis=0)

    def rolled(s):
        s %= c
        if s not in rolls:
            rolls[s] = b if s == 0 else pltpu.roll(b, s, 0)
        return rolls[s]

    row = lax.broadcasted_iota(jnp.int32, (c, w), 0)
    r = jnp.bitwise_and(row, m - 1)
    upper = jnp.bitwise_and(jnp.right_shift(row, lv), 1) == 1
    out = b
    for t in range(m):
        if reverse:
            out = jnp.where(upper & (r == t), rolled(t), out)
            out = jnp.where(jnp.logical_not(upper) & (r == t), rolled(-(m - t)), out)
        else:
            out = jnp.where(upper & (r == t), rolled(t + 1), out)
            out = jnp.where(jnp.logical_not(upper) & (r == t), rolled(-(m - 1 - t)), out)
    return out


def _gla_kernel(qf_ref, kf_ref, vf_ref, lrf_ref, qb_ref, kb_ref, vb_ref, lrb_ref, w2_ref, b2_ref, s0_ref,
                of_ref, ob_ref, sfin_ref, st_ref, *, n_chunks):
    c = GLA_C
    n = pl.program_id(1)

    @pl.when(n == 0)
    def _():
        st_ref[...] = s0_ref[0]

    ins = ((qf_ref, kf_ref, vf_ref, lrf_ref), (qb_ref, kb_ref, vb_ref, lrb_ref))
    outs = (of_ref, ob_ref)
    hw = GLA_H * GLA_DK
    row = lax.broadcasted_iota(jnp.int32, (c, hw), 0)
    ri = lax.broadcasted_iota(jnp.int32, (c, c), 0)
    ci = lax.broadcasted_iota(jnp.int32, (c, c), 1)
    chains = [(d, h) for d in range(2) for h in range(GLA_H)]

    def dk(h):
        return slice(h * GLA_DK, (h + 1) * GLA_DK)

    def dv(h):
        return slice(h * GLA_DV, (h + 1) * GLA_DV)

    q, k, v, b_inc, b_rest, tot, ql = {}, {}, {}, {}, {}, {}, {}
    for d in range(2):
        reverse = d == 1
        q_ref, k_ref, v_ref, lr_ref = ins[d]
        q[d] = q_ref[0].astype(F32) * (GLA_DK ** -0.5)
        k[d] = k_ref[0].astype(F32)
        v[d] = v_ref[0]
        gate = _log_sigmoid(_mm_split(lr_ref[0], w2_ref[d, 0], w2_ref[d, 1]) + b2_ref[d])
        gate = gate * (1.0 / GLA_GATE_NORM)
        b = _scan_rows(gate, reverse)
        tot[d] = b[0:1] if reverse else b[c - 1:c]
        b_inc[d], b_rest[d] = b, tot[d] - b
        rolls = {}
        for lv in range(GLA_LEVELS):
            be = _gla_boundary(b, lv, reverse, rolls)
            upper = jnp.bitwise_and(jnp.right_shift(row, lv), 1) == 1
            second = jnp.logical_not(upper) if reverse else upper
            e = jnp.exp(-jnp.abs(b - be))
            ql[d, lv] = jnp.where(second, q[d], k[d]) * e

    attn = {(d, h): jnp.where(ri == ci, jnp.sum(q[d][:, dk(h)] * k[d][:, dk(h)], axis=-1, keepdims=True), 0.0)
            for d, h in chains}
    for lv in range(GLA_LEVELS):
        same = jnp.right_shift(ri, lv + 1) == jnp.right_shift(ci, lv + 1)
        row_upper = jnp.bitwise_and(jnp.right_shift(ri, lv), 1) == 1
        col_upper = jnp.bitwise_and(jnp.right_shift(ci, lv), 1) == 1
        pair_mask = (same & row_upper & jnp.logical_not(col_upper), same & jnp.logical_not(row_upper) & col_upper)
        for d, h in chains:
            p = ql[d, lv][:, dk(h)]
            attn[d, h] = attn[d, h] + jnp.where(pair_mask[d], _mm_nt(p, p), 0.0)
    st_old = {ch: st_ref[ch[0], ch[1]] for ch in chains}
    o = {(d, h): _mm(attn[d, h], v[d][:, dv(h)]) + _mm_nt(q[d][:, dk(h)] * jnp.exp(b_inc[d][:, dk(h)]), st_old[d, h])
         for d, h in chains}
    st_new = {(d, h): st_old[d, h] * jnp.exp(tot[d][:, dk(h)])
              + _mm_tn(v[d][:, dv(h)], k[d][:, dk(h)] * jnp.exp(b_rest[d][:, dk(h)])) for d, h in chains}
    for d, h in chains:
        outs[d][0, :, dv(h)] = o[d, h].astype(outs[d].dtype)
        st_ref[d, h] = st_new[d, h]

    @pl.when(n == n_chunks - 1)
    def _():
        sfin_ref[0] = st_ref[...]


def _gla(pg, plr, w2p, b2, s0t):
    b, l, _ = pg.shape
    c = GLA_C
    nc = l // c
    qw, vw = GLA_H * GLA_DK, GLA_H * GLA_DV

    def fw(blk):
        return lambda i, n: (i, n, blk)

    def bw(blk):
        return lambda i, n: (i, nc - 1 - n, blk)

    st = pl.BlockSpec((1, 2, GLA_H, GLA_DV, GLA_DK), lambda i, n: (i, 0, 0, 0, 0))
    in_specs = [
        pl.BlockSpec((1, c, qw), fw(0)), pl.BlockSpec((1, c, qw), fw(1)), pl.BlockSpec((1, c, vw), fw(1)),
        pl.BlockSpec((1, c, LANES), fw(0)),
        pl.BlockSpec((1, c, qw), bw(0)), pl.BlockSpec((1, c, qw), bw(1)), pl.BlockSpec((1, c, vw), bw(1)),
        pl.BlockSpec((1, c, LANES), bw(0)),
        pl.BlockSpec((2, 2, LANES, qw), lambda i, n: (0, 0, 0, 0)),
        pl.BlockSpec((2, 1, qw), lambda i, n: (0, 0, 0)),
        st,
    ]
    out_specs = [pl.BlockSpec((1, c, vw), fw(0)), pl.BlockSpec((1, c, vw), bw(0)), st]
    sd = jax.ShapeDtypeStruct
    return pl.pallas_call(
        functools.partial(_gla_kernel, n_chunks=nc),
        grid=(b, nc),
        in_specs=in_specs,
        out_specs=out_specs,
        out_shape=[sd((b, l, vw), BF16), sd((b, l, vw), BF16), sd((b, 2, GLA_H, GLA_DV, GLA_DK), F32)],
        scratch_shapes=[pltpu.VMEM((2, GLA_H, GLA_DV, GLA_DK), F32)],
        compiler_params=_cparams(("arbitrary", "arbitrary")),
        name="gla",
    )(pg, pg, pg, plr, pg, pg, pg, plr, w2p, b2, s0t)


def _gla_gate_params(w2, b2):
    w2p = jnp.pad(w2, ((0, 0), (0, LANES - GLA_LOWRANK), (0, 0)))
    return jnp.stack(_split2(w2p), axis=1), b2.reshape(2, 1, GLA_H * GLA_DK)


def _head_sum(x):
    i = lax.broadcasted_iota(jnp.int32, (LANES, LANES), 0)
    j = lax.broadcasted_iota(jnp.int32, (LANES, LANES), 1)
    ones = jnp.where(jnp.right_shift(i, 6) == jnp.right_shift(j, 6), 1.0, 0.0).astype(BF16)
    out = []
    for t in range(x.shape[1] // LANES):
        parts = _split2(x[:, t * LANES:(t + 1) * LANES])
        out.append(sum(jnp.dot(p, ones, preferred_element_type=F32) for p in parts))
    return jnp.concatenate(out, axis=1)


RW_TM = 256
HALO = 2 * SUBLANES
N_RW_LORA = RW_W_LORA + RW_A_LORA + RW_G_LORA


def _rw_prep_kernel(x_ref, xp_ref, xn_ref, lo_ref, lop_ref, lon_ref, taps_ref, ltaps_ref, kkp_ref, ka_ref, rk_ref,
                    w0_ref, wb_ref, a0_ref, ab_ref, gb_ref, r_ref, v_ref, kk_ref, lw_ref, kd_ref, be_ref, bonus_ref,
                    g_ref, xbuf, lbuf, *, n_tiles):
    tm = RW_TM
    j = pl.program_id(1)
    first, last = j == 0, j == n_tiles - 1
    xbuf[0:HALO, :] = jnp.where(first, jnp.zeros_like(xp_ref[0]), xp_ref[0])
    xbuf[HALO:HALO + tm, :] = x_ref[0]
    xbuf[HALO + tm:2 * HALO + tm, :] = jnp.where(last, jnp.zeros_like(xn_ref[0]), xn_ref[0])
    ri = lax.broadcasted_iota(jnp.int32, (tm, tm + 2 * HALO), 0)
    ci = lax.broadcasted_iota(jnp.int32, (tm, tm + 2 * HALO), 1)
    pick_prev = jnp.where(ci == ri + (HALO - 1), 1.0, 0.0).astype(BF16)
    pick_next = jnp.where(ci == ri + (HALO + 1), 1.0, 0.0).astype(BF16)

    def shifted_main(c0, c1):
        x_all = xbuf[:, c0:c1]
        return (taps_ref[0:1, c0:c1] * jnp.dot(pick_prev, x_all, preferred_element_type=F32)
                + taps_ref[1:2, c0:c1] * x_ref[0, :, c0:c1].astype(F32)
                + taps_ref[2:3, c0:c1] * jnp.dot(pick_next, x_all, preferred_element_type=F32))

    lbuf[HALO:HALO + tm, :] = lo_ref[0]
    lbuf[0:HALO, :] = jnp.where(first, 0.0, lop_ref[0])
    lbuf[HALO + tm:2 * HALO + tm, :] = jnp.where(last, 0.0, lon_ref[0])
    lora = (ltaps_ref[0:1, :] * lbuf[HALO - 1:HALO - 1 + tm, :] + ltaps_ref[1:2, :] * lbuf[HALO:HALO + tm, :]
            + ltaps_ref[2:3, :] * lbuf[HALO + 1:HALO + 1 + tm, :])
    r = shifted_main(0, D)
    k = shifted_main(D, 2 * D)
    v = shifted_main(2 * D, 3 * D)
    xw = lora[:, 0:RW_W_LORA]
    xa = lora[:, RW_W_LORA:RW_W_LORA + RW_A_LORA]
    xg = lora[:, RW_W_LORA + RW_A_LORA:]
    r_ref[0] = r.astype(r_ref.dtype)
    v_ref[0] = v.astype(v_ref.dtype)
    kk = k * kkp_ref[...]
    kk = kk * lax.rsqrt(jnp.maximum(_head_sum(kk * kk), 1e-24))
    kk_ref[0] = kk.astype(kk_ref.dtype)
    g_ref[0] = _mm(_sigmoid(xg), gb_ref[...]).astype(g_ref.dtype)
    wh = jnp.tanh(xw)
    kd_sum = jnp.zeros_like(k)
    for d in range(2):
        w = -_softplus(-(w0_ref[d:d + 1, :] + _mm(wh, wb_ref[d]))) - 0.5
        lw_ref[d, 0] = -jnp.exp(w)
        a = _sigmoid(a0_ref[d:d + 1, :] + _mm(xa, ab_ref[d]))
        kd = k * (1.0 + (a - 1.0) * ka_ref[...])
        kd_ref[d, 0] = kd.astype(kd_ref.dtype)
        be_ref[d, 0] = (kk * a).astype(be_ref.dtype)
        kd_sum = kd_sum + kd
    bonus_ref[0] = (_head_sum(r * kd_sum * rk_ref[...]) * v).astype(bonus_ref.dtype)


def _rw_prep(rw, rlo, taps, kkp, ka, rk, w0, wb, a0, ab, gb):
    b, l, _ = rw.shape
    tm = RW_TM
    nt = l // tm
    hb = tm // HALO
    tok = pl.BlockSpec((1, tm, D), lambda i, j: (i, j, 0))
    tokd = pl.BlockSpec((2, 1, tm, D), lambda i, j: (0, i, j, 0))

    def full(*shape):
        return pl.BlockSpec(shape, lambda i, j: (0,) * len(shape))

    def halo_specs(n):
        return [pl.BlockSpec((1, tm, n), lambda i, j: (i, j, 0)),
                pl.BlockSpec((1, HALO, n), lambda i, j: (i, jnp.maximum(j * hb - 1, 0), 0)),
                pl.BlockSpec((1, HALO, n), lambda i, j: (i, jnp.minimum((j + 1) * hb, l // HALO - 1), 0))]

    in_specs = halo_specs(3 * D) + halo_specs(N_RW_LORA) + [
        full(3, 3 * D), full(3, N_RW_LORA), full(1, D), full(1, D), full(1, D),
        full(2, D), full(2, RW_W_LORA, D), full(2, D), full(2, RW_A_LORA, D), full(RW_G_LORA, D),
    ]
    sd = jax.ShapeDtypeStruct
    tok_o, dir_o, dir_f32 = sd((b, l, D), BF16), sd((2, b, l, D), BF16), sd((2, b, l, D), F32)
    return pl.pallas_call(
        functools.partial(_rw_prep_kernel, n_tiles=nt),
        grid=(b, nt),
        in_specs=in_specs,
        out_specs=[tok, tok, tok, tokd, tokd, tokd, tok, tok],
        out_shape=[tok_o, tok_o, tok_o, dir_f32, dir_o, dir_o, tok_o, tok_o],
        scratch_shapes=[pltpu.VMEM((tm + 2 * HALO, 3 * D), BF16), pltpu.VMEM((tm + 2 * HALO, N_RW_LORA), F32)],
        compiler_params=_cparams(("arbitrary", "arbitrary")),
        name="rwkv_prep",
    )(rw, rw, rw, rlo, rlo, rlo, taps[:, :3 * D], taps[:, 3 * D:], kkp.reshape(1, D), ka.reshape(1, D),
      rk.reshape(1, D), w0, wb, a0, ab, gb)


def _tile_rows(x, n):
    return jnp.concatenate([x] * n, axis=0)


def _rw_scan_kernel(rf_ref, vf_ref, kkf_ref, lwf_ref, kdf_ref, bef_ref, rb_ref, vb_ref, kkb_ref, lwb_ref, kdb_ref,
                    beb_ref, s0_ref, yf_ref, yb_ref, sfin_ref, zt_ref, sv_lhs, sv_w, sv_bk, sv_tot, *, n_tiles):
    c, hw, sw = RW_C, RW_GW, RW_SW
    pk = RW_G * c
    n_groups, n_sub = RW_H // RW_G, RW_GW // RW_SW
    n_chunks = rf_ref.shape[1] // c
    j = pl.program_id(1)

    @pl.when(j == 0)
    def _():
        zt_ref[...] = jnp.zeros_like(zt_ref)
        for bb in range(RW_BB):
            for d in range(2):
                for g in range(RW_H // 2):
                    for jj in range(2):
                        blk = slice(RW_N * jj, RW_N * (jj + 1))
                        zt_ref[bb, d, g, blk, blk] = s0_ref[bb, d, 2 * g + jj]

    ins = ((rf_ref, vf_ref, kkf_ref, lwf_ref, kdf_ref, bef_ref), (rb_ref, vb_ref, kkb_ref, lwb_ref, kdb_ref, beb_ref))
    outs = (yf_ref, yb_ref)
    ri = lax.broadcasted_iota(jnp.int32, (c, pk), 0)
    cs = jnp.bitwise_and(lax.broadcasted_iota(jnp.int32, (c, pk), 1), c - 1)
    eye = (ri == cs).astype(F32)
    strict, incl, lvl, quad = [], [], [], []
    for d in range(2):
        ti = ri if d == 0 else c - 1 - ri
        ts = cs if d == 0 else c - 1 - cs
        strict.append(ts < ti)
        incl.append(ts <= ti)
        same4 = jnp.right_shift(ti, 2) == jnp.right_shift(ts, 2)
        below = jnp.bitwise_and(ti, 3) - jnp.bitwise_and(ts, 3)
        quad.append(tuple(same4 & (below == k) for k in (1, 2, 3)))
        lv_masks = []
        for lv in range(RW_LEVELS):
            same = jnp.right_shift(ti, lv + 1) == jnp.right_shift(ts, lv + 1)
            lower = ((jnp.bitwise_and(jnp.right_shift(ti, lv), 1) == 1)
                     & (jnp.bitwise_and(jnp.right_shift(ts, lv), 1) == 0))
            lv_masks.append(same & lower)
        lvl.append(lv_masks)
    bi = lax.broadcasted_iota(jnp.int32, (pk, pk), 0)
    bj = lax.broadcasted_iota(jnp.int32, (pk, pk), 1)
    bd_p = jnp.right_shift(bi, RW_LEVELS) == jnp.right_shift(bj, RW_LEVELS)
    zi = lax.broadcasted_iota(jnp.int32, (sw, sw), 0)
    zj = lax.broadcasted_iota(jnp.int32, (sw, sw), 1)
    bd_z = jnp.right_shift(zi, 6) == jnp.right_shift(zj, 6)
    term_row = lax.broadcasted_iota(jnp.int32, (2 * SUBLANES, sw), 0)
    wi = lax.broadcasted_iota(jnp.int32, (pk, hw), 0)
    wj = lax.broadcasted_iota(jnp.int32, (pk, hw), 1)
    bd_w = jnp.right_shift(wi, RW_LEVELS) == jnp.right_shift(wj, 6)

    def bdp(x):
        return jnp.where(bd_p, _tile_rows(x, RW_G), 0.0)

    def bdw(x):
        return jnp.where(bd_w, _tile_rows(x, RW_G), 0.0)

    chains = [(bb, d, half) for bb in range(RW_BB) for d in range(2) for half in range(n_groups)]

    def each(fn):
        return {ch: fn(ch) for ch in chains}

    def chunk_rows(i):
        return (pl.ds(pl.multiple_of(i * c, c), c), pl.ds(pl.multiple_of((n_chunks - 1 - i) * c, c), c))

    def scale(i, slot):
        rows = chunk_rows(i)

        def load(ch):
            bb, d, half = ch
            ln = slice(half * hw, (half + 1) * hw)
            refs = ins[d]
            return (refs[0][bb, rows[d], ln].astype(F32), refs[2][bb, rows[d], ln].astype(F32),
                    refs[3][0, bb, rows[d], ln], refs[4][0, bb, rows[d], ln].astype(F32),
                    refs[5][0, bb, rows[d], ln].astype(F32))

        x = each(load)
        r, kk, lw = each(lambda ch: x[ch][0]), each(lambda ch: x[ch][1]), each(lambda ch: x[ch][2])
        kd, be = each(lambda ch: x[ch][3]), each(lambda ch: x[ch][4])
        b = each(lambda ch: _scan_rows(lw[ch], ch[1] == 1))
        tot = each(lambda ch: b[ch][0:1] if ch[1] == 1 else b[ch][c - 1:c])
        en = each(lambda ch: jnp.exp(-b[ch]))
        es = each(lambda ch: jnp.exp(tot[ch] - b[ch]))
        for n, ch in enumerate(chains):
            sv_lhs[slot, n] = jnp.concatenate([kk[ch] * jnp.exp(b[ch] - lw[ch]), r[ch] * jnp.exp(b[ch])],
                                              axis=0).astype(BF16)
            sv_w[slot, n] = jnp.concatenate([be[ch] * en[ch], kd[ch] * en[ch]], axis=0).astype(BF16)
            sv_bk[slot, n] = jnp.concatenate([be[ch] * es[ch], kd[ch] * es[ch]], axis=0).astype(BF16)
            sv_tot[slot, n] = jnp.broadcast_to(tot[ch], (SUBLANES, hw))

    scale(0, 0)

    def chunk(i, carry):
        rows = chunk_rows(i)
        slot = lax.rem(i, 2)
        index = {ch: n for n, ch in enumerate(chains)}
        lhs = each(lambda ch: sv_lhs[slot, index[ch]])
        w_en = each(lambda ch: sv_w[slot, index[ch]])
        bk = each(lambda ch: sv_bk[slot, index[ch]])
        tot = each(lambda ch: sv_tot[slot, index[ch]][0:1])
        v = each(lambda ch: ins[ch[1]][1][ch[0], rows[ch[1]], ch[2] * hw:(ch[2] + 1) * hw].astype(F32))
        gb = each(lambda ch: _mm_nt(lhs[ch], bdw(w_en[ch][0:c])))
        gk = each(lambda ch: _mm_nt(lhs[ch], bdw(w_en[ch][c:])))
        a_ab = each(lambda ch: jnp.where(strict[ch[1]], gb[ch][0:c], 0.0))
        a_rb = each(lambda ch: jnp.where(incl[ch[1]], gb[ch][c:], 0.0))
        a_ak = each(lambda ch: jnp.where(strict[ch[1]], gk[ch][0:c], 0.0))
        a_rk = each(lambda ch: jnp.where(incl[ch[1]], gk[ch][c:], 0.0))
        def inverse4(ch):
            d = ch[1]
            q1, q2, q3 = quad[d]
            a = jnp.where(q1 | q2 | q3, a_ab[ch], 0.0)
            col = (lambda x, k: pltpu.roll(x, pk - k, 1)) if d == 0 else (lambda x, k: pltpu.roll(x, k, 1))
            row = (lambda x, k: pltpu.roll(x, k, 0)) if d == 0 else (lambda x, k: pltpu.roll(x, c - k, 0))
            c1, c2, r1, r2 = col(a, 1), col(a, 2), row(a, 1), row(a, 2)
            two = jnp.where(q2, c1 * r1, 0.0)
            three = jnp.where(q3, c1 * r2 + c2 * r1 - c2 * row(c1, 1) * r2, 0.0)
            return eye - a + two + three

        t_inv = each(inverse4)
        for lv in range(2, RW_LEVELS):
            xm = each(lambda ch: _mm(jnp.where(lvl[ch[1]][lv], a_ab[ch], 0.0), bdp(t_inv[ch])))
            t_inv = each(lambda ch: t_inv[ch] - _mm(t_inv[ch], bdp(xm[ch])))
            if lv == 2:
                scale(jnp.minimum(i + 1, n_chunks - 1), 1 - slot)
        zt_old = {(ch, g): zt_ref[ch[0], ch[1], n_sub * ch[2] + g] for ch in chains for g in range(n_sub)}
        zz = each(lambda ch: jnp.concatenate(
            [_mm(lhs[ch][:, g * sw:(g + 1) * sw], zt_old[ch, g]) for g in range(n_sub)], axis=1))
        av = each(lambda ch: _mm(jnp.concatenate([a_ak[ch], a_rk[ch]], axis=0), bdw(v[ch])))
        u = each(lambda ch: _mm(t_inv[ch], bdw(-(zz[ch][0:c] + av[ch][0:c]))))
        y = each(lambda ch: zz[ch][c:] + _mm(a_rb[ch], bdw(u[ch])) + av[ch][c:])
        uv = each(lambda ch: jnp.concatenate([u[ch], v[ch]], axis=0))

        def decay_cols(ch, gl):
            x = tot[ch][:, gl]
            hi = x.astype(BF16).astype(F32)
            mid = (x - hi).astype(BF16).astype(F32)
            lo = x - hi - mid
            terms = jnp.where(term_row == 0, hi, jnp.where(term_row == 1, mid, jnp.where(term_row == 2, lo, 0.0)))
            return jnp.exp(_mm_tn(terms, jnp.ones((2 * SUBLANES, sw), BF16)))

        zt_new = {}
        for ch in chains:
            for g in range(n_sub):
                gl = slice(g * sw, (g + 1) * sw)
                upd = jnp.where(bd_z, _mm_tn(bk[ch][:, gl], uv[ch][:, gl]), 0.0)
                zt_new[ch, g] = zt_old[ch, g] * decay_cols(ch, gl) + upd
        for ch in chains:
            bb, d, half = ch
            outs[d][bb, rows[d], half * hw:(half + 1) * hw] = y[ch].astype(outs[d].dtype)
            for g in range(n_sub):
                zt_ref[bb, d, n_sub * half + g] = zt_new[ch, g]
        return carry

    lax.fori_loop(0, n_chunks, chunk, 0)

    @pl.when(j == n_tiles - 1)
    def _():
        for bb in range(RW_BB):
            for d in range(2):
                for g in range(RW_H // 2):
                    for jj in range(2):
                        blk = slice(RW_N * jj, RW_N * (jj + 1))
                        sfin_ref[bb, d, 2 * g + jj] = zt_ref[bb, d, g, blk, blk]


def _rw_scan(r, v, kk, lw, kd, be, s0):
    b, l, _ = r.shape
    tl, nb = min(RW_TL, l), RW_BB
    nt = l // tl
    n_ch = nb * 2 * (RW_H // RW_G)
    tok_f = pl.BlockSpec((nb, tl, D), lambda i, j: (i, j, 0))
    tok_b = pl.BlockSpec((nb, tl, D), lambda i, j: (i, nt - 1 - j, 0))
    dir_f = pl.BlockSpec((1, nb, tl, D), lambda i, j: (0, i, j, 0))
    dir_b = pl.BlockSpec((1, nb, tl, D), lambda i, j: (1, i, nt - 1 - j, 0))
    st = pl.BlockSpec((nb, 2, RW_H, RW_N, RW_N), lambda i, j: (i, 0, 0, 0, 0))
    sd = jax.ShapeDtypeStruct
    return pl.pallas_call(
        functools.partial(_rw_scan_kernel, n_tiles=nt),
        grid=(b // nb, nt),
        in_specs=[tok_f, tok_f, tok_f, dir_f, dir_f, dir_f, tok_b, tok_b, tok_b, dir_b, dir_b, dir_b, st],
        out_specs=[tok_f, tok_b, st],
        out_shape=[sd((b, l, D), BF16), sd((b, l, D), BF16), sd((b, 2, RW_H, RW_N, RW_N), F32)],
        scratch_shapes=[pltpu.VMEM((nb, 2, RW_H // 2, RW_SW, RW_SW), F32),
                        pltpu.VMEM((2, n_ch, 2 * RW_C, RW_GW), BF16),
                        pltpu.VMEM((2, n_ch, 2 * RW_C, RW_GW), BF16),
                        pltpu.VMEM((2, n_ch, 2 * RW_C, RW_GW), BF16),
                        pltpu.VMEM((2, n_ch, SUBLANES, RW_GW), F32)],
        compiler_params=_cparams(("arbitrary", "arbitrary")),
        name="rwkv_scan",
    )(r, v, kk, lw, kd, be, r, v, kk, lw, kd, be, s0)


MERGE_TM = 256
ROUTER_LANES = LANES


def _merge_kernel(orf_ref, orb_ref, rg_ref, ogf_ref, ogb_ref, gg_ref, yf_ref, yb_ref, bonus_ref, grw_ref,
                  gates_ref, x_ref, gt_ref, sh_ref, sc_ref, wbr_ref, wout_ref, rgn_ref, ggn_ref, gng_ref, gnb_ref,
                  l1g_ref, l1b_ref, wr_ref, br_ref, x1_ref, u2_ref, lg_ref):
    f32 = lambda ref: ref[0].astype(F32)
    o_ret = f32(orf_ref) + f32(orb_ref)
    o_gla = f32(ogf_ref) + f32(ogb_ref)
    y_ret, y_gla = [], []
    for h in range(RET_H):
        hs = slice(h * RET_DV, (h + 1) * RET_DV)
        y_ret.append(_ln(o_ret[:, hs]))
        og = o_gla[:, hs]
        y_gla.append(og * lax.rsqrt(jnp.mean(og * og, axis=-1, keepdims=True) + EPS))
    z_ret = _silu(f32(rg_ref)) * (jnp.concatenate(y_ret, axis=1) * rgn_ref[...])
    z_gla = _silu(f32(gg_ref)) * (jnp.concatenate(y_gla, axis=1) * ggn_ref[...])
    y = f32(yf_ref) + f32(yb_ref)
    mu = _head_sum(y) * (1.0 / RW_N)
    yc = y - mu
    var = _head_sum(yc * yc) * (1.0 / RW_N)
    z_rw = (yc * lax.rsqrt(var + EPS) * gng_ref[...] + gnb_ref[...] + f32(bonus_ref)) * f32(grw_ref)
    gates = f32(gates_ref)
    mixed = (gates[:, 0:D] * _mm(z_ret, wbr_ref[0]) + gates[:, D:2 * D] * _mm(z_gla, wbr_ref[1])
             + gates[:, 2 * D:] * _mm(z_rw, wbr_ref[2]))
    mix = _mm(mixed, wout_ref[...])
    x1 = _ln(ALPHA * x_ref[0] + gt_ref[0] * mix) * l1g_ref[...] + l1b_ref[...]
    x1_ref[0] = x1
    u2 = _ln(x1) * (1.0 + sc_ref[0]) + sh_ref[0]
    u2_ref[0] = u2.astype(u2_ref.dtype)
    lg_ref[...] = (_mm_split(u2, wr_ref[0], wr_ref[1]) + br_ref[...]).T


def _merge(orf, orb, pr, ogf, ogb, pg, yf, yb, bonus, grw, gates, x, gt1, sh2, sc2, wbr, wout, rgn, ggn, gng, gnb,
           l1g, l1b, wr, br):
    b, l, _ = x.shape
    tm = MERGE_TM
    tok = pl.BlockSpec((1, tm, D), lambda i, j: (i, j, 0))
    gate_blk = pl.BlockSpec((1, tm, D), lambda i, j: (i, j, 2))
    vec = pl.BlockSpec((1, 1, D), lambda i, j: (i, 0, 0))
    row = pl.BlockSpec((1, D), lambda i, j: (0, 0))
    in_specs = [
        tok, tok, gate_blk, tok, tok, gate_blk, tok, tok, tok, tok,
        pl.BlockSpec((1, tm, 3 * D), lambda i, j: (i, j, 0)),
        tok, vec, vec, vec,
        pl.BlockSpec((3, D, D), lambda i, j: (0, 0, 0)),
        pl.BlockSpec((D, D), lambda i, j: (0, 0)),
        row, row, row, row, row, row,
        pl.BlockSpec((2, D, ROUTER_LANES), lambda i, j: (0, 0, 0)),
        pl.BlockSpec((1, ROUTER_LANES), lambda i, j: (0, 0)),
    ]
    sd = jax.ShapeDtypeStruct
    r1 = lambda a: a.reshape(1, D)
    return pl.pallas_call(
        _merge_kernel,
        grid=(b, l // tm),
        in_specs=in_specs,
        out_specs=[tok, tok, pl.BlockSpec((ROUTER_LANES, tm), lambda i, j: (0, i * (l // tm) + j))],
        out_shape=[sd((b, l, D), F32), sd((b, l, D), BF16), sd((ROUTER_LANES, b * l), F32)],
        compiler_params=_cparams(("arbitrary", "arbitrary")),
        name="merge",
    )(orf, orb, pr, ogf, ogb, pg, yf, yb, bonus, grw, gates, x, gt1, sh2, sc2, wbr, wout,
      r1(rgn), r1(ggn), r1(gng), r1(gnb), r1(l1g), r1(l1b), wr, br)


MOE_TM = 1024


def _moe_block_rows(tm):
    rows = tm / N_GROUPS + 4.0 * np.sqrt(tm * (N_GROUPS - 1.0)) / N_GROUPS
    bf16_tile = 2 * SUBLANES
    return int(min(tm, -(-rows // bf16_tile) * bf16_tile))


def _first_argmax_rows(x, row):
    m = jnp.max(x, axis=0, keepdims=True)
    idx = jnp.min(jnp.where(x == m, row, x.shape[0]), axis=0, keepdims=True)
    return m, idx


def _routing(lt):
    tm = lt.shape[1]
    gl = lt[0:N_GROUPS]
    gmax, gidx = _first_argmax_rows(gl, lax.broadcasted_iota(jnp.int32, (N_GROUPS, tm), 0))
    g_w = 1.0 / jnp.sum(jnp.exp(gl - gmax), axis=0, keepdims=True)
    row = lax.broadcasted_iota(jnp.int32, (N_EXPERTS, tm), 0)
    neg = -jnp.inf
    el = jnp.where(jnp.right_shift(row, 2) == gidx, lt[N_GROUPS:N_GROUPS + N_EXPERTS], neg)
    m1, i1 = _first_argmax_rows(el, row)
    m2, i2 = _first_argmax_rows(jnp.where(row == i1, neg, el), row)
    e2 = jnp.exp(m2 - m1)
    w1 = 1.0 / (1.0 + e2)
    return gidx, g_w * (jnp.where(row == i1, w1, 0.0) + jnp.where(row == i2, e2 * w1, 0.0))


def _moe_kernel(u_ref, lt_ref, wg_ref, wu_ref, wd_ref, o_ref, acc_ref):
    g = pl.program_id(1)
    tm = acc_ref.shape[0]
    blk = _moe_block_rows(tm)

    @pl.when(g == 0)
    def _():
        acc_ref[...] = jnp.zeros_like(acc_ref)

    gidx, comb = _routing(lt_ref[...])
    member = jnp.broadcast_to((gidx == g).astype(F32), (SUBLANES, tm))
    lane = lax.broadcasted_iota(jnp.int32, (SUBLANES, tm), 1)
    count = member
    sh = 1
    while sh < tm:
        count = count + jnp.where(lane >= sh, pltpu.roll(count, sh, 1), 0.0)
        sh *= 2
    pos = jnp.where(member > 0.0, count - 1.0, -1.0).astype(jnp.int32)[0:1]
    n_tok = jnp.max(count).astype(jnp.int32)
    comb_pad = jnp.concatenate([comb, jnp.zeros((LANES - N_EXPERTS, tm), F32)], axis=0)
    comb_parts = _split2(comb_pad)
    u = u_ref[...]
    sel_lane = lax.broadcasted_iota(jnp.int32, (blk, LANES), 1)

    def body(i, carry):
        slot = lax.broadcasted_iota(jnp.int32, (blk, tm), 0) + i * blk
        onehot = jnp.where(slot == pos, 1.0, 0.0).astype(BF16)
        xg = jnp.dot(onehot, u, preferred_element_type=F32).astype(BF16)
        cw = sum(lax.dot_general(onehot, c, (((1,), (1,)), ((), ())), preferred_element_type=F32)
                 for c in comb_parts)
        y = jnp.zeros((blk, D), F32)
        for e in range(EPG):
            hid = (_silu(jnp.dot(xg, wg_ref[e], preferred_element_type=F32))
                   * jnp.dot(xg, wu_ref[e], preferred_element_type=F32))
            c_e = jnp.sum(jnp.where(sel_lane == g * EPG + e, cw, 0.0), axis=-1, keepdims=True)
            y = y + c_e * _mm(hid, wd_ref[e])
        acc_ref[...] += _mm_tn(onehot, y)
        return carry

    lax.fori_loop(0, (n_tok + blk - 1) // blk, body, 0)

    @pl.when(g == N_GROUPS - 1)
    def _():
        o_ref[...] = acc_ref[...].astype(o_ref.dtype)


def _moe(u2, logits_t, wg, wu, wd):
    b, l, _ = u2.shape
    t = b * l
    tm = min(MOE_TM, t)
    tok = pl.BlockSpec((tm, D), lambda i, g: (i, 0))
    in_specs = [
        tok,
        pl.BlockSpec((ROUTER_LANES, tm), lambda i, g: (0, i)),
        pl.BlockSpec((EPG, D, EXPERT_FF), lambda i, g: (g, 0, 0)),
        pl.BlockSpec((EPG, D, EXPERT_FF), lambda i, g: (g, 0, 0)),
        pl.BlockSpec((EPG, EXPERT_FF, D), lambda i, g: (g, 0, 0)),
    ]
    return pl.pallas_call(
        _moe_kernel,
        grid=(t // tm, N_GROUPS),
        in_specs=in_specs,
        out_specs=tok,
        out_shape=jax.ShapeDtypeStruct((t, D), BF16),
        scratch_shapes=[pltpu.VMEM((tm, D), F32)],
        compiler_params=_cparams(("arbitrary",) * 2),
        name="moe",
    )(u2.reshape(t, D), logits_t, wg, wu, wd).reshape(b, l, D)


def _post_kernel(x_ref, m_ref, gt_ref, l2g_ref, l2b_ref, sh_ref, sc_ref, o_ref, u_ref):
    x2 = _ln(ALPHA * x_ref[0] + gt_ref[0] * m_ref[0].astype(F32)) * l2g_ref[...] + l2b_ref[...]
    o_ref[0] = x2
    u_ref[0] = (_ln(x2) * (1.0 + sc_ref[0]) + sh_ref[0]).astype(u_ref.dtype)


def _post(x1, moe_out, gt2, l2g, l2b, sh_next, sc_next):
    b, l, _ = x1.shape
    tm = 512 if l % 512 == 0 else 256
    tok = pl.BlockSpec((1, tm, D), lambda i, j: (i, j, 0))
    vec = pl.BlockSpec((1, 1, D), lambda i, j: (i, 0, 0))
    row = pl.BlockSpec((1, D), lambda i, j: (0, 0))
    sd = jax.ShapeDtypeStruct
    return pl.pallas_call(
        _post_kernel,
        grid=(b, l // tm),
        in_specs=[tok, tok, vec, row, row, vec, vec],
        out_specs=[tok, tok],
        out_shape=[sd((b, l, D), F32), sd((b, l, D), BF16)],
        compiler_params=_cparams(("arbitrary", "arbitrary")),
        name="post_ln",
    )(x1, moe_out, gt2, l2g.reshape(1, D), l2b.reshape(1, D), sh_next, sc_next)


def _rope_tables(l):
    t = np.arange(l)
    quarter = RET_DK // 4
    freqs = (np.float32(ROPE_BASE) ** (-np.arange(quarter, dtype=np.float32) / quarter)).astype(np.float32)
    rows = (t // GRID_W).astype(np.float32)
    cols = (t % GRID_W).astype(np.float32)
    ang = jnp.asarray(np.concatenate([rows[:, None] * freqs, cols[:, None] * freqs], -1))
    cos, sin = jnp.cos(ang), jnp.sin(ang)
    return jnp.concatenate([cos, cos], -1), jnp.concatenate([-sin, sin], -1)


def _layer_weights(p):
    n_rg = 2 * (RET_H * RET_DK + RET_H * RET_DV)
    w_in = p["w_in"]
    w_lr = jnp.pad(w_in[:, 2 * n_rg:2 * n_rg + GLA_LOWRANK], ((0, 0), (0, LANES - GLA_LOWRANK)))
    w2p, b2p = _gla_gate_params(p["gla_w2"], p["gla_b"])
    pad = ROUTER_LANES - N_GROUPS - N_EXPERTS
    return dict(
        p,
        w_ret=w_in[:, :n_rg].astype(BF16),
        w_gla=w_in[:, n_rg:2 * n_rg].astype(BF16),
        w_glr=w_lr.astype(BF16),
        w_rw=w_in[:, 2 * n_rg + GLA_LOWRANK:2 * n_rg + GLA_LOWRANK + 3 * D].astype(BF16),
        w_rlo=w_in[:, 2 * n_rg + GLA_LOWRANK + 3 * D:].astype(BF16),
        w_merge=p["w_merge"].astype(BF16),
        w_br=p["w_br"].astype(BF16),
        w_out=p["w_out"].astype(BF16),
        gla_w2p=w2p, gla_b2p=b2p,
        w_router=jnp.stack(_split2(jnp.pad(jnp.concatenate([p["w_rg"], p["w_re"]], axis=1), ((0, 0), (0, pad))))),
        b_router=jnp.pad(jnp.concatenate([p["b_rg"], p["b_re"]]), (0, pad)).reshape(1, ROUTER_LANES),
        w_eg=p["w_eg"].astype(BF16), w_eu=p["w_eu"].astype(BF16), w_ed=p["w_ed"].astype(BF16),
    )


def _split_mod(mod):
    return [m.reshape(mod.shape[0], 1, D) for m in jnp.split(mod, 6, axis=-1)]


def _layer(x, u, mod, mod_next, s_ret, s_gla_t, s_rw, rope, p):
    b, l, _ = x.shape
    _, _, gt1, sh2, sc2, gt2 = _split_mod(mod)
    sh_next, sc_next = _split_mod(mod_next)[:2]
    u2d = u.reshape(b * l, D)

    def proj(w, tn, bias=None, **kw):
        bias = jnp.zeros((w.shape[1],), F32) if bias is None else bias
        return _proj(u2d, w, bias, tn, **kw).reshape(b, l, -1)

    pr = proj(p["w_ret"], 1024)
    pg = proj(p["w_gla"], 1024)
    plr = proj(p["w_glr"], LANES, out_dtype=F32)
    rw = proj(p["w_rw"], 1024)
    rlo = proj(p["w_rlo"], N_RW_LORA, out_dtype=F32)
    gates = proj(p["w_merge"], 1024, bias=p["b_merge"], act="sigmoid")

    cos2, sin2 = rope if rope is not None else (jnp.ones((l, RET_DK), F32), jnp.zeros((l, RET_DK), F32))
    orf, orb, ret_fin = _retention(pr, cos2, sin2, s_ret, rope is not None)
    ogf, ogb, gla_fin_t = _gla(pg, plr, p["gla_w2p"], p["gla_b2p"], s_gla_t)
    r, v, kk, lw, kd, be, bonus, grw = _rw_prep(rw, rlo, p["rwkv_shift"], p["rwkv_kk"], p["rwkv_ka"],
                                                p["rwkv_rk"].reshape(-1), p["rwkv_w0"], p["rwkv_wb"],
                                                p["rwkv_a0"], p["rwkv_ab"], p["rwkv_gb"])
    yf, yb, rw_fin = _rw_scan(r, v, kk, lw, kd, be, s_rw)
    x1, u2, logits_t = _merge(orf, orb, pr, ogf, ogb, pg, yf, yb, bonus, grw, gates, x, gt1, sh2, sc2,
                              p["w_br"], p["w_out"], p["ret_gn"], p["gla_gn"], p["rwkv_gn_g"], p["rwkv_gn_b"],
                              p["ln1_g"], p["ln1_b"], p["w_router"], p["b_router"])
    moe_out = _moe(u2, logits_t, p["w_eg"], p["w_eu"], p["w_ed"])
    x2, u_next = _post(x1, moe_out, gt2, p["ln2_g"], p["ln2_b"], sh_next, sc_next)
    return x2, u_next, (ret_fin, gla_fin_t, rw_fin)


_PARAM_NAMES = ("w_in", "rwkv_shift", "ret_gn", "gla_w2", "gla_b", "gla_gn", "rwkv_w0", "rwkv_wb", "rwkv_a0",
                "rwkv_ab", "rwkv_gb", "rwkv_kk", "rwkv_ka", "rwkv_rk", "rwkv_gn_g", "rwkv_gn_b", "w_br", "w_merge",
                "b_merge", "w_out", "ln1_g", "ln1_b", "ln2_g", "ln2_b", "w_rg", "b_rg", "w_re", "b_re", "w_eg",
                "w_eu", "w_ed")


def kernel(x_prompt, x_sample, state_ret, state_gla, state_rwkv, c, c_ctx, w_ada, b_ada, w_in, rwkv_shift, ret_gn,
           gla_w2, gla_b, gla_gn, rwkv_w0, rwkv_wb, rwkv_a0, rwkv_ab, rwkv_gb, rwkv_kk, rwkv_ka, rwkv_rk, rwkv_gn_g,
           rwkv_gn_b, w_br, w_merge, b_merge, w_out, ln1_g, ln1_b, ln2_g, ln2_b, w_rg, b_rg, w_re, b_re, w_eg, w_eu,
           w_ed):
    params = dict(zip(_PARAM_NAMES, (w_in, rwkv_shift, ret_gn, gla_w2, gla_b, gla_gn, rwkv_w0, rwkv_wb, rwkv_a0,
                                     rwkv_ab, rwkv_gb, rwkv_kk, rwkv_ka, rwkv_rk, rwkv_gn_g, rwkv_gn_b, w_br, w_merge,
                                     b_merge, w_out, ln1_g, ln1_b, ln2_g, ln2_b, w_rg, b_rg, w_re, b_re, w_eg, w_eu,
                                     w_ed)))
    bc, bl = x_prompt.shape[0], x_sample.shape[0]
    rope = _rope_tables(x_sample.shape[1])
    mod_rows = 2 * SUBLANES
    c_all = jnp.concatenate([c, c_ctx[None, :], jnp.zeros((mod_rows - bl - 1, D), F32)], axis=0)
    z_ret = jnp.zeros((bc, 2, RET_H, RET_DK, RET_DV), F32)
    z_gla_t = jnp.zeros((bc, 2, GLA_H, GLA_DV, GLA_DK), F32)
    z_rw = jnp.zeros((bc, 2, RW_H, RW_N, RW_N), F32)
    mods = [_modulation(c_all, w_ada[layer], b_ada[layer]) for layer in range(DEPTH)]
    mods.append(jnp.zeros_like(mods[0]))
    mods_lat = [m[:bl] for m in mods]
    mods_ctx = [jnp.broadcast_to(m[bl:bl + 1], (bc, 6 * D)) for m in mods]
    h_ctx, h_lat = x_prompt, x_sample
    sh, sc = _split_mod(mods_ctx[0])[:2]
    u_ctx = _lnmod(h_ctx, sh, sc)
    sh, sc = _split_mod(mods_lat[0])[:2]
    u_lat = _lnmod(h_lat, sh, sc)
    new_ret, new_gla, new_rw = [], [], []
    for layer in range(DEPTH):
        p = _layer_weights({k: v[layer] for k, v in params.items()})
        h_ctx, u_ctx, (s_ret, s_gla_t, s_rw) = _layer(h_ctx, u_ctx, mods_ctx[layer], mods_ctx[layer + 1],
                                                      z_ret, z_gla_t, z_rw, None, p)
        new_ret.append(s_ret)
        new_gla.append(jnp.swapaxes(s_gla_t, -1, -2))
        new_rw.append(jnp.swapaxes(s_rw, -1, -2))
        h_lat, u_lat, _ = _layer(h_lat, u_lat, mods_lat[layer], mods_lat[layer + 1], state_ret[:, layer],
                                 jnp.swapaxes(state_gla[:, layer], -1, -2),
                                 jnp.swapaxes(state_rwkv[:, layer], -1, -2), rope, p)
    return (h_ctx, h_lat, jnp.stack(new_ret, axis=1), jnp.stack(new_gla, axis=1), jnp.stack(new_rw, axis=1))
```

```python
import functools

import jax
import jax.numpy as jnp
import numpy as np
from jax import lax
from jax.experimental import pallas as pl
from jax.experimental.pallas import tpu as pltpu

F32 = jnp.float32
BF16 = jnp.bfloat16
HI = lax.Precision.HIGHEST

D = 1024
DEPTH = 2
GRID_W = 64
RET_H, RET_DK, RET_DV = 4, 128, 256
GLA_H, GLA_DK, GLA_DV = 4, 128, 256
GLA_LOWRANK = 16
GLA_GATE_NORM = 16.0
RW_H, RW_N = 16, 64
RW_W_LORA, RW_A_LORA, RW_G_LORA = 64, 64, 128
N_GROUPS, EPG, N_EXPERTS, EXPERT_FF = 4, 4, 16, 512
ALPHA = (2 * DEPTH) ** 0.25
EPS = 1e-5
ROPE_BASE = 10000.0

LANES = 128
SUBLANES = 8
VMEM_LIMIT = 56 * 1024 * 1024

RET_C = 128
GLA_C = 64
GLA_LEVELS = GLA_C.bit_length() - 1
RW_C = 32
RW_LEVELS = RW_C.bit_length() - 1
RW_N_BITS = RW_N.bit_length() - 1
EPG_BITS = EPG.bit_length() - 1
RW_TL = 256
RW_BB = 2
RW_G = 4
RW_GW = RW_G * RW_N
RW_SW = 2 * RW_N


def _cparams(sem):
    return pltpu.CompilerParams(dimension_semantics=sem, vmem_limit_bytes=VMEM_LIMIT)


def _mm(a, b):
    return jnp.dot(a.astype(BF16), b.astype(BF16), preferred_element_type=F32)


def _mm_nt(a, b):
    return lax.dot_general(a.astype(BF16), b.astype(BF16), (((1,), (1,)), ((), ())), preferred_element_type=F32)


def _mm_tn(a, b):
    return lax.dot_general(a.astype(BF16), b.astype(BF16), (((0,), (0,)), ((), ())), preferred_element_type=F32)


def _mm_hi(a, b):
    return jnp.dot(a, b, preferred_element_type=F32, precision=HI)


def _split2(x):
    hi = x.astype(BF16)
    return hi, (x - hi.astype(F32)).astype(BF16)


def _mm_split(a, b_hi, b_lo):
    a_hi, a_lo = _split2(a)
    dot = functools.partial(jnp.dot, preferred_element_type=F32)
    return dot(a_hi, b_hi) + (dot(a_hi, b_lo) + dot(a_lo, b_hi))


def _sigmoid(x):
    return 1.0 / (1.0 + jnp.exp(-x))


def _silu(x):
    return x * _sigmoid(x)


def _log_sigmoid(x):
    return jnp.minimum(x, 0.0) - jnp.log(1.0 + jnp.exp(-jnp.abs(x)))


def _softplus(x):
    return jnp.maximum(x, 0.0) + jnp.log(1.0 + jnp.exp(-jnp.abs(x)))


def _ln(x):
    mu = jnp.mean(x, axis=-1, keepdims=True)
    xc = x - mu
    var = jnp.mean(xc * xc, axis=-1, keepdims=True)
    return xc * lax.rsqrt(var + EPS)


def _scan_rows(x, reverse):
    n = x.shape[0]
    row = lax.broadcasted_iota(jnp.int32, x.shape, 0)
    sh = 1
    while sh < n:
        if reverse:
            x = x + jnp.where(row < n - sh, pltpu.roll(x, n - sh, 0), 0.0)
        else:
            x = x + jnp.where(row >= sh, pltpu.roll(x, sh, 0), 0.0)
        sh *= 2
    return x


def _mod_kernel(c_ref, w_ref, b_ref, o_ref):
    o_ref[...] = _mm_hi(_silu(c_ref[...]), w_ref[...]) + b_ref[...]


def _modulation(c_all, w, b):
    m, n, tn = c_all.shape[0], w.shape[1], 512
    return pl.pallas_call(
        _mod_kernel,
        grid=(n // tn,),
        in_specs=[pl.BlockSpec((m, D), lambda j: (0, 0)),
                  pl.BlockSpec((D, tn), lambda j: (0, j)),
                  pl.BlockSpec((1, tn), lambda j: (0, j))],
        out_specs=pl.BlockSpec((m, tn), lambda j: (0, j)),
        out_shape=jax.ShapeDtypeStruct((m, n), F32),
        compiler_params=_cparams(("arbitrary",)),
        name="adaln_mod",
    )(c_all, w, b.reshape(1, n))


def _lnmod_kernel(x_ref, sh_ref, sc_ref, u_ref):
    u_ref[0] = (_ln(x_ref[0]) * (1.0 + sc_ref[0]) + sh_ref[0]).astype(u_ref.dtype)


def _lnmod(x, sh, sc):
    b, l, _ = x.shape
    tm = 512 if l % 512 == 0 else 256
    vec = pl.BlockSpec((1, 1, D), lambda i, j: (i, 0, 0))
    return pl.pallas_call(
        _lnmod_kernel,
        grid=(b, l // tm),
        in_specs=[pl.BlockSpec((1, tm, D), lambda i, j: (i, j, 0)), vec, vec],
        out_specs=pl.BlockSpec((1, tm, D), lambda i, j: (i, j, 0)),
        out_shape=jax.ShapeDtypeStruct((b, l, D), BF16),
        compiler_params=_cparams(("arbitrary", "arbitrary")),
        name="ln_mod",
    )(x, sh, sc)


def _proj_kernel(u_ref, w_ref, b_ref, o_ref, *, act):
    y = jnp.dot(u_ref[...], w_ref[...], preferred_element_type=F32) + b_ref[...]
    if act == "sigmoid":
        y = _sigmoid(y)
    o_ref[...] = y.astype(o_ref.dtype)


def _proj(u2d, w, bias, tn, act=None, out_dtype=BF16):
    t, n = u2d.shape[0], w.shape[1]
    tm = min(t, 2048)
    return pl.pallas_call(
        functools.partial(_proj_kernel, act=act),
        grid=(t // tm, n // tn),
        in_specs=[pl.BlockSpec((tm, D), lambda i, j: (i, 0)),
                  pl.BlockSpec((D, tn), lambda i, j: (0, j)),
                  pl.BlockSpec((1, tn), lambda i, j: (0, j))],
        out_specs=pl.BlockSpec((tm, tn), lambda i, j: (i, j)),
        out_shape=jax.ShapeDtypeStruct((t, n), out_dtype),
        compiler_params=_cparams(("arbitrary", "arbitrary")),
        name="proj",
    )(u2d, w, bias.reshape(1, n))


def _ret_log_gamma(d, h):
    hh = h if d == 0 else RET_H - 1 - h
    return float(np.log(1.0 - 2.0 ** (-5.0 - hh)))


def _ret_tables():
    c = RET_C
    i = np.arange(c, dtype=np.float64)
    dec = np.zeros((2 * RET_H, c, c), np.float32)
    qd = np.zeros((2 * RET_H, c, RET_DK), np.float32)
    kd = np.zeros((2 * RET_H, c, RET_DK), np.float32)
    for d in range(2):
        tau = i if d == 0 else c - 1 - i
        rel = tau[:, None] - tau[None, :]
        for h in range(RET_H):
            lg = _ret_log_gamma(d, h)
            dec[d * RET_H + h] = np.where(rel >= 0, np.exp(np.maximum(rel, 0.0) * lg), 0.0)
            qd[d * RET_H + h] = np.exp((tau + 1.0) * lg)[:, None]
            kd[d * RET_H + h] = np.exp((c - 1.0 - tau) * lg)[:, None]
    return jnp.asarray(dec), jnp.asarray(qd), jnp.asarray(kd)


def _rope(x, cos2, sin2):
    return x * cos2 + pltpu.roll(x, RET_DK // 2, 1) * sin2


def _ret_kernel(qf_ref, kf_ref, vf_ref, qb_ref, kb_ref, vb_ref, cosf_ref, sinf_ref, cosb_ref, sinb_ref,
                dec_ref, qd_ref, kd_ref, s0_ref, of_ref, ob_ref, sfin_ref, s_ref, *, use_rope, n_chunks):
    n = pl.program_id(1)

    @pl.when(n == 0)
    def _():
        s_ref[...] = s0_ref[0]

    ins = ((qf_ref, kf_ref, vf_ref, cosf_ref, sinf_ref), (qb_ref, kb_ref, vb_ref, cosb_ref, sinb_ref))
    outs = (of_ref, ob_ref)
    chains = [(d, h) for d in range(2) for h in range(RET_H)]
    q, k, v = {}, {}, {}
    for d, h in chains:
        q_ref, k_ref, v_ref, cos_ref, sin_ref = ins[d]
        dk = slice(h * RET_DK, (h + 1) * RET_DK)
        q[d, h] = q_ref[0, :, dk].astype(F32)
        k[d, h] = k_ref[0, :, dk].astype(F32) * (RET_DK ** -0.5)
        if use_rope:
            q[d, h] = _rope(q[d, h], cos_ref[...], sin_ref[...])
            k[d, h] = _rope(k[d, h], cos_ref[...], sin_ref[...])
        v[d, h] = v_ref[0, :, h * RET_DV:(h + 1) * RET_DV]
    sc = {ch: _mm_nt(q[ch], k[ch]) * dec_ref[ch[0] * RET_H + ch[1]] for ch in chains}
    s_old = {ch: s_ref[ch[0], ch[1]] for ch in chains}
    o = {ch: _mm(sc[ch], v[ch]) + _mm(q[ch] * qd_ref[ch[0] * RET_H + ch[1]], s_old[ch]) for ch in chains}
    s_new = {ch: s_old[ch] * float(np.exp(RET_C * _ret_log_gamma(*ch)))
             + _mm_tn(k[ch] * kd_ref[ch[0] * RET_H + ch[1]], v[ch]) for ch in chains}
    for d, h in chains:
        outs[d][0, :, h * RET_DV:(h + 1) * RET_DV] = o[d, h].astype(outs[d].dtype)
        s_ref[d, h] = s_new[d, h]

    @pl.when(n == n_chunks - 1)
    def _():
        sfin_ref[0] = s_ref[...]


def _retention(pr, cos2, sin2, s0, use_rope):
    b, l, _ = pr.shape
    c = RET_C
    nc = l // c
    dec, qd, kd = _ret_tables()
    qw, vw = RET_H * RET_DK, RET_H * RET_DV

    def fw(blk):
        return lambda i, n: (i, n, blk)

    def bw(blk):
        return lambda i, n: (i, nc - 1 - n, blk)

    def const(*shape):
        return pl.BlockSpec(shape, lambda i, n: (0,) * len(shape))

    st = pl.BlockSpec((1, 2, RET_H, RET_DK, RET_DV), lambda i, n: (i, 0, 0, 0, 0))
    in_specs = [
        pl.BlockSpec((1, c, qw), fw(0)), pl.BlockSpec((1, c, qw), fw(1)), pl.BlockSpec((1, c, vw), fw(1)),
        pl.BlockSpec((1, c, qw), bw(0)), pl.BlockSpec((1, c, qw), bw(1)), pl.BlockSpec((1, c, vw), bw(1)),
        pl.BlockSpec((c, RET_DK), lambda i, n: (n, 0)), pl.BlockSpec((c, RET_DK), lambda i, n: (n, 0)),
        pl.BlockSpec((c, RET_DK), lambda i, n: (nc - 1 - n, 0)), pl.BlockSpec((c, RET_DK), lambda i, n: (nc - 1 - n, 0)),
        const(2 * RET_H, c, c), const(2 * RET_H, c, RET_DK), const(2 * RET_H, c, RET_DK), st,
    ]
    out_specs = [pl.BlockSpec((1, c, vw), fw(0)), pl.BlockSpec((1, c, vw), bw(0)), st]
    sd = jax.ShapeDtypeStruct
    return pl.pallas_call(
        functools.partial(_ret_kernel, use_rope=use_rope, n_chunks=nc),
        grid=(b, nc),
        in_specs=in_specs,
        out_specs=out_specs,
        out_shape=[sd((b, l, vw), BF16), sd((b, l, vw), BF16), sd((b, 2, RET_H, RET_DK, RET_DV), F32)],
        scratch_shapes=[pltpu.VMEM((2, RET_H, RET_DK, RET_DV), F32)],
        compiler_params=_cparams(("arbitrary", "arbitrary")),
        name="retention",
    )(pr, pr, pr, pr, pr, pr, cos2, sin2, cos2, sin2, dec, qd, kd, s0)


def _gla_boundary(b, lv, reverse, rolls):
    c, w = b.shape
    m = 1 << lv
    if m >= SUBLANES:
        parts = []
        for p0 in range(0, c, 2 * m):
            e = p0 + m if reverse else p0 + m - 1
            parts.append(jnp.broadcast_to(b[e:e + 1, :], (2 * m, w)))
        return jnp.concatenate(parts, axis=0)

    def rolled(s):
        s %= c
        if s not in rolls:
            rolls[s] = b if s == 0 else pltpu.roll(b, s, 0)
        return rolls[s]

    row = lax.broadcasted_iota(jnp.int32, (c, w), 0)
    r = jnp.bitwise_and(row, m - 1)
    upper = jnp.bitwise_and(jnp.right_shift(row, lv), 1) == 1
    out = b
    for t in range(m):
        if reverse:
            out = jnp.where(upper & (r == t), rolled(t), out)
            out = jnp.where(jnp.logical_not(upper) & (r == t), rolled(-(m - t)), out)
        else:
            out = jnp.where(upper & (r == t), rolled(t + 1), out)
            out = jnp.where(jnp.logical_not(upper) & (r == t), rolled(-(m - 1 - t)), out)
    return out


def _gla_kernel(qf_ref, kf_ref, vf_ref, lrf_ref, qb_ref, kb_ref, vb_ref, lrb_ref, w2_ref, b2_ref, s0_ref,
                of_ref, ob_ref, sfin_ref, st_ref, *, n_chunks):
    c = GLA_C
    n = pl.program_id(1)

    @pl.when(n == 0)
    def _():
        st_ref[...] = s0_ref[0]

    ins = ((qf_ref, kf_ref, vf_ref, lrf_ref), (qb_ref, kb_ref, vb_ref, lrb_ref))
    outs = (of_ref, ob_ref)
    hw = GLA_H * GLA_DK
    row = lax.broadcasted_iota(jnp.int32, (c, hw), 0)
    ri = lax.broadcasted_iota(jnp.int32, (c, c), 0)
    ci = lax.broadcasted_iota(jnp.int32, (c, c), 1)
    chains = [(d, h) for d in range(2) for h in range(GLA_H)]

    def dk(h):
        return slice(h * GLA_DK, (h + 1) * GLA_DK)

    def dv(h):
        return slice(h * GLA_DV, (h + 1) * GLA_DV)

    q, k, v, b_inc, b_rest, tot, ql = {}, {}, {}, {}, {}, {}, {}
    for d in range(2):
        reverse = d == 1
        q_ref, k_ref, v_ref, lr_ref = ins[d]
        q[d] = q_ref[0].astype(F32) * (GLA_DK ** -0.5)
        k[d] = k_ref[0].astype(F32)
        v[d] = v_ref[0]
        gate = _log_sigmoid(_mm_split(lr_ref[0], w2_ref[d, 0], w2_ref[d, 1]) + b2_ref[d])
        gate = gate * (1.0 / GLA_GATE_NORM)
        b = _scan_rows(gate, reverse)
        tot[d] = b[0:1] if reverse else b[c - 1:c]
        b_inc[d], b_rest[d] = b, tot[d] - b
        rolls = {}
        for lv in range(GLA_LEVELS):
            be = _gla_boundary(b, lv, reverse, rolls)
            upper = jnp.bitwise_and(jnp.right_shift(row, lv), 1) == 1
            second = jnp.logical_not(upper) if reverse else upper
            e = jnp.exp(-jnp.abs(b - be))
            ql[d, lv] = jnp.where(second, q[d], k[d]) * e

    attn = {(d, h): jnp.where(ri == ci, jnp.sum(q[d][:, dk(h)] * k[d][:, dk(h)], axis=-1, keepdims=True), 0.0)
            for d, h in chains}
    for lv in range(GLA_LEVELS):
        same = jnp.right_shift(ri, lv + 1) == jnp.right_shift(ci, lv + 1)
        row_upper = jnp.bitwise_and(jnp.right_shift(ri, lv), 1) == 1
        col_upper = jnp.bitwise_and(jnp.right_shift(ci, lv), 1) == 1
        pair_mask = (same & row_upper & jnp.logical_not(col_upper), same & jnp.logical_not(row_upper) & col_upper)
        for d, h in chains:
            p = ql[d, lv][:, dk(h)]
            attn[d, h] = attn[d, h] + jnp.where(pair_mask[d], _mm_nt(p, p), 0.0)
    st_old = {ch: st_ref[ch[0], ch[1]] for ch in chains}
    o = {(d, h): _mm(attn[d, h], v[d][:, dv(h)]) + _mm_nt(q[d][:, dk(h)] * jnp.exp(b_inc[d][:, dk(h)]), st_old[d, h])
         for d, h in chains}
    st_new = {(d, h): st_old[d, h] * jnp.exp(tot[d][:, dk(h)])
              + _mm_tn(v[d][:, dv(h)], k[d][:, dk(h)] * jnp.exp(b_rest[d][:, dk(h)])) for d, h in chains}
    for d, h in chains:
        outs[d][0, :, dv(h)] = o[d, h].astype(outs[d].dtype)
        st_ref[d, h] = st_new[d, h]

    @pl.when(n == n_chunks - 1)
    def _():
        sfin_ref[0] = st_ref[...]


def _gla(pg, plr, w2p, b2, s0t):
    b, l, _ = pg.shape
    c = GLA_C
    nc = l // c
    qw, vw = GLA_H * GLA_DK, GLA_H * GLA_DV

    def fw(blk):
        return lambda i, n: (i, n, blk)

    def bw(blk):
        return lambda i, n: (i, nc - 1 - n, blk)

    st = pl.BlockSpec((1, 2, GLA_H, GLA_DV, GLA_DK), lambda i, n: (i, 0, 0, 0, 0))
    in_specs = [
        pl.BlockSpec((1, c, qw), fw(0)), pl.BlockSpec((1, c, qw), fw(1)), pl.BlockSpec((1, c, vw), fw(1)),
        pl.BlockSpec((1, c, LANES), fw(0)),
        pl.BlockSpec((1, c, qw), bw(0)), pl.BlockSpec((1, c, qw), bw(1)), pl.BlockSpec((1, c, vw), bw(1)),
        pl.BlockSpec((1, c, LANES), bw(0)),
        pl.BlockSpec((2, 2, LANES, qw), lambda i, n: (0, 0, 0, 0)),
        pl.BlockSpec((2, 1, qw), lambda i, n: (0, 0, 0)),
        st,
    ]
    out_specs = [pl.BlockSpec((1, c, vw), fw(0)), pl.BlockSpec((1, c, vw), bw(0)), st]
    sd = jax.ShapeDtypeStruct
    return pl.pallas_call(
        functools.partial(_gla_kernel, n_chunks=nc),
        grid=(b, nc),
        in_specs=in_specs,
        out_specs=out_specs,
        out_shape=[sd((b, l, vw), BF16), sd((b, l, vw), BF16), sd((b, 2, GLA_H, GLA_DV, GLA_DK), F32)],
        scratch_shapes=[pltpu.VMEM((2, GLA_H, GLA_DV, GLA_DK), F32)],
        compiler_params=_cparams(("arbitrary", "arbitrary")),
        name="gla",
    )(pg, pg, pg, plr, pg, pg, pg, plr, w2p, b2, s0t)


def _gla_gate_params(w2, b2):
    w2p = jnp.pad(w2, ((0, 0), (0, LANES - GLA_LOWRANK), (0, 0)))
    return jnp.stack(_split2(w2p), axis=1), b2.reshape(2, 1, GLA_H * GLA_DK)


def _head_sum(x):
    i = lax.broadcasted_iota(jnp.int32, (LANES, LANES), 0)
    j = lax.broadcasted_iota(jnp.int32, (LANES, LANES), 1)
    ones = jnp.where(jnp.right_shift(i, RW_N_BITS) == jnp.right_shift(j, RW_N_BITS), 1.0, 0.0).astype(BF16)
    out = []
    for t in range(x.shape[1] // LANES):
        parts = _split2(x[:, t * LANES:(t + 1) * LANES])
        out.append(sum(jnp.dot(p, ones, preferred_element_type=F32) for p in parts))
    return jnp.concatenate(out, axis=1)


RW_TM = 256
HALO = 2 * SUBLANES
N_RW_LORA = RW_W_LORA + RW_A_LORA + RW_G_LORA


def _rw_prep_kernel(x_ref, xp_ref, xn_ref, lo_ref, lop_ref, lon_ref, taps_ref, ltaps_ref, kkp_ref, ka_ref, rk_ref,
                    w0_ref, wb_ref, a0_ref, ab_ref, gb_ref, r_ref, v_ref, kk_ref, lw_ref, kd_ref, be_ref, bonus_ref,
                    g_ref, xbuf, lbuf, *, n_tiles):
    tm = RW_TM
    j = pl.program_id(1)
    first, last = j == 0, j == n_tiles - 1
    xbuf[0:HALO, :] = jnp.where(first, jnp.zeros_like(xp_ref[0]), xp_ref[0])
    xbuf[HALO:HALO + tm, :] = x_ref[0]
    xbuf[HALO + tm:2 * HALO + tm, :] = jnp.where(last, jnp.zeros_like(xn_ref[0]), xn_ref[0])
    ri = lax.broadcasted_iota(jnp.int32, (tm, tm + 2 * HALO), 0)
    ci = lax.broadcasted_iota(jnp.int32, (tm, tm + 2 * HALO), 1)
    pick_prev = jnp.where(ci == ri + (HALO - 1), 1.0, 0.0).astype(BF16)
    pick_next = jnp.where(ci == ri + (HALO + 1), 1.0, 0.0).astype(BF16)

    def shifted_main(c0, c1):
        x_all = xbuf[:, c0:c1]
        return (taps_ref[0:1, c0:c1] * jnp.dot(pick_prev, x_all, preferred_element_type=F32)
                + taps_ref[1:2, c0:c1] * x_ref[0, :, c0:c1].astype(F32)
                + taps_ref[2:3, c0:c1] * jnp.dot(pick_next, x_all, preferred_element_type=F32))

    lbuf[HALO:HALO + tm, :] = lo_ref[0]
    lbuf[0:HALO, :] = jnp.where(first, 0.0, lop_ref[0])
    lbuf[HALO + tm:2 * HALO + tm, :] = jnp.where(last, 0.0, lon_ref[0])
    lora = (ltaps_ref[0:1, :] * lbuf[HALO - 1:HALO - 1 + tm, :] + ltaps_ref[1:2, :] * lbuf[HALO:HALO + tm, :]
            + ltaps_ref[2:3, :] * lbuf[HALO + 1:HALO + 1 + tm, :])
    r = shifted_main(0, D)
    k = shifted_main(D, 2 * D)
    v = shifted_main(2 * D, 3 * D)
    xw = lora[:, 0:RW_W_LORA]
    xa = lora[:, RW_W_LORA:RW_W_LORA + RW_A_LORA]
    xg = lora[:, RW_W_LORA + RW_A_LORA:]
    r_ref[0] = r.astype(r_ref.dtype)
    v_ref[0] = v.astype(v_ref.dtype)
    kk = k * kkp_ref[...]
    kk = kk * lax.rsqrt(jnp.maximum(_head_sum(kk * kk), 1e-24))
    kk_ref[0] = kk.astype(kk_ref.dtype)
    g_ref[0] = _mm(_sigmoid(xg), gb_ref[...]).astype(g_ref.dtype)
    wh = jnp.tanh(xw)
    kd_sum = jnp.zeros_like(k)
    for d in range(2):
        w = -_softplus(-(w0_ref[d:d + 1, :] + _mm(wh, wb_ref[d]))) - 0.5
        lw_ref[d, 0] = -jnp.exp(w)
        a = _sigmoid(a0_ref[d:d + 1, :] + _mm(xa, ab_ref[d]))
        kd = k * (1.0 + (a - 1.0) * ka_ref[...])
        kd_ref[d, 0] = kd.astype(kd_ref.dtype)
        be_ref[d, 0] = (kk * a).astype(be_ref.dtype)
        kd_sum = kd_sum + kd
    bonus_ref[0] = (_head_sum(r * kd_sum * rk_ref[...]) * v).astype(bonus_ref.dtype)


def _rw_prep(rw, rlo, taps, kkp, ka, rk, w0, wb, a0, ab, gb):
    b, l, _ = rw.shape
    tm = RW_TM
    nt = l // tm
    hb = tm // HALO
    tok = pl.BlockSpec((1, tm, D), lambda i, j: (i, j, 0))
    tokd = pl.BlockSpec((2, 1, tm, D), lambda i, j: (0, i, j, 0))

    def full(*shape):
        return pl.BlockSpec(shape, lambda i, j: (0,) * len(shape))

    def halo_specs(n):
        return [pl.BlockSpec((1, tm, n), lambda i, j: (i, j, 0)),
                pl.BlockSpec((1, HALO, n), lambda i, j: (i, jnp.maximum(j * hb - 1, 0), 0)),
                pl.BlockSpec((1, HALO, n), lambda i, j: (i, jnp.minimum((j + 1) * hb, l // HALO - 1), 0))]

    in_specs = halo_specs(3 * D) + halo_specs(N_RW_LORA) + [
        full(3, 3 * D), full(3, N_RW_LORA), full(1, D), full(1, D), full(1, D),
        full(2, D), full(2, RW_W_LORA, D), full(2, D), full(2, RW_A_LORA, D), full(RW_G_LORA, D),
    ]
    sd = jax.ShapeDtypeStruct
    tok_o, dir_o, dir_f32 = sd((b, l, D), BF16), sd((2, b, l, D), BF16), sd((2, b, l, D), F32)
    return pl.pallas_call(
        functools.partial(_rw_prep_kernel, n_tiles=nt),
        grid=(b, nt),
        in_specs=in_specs,
        out_specs=[tok, tok, tok, tokd, tokd, tokd, tok, tok],
        out_shape=[tok_o, tok_o, tok_o, dir_f32, dir_o, dir_o, tok_o, tok_o],
        scratch_shapes=[pltpu.VMEM((tm + 2 * HALO, 3 * D), BF16), pltpu.VMEM((tm + 2 * HALO, N_RW_LORA), F32)],
        compiler_params=_cparams(("arbitrary", "arbitrary")),
        name="rwkv_prep",
    )(rw, rw, rw, rlo, rlo, rlo, taps[:, :3 * D], taps[:, 3 * D:], kkp.reshape(1, D), ka.reshape(1, D),
      rk.reshape(1, D), w0, wb, a0, ab, gb)


def _tile_rows(x, n):
    return jnp.concatenate([x] * n, axis=0)


def _rw_scan_kernel(rf_ref, vf_ref, kkf_ref, lwf_ref, kdf_ref, bef_ref, rb_ref, vb_ref, kkb_ref, lwb_ref, kdb_ref,
                    beb_ref, s0_ref, yf_ref, yb_ref, sfin_ref, zt_ref, sv_lhs, sv_w, sv_bk, sv_tot, *, n_tiles):
    c, hw, sw = RW_C, RW_GW, RW_SW
    pk = RW_G * c
    n_groups, n_sub = RW_H // RW_G, RW_GW // RW_SW
    n_chunks = rf_ref.shape[1] // c
    j = pl.program_id(1)

    @pl.when(j == 0)
    def _():
        zt_ref[...] = jnp.zeros_like(zt_ref)
        for bb in range(RW_BB):
            for d in range(2):
                for g in range(RW_H // 2):
                    for jj in range(2):
                        blk = slice(RW_N * jj, RW_N * (jj + 1))
                        zt_ref[bb, d, g, blk, blk] = s0_ref[bb, d, 2 * g + jj]

    ins = ((rf_ref, vf_ref, kkf_ref, lwf_ref, kdf_ref, bef_ref), (rb_ref, vb_ref, kkb_ref, lwb_ref, kdb_ref, beb_ref))
    outs = (yf_ref, yb_ref)
    ri = lax.broadcasted_iota(jnp.int32, (c, pk), 0)
    cs = jnp.bitwise_and(lax.broadcasted_iota(jnp.int32, (c, pk), 1), c - 1)
    eye = (ri == cs).astype(F32)
    strict, incl, lvl, quad = [], [], [], []
    for d in range(2):
        ti = ri if d == 0 else c - 1 - ri
        ts = cs if d == 0 else c - 1 - cs
        strict.append(ts < ti)
        incl.append(ts <= ti)
        same4 = jnp.right_shift(ti, 2) == jnp.right_shift(ts, 2)
        below = jnp.bitwise_and(ti, 3) - jnp.bitwise_and(ts, 3)
        quad.append(tuple(same4 & (below == k) for k in (1, 2, 3)))
        lv_masks = {}
        for lv in range(2, RW_LEVELS):
            same = jnp.right_shift(ti, lv + 1) == jnp.right_shift(ts, lv + 1)
            lower = ((jnp.bitwise_and(jnp.right_shift(ti, lv), 1) == 1)
                     & (jnp.bitwise_and(jnp.right_shift(ts, lv), 1) == 0))
            lv_masks[lv] = same & lower
        lvl.append(lv_masks)
    bi = lax.broadcasted_iota(jnp.int32, (pk, pk), 0)
    bj = lax.broadcasted_iota(jnp.int32, (pk, pk), 1)
    bd_p = jnp.right_shift(bi, RW_LEVELS) == jnp.right_shift(bj, RW_LEVELS)
    zi = lax.broadcasted_iota(jnp.int32, (sw, sw), 0)
    zj = lax.broadcasted_iota(jnp.int32, (sw, sw), 1)
    bd_z = jnp.right_shift(zi, RW_N_BITS) == jnp.right_shift(zj, RW_N_BITS)
    term_row = lax.broadcasted_iota(jnp.int32, (2 * SUBLANES, sw), 0)
    wi = lax.broadcasted_iota(jnp.int32, (pk, hw), 0)
    wj = lax.broadcasted_iota(jnp.int32, (pk, hw), 1)
    bd_w = jnp.right_shift(wi, RW_LEVELS) == jnp.right_shift(wj, RW_N_BITS)

    def bdp(x):
        return jnp.where(bd_p, _tile_rows(x, RW_G), 0.0)

    def bdw(x):
        return jnp.where(bd_w, _tile_rows(x, RW_G), 0.0)

    chains = [(bb, d, half) for bb in range(RW_BB) for d in range(2) for half in range(n_groups)]

    def each(fn):
        return {ch: fn(ch) for ch in chains}

    def chunk_rows(i):
        return (pl.ds(pl.multiple_of(i * c, c), c), pl.ds(pl.multiple_of((n_chunks - 1 - i) * c, c), c))

    def scale(i, slot):
        rows = chunk_rows(i)

        def load(ch):
            bb, d, half = ch
            ln = slice(half * hw, (half + 1) * hw)
            refs = ins[d]
            return (refs[0][bb, rows[d], ln].astype(F32), refs[2][bb, rows[d], ln].astype(F32),
                    refs[3][0, bb, rows[d], ln], refs[4][0, bb, rows[d], ln].astype(F32),
                    refs[5][0, bb, rows[d], ln].astype(F32))

        x = each(load)
        r, kk, lw = each(lambda ch: x[ch][0]), each(lambda ch: x[ch][1]), each(lambda ch: x[ch][2])
        kd, be = each(lambda ch: x[ch][3]), each(lambda ch: x[ch][4])
        b = each(lambda ch: _scan_rows(lw[ch], ch[1] == 1))
        tot = each(lambda ch: b[ch][0:1] if ch[1] == 1 else b[ch][c - 1:c])
        en = each(lambda ch: jnp.exp(-b[ch]))
        es = each(lambda ch: jnp.exp(tot[ch] - b[ch]))
        for n, ch in enumerate(chains):
            sv_lhs[slot, n] = jnp.concatenate([kk[ch] * jnp.exp(b[ch] - lw[ch]), r[ch] * jnp.exp(b[ch])],
                                              axis=0).astype(BF16)
            sv_w[slot, n] = jnp.concatenate([be[ch] * en[ch], kd[ch] * en[ch]], axis=0).astype(BF16)
            sv_bk[slot, n] = jnp.concatenate([be[ch] * es[ch], kd[ch] * es[ch]], axis=0).astype(BF16)
            sv_tot[slot, n] = jnp.broadcast_to(tot[ch], (SUBLANES, hw))

    scale(0, 0)

    def chunk(i, carry):
        rows = chunk_rows(i)
        slot = lax.rem(i, 2)
        index = {ch: n for n, ch in enumerate(chains)}
        lhs = each(lambda ch: sv_lhs[slot, index[ch]])
        w_en = each(lambda ch: sv_w[slot, index[ch]])
        bk = each(lambda ch: sv_bk[slot, index[ch]])
        tot = each(lambda ch: sv_tot[slot, index[ch]][0:1])
        v = each(lambda ch: ins[ch[1]][1][ch[0], rows[ch[1]], ch[2] * hw:(ch[2] + 1) * hw].astype(F32))
        gb = each(lambda ch: _mm_nt(lhs[ch], bdw(w_en[ch][0:c])))
        gk = each(lambda ch: _mm_nt(lhs[ch], bdw(w_en[ch][c:])))
        a_ab = each(lambda ch: jnp.where(strict[ch[1]], gb[ch][0:c], 0.0))
        a_rb = each(lambda ch: jnp.where(incl[ch[1]], gb[ch][c:], 0.0))
        a_ak = each(lambda ch: jnp.where(strict[ch[1]], gk[ch][0:c], 0.0))
        a_rk = each(lambda ch: jnp.where(incl[ch[1]], gk[ch][c:], 0.0))
        def inverse4(ch):
            d = ch[1]
            q1, q2, q3 = quad[d]
            a = jnp.where(q1 | q2 | q3, a_ab[ch], 0.0)
            col = (lambda x, k: pltpu.roll(x, pk - k, 1)) if d == 0 else (lambda x, k: pltpu.roll(x, k, 1))
            row = (lambda x, k: pltpu.roll(x, k, 0)) if d == 0 else (lambda x, k: pltpu.roll(x, c - k, 0))
            c1, c2, r1, r2 = col(a, 1), col(a, 2), row(a, 1), row(a, 2)
            two = jnp.where(q2, c1 * r1, 0.0)
            three = jnp.where(q3, c1 * r2 + c2 * r1 - c2 * row(c1, 1) * r2, 0.0)
            return eye - a + two + three

        t_inv = each(inverse4)
        for lv in range(2, RW_LEVELS):
            xm = each(lambda ch: _mm(jnp.where(lvl[ch[1]][lv], a_ab[ch], 0.0), bdp(t_inv[ch])))
            t_inv = each(lambda ch: t_inv[ch] - _mm(t_inv[ch], bdp(xm[ch])))
            if lv == 2:
                scale(jnp.minimum(i + 1, n_chunks - 1), 1 - slot)
        zt_old = {(ch, g): zt_ref[ch[0], ch[1], n_sub * ch[2] + g] for ch in chains for g in range(n_sub)}
        zz = each(lambda ch: jnp.concatenate(
            [_mm(lhs[ch][:, g * sw:(g + 1) * sw], zt_old[ch, g]) for g in range(n_sub)], axis=1))
        av = each(lambda ch: _mm(jnp.concatenate([a_ak[ch], a_rk[ch]], axis=0), bdw(v[ch])))
        u = each(lambda ch: _mm(t_inv[ch], bdw(-(zz[ch][0:c] + av[ch][0:c]))))
        y = each(lambda ch: zz[ch][c:] + _mm(a_rb[ch], bdw(u[ch])) + av[ch][c:])
        uv = each(lambda ch: jnp.concatenate([u[ch], v[ch]], axis=0))

        def decay_cols(ch, gl):
            x = tot[ch][:, gl]
            hi = x.astype(BF16).astype(F32)
            mid = (x - hi).astype(BF16).astype(F32)
            lo = x - hi - mid
            terms = jnp.where(term_row == 0, hi, jnp.where(term_row == 1, mid, jnp.where(term_row == 2, lo, 0.0)))
            return jnp.exp(_mm_tn(terms, jnp.ones((2 * SUBLANES, sw), BF16)))

        zt_new = {}
        for ch in chains:
            for g in range(n_sub):
                gl = slice(g * sw, (g + 1) * sw)
                upd = jnp.where(bd_z, _mm_tn(bk[ch][:, gl], uv[ch][:, gl]), 0.0)
                zt_new[ch, g] = zt_old[ch, g] * decay_cols(ch, gl) + upd
        for ch in chains:
            bb, d, half = ch
            outs[d][bb, rows[d], half * hw:(half + 1) * hw] = y[ch].astype(outs[d].dtype)
            for g in range(n_sub):
                zt_ref[bb, d, n_sub * half + g] = zt_new[ch, g]
        return carry

    lax.fori_loop(0, n_chunks, chunk, 0)

    @pl.when(j == n_tiles - 1)
    def _():
        for bb in range(RW_BB):
            for d in range(2):
                for g in range(RW_H // 2):
                    for jj in range(2):
                        blk = slice(RW_N * jj, RW_N * (jj + 1))
                        sfin_ref[bb, d, 2 * g + jj] = zt_ref[bb, d, g, blk, blk]


def _rw_scan(r, v, kk, lw, kd, be, s0):
    b, l, _ = r.shape
    tl, nb = min(RW_TL, l), RW_BB
    nt = l // tl
    n_ch = nb * 2 * (RW_H // RW_G)
    tok_f = pl.BlockSpec((nb, tl, D), lambda i, j: (i, j, 0))
    tok_b = pl.BlockSpec((nb, tl, D), lambda i, j: (i, nt - 1 - j, 0))
    dir_f = pl.BlockSpec((1, nb, tl, D), lambda i, j: (0, i, j, 0))
    dir_b = pl.BlockSpec((1, nb, tl, D), lambda i, j: (1, i, nt - 1 - j, 0))
    st = pl.BlockSpec((nb, 2, RW_H, RW_N, RW_N), lambda i, j: (i, 0, 0, 0, 0))
    sd = jax.ShapeDtypeStruct
    return pl.pallas_call(
        functools.partial(_rw_scan_kernel, n_tiles=nt),
        grid=(b // nb, nt),
        in_specs=[tok_f, tok_f, tok_f, dir_f, dir_f, dir_f, tok_b, tok_b, tok_b, dir_b, dir_b, dir_b, st],
        out_specs=[tok_f, tok_b, st],
        out_shape=[sd((b, l, D), BF16), sd((b, l, D), BF16), sd((b, 2, RW_H, RW_N, RW_N), F32)],
        scratch_shapes=[pltpu.VMEM((nb, 2, RW_H // 2, RW_SW, RW_SW), F32),
                        pltpu.VMEM((2, n_ch, 2 * RW_C, RW_GW), BF16),
                        pltpu.VMEM((2, n_ch, 2 * RW_C, RW_GW), BF16),
                        pltpu.VMEM((2, n_ch, 2 * RW_C, RW_GW), BF16),
                        pltpu.VMEM((2, n_ch, SUBLANES, RW_GW), F32)],
        compiler_params=_cparams(("arbitrary", "arbitrary")),
        name="rwkv_scan",
    )(r, v, kk, lw, kd, be, r, v, kk, lw, kd, be, s0)


MERGE_TM = 256
ROUTER_LANES = LANES


def _merge_kernel(orf_ref, orb_ref, rg_ref, ogf_ref, ogb_ref, gg_ref, yf_ref, yb_ref, bonus_ref, grw_ref,
                  gates_ref, x_ref, gt_ref, sh_ref, sc_ref, wbr_ref, wout_ref, rgn_ref, ggn_ref, gng_ref, gnb_ref,
                  l1g_ref, l1b_ref, wr_ref, br_ref, x1_ref, u2_ref, lg_ref):
    f32 = lambda ref: ref[0].astype(F32)
    o_ret = f32(orf_ref) + f32(orb_ref)
    o_gla = f32(ogf_ref) + f32(ogb_ref)
    y_ret, y_gla = [], []
    for h in range(RET_H):
        hs = slice(h * RET_DV, (h + 1) * RET_DV)
        y_ret.append(_ln(o_ret[:, hs]))
        og = o_gla[:, hs]
        y_gla.append(og * lax.rsqrt(jnp.mean(og * og, axis=-1, keepdims=True) + EPS))
    z_ret = _silu(f32(rg_ref)) * (jnp.concatenate(y_ret, axis=1) * rgn_ref[...])
    z_gla = _silu(f32(gg_ref)) * (jnp.concatenate(y_gla, axis=1) * ggn_ref[...])
    y = f32(yf_ref) + f32(yb_ref)
    mu = _head_sum(y) * (1.0 / RW_N)
    yc = y - mu
    var = _head_sum(yc * yc) * (1.0 / RW_N)
    z_rw = (yc * lax.rsqrt(var + EPS) * gng_ref[...] + gnb_ref[...] + f32(bonus_ref)) * f32(grw_ref)
    gates = f32(gates_ref)
    mixed = (gates[:, 0:D] * _mm(z_ret, wbr_ref[0]) + gates[:, D:2 * D] * _mm(z_gla, wbr_ref[1])
             + gates[:, 2 * D:] * _mm(z_rw, wbr_ref[2]))
    mix = _mm(mixed, wout_ref[...])
    x1 = _ln(ALPHA * x_ref[0] + gt_ref[0] * mix) * l1g_ref[...] + l1b_ref[...]
    x1_ref[0] = x1
    u2 = _ln(x1) * (1.0 + sc_ref[0]) + sh_ref[0]
    u2_ref[0] = u2.astype(u2_ref.dtype)
    lg_ref[...] = (_mm_split(u2, wr_ref[0], wr_ref[1]) + br_ref[...]).T


def _merge(orf, orb, pr, ogf, ogb, pg, yf, yb, bonus, grw, gates, x, gt1, sh2, sc2, wbr, wout, rgn, ggn, gng, gnb,
           l1g, l1b, wr, br):
    b, l, _ = x.shape
    tm = MERGE_TM
    tok = pl.BlockSpec((1, tm, D), lambda i, j: (i, j, 0))
    gate_blk = pl.BlockSpec((1, tm, D), lambda i, j: (i, j, 2))
    vec = pl.BlockSpec((1, 1, D), lambda i, j: (i, 0, 0))
    row = pl.BlockSpec((1, D), lambda i, j: (0, 0))
    in_specs = [
        tok, tok, gate_blk, tok, tok, gate_blk, tok, tok, tok, tok,
        pl.BlockSpec((1, tm, 3 * D), lambda i, j: (i, j, 0)),
        tok, vec, vec, vec,
        pl.BlockSpec((3, D, D), lambda i, j: (0, 0, 0)),
        pl.BlockSpec((D, D), lambda i, j: (0, 0)),
        row, row, row, row, row, row,
        pl.BlockSpec((2, D, ROUTER_LANES), lambda i, j: (0, 0, 0)),
        pl.BlockSpec((1, ROUTER_LANES), lambda i, j: (0, 0)),
    ]
    sd = jax.ShapeDtypeStruct
    r1 = lambda a: a.reshape(1, D)
    return pl.pallas_call(
        _merge_kernel,
        grid=(b, l // tm),
        in_specs=in_specs,
        out_specs=[tok, tok, pl.BlockSpec((ROUTER_LANES, tm), lambda i, j: (0, i * (l // tm) + j))],
        out_shape=[sd((b, l, D), F32), sd((b, l, D), BF16), sd((ROUTER_LANES, b * l), F32)],
        compiler_params=_cparams(("arbitrary", "arbitrary")),
        name="merge",
    )(orf, orb, pr, ogf, ogb, pg, yf, yb, bonus, grw, gates, x, gt1, sh2, sc2, wbr, wout,
      r1(rgn), r1(ggn), r1(gng), r1(gnb), r1(l1g), r1(l1b), wr, br)


MOE_TM = 1024


def _moe_block_rows(tm):
    rows = tm / N_GROUPS + 4.0 * np.sqrt(tm * (N_GROUPS - 1.0)) / N_GROUPS
    bf16_tile = 2 * SUBLANES
    return int(min(tm, -(-rows // bf16_tile) * bf16_tile))


def _first_argmax_rows(x, row):
    m = jnp.max(x, axis=0, keepdims=True)
    idx = jnp.min(jnp.where(x == m, row, x.shape[0]), axis=0, keepdims=True)
    return m, idx


def _routing(lt):
    tm = lt.shape[1]
    gl = lt[0:N_GROUPS]
    gmax, gidx = _first_argmax_rows(gl, lax.broadcasted_iota(jnp.int32, (N_GROUPS, tm), 0))
    g_w = 1.0 / jnp.sum(jnp.exp(gl - gmax), axis=0, keepdims=True)
    row = lax.broadcasted_iota(jnp.int32, (N_EXPERTS, tm), 0)
    neg = -jnp.inf
    el = jnp.where(jnp.right_shift(row, EPG_BITS) == gidx, lt[N_GROUPS:N_GROUPS + N_EXPERTS], neg)
    m1, i1 = _first_argmax_rows(el, row)
    m2, i2 = _first_argmax_rows(jnp.where(row == i1, neg, el), row)
    e2 = jnp.exp(m2 - m1)
    w1 = 1.0 / (1.0 + e2)
    return gidx, g_w * (jnp.where(row == i1, w1, 0.0) + jnp.where(row == i2, e2 * w1, 0.0))


def _moe_kernel(u_ref, lt_ref, wg_ref, wu_ref, wd_ref, o_ref, acc_ref):
    g = pl.program_id(1)
    tm = acc_ref.shape[0]
    blk = _moe_block_rows(tm)

    @pl.when(g == 0)
    def _():
        acc_ref[...] = jnp.zeros_like(acc_ref)

    gidx, comb = _routing(lt_ref[...])
    member = jnp.broadcast_to((gidx == g).astype(F32), (SUBLANES, tm))
    lane = lax.broadcasted_iota(jnp.int32, (SUBLANES, tm), 1)
    count = member
    sh = 1
    while sh < tm:
        count = count + jnp.where(lane >= sh, pltpu.roll(count, sh, 1), 0.0)
        sh *= 2
    pos = jnp.where(member > 0.0, count - 1.0, -1.0).astype(jnp.int32)[0:1]
    n_tok = jnp.max(count).astype(jnp.int32)
    comb_pad = jnp.concatenate([comb, jnp.zeros((LANES - N_EXPERTS, tm), F32)], axis=0)
    comb_parts = _split2(comb_pad)
    u = u_ref[...]
    sel_lane = lax.broadcasted_iota(jnp.int32, (blk, LANES), 1)

    def body(i, carry):
        slot = lax.broadcasted_iota(jnp.int32, (blk, tm), 0) + i * blk
        onehot = jnp.where(slot == pos, 1.0, 0.0).astype(BF16)
        xg = jnp.dot(onehot, u, preferred_element_type=F32).astype(BF16)
        cw = sum(lax.dot_general(onehot, c, (((1,), (1,)), ((), ())), preferred_element_type=F32)
                 for c in comb_parts)
        y = jnp.zeros((blk, D), F32)
        for e in range(EPG):
            hid = (_silu(jnp.dot(xg, wg_ref[e], preferred_element_type=F32))
                   * jnp.dot(xg, wu_ref[e], preferred_element_type=F32))
            c_e = jnp.sum(jnp.where(sel_lane == g * EPG + e, cw, 0.0), axis=-1, keepdims=True)
            y = y + c_e * _mm(hid, wd_ref[e])
        acc_ref[...] += _mm_tn(onehot, y)
        return carry

    lax.fori_loop(0, (n_tok + blk - 1) // blk, body, 0)

    @pl.when(g == N_GROUPS - 1)
    def _():
        o_ref[...] = acc_ref[...].astype(o_ref.dtype)


def _moe(u2, logits_t, wg, wu, wd):
    b, l, _ = u2.shape
    t = b * l
    tm = min(MOE_TM, t)
    tok = pl.BlockSpec((tm, D), lambda i, g: (i, 0))
    in_specs = [
        tok,
        pl.BlockSpec((ROUTER_LANES, tm), lambda i, g: (0, i)),
        pl.BlockSpec((EPG, D, EXPERT_FF), lambda i, g: (g, 0, 0)),
        pl.BlockSpec((EPG, D, EXPERT_FF), lambda i, g: (g, 0, 0)),
        pl.BlockSpec((EPG, EXPERT_FF, D), lambda i, g: (g, 0, 0)),
    ]
    return pl.pallas_call(
        _moe_kernel,
        grid=(t // tm, N_GROUPS),
        in_specs=in_specs,
        out_specs=tok,
        out_shape=jax.ShapeDtypeStruct((t, D), BF16),
        scratch_shapes=[pltpu.VMEM((tm, D), F32)],
        compiler_params=_cparams(("arbitrary",) * 2),
        name="moe",
    )(u2.reshape(t, D), logits_t, wg, wu, wd).reshape(b, l, D)


def _post_kernel(x_ref, m_ref, gt_ref, l2g_ref, l2b_ref, sh_ref, sc_ref, o_ref, u_ref):
    x2 = _ln(ALPHA * x_ref[0] + gt_ref[0] * m_ref[0].astype(F32)) * l2g_ref[...] + l2b_ref[...]
    o_ref[0] = x2
    u_ref[0] = (_ln(x2) * (1.0 + sc_ref[0]) + sh_ref[0]).astype(u_ref.dtype)


def _post(x1, moe_out, gt2, l2g, l2b, sh_next, sc_next):
    b, l, _ = x1.shape
    tm = 512 if l % 512 == 0 else 256
    tok = pl.BlockSpec((1, tm, D), lambda i, j: (i, j, 0))
    vec = pl.BlockSpec((1, 1, D), lambda i, j: (i, 0, 0))
    row = pl.BlockSpec((1, D), lambda i, j: (0, 0))
    sd = jax.ShapeDtypeStruct
    return pl.pallas_call(
        _post_kernel,
        grid=(b, l // tm),
        in_specs=[tok, tok, vec, row, row, vec, vec],
        out_specs=[tok, tok],
        out_shape=[sd((b, l, D), F32), sd((b, l, D), BF16)],
        compiler_params=_cparams(("arbitrary", "arbitrary")),
        name="post_ln",
    )(x1, moe_out, gt2, l2g.reshape(1, D), l2b.reshape(1, D), sh_next, sc_next)


def _rope_tables(l):
    t = np.arange(l)
    quarter = RET_DK // 4
    freqs = (np.float32(ROPE_BASE) ** (-np.arange(quarter, dtype=np.float32) / quarter)).astype(np.float32)
    rows = (t // GRID_W).astype(np.float32)
    cols = (t % GRID_W).astype(np.float32)
    ang = jnp.asarray(np.concatenate([rows[:, None] * freqs, cols[:, None] * freqs], -1))
    cos, sin = jnp.cos(ang), jnp.sin(ang)
    return jnp.concatenate([cos, cos], -1), jnp.concatenate([-sin, sin], -1)


def _layer_weights(p):
    n_rg = 2 * (RET_H * RET_DK + RET_H * RET_DV)
    w_in = p["w_in"]
    w_lr = jnp.pad(w_in[:, 2 * n_rg:2 * n_rg + GLA_LOWRANK], ((0, 0), (0, LANES - GLA_LOWRANK)))
    w2p, b2p = _gla_gate_params(p["gla_w2"], p["gla_b"])
    pad = ROUTER_LANES - N_GROUPS - N_EXPERTS
    return dict(
        p,
        w_ret=w_in[:, :n_rg].astype(BF16),
        w_gla=w_in[:, n_rg:2 * n_rg].astype(BF16),
        w_glr=w_lr.astype(BF16),
        w_rw=w_in[:, 2 * n_rg + GLA_LOWRANK:2 * n_rg + GLA_LOWRANK + 3 * D].astype(BF16),
        w_rlo=w_in[:, 2 * n_rg + GLA_LOWRANK + 3 * D:].astype(BF16),
        w_merge=p["w_merge"].astype(BF16),
        w_br=p["w_br"].astype(BF16),
        w_out=p["w_out"].astype(BF16),
        gla_w2p=w2p, gla_b2p=b2p,
        w_router=jnp.stack(_split2(jnp.pad(jnp.concatenate([p["w_rg"], p["w_re"]], axis=1), ((0, 0), (0, pad))))),
        b_router=jnp.pad(jnp.concatenate([p["b_rg"], p["b_re"]]), (0, pad)).reshape(1, ROUTER_LANES),
        w_eg=p["w_eg"].astype(BF16), w_eu=p["w_eu"].astype(BF16), w_ed=p["w_ed"].astype(BF16),
    )


def _split_mod(mod):
    return [m.reshape(mod.shape[0], 1, D) for m in jnp.split(mod, 6, axis=-1)]


def _layer(x, u, mod, mod_next, s_ret, s_gla_t, s_rw, rope, p):
    b, l, _ = x.shape
    _, _, gt1, sh2, sc2, gt2 = _split_mod(mod)
    sh_next, sc_next = _split_mod(mod_next)[:2]
    u2d = u.reshape(b * l, D)

    def proj(w, tn, bias=None, **kw):
        bias = jnp.zeros((w.shape[1],), F32) if bias is None else bias
        return _proj(u2d, w, bias, tn, **kw).reshape(b, l, -1)

    pr = proj(p["w_ret"], 1024)
    pg = proj(p["w_gla"], 1024)
    plr = proj(p["w_glr"], LANES, out_dtype=F32)
    rw = proj(p["w_rw"], 1024)
    rlo = proj(p["w_rlo"], N_RW_LORA, out_dtype=F32)
    gates = proj(p["w_merge"], 1024, bias=p["b_merge"], act="sigmoid")

    cos2, sin2 = rope if rope is not None else (jnp.ones((l, RET_DK), F32), jnp.zeros((l, RET_DK), F32))
    orf, orb, ret_fin = _retention(pr, cos2, sin2, s_ret, rope is not None)
    ogf, ogb, gla_fin_t = _gla(pg, plr, p["gla_w2p"], p["gla_b2p"], s_gla_t)
    r, v, kk, lw, kd, be, bonus, grw = _rw_prep(rw, rlo, p["rwkv_shift"], p["rwkv_kk"], p["rwkv_ka"],
                                                p["rwkv_rk"].reshape(-1), p["rwkv_w0"], p["rwkv_wb"],
                                                p["rwkv_a0"], p["rwkv_ab"], p["rwkv_gb"])
    yf, yb, rw_fin = _rw_scan(r, v, kk, lw, kd, be, s_rw)
    x1, u2, logits_t = _merge(orf, orb, pr, ogf, ogb, pg, yf, yb, bonus, grw, gates, x, gt1, sh2, sc2,
                              p["w_br"], p["w_out"], p["ret_gn"], p["gla_gn"], p["rwkv_gn_g"], p["rwkv_gn_b"],
                              p["ln1_g"], p["ln1_b"], p["w_router"], p["b_router"])
    moe_out = _moe(u2, logits_t, p["w_eg"], p["w_eu"], p["w_ed"])
    x2, u_next = _post(x1, moe_out, gt2, p["ln2_g"], p["ln2_b"], sh_next, sc_next)
    return x2, u_next, (ret_fin, gla_fin_t, rw_fin)


_PARAM_NAMES = ("w_in", "rwkv_shift", "ret_gn", "gla_w2", "gla_b", "gla_gn", "rwkv_w0", "rwkv_wb", "rwkv_a0",
                "rwkv_ab", "rwkv_gb", "rwkv_kk", "rwkv_ka", "rwkv_rk", "rwkv_gn_g", "rwkv_gn_b", "w_br", "w_merge",
                "b_merge", "w_out", "ln1_g", "ln1_b", "ln2_g", "ln2_b", "w_rg", "b_rg", "w_re", "b_re", "w_eg",
                "w_eu", "w_ed")


def kernel(x_prompt, x_sample, state_ret, state_gla, state_rwkv, c, c_ctx, w_ada, b_ada, w_in, rwkv_shift, ret_gn,
           gla_w2, gla_b, gla_gn, rwkv_w0, rwkv_wb, rwkv_a0, rwkv_ab, rwkv_gb, rwkv_kk, rwkv_ka, rwkv_rk, rwkv_gn_g,
           rwkv_gn_b, w_br, w_merge, b_merge, w_out, ln1_g, ln1_b, ln2_g, ln2_b, w_rg, b_rg, w_re, b_re, w_eg, w_eu,
           w_ed):
    params = dict(zip(_PARAM_NAMES, (w_in, rwkv_shift, ret_gn, gla_w2, gla_b, gla_gn, rwkv_w0, rwkv_wb, rwkv_a0,
                                     rwkv_ab, rwkv_gb, rwkv_kk, rwkv_ka, rwkv_rk, rwkv_gn_g, rwkv_gn_b, w_br, w_merge,
                                     b_merge, w_out, ln1_g, ln1_b, ln2_g, ln2_b, w_rg, b_rg, w_re, b_re, w_eg, w_eu,
                                     w_ed)))
    bc, bl = x_prompt.shape[0], x_sample.shape[0]
    rope = _rope_tables(x_sample.shape[1])
    mod_rows = 2 * SUBLANES
    c_all = jnp.concatenate([c, c_ctx[None, :], jnp.zeros((mod_rows - bl - 1, D), F32)], axis=0)
    z_ret = jnp.zeros((bc, 2, RET_H, RET_DK, RET_DV), F32)
    z_gla_t = jnp.zeros((bc, 2, GLA_H, GLA_DV, GLA_DK), F32)
    z_rw = jnp.zeros((bc, 2, RW_H, RW_N, RW_N), F32)
    mods = [_modulation(c_all, w_ada[layer], b_ada[layer]) for layer in range(DEPTH)]
    mods.append(jnp.zeros_like(mods[0]))
    mods_lat = [m[:bl] for m in mods]
    mods_ctx = [jnp.broadcast_to(m[bl:bl + 1], (bc, 6 * D)) for m in mods]
    h_ctx, h_lat = x_prompt, x_sample
    sh, sc = _split_mod(mods_ctx[0])[:2]
    u_ctx = _lnmod(h_ctx, sh, sc)
    sh, sc = _split_mod(mods_lat[0])[:2]
    u_lat = _lnmod(h_lat, sh, sc)
    new_ret, new_gla, new_rw = [], [], []
    for layer in range(DEPTH):
        p = _layer_weights({k: v[layer] for k, v in params.items()})
        h_ctx, u_ctx, (s_ret, s_gla_t, s_rw) = _layer(h_ctx, u_ctx, mods_ctx[layer], mods_ctx[layer + 1],
                                                      z_ret, z_gla_t, z_rw, None, p)
        new_ret.append(s_ret)
        new_gla.append(jnp.swapaxes(s_gla_t, -1, -2))
        new_rw.append(jnp.swapaxes(s_rw, -1, -2))
        h_lat, u_lat, _ = _layer(h_lat, u_lat, mods_lat[layer], mods_lat[layer + 1], state_ret[:, layer],
                                 jnp.swapaxes(state_gla[:, layer], -1, -2),
                                 jnp.swapaxes(state_rwkv[:, layer], -1, -2), rope, p)
    return (h_ctx, h_lat, jnp.stack(new_ret, axis=1), jnp.stack(new_gla, axis=1), jnp.stack(new_rw, axis=1))
```

```python
import functools

import jax
import jax.numpy as jnp
import numpy as np
from jax import lax
from jax.experimental import pallas as pl
from jax.experimental.pallas import tpu as pltpu

F32 = jnp.float32
BF16 = jnp.bfloat16
HI = lax.Precision.HIGHEST

D = 1024
DEPTH = 2
GRID_W = 64
RET_H, RET_DK, RET_DV = 4, 128, 256
GLA_H, GLA_DK, GLA_DV = 4, 128, 256
GLA_LOWRANK = 16
GLA_GATE_NORM = 16.0
RW_H, RW_N = 16, 64
RW_W_LORA, RW_A_LORA, RW_G_LORA = 64, 64, 128
N_GROUPS, EPG, N_EXPERTS, EXPERT_FF = 4, 4, 16, 512
ALPHA = (2 * DEPTH) ** 0.25
EPS = 1e-5
ROPE_BASE = 10000.0

LANES = 128
SUBLANES = 8
VMEM_LIMIT = 56 * 1024 * 1024

RET_C = 128
GLA_C = 64
GLA_LEVELS = GLA_C.bit_length() - 1
RW_C = 32
RW_LEVELS = RW_C.bit_length() - 1
RW_N_BITS = RW_N.bit_length() - 1
EPG_BITS = EPG.bit_length() - 1
RW_TL = 256
RW_BB = 2
RW_G = 4
RW_GW = RW_G * RW_N
RW_SW = 2 * RW_N


def _cparams(sem):
    return pltpu.CompilerParams(dimension_semantics=sem, vmem_limit_bytes=VMEM_LIMIT)


def _mm(a, b):
    return jnp.dot(a.astype(BF16), b.astype(BF16), preferred_element_type=F32)


def _mm_nt(a, b):
    return lax.dot_general(a.astype(BF16), b.astype(BF16), (((1,), (1,)), ((), ())), preferred_element_type=F32)


def _mm_tn(a, b):
    return lax.dot_general(a.astype(BF16), b.astype(BF16), (((0,), (0,)), ((), ())), preferred_element_type=F32)


def _mm_hi(a, b):
    return jnp.dot(a, b, preferred_element_type=F32, precision=HI)


def _split2(x):
    hi = x.astype(BF16)
    return hi, (x - hi.astype(F32)).astype(BF16)


def _mm_split(a, b_hi, b_lo):
    a_hi, a_lo = _split2(a)
    dot = functools.partial(jnp.dot, preferred_element_type=F32)
    return dot(a_hi, b_hi) + (dot(a_hi, b_lo) + dot(a_lo, b_hi))


def _sigmoid(x):
    return 1.0 / (1.0 + jnp.exp(-x))


def _silu(x):
    return x * _sigmoid(x)


def _log_sigmoid(x):
    return jnp.minimum(x, 0.0) - jnp.log(1.0 + jnp.exp(-jnp.abs(x)))


def _ln(x):
    mu = jnp.mean(x, axis=-1, keepdims=True)
    xc = x - mu
    var = jnp.mean(xc * xc, axis=-1, keepdims=True)
    return xc * lax.rsqrt(var + EPS)


def _scan_rows(x, reverse):
    n = x.shape[0]
    row = lax.broadcasted_iota(jnp.int32, x.shape, 0)
    sh = 1
    while sh < n:
        if reverse:
            x = x + jnp.where(row < n - sh, pltpu.roll(x, n - sh, 0), 0.0)
        else:
            x = x + jnp.where(row >= sh, pltpu.roll(x, sh, 0), 0.0)
        sh *= 2
    return x


def _mod_kernel(c_ref, w_ref, b_ref, o_ref):
    o_ref[...] = _mm_hi(_silu(c_ref[...]), w_ref[...]) + b_ref[...]


def _modulation(c_all, w, b):
    m, n, tn = c_all.shape[0], w.shape[1], 512
    return pl.pallas_call(
        _mod_kernel,
        grid=(n // tn,),
        in_specs=[pl.BlockSpec((m, D), lambda j: (0, 0)),
                  pl.BlockSpec((D, tn), lambda j: (0, j)),
                  pl.BlockSpec((1, tn), lambda j: (0, j))],
        out_specs=pl.BlockSpec((m, tn), lambda j: (0, j)),
        out_shape=jax.ShapeDtypeStruct((m, n), F32),
        compiler_params=_cparams(("arbitrary",)),
        name="adaln_mod",
    )(c_all, w, b.reshape(1, n))


def _lnmod_kernel(x_ref, sh_ref, sc_ref, u_ref):
    u_ref[0] = (_ln(x_ref[0]) * (1.0 + sc_ref[0]) + sh_ref[0]).astype(u_ref.dtype)


def _lnmod(x, sh, sc):
    b, l, _ = x.shape
    tm = 512 if l % 512 == 0 else 256
    vec = pl.BlockSpec((1, 1, D), lambda i, j: (i, 0, 0))
    return pl.pallas_call(
        _lnmod_kernel,
        grid=(b, l // tm),
        in_specs=[pl.BlockSpec((1, tm, D), lambda i, j: (i, j, 0)), vec, vec],
        out_specs=pl.BlockSpec((1, tm, D), lambda i, j: (i, j, 0)),
        out_shape=jax.ShapeDtypeStruct((b, l, D), BF16),
        compiler_params=_cparams(("arbitrary", "arbitrary")),
        name="ln_mod",
    )(x, sh, sc)


def _proj_kernel(u_ref, w_ref, b_ref, o_ref, *, act):
    y = jnp.dot(u_ref[...], w_ref[...], preferred_element_type=F32) + b_ref[...]
    if act == "sigmoid":
        y = _sigmoid(y)
    o_ref[...] = y.astype(o_ref.dtype)


def _proj(u2d, w, bias, tn, act=None, out_dtype=BF16):
    t, n = u2d.shape[0], w.shape[1]
    tm = min(t, 2048)
    return pl.pallas_call(
        functools.partial(_proj_kernel, act=act),
        grid=(t // tm, n // tn),
        in_specs=[pl.BlockSpec((tm, D), lambda i, j: (i, 0)),
                  pl.BlockSpec((D, tn), lambda i, j: (0, j)),
                  pl.BlockSpec((1, tn), lambda i, j: (0, j))],
        out_specs=pl.BlockSpec((tm, tn), lambda i, j: (i, j)),
        out_shape=jax.ShapeDtypeStruct((t, n), out_dtype),
        compiler_params=_cparams(("arbitrary", "arbitrary")),
        name="proj",
    )(u2d, w, bias.reshape(1, n))


def _ret_log_gamma(d, h):
    hh = h if d == 0 else RET_H - 1 - h
    return float(np.log(1.0 - 2.0 ** (-5.0 - hh)))


def _ret_tables():
    c = RET_C
    i = np.arange(c, dtype=np.float64)
    dec = np.zeros((2 * RET_H, c, c), np.float32)
    qd = np.zeros((2 * RET_H, c, RET_DK), np.float32)
    kd = np.zeros((2 * RET_H, c, RET_DK), np.float32)
    for d in range(2):
        tau = i if d == 0 else c - 1 - i
        rel = tau[:, None] - tau[None, :]
        for h in range(RET_H):
            lg = _ret_log_gamma(d, h)
            dec[d * RET_H + h] = np.where(rel >= 0, np.exp(np.maximum(rel, 0.0) * lg), 0.0)
            qd[d * RET_H + h] = np.exp((tau + 1.0) * lg)[:, None]
            kd[d * RET_H + h] = np.exp((c - 1.0 - tau) * lg)[:, None]
    return jnp.asarray(dec), jnp.asarray(qd), jnp.asarray(kd)


def _rope(x, cos2, sin2):
    return x * cos2 + pltpu.roll(x, RET_DK // 2, 1) * sin2


def _ret_kernel(qf_ref, kf_ref, vf_ref, qb_ref, kb_ref, vb_ref, cosf_ref, sinf_ref, cosb_ref, sinb_ref,
                dec_ref, qd_ref, kd_ref, s0_ref, of_ref, ob_ref, sfin_ref, s_ref, *, use_rope, n_chunks):
    n = pl.program_id(1)

    @pl.when(n == 0)
    def _():
        s_ref[...] = s0_ref[0]

    ins = ((qf_ref, kf_ref, vf_ref, cosf_ref, sinf_ref), (qb_ref, kb_ref, vb_ref, cosb_ref, sinb_ref))
    outs = (of_ref, ob_ref)
    chains = [(d, h) for d in range(2) for h in range(RET_H)]
    q, k, v = {}, {}, {}
    for d, h in chains:
        q_ref, k_ref, v_ref, cos_ref, sin_ref = ins[d]
        dk = slice(h * RET_DK, (h + 1) * RET_DK)
        q[d, h] = q_ref[0, :, dk].astype(F32)
        k[d, h] = k_ref[0, :, dk].astype(F32) * (RET_DK ** -0.5)
        if use_rope:
            q[d, h] = _rope(q[d, h], cos_ref[...], sin_ref[...])
            k[d, h] = _rope(k[d, h], cos_ref[...], sin_ref[...])
        v[d, h] = v_ref[0, :, h * RET_DV:(h + 1) * RET_DV]
    sc = {ch: _mm_nt(q[ch], k[ch]) * dec_ref[ch[0] * RET_H + ch[1]] for ch in chains}
    s_old = {ch: s_ref[ch[0], ch[1]] for ch in chains}
    o = {ch: _mm(sc[ch], v[ch]) + _mm(q[ch] * qd_ref[ch[0] * RET_H + ch[1]], s_old[ch]) for ch in chains}
    s_new = {ch: s_old[ch] * float(np.exp(RET_C * _ret_log_gamma(*ch)))
             + _mm_tn(k[ch] * kd_ref[ch[0] * RET_H + ch[1]], v[ch]) for ch in chains}
    for d, h in chains:
        outs[d][0, :, h * RET_DV:(h + 1) * RET_DV] = o[d, h].astype(outs[d].dtype)
        s_ref[d, h] = s_new[d, h]

    @pl.when(n == n_chunks - 1)
    def _():
        sfin_ref[0] = s_ref[...]


def _retention(pr, cos2, sin2, s0, use_rope):
    b, l, _ = pr.shape
    c = RET_C
    nc = l // c
    dec, qd, kd = _ret_tables()
    qw, vw = RET_H * RET_DK, RET_H * RET_DV

    def fw(blk):
        return lambda i, n: (i, n, blk)

    def bw(blk):
        return lambda i, n: (i, nc - 1 - n, blk)

    def const(*shape):
        return pl.BlockSpec(shape, lambda i, n: (0,) * len(shape))

    st = pl.BlockSpec((1, 2, RET_H, RET_DK, RET_DV), lambda i, n: (i, 0, 0, 0, 0))
    in_specs = [
        pl.BlockSpec((1, c, qw), fw(0)), pl.BlockSpec((1, c, qw), fw(1)), pl.BlockSpec((1, c, vw), fw(1)),
        pl.BlockSpec((1, c, qw), bw(0)), pl.BlockSpec((1, c, qw), bw(1)), pl.BlockSpec((1, c, vw), bw(1)),
        pl.BlockSpec((c, RET_DK), lambda i, n: (n, 0)), pl.BlockSpec((c, RET_DK), lambda i, n: (n, 0)),
        pl.BlockSpec((c, RET_DK), lambda i, n: (nc - 1 - n, 0)), pl.BlockSpec((c, RET_DK), lambda i, n: (nc - 1 - n, 0)),
        const(2 * RET_H, c, c), const(2 * RET_H, c, RET_DK), const(2 * RET_H, c, RET_DK), st,
    ]
    out_specs = [pl.BlockSpec((1, c, vw), fw(0)), pl.BlockSpec((1, c, vw), bw(0)), st]
    sd = jax.ShapeDtypeStruct
    return pl.pallas_call(
        functools.partial(_ret_kernel, use_rope=use_rope, n_chunks=nc),
        grid=(b, nc),
        in_specs=in_specs,
        out_specs=out_specs,
        out_shape=[sd((b, l, vw), BF16), sd((b, l, vw), BF16), sd((b, 2, RET_H, RET_DK, RET_DV), F32)],
        scratch_shapes=[pltpu.VMEM((2, RET_H, RET_DK, RET_DV), F32)],
        compiler_params=_cparams(("arbitrary", "arbitrary")),
        name="retention",
    )(pr, pr, pr, pr, pr, pr, cos2, sin2, cos2, sin2, dec, qd, kd, s0)


def _gla_boundary(b, lv, reverse, rolls):
    c, w = b.shape
    m = 1 << lv
    if m >= SUBLANES:
        parts = []
        for p0 in range(0, c, 2 * m):
            e = p0 + m if reverse else p0 + m - 1
            parts.append(jnp.broadcast_to(b[e:e + 1, :], (2 * m, w)))
        return jnp.concatenate(parts, axis=0)

    def rolled(s):
        s %= c
        if s not in rolls:
            rolls[s] = b if s == 0 else pltpu.roll(b, s, 0)
        return rolls[s]

    row = lax.broadcasted_iota(jnp.int32, (c, w), 0)
    r = jnp.bitwise_and(row, m - 1)
    upper = jnp.bitwise_and(jnp.right_shift(row, lv), 1) == 1
    out = b
    for t in range(m):
        if reverse:
            out = jnp.where(upper & (r == t), rolled(t), out)
            out = jnp.where(jnp.logical_not(upper) & (r == t), rolled(-(m - t)), out)
        else:
            out = jnp.where(upper & (r == t), rolled(t + 1), out)
            out = jnp.where(jnp.logical_not(upper) & (r == t), rolled(-(m - 1 - t)), out)
    return out


def _gla_kernel(qf_ref, kf_ref, vf_ref, lrf_ref, qb_ref, kb_ref, vb_ref, lrb_ref, w2_ref, b2_ref, s0_ref,
                of_ref, ob_ref, sfin_ref, st_ref, *, n_chunks):
    c = GLA_C
    n = pl.program_id(1)

    @pl.when(n == 0)
    def _():
        st_ref[...] = s0_ref[0]

    ins = ((qf_ref, kf_ref, vf_ref, lrf_ref), (qb_ref, kb_ref, vb_ref, lrb_ref))
    outs = (of_ref, ob_ref)
    hw = GLA_H * GLA_DK
    row = lax.broadcasted_iota(jnp.int32, (c, hw), 0)
    ri = lax.broadcasted_iota(jnp.int32, (c, c), 0)
    ci = lax.broadcasted_iota(jnp.int32, (c, c), 1)
    chains = [(d, h) for d in range(2) for h in range(GLA_H)]

    def dk(h):
        return slice(h * GLA_DK, (h + 1) * GLA_DK)

    def dv(h):
        return slice(h * GLA_DV, (h + 1) * GLA_DV)

    q, k, v, b_inc, b_rest, tot, ql = {}, {}, {}, {}, {}, {}, {}
    for d in range(2):
        reverse = d == 1
        q_ref, k_ref, v_ref, lr_ref = ins[d]
        q[d] = q_ref[0].astype(F32) * (GLA_DK ** -0.5)
        k[d] = k_ref[0].astype(F32)
        v[d] = v_ref[0]
        gate = _log_sigmoid(_mm_split(lr_ref[0], w2_ref[d, 0], w2_ref[d, 1]) + b2_ref[d])
        gate = gate * (1.0 / GLA_GATE_NORM)
        b = _scan_rows(gate, reverse)
        tot[d] = b[0:1] if reverse else b[c - 1:c]
        b_inc[d], b_rest[d] = b, tot[d] - b
        rolls = {}
        for lv in range(GLA_LEVELS):
            be = _gla_boundary(b, lv, reverse, rolls)
            upper = jnp.bitwise_and(jnp.right_shift(row, lv), 1) == 1
            second = jnp.logical_not(upper) if reverse else upper
            e = jnp.exp(-jnp.abs(b - be))
            ql[d, lv] = jnp.where(second, q[d], k[d]) * e

    attn = {(d, h): jnp.where(ri == ci, jnp.sum(q[d][:, dk(h)] * k[d][:, dk(h)], axis=-1, keepdims=True), 0.0)
            for d, h in chains}
    for lv in range(GLA_LEVELS):
        same = jnp.right_shift(ri, lv + 1) == jnp.right_shift(ci, lv + 1)
        row_upper = jnp.bitwise_and(jnp.right_shift(ri, lv), 1) == 1
        col_upper = jnp.bitwise_and(jnp.right_shift(ci, lv), 1) == 1
        pair_mask = (same & row_upper & jnp.logical_not(col_upper), same & jnp.logical_not(row_upper) & col_upper)
        for d, h in chains:
            p = ql[d, lv][:, dk(h)]
            attn[d, h] = attn[d, h] + jnp.where(pair_mask[d], _mm_nt(p, p), 0.0)
    st_old = {ch: st_ref[ch[0], ch[1]] for ch in chains}
    o = {(d, h): _mm(attn[d, h], v[d][:, dv(h)]) + _mm_nt(q[d][:, dk(h)] * jnp.exp(b_inc[d][:, dk(h)]), st_old[d, h])
         for d, h in chains}
    st_new = {(d, h): st_old[d, h] * jnp.exp(tot[d][:, dk(h)])
              + _mm_tn(v[d][:, dv(h)], k[d][:, dk(h)] * jnp.exp(b_rest[d][:, dk(h)])) for d, h in chains}
    for d, h in chains:
        outs[d][0, :, dv(h)] = o[d, h].astype(outs[d].dtype)
        st_ref[d, h] = st_new[d, h]

    @pl.when(n == n_chunks - 1)
    def _():
        sfin_ref[0] = st_ref[...]


def _gla(pg, plr, w2p, b2, s0t):
    b, l, _ = pg.shape
    c = GLA_C
    nc = l // c
    qw, vw = GLA_H * GLA_DK, GLA_H * GLA_DV

    def fw(blk):
        return lambda i, n: (i, n, blk)

    def bw(blk):
        return lambda i, n: (i, nc - 1 - n, blk)

    st = pl.BlockSpec((1, 2, GLA_H, GLA_DV, GLA_DK), lambda i, n: (i, 0, 0, 0, 0))
    in_specs = [
        pl.BlockSpec((1, c, qw), fw(0)), pl.BlockSpec((1, c, qw), fw(1)), pl.BlockSpec((1, c, vw), fw(1)),
        pl.BlockSpec((1, c, LANES), fw(0)),
        pl.BlockSpec((1, c, qw), bw(0)), pl.BlockSpec((1, c, qw), bw(1)), pl.BlockSpec((1, c, vw), bw(1)),
        pl.BlockSpec((1, c, LANES), bw(0)),
        pl.BlockSpec((2, 2, LANES, qw), lambda i, n: (0, 0, 0, 0)),
        pl.BlockSpec((2, 1, qw), lambda i, n: (0, 0, 0)),
        st,
    ]
    out_specs = [pl.BlockSpec((1, c, vw), fw(0)), pl.BlockSpec((1, c, vw), bw(0)), st]
    sd = jax.ShapeDtypeStruct
    return pl.pallas_call(
        functools.partial(_gla_kernel, n_chunks=nc),
        grid=(b, nc),
        in_specs=in_specs,
        out_specs=out_specs,
        out_shape=[sd((b, l, vw), BF16), sd((b, l, vw), BF16), sd((b, 2, GLA_H, GLA_DV, GLA_DK), F32)],
        scratch_shapes=[pltpu.VMEM((2, GLA_H, GLA_DV, GLA_DK), F32)],
        compiler_params=_cparams(("arbitrary", "arbitrary")),
        name="gla",
    )(pg, pg, pg, plr, pg, pg, pg, plr, w2p, b2, s0t)


def _gla_gate_params(w2, b2):
    w2p = jnp.pad(w2, ((0, 0), (0, LANES - GLA_LOWRANK), (0, 0)))
    return jnp.stack(_split2(w2p), axis=1), b2.reshape(2, 1, GLA_H * GLA_DK)


def _head_sum(x):
    i = lax.broadcasted_iota(jnp.int32, (LANES, LANES), 0)
    j = lax.broadcasted_iota(jnp.int32, (LANES, LANES), 1)
    ones = jnp.where(jnp.right_shift(i, RW_N_BITS) == jnp.right_shift(j, RW_N_BITS), 1.0, 0.0).astype(BF16)
    out = []
    for t in range(x.shape[1] // LANES):
        parts = _split2(x[:, t * LANES:(t + 1) * LANES])
        out.append(sum(jnp.dot(p, ones, preferred_element_type=F32) for p in parts))
    return jnp.concatenate(out, axis=1)


RW_TM = 256
HALO = 2 * SUBLANES
N_RW_LORA = RW_W_LORA + RW_A_LORA + RW_G_LORA


def _rw_prep_kernel(x_ref, xp_ref, xn_ref, lo_ref, lop_ref, lon_ref, taps_ref, ltaps_ref, kkp_ref, ka_ref, rk_ref,
                    w0_ref, wb_ref, a0_ref, ab_ref, gb_ref, r_ref, v_ref, kk_ref, lw_ref, kd_ref, be_ref, bonus_ref,
                    g_ref, lbuf, *, n_tiles):
    tm = RW_TM
    j = pl.program_id(1)
    first, last = j == 0, j == n_tiles - 1
    ri = lax.broadcasted_iota(jnp.int32, (tm, tm), 0)
    ci = lax.broadcasted_iota(jnp.int32, (tm, tm), 1)
    pick_prev = jnp.where(ci == ri - 1, 1.0, 0.0).astype(BF16)
    pick_next = jnp.where(ci == ri + 1, 1.0, 0.0).astype(BF16)
    before = jnp.where(first, 0.0, xp_ref[0].astype(F32))[HALO - 1:HALO]
    after = jnp.where(last, 0.0, xn_ref[0].astype(F32))[0:1]
    edge = lax.broadcasted_iota(jnp.int32, (SUBLANES, 1), 0)

    def shifted_main(c0, c1):
        x = x_ref[0, :, c0:c1]
        prev = jnp.dot(pick_prev, x, preferred_element_type=F32)
        nxt = jnp.dot(pick_next, x, preferred_element_type=F32)
        prev = jnp.concatenate([prev[0:SUBLANES] + jnp.where(edge == 0, before[:, c0:c1], 0.0),
                                prev[SUBLANES:]], axis=0)
        nxt = jnp.concatenate([nxt[0:tm - SUBLANES], nxt[tm - SUBLANES:]
                               + jnp.where(edge == SUBLANES - 1, after[:, c0:c1], 0.0)], axis=0)
        return (taps_ref[0:1, c0:c1] * prev + taps_ref[1:2, c0:c1] * x.astype(F32) + taps_ref[2:3, c0:c1] * nxt)

    lbuf[HALO:HALO + tm, :] = lo_ref[0]
    lbuf[0:HALO, :] = jnp.where(first, 0.0, lop_ref[0])
    lbuf[HALO + tm:2 * HALO + tm, :] = jnp.where(last, 0.0, lon_ref[0])
    lora = (ltaps_ref[0:1, :] * lbuf[HALO - 1:HALO - 1 + tm, :] + ltaps_ref[1:2, :] * lbuf[HALO:HALO + tm, :]
            + ltaps_ref[2:3, :] * lbuf[HALO + 1:HALO + 1 + tm, :])
    r = shifted_main(0, D)
    k = shifted_main(D, 2 * D)
    v = shifted_main(2 * D, 3 * D)
    xw = lora[:, 0:RW_W_LORA]
    xa = lora[:, RW_W_LORA:RW_W_LORA + RW_A_LORA]
    xg = lora[:, RW_W_LORA + RW_A_LORA:]
    r_ref[0] = r.astype(r_ref.dtype)
    v_ref[0] = v.astype(v_ref.dtype)
    kk = k * kkp_ref[...]
    kk = kk * lax.rsqrt(jnp.maximum(_head_sum(kk * kk), 1e-24))
    kk_ref[0] = kk.astype(kk_ref.dtype)
    g_ref[0] = _mm(_sigmoid(xg), gb_ref[...]).astype(g_ref.dtype)
    wh = jnp.tanh(xw)
    kd_sum = jnp.zeros_like(k)
    for d in range(2):
        lw_ref[d, 0] = -float(np.exp(-0.5)) * _sigmoid(w0_ref[d:d + 1, :] + _mm(wh, wb_ref[d]))
        a = _sigmoid(a0_ref[d:d + 1, :] + _mm(xa, ab_ref[d]))
        kd = k * (1.0 + (a - 1.0) * ka_ref[...])
        kd_ref[d, 0] = kd.astype(kd_ref.dtype)
        be_ref[d, 0] = (kk * a).astype(be_ref.dtype)
        kd_sum = kd_sum + kd
    bonus_ref[0] = (_head_sum(r * kd_sum * rk_ref[...]) * v).astype(bonus_ref.dtype)


def _rw_prep(rw, rlo, taps, kkp, ka, rk, w0, wb, a0, ab, gb):
    b, l, _ = rw.shape
    tm = RW_TM
    nt = l // tm
    hb = tm // HALO
    tok = pl.BlockSpec((1, tm, D), lambda i, j: (i, j, 0))
    tokd = pl.BlockSpec((2, 1, tm, D), lambda i, j: (0, i, j, 0))

    def full(*shape):
        return pl.BlockSpec(shape, lambda i, j: (0,) * len(shape))

    def halo_specs(n):
        return [pl.BlockSpec((1, tm, n), lambda i, j: (i, j, 0)),
                pl.BlockSpec((1, HALO, n), lambda i, j: (i, jnp.maximum(j * hb - 1, 0), 0)),
                pl.BlockSpec((1, HALO, n), lambda i, j: (i, jnp.minimum((j + 1) * hb, l // HALO - 1), 0))]

    in_specs = halo_specs(3 * D) + halo_specs(N_RW_LORA) + [
        full(3, 3 * D), full(3, N_RW_LORA), full(1, D), full(1, D), full(1, D),
        full(2, D), full(2, RW_W_LORA, D), full(2, D), full(2, RW_A_LORA, D), full(RW_G_LORA, D),
    ]
    sd = jax.ShapeDtypeStruct
    tok_o, dir_o, dir_f32 = sd((b, l, D), BF16), sd((2, b, l, D), BF16), sd((2, b, l, D), F32)
    return pl.pallas_call(
        functools.partial(_rw_prep_kernel, n_tiles=nt),
        grid=(b, nt),
        in_specs=in_specs,
        out_specs=[tok, tok, tok, tokd, tokd, tokd, tok, tok],
        out_shape=[tok_o, tok_o, tok_o, dir_f32, dir_o, dir_o, tok_o, tok_o],
        scratch_shapes=[pltpu.VMEM((tm + 2 * HALO, N_RW_LORA), F32)],
        compiler_params=_cparams(("arbitrary", "arbitrary")),
        name="rwkv_prep",
    )(rw, rw, rw, rlo, rlo, rlo, taps[:, :3 * D], taps[:, 3 * D:], kkp.reshape(1, D), ka.reshape(1, D),
      rk.reshape(1, D), w0, wb, a0, ab, gb)


def _tile_rows(x, n):
    return jnp.concatenate([x] * n, axis=0)


def _rw_scan_kernel(rf_ref, vf_ref, kkf_ref, lwf_ref, kdf_ref, bef_ref, rb_ref, vb_ref, kkb_ref, lwb_ref, kdb_ref,
                    beb_ref, s0_ref, yf_ref, yb_ref, sfin_ref, zt_ref, sv_lhs, sv_w, sv_bk, sv_tot, *, n_tiles):
    c, hw, sw = RW_C, RW_GW, RW_SW
    pk = RW_G * c
    n_groups, n_sub = RW_H // RW_G, RW_GW // RW_SW
    n_chunks = rf_ref.shape[1] // c
    j = pl.program_id(1)

    @pl.when(j == 0)
    def _():
        zt_ref[...] = jnp.zeros_like(zt_ref)
        for bb in range(RW_BB):
            for d in range(2):
                for g in range(RW_H // 2):
                    for jj in range(2):
                        blk = slice(RW_N * jj, RW_N * (jj + 1))
                        zt_ref[bb, d, g, blk, blk] = s0_ref[bb, d, 2 * g + jj]

    ins = ((rf_ref, vf_ref, kkf_ref, lwf_ref, kdf_ref, bef_ref), (rb_ref, vb_ref, kkb_ref, lwb_ref, kdb_ref, beb_ref))
    outs = (yf_ref, yb_ref)
    ri = lax.broadcasted_iota(jnp.int32, (c, pk), 0)
    cs = jnp.bitwise_and(lax.broadcasted_iota(jnp.int32, (c, pk), 1), c - 1)
    eye = (ri == cs).astype(F32)
    strict, incl, lvl, quad = [], [], [], []
    for d in range(2):
        ti = ri if d == 0 else c - 1 - ri
        ts = cs if d == 0 else c - 1 - cs
        strict.append(ts < ti)
        incl.append(ts <= ti)
        same4 = jnp.right_shift(ti, 2) == jnp.right_shift(ts, 2)
        below = jnp.bitwise_and(ti, 3) - jnp.bitwise_and(ts, 3)
        quad.append(tuple(same4 & (below == k) for k in (1, 2, 3)))
        lv_masks = {}
        for lv in range(2, RW_LEVELS):
            same = jnp.right_shift(ti, lv + 1) == jnp.right_shift(ts, lv + 1)
            lower = ((jnp.bitwise_and(jnp.right_shift(ti, lv), 1) == 1)
                     & (jnp.bitwise_and(jnp.right_shift(ts, lv), 1) == 0))
            lv_masks[lv] = same & lower
        lvl.append(lv_masks)
    bi = lax.broadcasted_iota(jnp.int32, (pk, pk), 0)
    bj = lax.broadcasted_iota(jnp.int32, (pk, pk), 1)
    bd_p = jnp.right_shift(bi, RW_LEVELS) == jnp.right_shift(bj, RW_LEVELS)
    zi = lax.broadcasted_iota(jnp.int32, (sw, sw), 0)
    zj = lax.broadcasted_iota(jnp.int32, (sw, sw), 1)
    bd_z = jnp.right_shift(zi, RW_N_BITS) == jnp.right_shift(zj, RW_N_BITS)
    term_row = lax.broadcasted_iota(jnp.int32, (2 * SUBLANES, sw), 0)
    wi = lax.broadcasted_iota(jnp.int32, (pk, hw), 0)
    wj = lax.broadcasted_iota(jnp.int32, (pk, hw), 1)
    bd_w = jnp.right_shift(wi, RW_LEVELS) == jnp.right_shift(wj, RW_N_BITS)

    def bdp(x):
        return jnp.where(bd_p, _tile_rows(x, RW_G), 0.0)

    def bdw(x):
        return jnp.where(bd_w, _tile_rows(x, RW_G), 0.0)

    chains = [(bb, d, half) for bb in range(RW_BB) for d in range(2) for half in range(n_groups)]

    def each(fn):
        return {ch: fn(ch) for ch in chains}

    def chunk_rows(i):
        return (pl.ds(pl.multiple_of(i * c, c), c), pl.ds(pl.multiple_of((n_chunks - 1 - i) * c, c), c))

    def scale(i, slot):
        rows = chunk_rows(i)

        def load(ch):
            bb, d, half = ch
            ln = slice(half * hw, (half + 1) * hw)
            refs = ins[d]
            return (refs[0][bb, rows[d], ln].astype(F32), refs[2][bb, rows[d], ln].astype(F32),
                    refs[3][0, bb, rows[d], ln], refs[4][0, bb, rows[d], ln].astype(F32),
                    refs[5][0, bb, rows[d], ln].astype(F32))

        x = each(load)
        r, kk, lw = each(lambda ch: x[ch][0]), each(lambda ch: x[ch][1]), each(lambda ch: x[ch][2])
        kd, be = each(lambda ch: x[ch][3]), each(lambda ch: x[ch][4])
        b = each(lambda ch: _scan_rows(lw[ch], ch[1] == 1))
        tot = each(lambda ch: b[ch][0:1] if ch[1] == 1 else b[ch][c - 1:c])
        en = each(lambda ch: jnp.exp(-b[ch]))
        es = each(lambda ch: jnp.exp(tot[ch] - b[ch]))
        for n, ch in enumerate(chains):
            sv_lhs[slot, n] = jnp.concatenate([kk[ch] * jnp.exp(b[ch] - lw[ch]), r[ch] * jnp.exp(b[ch])],
                                              axis=0).astype(BF16)
            sv_w[slot, n] = jnp.concatenate([be[ch] * en[ch], kd[ch] * en[ch]], axis=0).astype(BF16)
            sv_bk[slot, n] = jnp.concatenate([be[ch] * es[ch], kd[ch] * es[ch]], axis=0).astype(BF16)
            sv_tot[slot, n] = jnp.broadcast_to(tot[ch], (SUBLANES, hw))

    scale(0, 0)

    def chunk(i, carry):
        rows = chunk_rows(i)
        slot = lax.rem(i, 2)
        index = {ch: n for n, ch in enumerate(chains)}
        lhs = each(lambda ch: sv_lhs[slot, index[ch]])
        w_en = each(lambda ch: sv_w[slot, index[ch]])
        bk = each(lambda ch: sv_bk[slot, index[ch]])
        tot = each(lambda ch: sv_tot[slot, index[ch]][0:1])
        v = each(lambda ch: ins[ch[1]][1][ch[0], rows[ch[1]], ch[2] * hw:(ch[2] + 1) * hw].astype(F32))
        gb = each(lambda ch: _mm_nt(lhs[ch], bdw(w_en[ch][0:c])))
        gk = each(lambda ch: _mm_nt(lhs[ch], bdw(w_en[ch][c:])))
        a_ab = each(lambda ch: jnp.where(strict[ch[1]], gb[ch][0:c], 0.0))
        a_rb = each(lambda ch: jnp.where(incl[ch[1]], gb[ch][c:], 0.0))
        a_ak = each(lambda ch: jnp.where(strict[ch[1]], gk[ch][0:c], 0.0))
        a_rk = each(lambda ch: jnp.where(incl[ch[1]], gk[ch][c:], 0.0))
        def inverse4(ch):
            d = ch[1]
            q1, q2, q3 = quad[d]
            a = jnp.where(q1 | q2 | q3, a_ab[ch], 0.0)
            col = (lambda x, k: pltpu.roll(x, pk - k, 1)) if d == 0 else (lambda x, k: pltpu.roll(x, k, 1))
            row = (lambda x, k: pltpu.roll(x, k, 0)) if d == 0 else (lambda x, k: pltpu.roll(x, c - k, 0))
            c1, c2, r1, r2 = col(a, 1), col(a, 2), row(a, 1), row(a, 2)
            two = jnp.where(q2, c1 * r1, 0.0)
            three = jnp.where(q3, c1 * r2 + c2 * r1 - c2 * row(c1, 1) * r2, 0.0)
            return eye - a + two + three

        t_inv = each(inverse4)
        for lv in range(2, RW_LEVELS):
            xm = each(lambda ch: _mm(jnp.where(lvl[ch[1]][lv], a_ab[ch], 0.0), bdp(t_inv[ch])))
            t_inv = each(lambda ch: t_inv[ch] - _mm(t_inv[ch], bdp(xm[ch])))
            if lv == 2:
                scale(jnp.minimum(i + 1, n_chunks - 1), 1 - slot)
        zt_old = {(ch, g): zt_ref[ch[0], ch[1], n_sub * ch[2] + g] for ch in chains for g in range(n_sub)}
        zz = each(lambda ch: jnp.concatenate(
            [_mm(lhs[ch][:, g * sw:(g + 1) * sw], zt_old[ch, g]) for g in range(n_sub)], axis=1))
        av = each(lambda ch: _mm(jnp.concatenate([a_ak[ch], a_rk[ch]], axis=0), bdw(v[ch])))
        u = each(lambda ch: _mm(t_inv[ch], bdw(-(zz[ch][0:c] + av[ch][0:c]))))
        y = each(lambda ch: zz[ch][c:] + _mm(a_rb[ch], bdw(u[ch])) + av[ch][c:])
        uv = each(lambda ch: jnp.concatenate([u[ch], v[ch]], axis=0))

        def decay_cols(ch, gl):
            x = tot[ch][:, gl]
            hi = x.astype(BF16).astype(F32)
            mid = (x - hi).astype(BF16).astype(F32)
            lo = x - hi - mid
            terms = jnp.where(term_row == 0, hi, jnp.where(term_row == 1, mid, jnp.where(term_row == 2, lo, 0.0)))
            return jnp.exp(_mm_tn(terms, jnp.ones((2 * SUBLANES, sw), BF16)))

        zt_new = {}
        for ch in chains:
            for g in range(n_sub):
                gl = slice(g * sw, (g + 1) * sw)
                upd = jnp.where(bd_z, _mm_tn(bk[ch][:, gl], uv[ch][:, gl]), 0.0)
                zt_new[ch, g] = zt_old[ch, g] * decay_cols(ch, gl) + upd
        for ch in chains:
            bb, d, half = ch
            outs[d][bb, rows[d], half * hw:(half + 1) * hw] = y[ch].astype(outs[d].dtype)
            for g in range(n_sub):
                zt_ref[bb, d, n_sub * half + g] = zt_new[ch, g]
        return carry

    lax.fori_loop(0, n_chunks, chunk, 0)

    @pl.when(j == n_tiles - 1)
    def _():
        for bb in range(RW_BB):
            for d in range(2):
                for g in range(RW_H // 2):
                    for jj in range(2):
                        blk = slice(RW_N * jj, RW_N * (jj + 1))
                        sfin_ref[bb, d, 2 * g + jj] = zt_ref[bb, d, g, blk, blk]


def _rw_scan(r, v, kk, lw, kd, be, s0):
    b, l, _ = r.shape
    tl, nb = min(RW_TL, l), RW_BB
    nt = l // tl
    n_ch = nb * 2 * (RW_H // RW_G)
    tok_f = pl.BlockSpec((nb, tl, D), lambda i, j: (i, j, 0))
    tok_b = pl.BlockSpec((nb, tl, D), lambda i, j: (i, nt - 1 - j, 0))
    dir_f = pl.BlockSpec((1, nb, tl, D), lambda i, j: (0, i, j, 0))
    dir_b = pl.BlockSpec((1, nb, tl, D), lambda i, j: (1, i, nt - 1 - j, 0))
    st = pl.BlockSpec((nb, 2, RW_H, RW_N, RW_N), lambda i, j: (i, 0, 0, 0, 0))
    sd = jax.ShapeDtypeStruct
    return pl.pallas_call(
        functools.partial(_rw_scan_kernel, n_tiles=nt),
        grid=(b // nb, nt),
        in_specs=[tok_f, tok_f, tok_f, dir_f, dir_f, dir_f, tok_b, tok_b, tok_b, dir_b, dir_b, dir_b, st],
        out_specs=[tok_f, tok_b, st],
        out_shape=[sd((b, l, D), BF16), sd((b, l, D), BF16), sd((b, 2, RW_H, RW_N, RW_N), F32)],
        scratch_shapes=[pltpu.VMEM((nb, 2, RW_H // 2, RW_SW, RW_SW), F32),
                        pltpu.VMEM((2, n_ch, 2 * RW_C, RW_GW), BF16),
                        pltpu.VMEM((2, n_ch, 2 * RW_C, RW_GW), BF16),
                        pltpu.VMEM((2, n_ch, 2 * RW_C, RW_GW), BF16),
                        pltpu.VMEM((2, n_ch, SUBLANES, RW_GW), F32)],
        compiler_params=_cparams(("arbitrary", "arbitrary")),
        name="rwkv_scan",
    )(r, v, kk, lw, kd, be, r, v, kk, lw, kd, be, s0)


MERGE_TM = 256
ROUTER_LANES = LANES


def _merge_kernel(orf_ref, orb_ref, rg_ref, ogf_ref, ogb_ref, gg_ref, yf_ref, yb_ref, bonus_ref, grw_ref,
                  gates_ref, x_ref, gt_ref, sh_ref, sc_ref, wbr_ref, wout_ref, rgn_ref, ggn_ref, gng_ref, gnb_ref,
                  l1g_ref, l1b_ref, wr_ref, br_ref, x1_ref, u2_ref, lg_ref):
    f32 = lambda ref: ref[0].astype(F32)
    o_ret = f32(orf_ref) + f32(orb_ref)
    o_gla = f32(ogf_ref) + f32(ogb_ref)
    y_ret, y_gla = [], []
    for h in range(RET_H):
        hs = slice(h * RET_DV, (h + 1) * RET_DV)
        y_ret.append(_ln(o_ret[:, hs]))
        og = o_gla[:, hs]
        y_gla.append(og * lax.rsqrt(jnp.mean(og * og, axis=-1, keepdims=True) + EPS))
    z_ret = _silu(f32(rg_ref)) * (jnp.concatenate(y_ret, axis=1) * rgn_ref[...])
    z_gla = _silu(f32(gg_ref)) * (jnp.concatenate(y_gla, axis=1) * ggn_ref[...])
    y = f32(yf_ref) + f32(yb_ref)
    mu = _head_sum(y) * (1.0 / RW_N)
    yc = y - mu
    var = _head_sum(yc * yc) * (1.0 / RW_N)
    z_rw = (yc * lax.rsqrt(var + EPS) * gng_ref[...] + gnb_ref[...] + f32(bonus_ref)) * f32(grw_ref)
    gates = f32(gates_ref)
    mixed = (gates[:, 0:D] * _mm(z_ret, wbr_ref[0]) + gates[:, D:2 * D] * _mm(z_gla, wbr_ref[1])
             + gates[:, 2 * D:] * _mm(z_rw, wbr_ref[2]))
    mix = _mm(mixed, wout_ref[...])
    x1 = _ln(ALPHA * x_ref[0] + gt_ref[0] * mix) * l1g_ref[...] + l1b_ref[...]
    x1_ref[0] = x1
    u2 = _ln(x1) * (1.0 + sc_ref[0]) + sh_ref[0]
    u2_ref[0] = u2.astype(u2_ref.dtype)
    lg_ref[...] = (_mm_split(u2, wr_ref[0], wr_ref[1]) + br_ref[...]).T


def _merge(orf, orb, pr, ogf, ogb, pg, yf, yb, bonus, grw, gates, x, gt1, sh2, sc2, wbr, wout, rgn, ggn, gng, gnb,
           l1g, l1b, wr, br):
    b, l, _ = x.shape
    tm = MERGE_TM
    tok = pl.BlockSpec((1, tm, D), lambda i, j: (i, j, 0))
    gate_blk = pl.BlockSpec((1, tm, D), lambda i, j: (i, j, 2))
    vec = pl.BlockSpec((1, 1, D), lambda i, j: (i, 0, 0))
    row = pl.BlockSpec((1, D), lambda i, j: (0, 0))
    in_specs = [
        tok, tok, gate_blk, tok, tok, gate_blk, tok, tok, tok, tok,
        pl.BlockSpec((1, tm, 3 * D), lambda i, j: (i, j, 0)),
        tok, vec, vec, vec,
        pl.BlockSpec((3, D, D), lambda i, j: (0, 0, 0)),
        pl.BlockSpec((D, D), lambda i, j: (0, 0)),
        row, row, row, row, row, row,
        pl.BlockSpec((2, D, ROUTER_LANES), lambda i, j: (0, 0, 0)),
        pl.BlockSpec((1, ROUTER_LANES), lambda i, j: (0, 0)),
    ]
    sd = jax.ShapeDtypeStruct
    r1 = lambda a: a.reshape(1, D)
    return pl.pallas_call(
        _merge_kernel,
        grid=(b, l // tm),
        in_specs=in_specs,
        out_specs=[tok, tok, pl.BlockSpec((ROUTER_LANES, tm), lambda i, j: (0, i * (l // tm) + j))],
        out_shape=[sd((b, l, D), F32), sd((b, l, D), BF16), sd((ROUTER_LANES, b * l), F32)],
        compiler_params=_cparams(("arbitrary", "arbitrary")),
        name="merge",
    )(orf, orb, pr, ogf, ogb, pg, yf, yb, bonus, grw, gates, x, gt1, sh2, sc2, wbr, wout,
      r1(rgn), r1(ggn), r1(gng), r1(gnb), r1(l1g), r1(l1b), wr, br)


MOE_TM = 1024


def _moe_block_rows(tm):
    rows = tm / N_GROUPS + 4.0 * np.sqrt(tm * (N_GROUPS - 1.0)) / N_GROUPS
    bf16_tile = 2 * SUBLANES
    return int(min(tm, -(-rows // bf16_tile) * bf16_tile))


def _first_argmax_rows(x, row):
    m = jnp.max(x, axis=0, keepdims=True)
    idx = jnp.min(jnp.where(x == m, row, x.shape[0]), axis=0, keepdims=True)
    return m, idx


def _routing(lt):
    tm = lt.shape[1]
    gl = lt[0:N_GROUPS]
    gmax, gidx = _first_argmax_rows(gl, lax.broadcasted_iota(jnp.int32, (N_GROUPS, tm), 0))
    g_w = 1.0 / jnp.sum(jnp.exp(gl - gmax), axis=0, keepdims=True)
    row = lax.broadcasted_iota(jnp.int32, (N_EXPERTS, tm), 0)
    neg = -jnp.inf
    el = jnp.where(jnp.right_shift(row, EPG_BITS) == gidx, lt[N_GROUPS:N_GROUPS + N_EXPERTS], neg)
    m1, i1 = _first_argmax_rows(el, row)
    m2, i2 = _first_argmax_rows(jnp.where(row == i1, neg, el), row)
    e2 = jnp.exp(m2 - m1)
    w1 = 1.0 / (1.0 + e2)
    return gidx, g_w * (jnp.where(row == i1, w1, 0.0) + jnp.where(row == i2, e2 * w1, 0.0))


def _moe_kernel(u_ref, lt_ref, wg_ref, wu_ref, wd_ref, o_ref, acc_ref):
    g = pl.program_id(1)
    tm = acc_ref.shape[0]
    blk = _moe_block_rows(tm)

    @pl.when(g == 0)
    def _():
        acc_ref[...] = jnp.zeros_like(acc_ref)

    gidx, comb = _routing(lt_ref[...])
    member = jnp.broadcast_to((gidx == g).astype(F32), (SUBLANES, tm))
    lane = lax.broadcasted_iota(jnp.int32, (SUBLANES, tm), 1)
    count = member
    sh = 1
    while sh < tm:
        count = count + jnp.where(lane >= sh, pltpu.roll(count, sh, 1), 0.0)
        sh *= 2
    pos = jnp.where(member > 0.0, count - 1.0, -1.0).astype(jnp.int32)[0:1]
    n_tok = jnp.max(count).astype(jnp.int32)
    comb_pad = jnp.concatenate([comb, jnp.zeros((LANES - N_EXPERTS, tm), F32)], axis=0)
    comb_parts = _split2(comb_pad)
    u = u_ref[...]
    sel_lane = lax.broadcasted_iota(jnp.int32, (blk, LANES), 1)

    def body(i, carry):
        slot = lax.broadcasted_iota(jnp.int32, (blk, tm), 0) + i * blk
        onehot = jnp.where(slot == pos, 1.0, 0.0).astype(BF16)
        xg = jnp.dot(onehot, u, preferred_element_type=F32).astype(BF16)
        cw = sum(lax.dot_general(onehot, c, (((1,), (1,)), ((), ())), preferred_element_type=F32)
                 for c in comb_parts)
        y = jnp.zeros((blk, D), F32)
        for e in range(EPG):
            hid = (_silu(jnp.dot(xg, wg_ref[e], preferred_element_type=F32))
                   * jnp.dot(xg, wu_ref[e], preferred_element_type=F32))
            c_e = jnp.sum(jnp.where(sel_lane == g * EPG + e, cw, 0.0), axis=-1, keepdims=True)
            y = y + c_e * _mm(hid, wd_ref[e])
        acc_ref[...] += _mm_tn(onehot, y)
        return carry

    lax.fori_loop(0, (n_tok + blk - 1) // blk, body, 0)

    @pl.when(g == N_GROUPS - 1)
    def _():
        o_ref[...] = acc_ref[...].astype(o_ref.dtype)


def _moe(u2, logits_t, wg, wu, wd):
    b, l, _ = u2.shape
    t = b * l
    tm = min(MOE_TM, t)
    tok = pl.BlockSpec((tm, D), lambda i, g: (i, 0))
    in_specs = [
        tok,
        pl.BlockSpec((ROUTER_LANES, tm), lambda i, g: (0, i)),
        pl.BlockSpec((EPG, D, EXPERT_FF), lambda i, g: (g, 0, 0)),
        pl.BlockSpec((EPG, D, EXPERT_FF), lambda i, g: (g, 0, 0)),
        pl.BlockSpec((EPG, EXPERT_FF, D), lambda i, g: (g, 0, 0)),
    ]
    return pl.pallas_call(
        _moe_kernel,
        grid=(t // tm, N_GROUPS),
        in_specs=in_specs,
        out_specs=tok,
        out_shape=jax.ShapeDtypeStruct((t, D), BF16),
        scratch_shapes=[pltpu.VMEM((tm, D), F32)],
        compiler_params=_cparams(("arbitrary",) * 2),
        name="moe",
    )(u2.reshape(t, D), logits_t, wg, wu, wd).reshape(b, l, D)


def _post_kernel(x_ref, m_ref, gt_ref, l2g_ref, l2b_ref, sh_ref, sc_ref, o_ref, u_ref):
    x2 = _ln(ALPHA * x_ref[0] + gt_ref[0] * m_ref[0].astype(F32)) * l2g_ref[...] + l2b_ref[...]
    o_ref[0] = x2
    u_ref[0] = (_ln(x2) * (1.0 + sc_ref[0]) + sh_ref[0]).astype(u_ref.dtype)


def _post(x1, moe_out, gt2, l2g, l2b, sh_next, sc_next):
    b, l, _ = x1.shape
    tm = 512 if l % 512 == 0 else 256
    tok = pl.BlockSpec((1, tm, D), lambda i, j: (i, j, 0))
    vec = pl.BlockSpec((1, 1, D), lambda i, j: (i, 0, 0))
    row = pl.BlockSpec((1, D), lambda i, j: (0, 0))
    sd = jax.ShapeDtypeStruct
    return pl.pallas_call(
        _post_kernel,
        grid=(b, l // tm),
        in_specs=[tok, tok, vec, row, row, vec, vec],
        out_specs=[tok, tok],
        out_shape=[sd((b, l, D), F32), sd((b, l, D), BF16)],
        compiler_params=_cparams(("arbitrary", "arbitrary")),
        name="post_ln",
    )(x1, moe_out, gt2, l2g.reshape(1, D), l2b.reshape(1, D), sh_next, sc_next)


def _rope_tables(l):
    t = np.arange(l)
    quarter = RET_DK // 4
    freqs = (np.float32(ROPE_BASE) ** (-np.arange(quarter, dtype=np.float32) / quarter)).astype(np.float32)
    rows = (t // GRID_W).astype(np.float32)
    cols = (t % GRID_W).astype(np.float32)
    ang = jnp.asarray(np.concatenate([rows[:, None] * freqs, cols[:, None] * freqs], -1))
    cos, sin = jnp.cos(ang), jnp.sin(ang)
    return jnp.concatenate([cos, cos], -1), jnp.concatenate([-sin, sin], -1)


def _layer_weights(p):
    n_rg = 2 * (RET_H * RET_DK + RET_H * RET_DV)
    w_in = p["w_in"]
    w_lr = jnp.pad(w_in[:, 2 * n_rg:2 * n_rg + GLA_LOWRANK], ((0, 0), (0, LANES - GLA_LOWRANK)))
    w2p, b2p = _gla_gate_params(p["gla_w2"], p["gla_b"])
    pad = ROUTER_LANES - N_GROUPS - N_EXPERTS
    return dict(
        p,
        w_ret=w_in[:, :n_rg].astype(BF16),
        w_gla=w_in[:, n_rg:2 * n_rg].astype(BF16),
        w_glr=w_lr.astype(BF16),
        w_rw=w_in[:, 2 * n_rg + GLA_LOWRANK:2 * n_rg + GLA_LOWRANK + 3 * D].astype(BF16),
        w_rlo=w_in[:, 2 * n_rg + GLA_LOWRANK + 3 * D:].astype(BF16),
        w_merge=p["w_merge"].astype(BF16),
        w_br=p["w_br"].astype(BF16),
        w_out=p["w_out"].astype(BF16),
        gla_w2p=w2p, gla_b2p=b2p,
        w_router=jnp.stack(_split2(jnp.pad(jnp.concatenate([p["w_rg"], p["w_re"]], axis=1), ((0, 0), (0, pad))))),
        b_router=jnp.pad(jnp.concatenate([p["b_rg"], p["b_re"]]), (0, pad)).reshape(1, ROUTER_LANES),
        w_eg=p["w_eg"].astype(BF16), w_eu=p["w_eu"].astype(BF16), w_ed=p["w_ed"].astype(BF16),
    )


def _split_mod(mod):
    return [m.reshape(mod.shape[0], 1, D) for m in jnp.split(mod, 6, axis=-1)]


def _layer(x, u, mod, mod_next, s_ret, s_gla_t, s_rw, rope, p):
    b, l, _ = x.shape
    _, _, gt1, sh2, sc2, gt2 = _split_mod(mod)
    sh_next, sc_next = _split_mod(mod_next)[:2]
    u2d = u.reshape(b * l, D)

    def proj(w, tn, bias=None, **kw):
        bias = jnp.zeros((w.shape[1],), F32) if bias is None else bias
        return _proj(u2d, w, bias, tn, **kw).reshape(b, l, -1)

    pr = proj(p["w_ret"], 1024)
    pg = proj(p["w_gla"], 1024)
    plr = proj(p["w_glr"], LANES, out_dtype=F32)
    rw = proj(p["w_rw"], 1024)
    rlo = proj(p["w_rlo"], N_RW_LORA, out_dtype=F32)
    gates = proj(p["w_merge"], 1024, bias=p["b_merge"], act="sigmoid")

    cos2, sin2 = rope if rope is not None else (jnp.ones((l, RET_DK), F32), jnp.zeros((l, RET_DK), F32))
    orf, orb, ret_fin = _retention(pr, cos2, sin2, s_ret, rope is not None)
    ogf, ogb, gla_fin_t = _gla(pg, plr, p["gla_w2p"], p["gla_b2p"], s_gla_t)
    r, v, kk, lw, kd, be, bonus, grw = _rw_prep(rw, rlo, p["rwkv_shift"], p["rwkv_kk"], p["rwkv_ka"],
                                                p["rwkv_rk"].reshape(-1), p["rwkv_w0"], p["rwkv_wb"],
                                                p["rwkv_a0"], p["rwkv_ab"], p["rwkv_gb"])
    yf, yb, rw_fin = _rw_scan(r, v, kk, lw, kd, be, s_rw)
    x1, u2, logits_t = _merge(orf, orb, pr, ogf, ogb, pg, yf, yb, bonus, grw, gates, x, gt1, sh2, sc2,
                              p["w_br"], p["w_out"], p["ret_gn"], p["gla_gn"], p["rwkv_gn_g"], p["rwkv_gn_b"],
                              p["ln1_g"], p["ln1_b"], p["w_router"], p["b_router"])
    moe_out = _moe(u2, logits_t, p["w_eg"], p["w_eu"], p["w_ed"])
    x2, u_next = _post(x1, moe_out, gt2, p["ln2_g"], p["ln2_b"], sh_next, sc_next)
    return x2, u_next, (ret_fin, gla_fin_t, rw_fin)


_PARAM_NAMES = ("w_in", "rwkv_shift", "ret_gn", "gla_w2", "gla_b", "gla_gn", "rwkv_w0", "rwkv_wb", "rwkv_a0",
                "rwkv_ab", "rwkv_gb", "rwkv_kk", "rwkv_ka", "rwkv_rk", "rwkv_gn_g", "rwkv_gn_b", "w_br", "w_merge",
                "b_merge", "w_out", "ln1_g", "ln1_b", "ln2_g", "ln2_b", "w_rg", "b_rg", "w_re", "b_re", "w_eg",
                "w_eu", "w_ed")


def kernel(x_prompt, x_sample, state_ret, state_gla, state_rwkv, c, c_ctx, w_ada, b_ada, w_in, rwkv_shift, ret_gn,
           gla_w2, gla_b, gla_gn, rwkv_w0, rwkv_wb, rwkv_a0, rwkv_ab, rwkv_gb, rwkv_kk, rwkv_ka, rwkv_rk, rwkv_gn_g,
           rwkv_gn_b, w_br, w_merge, b_merge, w_out, ln1_g, ln1_b, ln2_g, ln2_b, w_rg, b_rg, w_re, b_re, w_eg, w_eu,
           w_ed):
    params = dict(zip(_PARAM_NAMES, (w_in, rwkv_shift, ret_gn, gla_w2, gla_b, gla_gn, rwkv_w0, rwkv_wb, rwkv_a0,
                                     rwkv_ab, rwkv_gb, rwkv_kk, rwkv_ka, rwkv_rk, rwkv_gn_g, rwkv_gn_b, w_br, w_merge,
                                     b_merge, w_out, ln1_g, ln1_b, ln2_g, ln2_b, w_rg, b_rg, w_re, b_re, w_eg, w_eu,
                                     w_ed)))
    bc, bl = x_prompt.shape[0], x_sample.shape[0]
    rope = _rope_tables(x_sample.shape[1])
    mod_rows = 2 * SUBLANES
    c_all = jnp.concatenate([c, c_ctx[None, :], jnp.zeros((mod_rows - bl - 1, D), F32)], axis=0)
    z_ret = jnp.zeros((bc, 2, RET_H, RET_DK, RET_DV), F32)
    z_gla_t = jnp.zeros((bc, 2, GLA_H, GLA_DV, GLA_DK), F32)
    z_rw = jnp.zeros((bc, 2, RW_H, RW_N, RW_N), F32)
    mods = [_modulation(c_all, w_ada[layer], b_ada[layer]) for layer in range(DEPTH)]
    mods.append(jnp.zeros_like(mods[0]))
    mods_lat = [m[:bl] for m in mods]
    mods_ctx = [jnp.broadcast_to(m[bl:bl + 1], (bc, 6 * D)) for m in mods]
    h_ctx, h_lat = x_prompt, x_sample
    sh, sc = _split_mod(mods_ctx[0])[:2]
    u_ctx = _lnmod(h_ctx, sh, sc)
    sh, sc = _split_mod(mods_lat[0])[:2]
    u_lat = _lnmod(h_lat, sh, sc)
    new_ret, new_gla, new_rw = [], [], []
    for layer in range(DEPTH):
        p = _layer_weights({k: v[layer] for k, v in params.items()})
        h_ctx, u_ctx, (s_ret, s_gla_t, s_rw) = _layer(h_ctx, u_ctx, mods_ctx[layer], mods_ctx[layer + 1],
                                                      z_ret, z_gla_t, z_rw, None, p)
        new_ret.append(s_ret)
        new_gla.append(jnp.swapaxes(s_gla_t, -1, -2))
        new_rw.append(jnp.swapaxes(s_rw, -1, -2))
        h_lat, u_lat, _ = _layer(h_lat, u_lat, mods_lat[layer], mods_lat[layer + 1], state_ret[:, layer],
                                 jnp.swapaxes(state_gla[:, layer], -1, -2),
                                 jnp.swapaxes(state_rwkv[:, layer], -1, -2), rope, p)
    return (h_ctx, h_lat, jnp.stack(new_ret, axis=1), jnp.stack(new_gla, axis=1), jnp.stack(new_rw, axis=1))
```

```python
import functools

import jax
import jax.numpy as jnp
import numpy as np
from jax import lax
from jax.experimental import pallas as pl
from jax.experimental.pallas import tpu as pltpu

F32 = jnp.float32
BF16 = jnp.bfloat16
HI = lax.Precision.HIGHEST

D = 1024
DEPTH = 2
GRID_W = 64
RET_H, RET_DK, RET_DV = 4, 128, 256
GLA_H, GLA_DK, GLA_DV = 4, 128, 256
GLA_LOWRANK = 16
GLA_GATE_NORM = 16.0
RW_H, RW_N = 16, 64
RW_W_LORA, RW_A_LORA, RW_G_LORA = 64, 64, 128
N_GROUPS, EPG, N_EXPERTS, EXPERT_FF = 4, 4, 16, 512
ALPHA = (2 * DEPTH) ** 0.25
EPS = 1e-5
ROPE_BASE = 10000.0

LANES = 128
SUBLANES = 8
VMEM_LIMIT = 56 * 1024 * 1024

RET_C = 256
GLA_C = 64
GLA_LEVELS = GLA_C.bit_length() - 1
RW_C = 32
RW_LEVELS = RW_C.bit_length() - 1
RW_N_BITS = RW_N.bit_length() - 1
EPG_BITS = EPG.bit_length() - 1
RW_TL = 256
RW_BB = 2
RW_G = 4
RW_GW = RW_G * RW_N
RW_SW = 2 * RW_N


def _cparams(sem):
    return pltpu.CompilerParams(dimension_semantics=sem, vmem_limit_bytes=VMEM_LIMIT)


def _mm(a, b):
    return jnp.dot(a.astype(BF16), b.astype(BF16), preferred_element_type=F32)


def _mm_nt(a, b):
    return lax.dot_general(a.astype(BF16), b.astype(BF16), (((1,), (1,)), ((), ())), preferred_element_type=F32)


def _mm_tn(a, b):
    return lax.dot_general(a.astype(BF16), b.astype(BF16), (((0,), (0,)), ((), ())), preferred_element_type=F32)


def _mm_hi(a, b):
    return jnp.dot(a, b, preferred_element_type=F32, precision=HI)


def _split2(x):
    hi = x.astype(BF16)
    return hi, (x - hi.astype(F32)).astype(BF16)


def _mm_split(a, b_hi, b_lo):
    a_hi, a_lo = _split2(a)
    dot = functools.partial(jnp.dot, preferred_element_type=F32)
    return dot(a_hi, b_hi) + (dot(a_hi, b_lo) + dot(a_lo, b_hi))


def _sigmoid(x):
    return 1.0 / (1.0 + jnp.exp(-x))


def _silu(x):
    return x * _sigmoid(x)


def _log_sigmoid(x):
    return jnp.minimum(x, 0.0) - jnp.log(1.0 + jnp.exp(-jnp.abs(x)))


def _ln(x):
    mu = jnp.mean(x, axis=-1, keepdims=True)
    xc = x - mu
    var = jnp.mean(xc * xc, axis=-1, keepdims=True)
    return xc * lax.rsqrt(var + EPS)


def _scan_rows(x, reverse):
    n = x.shape[0]
    row = lax.broadcasted_iota(jnp.int32, x.shape, 0)
    sh = 1
    while sh < n:
        if reverse:
            x = x + jnp.where(row < n - sh, pltpu.roll(x, n - sh, 0), 0.0)
        else:
            x = x + jnp.where(row >= sh, pltpu.roll(x, sh, 0), 0.0)
        sh *= 2
    return x


def _mod_kernel(c_ref, w_ref, b_ref, o_ref):
    o_ref[...] = _mm_hi(_silu(c_ref[...]), w_ref[...]) + b_ref[...]


def _modulation(c_all, w, b):
    m, n, tn = c_all.shape[0], w.shape[1], 512
    return pl.pallas_call(
        _mod_kernel,
        grid=(n // tn,),
        in_specs=[pl.BlockSpec((m, D), lambda j: (0, 0)),
                  pl.BlockSpec((D, tn), lambda j: (0, j)),
                  pl.BlockSpec((1, tn), lambda j: (0, j))],
        out_specs=pl.BlockSpec((m, tn), lambda j: (0, j)),
        out_shape=jax.ShapeDtypeStruct((m, n), F32),
        compiler_params=_cparams(("arbitrary",)),
        name="adaln_mod",
    )(c_all, w, b.reshape(1, n))


def _lnmod_kernel(x_ref, sh_ref, sc_ref, u_ref):
    u_ref[0] = (_ln(x_ref[0]) * (1.0 + sc_ref[0]) + sh_ref[0]).astype(u_ref.dtype)


def _lnmod(x, sh, sc):
    b, l, _ = x.shape
    tm = 512 if l % 512 == 0 else 256
    vec = pl.BlockSpec((1, 1, D), lambda i, j: (i, 0, 0))
    return pl.pallas_call(
        _lnmod_kernel,
        grid=(b, l // tm),
        in_specs=[pl.BlockSpec((1, tm, D), lambda i, j: (i, j, 0)), vec, vec],
        out_specs=pl.BlockSpec((1, tm, D), lambda i, j: (i, j, 0)),
        out_shape=jax.ShapeDtypeStruct((b, l, D), BF16),
        compiler_params=_cparams(("arbitrary", "arbitrary")),
        name="ln_mod",
    )(x, sh, sc)


def _proj_kernel(u_ref, w_ref, b_ref, o_ref, *, act):
    y = jnp.dot(u_ref[...], w_ref[...], preferred_element_type=F32) + b_ref[...]
    if act == "sigmoid":
        y = _sigmoid(y)
    o_ref[...] = y.astype(o_ref.dtype)


def _proj(u2d, w, bias, tn, act=None, out_dtype=BF16):
    t, n = u2d.shape[0], w.shape[1]
    tm = min(t, 2048)
    return pl.pallas_call(
        functools.partial(_proj_kernel, act=act),
        grid=(t // tm, n // tn),
        in_specs=[pl.BlockSpec((tm, D), lambda i, j: (i, 0)),
                  pl.BlockSpec((D, tn), lambda i, j: (0, j)),
                  pl.BlockSpec((1, tn), lambda i, j: (0, j))],
        out_specs=pl.BlockSpec((tm, tn), lambda i, j: (i, j)),
        out_shape=jax.ShapeDtypeStruct((t, n), out_dtype),
        compiler_params=_cparams(("arbitrary", "arbitrary")),
        name="proj",
    )(u2d, w, bias.reshape(1, n))


def _ret_log_gamma(d, h):
    hh = h if d == 0 else RET_H - 1 - h
    return float(np.log(1.0 - 2.0 ** (-5.0 - hh)))


def _ret_tables():
    c = RET_C
    i = np.arange(c, dtype=np.float64)
    dec = np.zeros((2 * RET_H, c, c), np.float32)
    qd = np.zeros((2 * RET_H, c, RET_DK), np.float32)
    kd = np.zeros((2 * RET_H, c, RET_DK), np.float32)
    for d in range(2):
        tau = i if d == 0 else c - 1 - i
        rel = tau[:, None] - tau[None, :]
        for h in range(RET_H):
            lg = _ret_log_gamma(d, h)
            dec[d * RET_H + h] = np.where(rel >= 0, np.exp(np.maximum(rel, 0.0) * lg), 0.0)
            qd[d * RET_H + h] = np.exp((tau + 1.0) * lg)[:, None]
            kd[d * RET_H + h] = np.exp((c - 1.0 - tau) * lg)[:, None]
    return jnp.asarray(dec), jnp.asarray(qd), jnp.asarray(kd)


def _rope(x, cos2, sin2):
    return x * cos2 + pltpu.roll(x, RET_DK // 2, 1) * sin2


def _ret_kernel(qf_ref, kf_ref, vf_ref, qb_ref, kb_ref, vb_ref, cosf_ref, sinf_ref, cosb_ref, sinb_ref,
                dec_ref, qd_ref, kd_ref, s0_ref, of_ref, ob_ref, sfin_ref, s_ref, *, use_rope, n_chunks):
    n = pl.program_id(1)

    @pl.when(n == 0)
    def _():
        s_ref[...] = s0_ref[0]

    ins = ((qf_ref, kf_ref, vf_ref, cosf_ref, sinf_ref), (qb_ref, kb_ref, vb_ref, cosb_ref, sinb_ref))
    outs = (of_ref, ob_ref)
    chains = [(d, h) for d in range(2) for h in range(RET_H)]
    q, k, v = {}, {}, {}
    for d, h in chains:
        q_ref, k_ref, v_ref, cos_ref, sin_ref = ins[d]
        dk = slice(h * RET_DK, (h + 1) * RET_DK)
        q[d, h] = q_ref[0, :, dk].astype(F32)
        k[d, h] = k_ref[0, :, dk].astype(F32) * (RET_DK ** -0.5)
        if use_rope:
            q[d, h] = _rope(q[d, h], cos_ref[...], sin_ref[...])
            k[d, h] = _rope(k[d, h], cos_ref[...], sin_ref[...])
        v[d, h] = v_ref[0, :, h * RET_DV:(h + 1) * RET_DV]
    sc = {ch: _mm_nt(q[ch], k[ch]) * dec_ref[ch[0] * RET_H + ch[1]] for ch in chains}
    s_old = {ch: s_ref[ch[0], ch[1]] for ch in chains}
    o = {ch: _mm(sc[ch], v[ch]) + _mm(q[ch] * qd_ref[ch[0] * RET_H + ch[1]], s_old[ch]) for ch in chains}
    s_new = {ch: s_old[ch] * float(np.exp(RET_C * _ret_log_gamma(*ch)))
             + _mm_tn(k[ch] * kd_ref[ch[0] * RET_H + ch[1]], v[ch]) for ch in chains}
    for d, h in chains:
        outs[d][0, :, h * RET_DV:(h + 1) * RET_DV] = o[d, h].astype(outs[d].dtype)
        s_ref[d, h] = s_new[d, h]

    @pl.when(n == n_chunks - 1)
    def _():
        sfin_ref[0] = s_ref[...]


def _retention(pr, cos2, sin2, s0, use_rope):
    b, l, _ = pr.shape
    c = RET_C
    nc = l // c
    dec, qd, kd = _ret_tables()
    qw, vw = RET_H * RET_DK, RET_H * RET_DV

    def fw(blk):
        return lambda i, n: (i, n, blk)

    def bw(blk):
        return lambda i, n: (i, nc - 1 - n, blk)

    def const(*shape):
        return pl.BlockSpec(shape, lambda i, n: (0,) * len(shape))

    st = pl.BlockSpec((1, 2, RET_H, RET_DK, RET_DV), lambda i, n: (i, 0, 0, 0, 0))
    in_specs = [
        pl.BlockSpec((1, c, qw), fw(0)), pl.BlockSpec((1, c, qw), fw(1)), pl.BlockSpec((1, c, vw), fw(1)),
        pl.BlockSpec((1, c, qw), bw(0)), pl.BlockSpec((1, c, qw), bw(1)), pl.BlockSpec((1, c, vw), bw(1)),
        pl.BlockSpec((c, RET_DK), lambda i, n: (n, 0)), pl.BlockSpec((c, RET_DK), lambda i, n: (n, 0)),
        pl.BlockSpec((c, RET_DK), lambda i, n: (nc - 1 - n, 0)), pl.BlockSpec((c, RET_DK), lambda i, n: (nc - 1 - n, 0)),
        const(2 * RET_H, c, c), const(2 * RET_H, c, RET_DK), const(2 * RET_H, c, RET_DK), st,
    ]
    out_specs = [pl.BlockSpec((1, c, vw), fw(0)), pl.BlockSpec((1, c, vw), bw(0)), st]
    sd = jax.ShapeDtypeStruct
    return pl.pallas_call(
        functools.partial(_ret_kernel, use_rope=use_rope, n_chunks=nc),
        grid=(b, nc),
        in_specs=in_specs,
        out_specs=out_specs,
        out_shape=[sd((b, l, vw), BF16), sd((b, l, vw), BF16), sd((b, 2, RET_H, RET_DK, RET_DV), F32)],
        scratch_shapes=[pltpu.VMEM((2, RET_H, RET_DK, RET_DV), F32)],
        compiler_params=_cparams(("arbitrary", "arbitrary")),
        name="retention",
    )(pr, pr, pr, pr, pr, pr, cos2, sin2, cos2, sin2, dec, qd, kd, s0)


def _gla_boundary(b, lv, reverse, rolls):
    c, w = b.shape
    m = 1 << lv
    if m >= SUBLANES:
        parts = []
        for p0 in range(0, c, 2 * m):
            e = p0 + m if reverse else p0 + m - 1
            parts.append(jnp.broadcast_to(b[e:e + 1, :], (2 * m, w)))
        return jnp.concatenate(parts, axis=0)

    def rolled(s):
        s %= c
        if s not in rolls:
            rolls[s] = b if s == 0 else pltpu.roll(b, s, 0)
        return rolls[s]

    row = lax.broadcasted_iota(jnp.int32, (c, w), 0)
    r = jnp.bitwise_and(row, m - 1)
    upper = jnp.bitwise_and(jnp.right_shift(row, lv), 1) == 1
    out = b
    for t in range(m):
        if reverse:
            out = jnp.where(upper & (r == t), rolled(t), out)
            out = jnp.where(jnp.logical_not(upper) & (r == t), rolled(-(m - t)), out)
        else:
            out = jnp.where(upper & (r == t), rolled(t + 1), out)
            out = jnp.where(jnp.logical_not(upper) & (r == t), rolled(-(m - 1 - t)), out)
    return out


def _gla_kernel(qf_ref, kf_ref, vf_ref, lrf_ref, qb_ref, kb_ref, vb_ref, lrb_ref, w2_ref, b2_ref, s0_ref,
                of_ref, ob_ref, sfin_ref, st_ref, *, n_chunks):
    c = GLA_C
    n = pl.program_id(1)

    @pl.when(n == 0)
    def _():
        st_ref[...] = s0_ref[0]

    ins = ((qf_ref, kf_ref, vf_ref, lrf_ref), (qb_ref, kb_ref, vb_ref, lrb_ref))
    outs = (of_ref, ob_ref)
    hw = GLA_H * GLA_DK
    row = lax.broadcasted_iota(jnp.int32, (c, hw), 0)
    ri = lax.broadcasted_iota(jnp.int32, (c, c), 0)
    ci = lax.broadcasted_iota(jnp.int32, (c, c), 1)
    chains = [(d, h) for d in range(2) for h in range(GLA_H)]

    def dk(h):
        return slice(h * GLA_DK, (h + 1) * GLA_DK)

    def dv(h):
        return slice(h * GLA_DV, (h + 1) * GLA_DV)

    q, k, v, b_inc, b_rest, tot, ql = {}, {}, {}, {}, {}, {}, {}
    for d in range(2):
        reverse = d == 1
        q_ref, k_ref, v_ref, lr_ref = ins[d]
        q[d] = q_ref[0].astype(F32) * (GLA_DK ** -0.5)
        k[d] = k_ref[0].astype(F32)
        v[d] = v_ref[0]
        gate = _log_sigmoid(_mm_split(lr_ref[0], w2_ref[d, 0], w2_ref[d, 1]) + b2_ref[d])
        gate = gate * (1.0 / GLA_GATE_NORM)
        b = _scan_rows(gate, reverse)
        tot[d] = b[0:1] if reverse else b[c - 1:c]
        b_inc[d], b_rest[d] = b, tot[d] - b
        rolls = {}
        for lv in range(GLA_LEVELS):
            be = _gla_boundary(b, lv, reverse, rolls)
            upper = jnp.bitwise_and(jnp.right_shift(row, lv), 1) == 1
            second = jnp.logical_not(upper) if reverse else upper
            e = jnp.exp(-jnp.abs(b - be))
            ql[d, lv] = jnp.where(second, q[d], k[d]) * e

    attn = {(d, h): jnp.where(ri == ci, jnp.sum(q[d][:, dk(h)] * k[d][:, dk(h)], axis=-1, keepdims=True), 0.0)
            for d, h in chains}
    for lv in range(GLA_LEVELS):
        same = jnp.right_shift(ri, lv + 1) == jnp.right_shift(ci, lv + 1)
        row_upper = jnp.bitwise_and(jnp.right_shift(ri, lv), 1) == 1
        col_upper = jnp.bitwise_and(jnp.right_shift(ci, lv), 1) == 1
        pair_mask = (same & row_upper & jnp.logical_not(col_upper), same & jnp.logical_not(row_upper) & col_upper)
        for d, h in chains:
            p = ql[d, lv][:, dk(h)]
            attn[d, h] = attn[d, h] + jnp.where(pair_mask[d], _mm_nt(p, p), 0.0)
    st_old = {ch: st_ref[ch[0], ch[1]] for ch in chains}
    o = {(d, h): _mm(attn[d, h], v[d][:, dv(h)]) + _mm_nt(q[d][:, dk(h)] * jnp.exp(b_inc[d][:, dk(h)]), st_old[d, h])
         for d, h in chains}
    st_new = {(d, h): st_old[d, h] * jnp.exp(tot[d][:, dk(h)])
              + _mm_tn(v[d][:, dv(h)], k[d][:, dk(h)] * jnp.exp(b_rest[d][:, dk(h)])) for d, h in chains}
    for d, h in chains:
        outs[d][0, :, dv(h)] = o[d, h].astype(outs[d].dtype)
        st_ref[d, h] = st_new[d, h]

    @pl.when(n == n_chunks - 1)
    def _():
        sfin_ref[0] = st_ref[...]


def _gla(pg, plr, w2p, b2, s0t):
    b, l, _ = pg.shape
    c = GLA_C
    nc = l // c
    qw, vw = GLA_H * GLA_DK, GLA_H * GLA_DV

    def fw(blk):
        return lambda i, n: (i, n, blk)

    def bw(blk):
        return lambda i, n: (i, nc - 1 - n, blk)

    st = pl.BlockSpec((1, 2, GLA_H, GLA_DV, GLA_DK), lambda i, n: (i, 0, 0, 0, 0))
    in_specs = [
        pl.BlockSpec((1, c, qw), fw(0)), pl.BlockSpec((1, c, qw), fw(1)), pl.BlockSpec((1, c, vw), fw(1)),
        pl.BlockSpec((1, c, LANES), fw(0)),
        pl.BlockSpec((1, c, qw), bw(0)), pl.BlockSpec((1, c, qw), bw(1)), pl.BlockSpec((1, c, vw), bw(1)),
        pl.BlockSpec((1, c, LANES), bw(0)),
        pl.BlockSpec((2, 2, LANES, qw), lambda i, n: (0, 0, 0, 0)),
        pl.BlockSpec((2, 1, qw), lambda i, n: (0, 0, 0)),
        st,
    ]
    out_specs = [pl.BlockSpec((1, c, vw), fw(0)), pl.BlockSpec((1, c, vw), bw(0)), st]
    sd = jax.ShapeDtypeStruct
    return pl.pallas_call(
        functools.partial(_gla_kernel, n_chunks=nc),
        grid=(b, nc),
        in_specs=in_specs,
        out_specs=out_specs,
        out_shape=[sd((b, l, vw), BF16), sd((b, l, vw), BF16), sd((b, 2, GLA_H, GLA_DV, GLA_DK), F32)],
        scratch_shapes=[pltpu.VMEM((2, GLA_H, GLA_DV, GLA_DK), F32)],
        compiler_params=_cparams(("arbitrary", "arbitrary")),
        name="gla",
    )(pg, pg, pg, plr, pg, pg, pg, plr, w2p, b2, s0t)


def _gla_gate_params(w2, b2):
    w2p = jnp.pad(w2, ((0, 0), (0, LANES - GLA_LOWRANK), (0, 0)))
    return jnp.stack(_split2(w2p), axis=1), b2.reshape(2, 1, GLA_H * GLA_DK)


def _head_sum(x):
    i = lax.broadcasted_iota(jnp.int32, (LANES, LANES), 0)
    j = lax.broadcasted_iota(jnp.int32, (LANES, LANES), 1)
    ones = jnp.where(jnp.right_shift(i, RW_N_BITS) == jnp.right_shift(j, RW_N_BITS), 1.0, 0.0).astype(BF16)
    out = []
    for t in range(x.shape[1] // LANES):
        parts = _split2(x[:, t * LANES:(t + 1) * LANES])
        out.append(sum(jnp.dot(p, ones, preferred_element_type=F32) for p in parts))
    return jnp.concatenate(out, axis=1)


RW_TM = 256
HALO = 2 * SUBLANES
N_RW_LORA = RW_W_LORA + RW_A_LORA + RW_G_LORA


def _rw_prep_kernel(x_ref, xp_ref, xn_ref, lo_ref, lop_ref, lon_ref, taps_ref, ltaps_ref, kkp_ref, ka_ref, rk_ref,
                    w0_ref, wb_ref, a0_ref, ab_ref, gb_ref, r_ref, v_ref, kk_ref, lw_ref, kd_ref, be_ref, bonus_ref,
                    g_ref, lbuf, *, n_tiles):
    tm = RW_TM
    j = pl.program_id(1)
    first, last = j == 0, j == n_tiles - 1
    ri = lax.broadcasted_iota(jnp.int32, (tm, tm), 0)
    ci = lax.broadcasted_iota(jnp.int32, (tm, tm), 1)
    pick_prev = jnp.where(ci == ri - 1, 1.0, 0.0).astype(BF16)
    pick_next = jnp.where(ci == ri + 1, 1.0, 0.0).astype(BF16)
    before = jnp.where(first, 0.0, xp_ref[0].astype(F32))[HALO - 1:HALO]
    after = jnp.where(last, 0.0, xn_ref[0].astype(F32))[0:1]
    edge = lax.broadcasted_iota(jnp.int32, (SUBLANES, 1), 0)

    def shifted_main(c0, c1):
        x = x_ref[0, :, c0:c1]
        prev = jnp.dot(pick_prev, x, preferred_element_type=F32)
        nxt = jnp.dot(pick_next, x, preferred_element_type=F32)
        prev = jnp.concatenate([prev[0:SUBLANES] + jnp.where(edge == 0, before[:, c0:c1], 0.0),
                                prev[SUBLANES:]], axis=0)
        nxt = jnp.concatenate([nxt[0:tm - SUBLANES], nxt[tm - SUBLANES:]
                               + jnp.where(edge == SUBLANES - 1, after[:, c0:c1], 0.0)], axis=0)
        return (taps_ref[0:1, c0:c1] * prev + taps_ref[1:2, c0:c1] * x.astype(F32) + taps_ref[2:3, c0:c1] * nxt)

    lbuf[HALO:HALO + tm, :] = lo_ref[0]
    lbuf[0:HALO, :] = jnp.where(first, 0.0, lop_ref[0])
    lbuf[HALO + tm:2 * HALO + tm, :] = jnp.where(last, 0.0, lon_ref[0])
    lora = (ltaps_ref[0:1, :] * lbuf[HALO - 1:HALO - 1 + tm, :] + ltaps_ref[1:2, :] * lbuf[HALO:HALO + tm, :]
            + ltaps_ref[2:3, :] * lbuf[HALO + 1:HALO + 1 + tm, :])
    r = shifted_main(0, D)
    k = shifted_main(D, 2 * D)
    v = shifted_main(2 * D, 3 * D)
    xw = lora[:, 0:RW_W_LORA]
    xa = lora[:, RW_W_LORA:RW_W_LORA + RW_A_LORA]
    xg = lora[:, RW_W_LORA + RW_A_LORA:]
    r_ref[0] = r.astype(r_ref.dtype)
    v_ref[0] = v.astype(v_ref.dtype)
    kk = k * kkp_ref[...]
    kk = kk * lax.rsqrt(jnp.maximum(_head_sum(kk * kk), 1e-24))
    kk_ref[0] = kk.astype(kk_ref.dtype)
    g_ref[0] = _mm(_sigmoid(xg), gb_ref[...]).astype(g_ref.dtype)
    wh = jnp.tanh(xw)
    kd_sum = jnp.zeros_like(k)
    for d in range(2):
        lw_ref[d, 0] = -float(np.exp(-0.5)) * _sigmoid(w0_ref[d:d + 1, :] + _mm(wh, wb_ref[d]))
        a = _sigmoid(a0_ref[d:d + 1, :] + _mm(xa, ab_ref[d]))
        kd = k * (1.0 + (a - 1.0) * ka_ref[...])
        kd_ref[d, 0] = kd.astype(kd_ref.dtype)
        be_ref[d, 0] = (kk * a).astype(be_ref.dtype)
        kd_sum = kd_sum + kd
    bonus_ref[0] = (_head_sum(r * kd_sum * rk_ref[...]) * v).astype(bonus_ref.dtype)


def _rw_prep(rw, rlo, taps, kkp, ka, rk, w0, wb, a0, ab, gb):
    b, l, _ = rw.shape
    tm = RW_TM
    nt = l // tm
    hb = tm // HALO
    tok = pl.BlockSpec((1, tm, D), lambda i, j: (i, j, 0))
    tokd = pl.BlockSpec((2, 1, tm, D), lambda i, j: (0, i, j, 0))

    def full(*shape):
        return pl.BlockSpec(shape, lambda i, j: (0,) * len(shape))

    def halo_specs(n):
        return [pl.BlockSpec((1, tm, n), lambda i, j: (i, j, 0)),
                pl.BlockSpec((1, HALO, n), lambda i, j: (i, jnp.maximum(j * hb - 1, 0), 0)),
                pl.BlockSpec((1, HALO, n), lambda i, j: (i, jnp.minimum((j + 1) * hb, l // HALO - 1), 0))]

    in_specs = halo_specs(3 * D) + halo_specs(N_RW_LORA) + [
        full(3, 3 * D), full(3, N_RW_LORA), full(1, D), full(1, D), full(1, D),
        full(2, D), full(2, RW_W_LORA, D), full(2, D), full(2, RW_A_LORA, D), full(RW_G_LORA, D),
    ]
    sd = jax.ShapeDtypeStruct
    tok_o, dir_o, dir_f32 = sd((b, l, D), BF16), sd((2, b, l, D), BF16), sd((2, b, l, D), F32)
    return pl.pallas_call(
        functools.partial(_rw_prep_kernel, n_tiles=nt),
        grid=(b, nt),
        in_specs=in_specs,
        out_specs=[tok, tok, tok, tokd, tokd, tokd, tok, tok],
        out_shape=[tok_o, tok_o, tok_o, dir_f32, dir_o, dir_o, tok_o, tok_o],
        scratch_shapes=[pltpu.VMEM((tm + 2 * HALO, N_RW_LORA), F32)],
        compiler_params=_cparams(("arbitrary", "arbitrary")),
        name="rwkv_prep",
    )(rw, rw, rw, rlo, rlo, rlo, taps[:, :3 * D], taps[:, 3 * D:], kkp.reshape(1, D), ka.reshape(1, D),
      rk.reshape(1, D), w0, wb, a0, ab, gb)


def _tile_rows(x, n):
    return jnp.concatenate([x] * n, axis=0)


def _rw_scan_kernel(rf_ref, vf_ref, kkf_ref, lwf_ref, kdf_ref, bef_ref, rb_ref, vb_ref, kkb_ref, lwb_ref, kdb_ref,
                    beb_ref, s0_ref, yf_ref, yb_ref, sfin_ref, zt_ref, sv_lhs, sv_w, sv_bk, sv_tot, *, n_tiles):
    c, hw, sw = RW_C, RW_GW, RW_SW
    pk = RW_G * c
    n_groups, n_sub = RW_H // RW_G, RW_GW // RW_SW
    n_chunks = rf_ref.shape[1] // c
    j = pl.program_id(1)

    @pl.when(j == 0)
    def _():
        zt_ref[...] = jnp.zeros_like(zt_ref)
        for bb in range(RW_BB):
            for d in range(2):
                for g in range(RW_H // 2):
                    for jj in range(2):
                        blk = slice(RW_N * jj, RW_N * (jj + 1))
                        zt_ref[bb, d, g, blk, blk] = s0_ref[bb, d, 2 * g + jj]

    ins = ((rf_ref, vf_ref, kkf_ref, lwf_ref, kdf_ref, bef_ref), (rb_ref, vb_ref, kkb_ref, lwb_ref, kdb_ref, beb_ref))
    outs = (yf_ref, yb_ref)
    ri = lax.broadcasted_iota(jnp.int32, (c, pk), 0)
    cs = jnp.bitwise_and(lax.broadcasted_iota(jnp.int32, (c, pk), 1), c - 1)
    eye = (ri == cs).astype(F32)
    strict, incl, lvl, quad = [], [], [], []
    for d in range(2):
        ti = ri if d == 0 else c - 1 - ri
        ts = cs if d == 0 else c - 1 - cs
        strict.append(ts < ti)
        incl.append(ts <= ti)
        same4 = jnp.right_shift(ti, 2) == jnp.right_shift(ts, 2)
        below = jnp.bitwise_and(ti, 3) - jnp.bitwise_and(ts, 3)
        quad.append(tuple(same4 & (below == k) for k in (1, 2, 3)))
        lv_masks = {}
        for lv in range(2, RW_LEVELS):
            same = jnp.right_shift(ti, lv + 1) == jnp.right_shift(ts, lv + 1)
            lower = ((jnp.bitwise_and(jnp.right_shift(ti, lv), 1) == 1)
                     & (jnp.bitwise_and(jnp.right_shift(ts, lv), 1) == 0))
            lv_masks[lv] = same & lower
        lvl.append(lv_masks)
    bi = lax.broadcasted_iota(jnp.int32, (pk, pk), 0)
    bj = lax.broadcasted_iota(jnp.int32, (pk, pk), 1)
    bd_p = jnp.right_shift(bi, RW_LEVELS) == jnp.right_shift(bj, RW_LEVELS)
    zi = lax.broadcasted_iota(jnp.int32, (sw, sw), 0)
    zj = lax.broadcasted_iota(jnp.int32, (sw, sw), 1)
    bd_z = jnp.right_shift(zi, RW_N_BITS) == jnp.right_shift(zj, RW_N_BITS)
    term_row = lax.broadcasted_iota(jnp.int32, (2 * SUBLANES, sw), 0)
    wi = lax.broadcasted_iota(jnp.int32, (pk, hw), 0)
    wj = lax.broadcasted_iota(jnp.int32, (pk, hw), 1)
    bd_w = jnp.right_shift(wi, RW_LEVELS) == jnp.right_shift(wj, RW_N_BITS)

    def bdp(x):
        return jnp.where(bd_p, _tile_rows(x, RW_G), 0.0)

    def bdw(x):
        return jnp.where(bd_w, _tile_rows(x, RW_G), 0.0)

    chains = [(bb, d, half) for bb in range(RW_BB) for d in range(2) for half in range(n_groups)]

    def each(fn):
        return {ch: fn(ch) for ch in chains}

    def chunk_rows(i):
        return (pl.ds(pl.multiple_of(i * c, c), c), pl.ds(pl.multiple_of((n_chunks - 1 - i) * c, c), c))

    def scale(i, slot):
        rows = chunk_rows(i)

        def load(ch):
            bb, d, half = ch
            ln = slice(half * hw, (half + 1) * hw)
            refs = ins[d]
            return (refs[0][bb, rows[d], ln].astype(F32), refs[2][bb, rows[d], ln].astype(F32),
                    refs[3][0, bb, rows[d], ln], refs[4][0, bb, rows[d], ln].astype(F32),
                    refs[5][0, bb, rows[d], ln].astype(F32))

        x = each(load)
        r, kk, lw = each(lambda ch: x[ch][0]), each(lambda ch: x[ch][1]), each(lambda ch: x[ch][2])
        kd, be = each(lambda ch: x[ch][3]), each(lambda ch: x[ch][4])
        b = each(lambda ch: _scan_rows(lw[ch], ch[1] == 1))
        tot = each(lambda ch: b[ch][0:1] if ch[1] == 1 else b[ch][c - 1:c])
        en = each(lambda ch: jnp.exp(-b[ch]))
        es = each(lambda ch: jnp.exp(tot[ch] - b[ch]))
        for n, ch in enumerate(chains):
            sv_lhs[slot, n] = jnp.concatenate([kk[ch] * jnp.exp(b[ch] - lw[ch]), r[ch] * jnp.exp(b[ch])],
                                              axis=0).astype(BF16)
            sv_w[slot, n] = jnp.concatenate([be[ch] * en[ch], kd[ch] * en[ch]], axis=0).astype(BF16)
            sv_bk[slot, n] = jnp.concatenate([be[ch] * es[ch], kd[ch] * es[ch]], axis=0).astype(BF16)
            sv_tot[slot, n] = jnp.broadcast_to(tot[ch], (SUBLANES, hw))

    scale(0, 0)

    def chunk(i, carry):
        rows = chunk_rows(i)
        slot = lax.rem(i, 2)
        index = {ch: n for n, ch in enumerate(chains)}
        lhs = each(lambda ch: sv_lhs[slot, index[ch]])
        w_en = each(lambda ch: sv_w[slot, index[ch]])
        bk = each(lambda ch: sv_bk[slot, index[ch]])
        tot = each(lambda ch: sv_tot[slot, index[ch]][0:1])
        v = each(lambda ch: ins[ch[1]][1][ch[0], rows[ch[1]], ch[2] * hw:(ch[2] + 1) * hw].astype(F32))
        gb = each(lambda ch: _mm_nt(lhs[ch], bdw(w_en[ch][0:c])))
        gk = each(lambda ch: _mm_nt(lhs[ch], bdw(w_en[ch][c:])))
        a_ab = each(lambda ch: jnp.where(strict[ch[1]], gb[ch][0:c], 0.0))
        a_rb = each(lambda ch: jnp.where(incl[ch[1]], gb[ch][c:], 0.0))
        a_ak = each(lambda ch: jnp.where(strict[ch[1]], gk[ch][0:c], 0.0))
        a_rk = each(lambda ch: jnp.where(incl[ch[1]], gk[ch][c:], 0.0))
        def inverse4(ch):
            d = ch[1]
            q1, q2, q3 = quad[d]
            a = jnp.where(q1 | q2 | q3, a_ab[ch], 0.0)
            col = (lambda x, k: pltpu.roll(x, pk - k, 1)) if d == 0 else (lambda x, k: pltpu.roll(x, k, 1))
            row = (lambda x, k: pltpu.roll(x, k, 0)) if d == 0 else (lambda x, k: pltpu.roll(x, c - k, 0))
            c1, c2, r1, r2 = col(a, 1), col(a, 2), row(a, 1), row(a, 2)
            two = jnp.where(q2, c1 * r1, 0.0)
            three = jnp.where(q3, c1 * r2 + c2 * r1 - c2 * row(c1, 1) * r2, 0.0)
            return eye - a + two + three

        t_inv = each(inverse4)
        for lv in range(2, RW_LEVELS):
            xm = each(lambda ch: _mm(jnp.where(lvl[ch[1]][lv], a_ab[ch], 0.0), bdp(t_inv[ch])))
            t_inv = each(lambda ch: t_inv[ch] - _mm(t_inv[ch], bdp(xm[ch])))
            if lv == 2:
                scale(jnp.minimum(i + 1, n_chunks - 1), 1 - slot)
        zt_old = {(ch, g): zt_ref[ch[0], ch[1], n_sub * ch[2] + g] for ch in chains for g in range(n_sub)}
        zz = each(lambda ch: jnp.concatenate(
            [_mm(lhs[ch][:, g * sw:(g + 1) * sw], zt_old[ch, g]) for g in range(n_sub)], axis=1))
        av = each(lambda ch: _mm(jnp.concatenate([a_ak[ch], a_rk[ch]], axis=0), bdw(v[ch])))
        u = each(lambda ch: _mm(t_inv[ch], bdw(-(zz[ch][0:c] + av[ch][0:c]))))
        y = each(lambda ch: zz[ch][c:] + _mm(a_rb[ch], bdw(u[ch])) + av[ch][c:])
        uv = each(lambda ch: jnp.concatenate([u[ch], v[ch]], axis=0))

        def decay_cols(ch, gl):
            x = tot[ch][:, gl]
            hi = x.astype(BF16).astype(F32)
            mid = (x - hi).astype(BF16).astype(F32)
            lo = x - hi - mid
            terms = jnp.where(term_row == 0, hi, jnp.where(term_row == 1, mid, jnp.where(term_row == 2, lo, 0.0)))
            return jnp.exp(_mm_tn(terms, jnp.ones((2 * SUBLANES, sw), BF16)))

        zt_new = {}
        for ch in chains:
            for g in range(n_sub):
                gl = slice(g * sw, (g + 1) * sw)
                upd = jnp.where(bd_z, _mm_tn(bk[ch][:, gl], uv[ch][:, gl]), 0.0)
                zt_new[ch, g] = zt_old[ch, g] * decay_cols(ch, gl) + upd
        for ch in chains:
            bb, d, half = ch
            outs[d][bb, rows[d], half * hw:(half + 1) * hw] = y[ch].astype(outs[d].dtype)
            for g in range(n_sub):
                zt_ref[bb, d, n_sub * half + g] = zt_new[ch, g]
        return carry

    lax.fori_loop(0, n_chunks, chunk, 0)

    @pl.when(j == n_tiles - 1)
    def _():
        for bb in range(RW_BB):
            for d in range(2):
                for g in range(RW_H // 2):
                    for jj in range(2):
                        blk = slice(RW_N * jj, RW_N * (jj + 1))
                        sfin_ref[bb, d, 2 * g + jj] = zt_ref[bb, d, g, blk, blk]


def _rw_scan(r, v, kk, lw, kd, be, s0):
    b, l, _ = r.shape
    tl, nb = min(RW_TL, l), RW_BB
    nt = l // tl
    n_ch = nb * 2 * (RW_H // RW_G)
    tok_f = pl.BlockSpec((nb, tl, D), lambda i, j: (i, j, 0))
    tok_b = pl.BlockSpec((nb, tl, D), lambda i, j: (i, nt - 1 - j, 0))
    dir_f = pl.BlockSpec((1, nb, tl, D), lambda i, j: (0, i, j, 0))
    dir_b = pl.BlockSpec((1, nb, tl, D), lambda i, j: (1, i, nt - 1 - j, 0))
    st = pl.BlockSpec((nb, 2, RW_H, RW_N, RW_N), lambda i, j: (i, 0, 0, 0, 0))
    sd = jax.ShapeDtypeStruct
    return pl.pallas_call(
        functools.partial(_rw_scan_kernel, n_tiles=nt),
        grid=(b // nb, nt),
        in_specs=[tok_f, tok_f, tok_f, dir_f, dir_f, dir_f, tok_b, tok_b, tok_b, dir_b, dir_b, dir_b, st],
        out_specs=[tok_f, tok_b, st],
        out_shape=[sd((b, l, D), BF16), sd((b, l, D), BF16), sd((b, 2, RW_H, RW_N, RW_N), F32)],
        scratch_shapes=[pltpu.VMEM((nb, 2, RW_H // 2, RW_SW, RW_SW), F32),
                        pltpu.VMEM((2, n_ch, 2 * RW_C, RW_GW), BF16),
                        pltpu.VMEM((2, n_ch, 2 * RW_C, RW_GW), BF16),
                        pltpu.VMEM((2, n_ch, 2 * RW_C, RW_GW), BF16),
                        pltpu.VMEM((2, n_ch, SUBLANES, RW_GW), F32)],
        compiler_params=_cparams(("arbitrary", "arbitrary")),
        name="rwkv_scan",
    )(r, v, kk, lw, kd, be, r, v, kk, lw, kd, be, s0)


MERGE_TM = 256
ROUTER_LANES = LANES


def _merge_kernel(orf_ref, orb_ref, rg_ref, ogf_ref, ogb_ref, gg_ref, yf_ref, yb_ref, bonus_ref, grw_ref,
                  gates_ref, x_ref, gt_ref, sh_ref, sc_ref, wbr_ref, wout_ref, rgn_ref, ggn_ref, gng_ref, gnb_ref,
                  l1g_ref, l1b_ref, wr_ref, br_ref, x1_ref, u2_ref, lg_ref):
    f32 = lambda ref: ref[0].astype(F32)
    o_ret = f32(orf_ref) + f32(orb_ref)
    o_gla = f32(ogf_ref) + f32(ogb_ref)
    y_ret, y_gla = [], []
    for h in range(RET_H):
        hs = slice(h * RET_DV, (h + 1) * RET_DV)
        y_ret.append(_ln(o_ret[:, hs]))
        og = o_gla[:, hs]
        y_gla.append(og * lax.rsqrt(jnp.mean(og * og, axis=-1, keepdims=True) + EPS))
    z_ret = _silu(f32(rg_ref)) * (jnp.concatenate(y_ret, axis=1) * rgn_ref[...])
    z_gla = _silu(f32(gg_ref)) * (jnp.concatenate(y_gla, axis=1) * ggn_ref[...])
    y = f32(yf_ref) + f32(yb_ref)
    mu = _head_sum(y) * (1.0 / RW_N)
    yc = y - mu
    var = _head_sum(yc * yc) * (1.0 / RW_N)
    z_rw = (yc * lax.rsqrt(var + EPS) * gng_ref[...] + gnb_ref[...] + f32(bonus_ref)) * f32(grw_ref)
    gates = f32(gates_ref)
    mixed = (gates[:, 0:D] * _mm(z_ret, wbr_ref[0]) + gates[:, D:2 * D] * _mm(z_gla, wbr_ref[1])
             + gates[:, 2 * D:] * _mm(z_rw, wbr_ref[2]))
    mix = _mm(mixed, wout_ref[...])
    x1 = _ln(ALPHA * x_ref[0] + gt_ref[0] * mix) * l1g_ref[...] + l1b_ref[...]
    x1_ref[0] = x1
    u2 = _ln(x1) * (1.0 + sc_ref[0]) + sh_ref[0]
    u2_ref[0] = u2.astype(u2_ref.dtype)
    lg_ref[...] = (_mm_split(u2, wr_ref[0], wr_ref[1]) + br_ref[...]).T


def _merge(orf, orb, pr, ogf, ogb, pg, yf, yb, bonus, grw, gates, x, gt1, sh2, sc2, wbr, wout, rgn, ggn, gng, gnb,
           l1g, l1b, wr, br):
    b, l, _ = x.shape
    tm = MERGE_TM
    tok = pl.BlockSpec((1, tm, D), lambda i, j: (i, j, 0))
    gate_blk = pl.BlockSpec((1, tm, D), lambda i, j: (i, j, 2))
    vec = pl.BlockSpec((1, 1, D), lambda i, j: (i, 0, 0))
    row = pl.BlockSpec((1, D), lambda i, j: (0, 0))
    in_specs = [
        tok, tok, gate_blk, tok, tok, gate_blk, tok, tok, tok, tok,
        pl.BlockSpec((1, tm, 3 * D), lambda i, j: (i, j, 0)),
        tok, vec, vec, vec,
        pl.BlockSpec((3, D, D), lambda i, j: (0, 0, 0)),
        pl.BlockSpec((D, D), lambda i, j: (0, 0)),
        row, row, row, row, row, row,
        pl.BlockSpec((2, D, ROUTER_LANES), lambda i, j: (0, 0, 0)),
        pl.BlockSpec((1, ROUTER_LANES), lambda i, j: (0, 0)),
    ]
    sd = jax.ShapeDtypeStruct
    r1 = lambda a: a.reshape(1, D)
    return pl.pallas_call(
        _merge_kernel,
        grid=(b, l // tm),
        in_specs=in_specs,
        out_specs=[tok, tok, pl.BlockSpec((ROUTER_LANES, tm), lambda i, j: (0, i * (l // tm) + j))],
        out_shape=[sd((b, l, D), F32), sd((b, l, D), BF16), sd((ROUTER_LANES, b * l), F32)],
        compiler_params=_cparams(("arbitrary", "arbitrary")),
        name="merge",
    )(orf, orb, pr, ogf, ogb, pg, yf, yb, bonus, grw, gates, x, gt1, sh2, sc2, wbr, wout,
      r1(rgn), r1(ggn), r1(gng), r1(gnb), r1(l1g), r1(l1b), wr, br)


MOE_TM = 1024


def _moe_block_rows(tm):
    rows = tm / N_GROUPS + 4.0 * np.sqrt(tm * (N_GROUPS - 1.0)) / N_GROUPS
    bf16_tile = 2 * SUBLANES
    return int(min(tm, -(-rows // bf16_tile) * bf16_tile))


def _first_argmax_rows(x, row):
    m = jnp.max(x, axis=0, keepdims=True)
    idx = jnp.min(jnp.where(x == m, row, x.shape[0]), axis=0, keepdims=True)
    return m, idx


def _routing(lt):
    tm = lt.shape[1]
    gl = lt[0:N_GROUPS]
    gmax, gidx = _first_argmax_rows(gl, lax.broadcasted_iota(jnp.int32, (N_GROUPS, tm), 0))
    g_w = 1.0 / jnp.sum(jnp.exp(gl - gmax), axis=0, keepdims=True)
    row = lax.broadcasted_iota(jnp.int32, (N_EXPERTS, tm), 0)
    neg = -jnp.inf
    el = jnp.where(jnp.right_shift(row, EPG_BITS) == gidx, lt[N_GROUPS:N_GROUPS + N_EXPERTS], neg)
    m1, i1 = _first_argmax_rows(el, row)
    m2, i2 = _first_argmax_rows(jnp.where(row == i1, neg, el), row)
    e2 = jnp.exp(m2 - m1)
    w1 = 1.0 / (1.0 + e2)
    return gidx, g_w * (jnp.where(row == i1, w1, 0.0) + jnp.where(row == i2, e2 * w1, 0.0))


def _moe_kernel(u_ref, lt_ref, wg_ref, wu_ref, wd_ref, o_ref, acc_ref):
    g = pl.program_id(1)
    tm = acc_ref.shape[0]
    blk = _moe_block_rows(tm)

    @pl.when(g == 0)
    def _():
        acc_ref[...] = jnp.zeros_like(acc_ref)

    gidx, comb = _routing(lt_ref[...])
    member = jnp.broadcast_to((gidx == g).astype(F32), (SUBLANES, tm))
    lane = lax.broadcasted_iota(jnp.int32, (SUBLANES, tm), 1)
    count = member
    sh = 1
    while sh < tm:
        count = count + jnp.where(lane >= sh, pltpu.roll(count, sh, 1), 0.0)
        sh *= 2
    pos = jnp.where(member > 0.0, count - 1.0, -1.0).astype(jnp.int32)[0:1]
    n_tok = jnp.max(count).astype(jnp.int32)
    comb_pad = jnp.concatenate([comb, jnp.zeros((LANES - N_EXPERTS, tm), F32)], axis=0)
    comb_parts = _split2(comb_pad)
    u = u_ref[...]
    sel_lane = lax.broadcasted_iota(jnp.int32, (blk, LANES), 1)

    def body(i, carry):
        slot = lax.broadcasted_iota(jnp.int32, (blk, tm), 0) + i * blk
        onehot = jnp.where(slot == pos, 1.0, 0.0).astype(BF16)
        xg = jnp.dot(onehot, u, preferred_element_type=F32).astype(BF16)
        cw = sum(lax.dot_general(onehot, c, (((1,), (1,)), ((), ())), preferred_element_type=F32)
                 for c in comb_parts)
        y = jnp.zeros((blk, D), F32)
        for e in range(EPG):
            hid = (_silu(jnp.dot(xg, wg_ref[e], preferred_element_type=F32))
                   * jnp.dot(xg, wu_ref[e], preferred_element_type=F32))
            c_e = jnp.sum(jnp.where(sel_lane == g * EPG + e, cw, 0.0), axis=-1, keepdims=True)
            y = y + c_e * _mm(hid, wd_ref[e])
        acc_ref[...] += _mm_tn(onehot, y)
        return carry

    lax.fori_loop(0, (n_tok + blk - 1) // blk, body, 0)

    @pl.when(g == N_GROUPS - 1)
    def _():
        o_ref[...] = acc_ref[...].astype(o_ref.dtype)


def _moe(u2, logits_t, wg, wu, wd):
    b, l, _ = u2.shape
    t = b * l
    tm = min(MOE_TM, t)
    tok = pl.BlockSpec((tm, D), lambda i, g: (i, 0))
    in_specs = [
        tok,
        pl.BlockSpec((ROUTER_LANES, tm), lambda i, g: (0, i)),
        pl.BlockSpec((EPG, D, EXPERT_FF), lambda i, g: (g, 0, 0)),
        pl.BlockSpec((EPG, D, EXPERT_FF), lambda i, g: (g, 0, 0)),
        pl.BlockSpec((EPG, EXPERT_FF, D), lambda i, g: (g, 0, 0)),
    ]
    return pl.pallas_call(
        _moe_kernel,
        grid=(t // tm, N_GROUPS),
        in_specs=in_specs,
        out_specs=tok,
        out_shape=jax.ShapeDtypeStruct((t, D), BF16),
        scratch_shapes=[pltpu.VMEM((tm, D), F32)],
        compiler_params=_cparams(("arbitrary",) * 2),
        name="moe",
    )(u2.reshape(t, D), logits_t, wg, wu, wd).reshape(b, l, D)


def _post_kernel(x_ref, m_ref, gt_ref, l2g_ref, l2b_ref, sh_ref, sc_ref, o_ref, u_ref):
    x2 = _ln(ALPHA * x_ref[0] + gt_ref[0] * m_ref[0].astype(F32)) * l2g_ref[...] + l2b_ref[...]
    o_ref[0] = x2
    u_ref[0] = (_ln(x2) * (1.0 + sc_ref[0]) + sh_ref[0]).astype(u_ref.dtype)


def _post(x1, moe_out, gt2, l2g, l2b, sh_next, sc_next):
    b, l, _ = x1.shape
    tm = 512 if l % 512 == 0 else 256
    tok = pl.BlockSpec((1, tm, D), lambda i, j: (i, j, 0))
    vec = pl.BlockSpec((1, 1, D), lambda i, j: (i, 0, 0))
    row = pl.BlockSpec((1, D), lambda i, j: (0, 0))
    sd = jax.ShapeDtypeStruct
    return pl.pallas_call(
        _post_kernel,
        grid=(b, l // tm),
        in_specs=[tok, tok, vec, row, row, vec, vec],
        out_specs=[tok, tok],
        out_shape=[sd((b, l, D), F32), sd((b, l, D), BF16)],
        compiler_params=_cparams(("arbitrary", "arbitrary")),
        name="post_ln",
    )(x1, moe_out, gt2, l2g.reshape(1, D), l2b.reshape(1, D), sh_next, sc_next)


def _rope_tables(l):
    t = np.arange(l)
    quarter = RET_DK // 4
    freqs = (np.float32(ROPE_BASE) ** (-np.arange(quarter, dtype=np.float32) / quarter)).astype(np.float32)
    rows = (t // GRID_W).astype(np.float32)
    cols = (t % GRID_W).astype(np.float32)
    ang = jnp.asarray(np.concatenate([rows[:, None] * freqs, cols[:, None] * freqs], -1))
    cos, sin = jnp.cos(ang), jnp.sin(ang)
    return jnp.concatenate([cos, cos], -1), jnp.concatenate([-sin, sin], -1)


def _layer_weights(p):
    n_rg = 2 * (RET_H * RET_DK + RET_H * RET_DV)
    w_in = p["w_in"]
    w_lr = jnp.pad(w_in[:, 2 * n_rg:2 * n_rg + GLA_LOWRANK], ((0, 0), (0, LANES - GLA_LOWRANK)))
    w2p, b2p = _gla_gate_params(p["gla_w2"], p["gla_b"])
    pad = ROUTER_LANES - N_GROUPS - N_EXPERTS
    return dict(
        p,
        w_ret=w_in[:, :n_rg].astype(BF16),
        w_gla=w_in[:, n_rg:2 * n_rg].astype(BF16),
        w_glr=w_lr.astype(BF16),
        w_rw=w_in[:, 2 * n_rg + GLA_LOWRANK:2 * n_rg + GLA_LOWRANK + 3 * D].astype(BF16),
        w_rlo=w_in[:, 2 * n_rg + GLA_LOWRANK + 3 * D:].astype(BF16),
        w_merge=p["w_merge"].astype(BF16),
        w_br=p["w_br"].astype(BF16),
        w_out=p["w_out"].astype(BF16),
        gla_w2p=w2p, gla_b2p=b2p,
        w_router=jnp.stack(_split2(jnp.pad(jnp.concatenate([p["w_rg"], p["w_re"]], axis=1), ((0, 0), (0, pad))))),
        b_router=jnp.pad(jnp.concatenate([p["b_rg"], p["b_re"]]), (0, pad)).reshape(1, ROUTER_LANES),
        w_eg=p["w_eg"].astype(BF16), w_eu=p["w_eu"].astype(BF16), w_ed=p["w_ed"].astype(BF16),
    )


def _split_mod(mod):
    return [m.reshape(mod.shape[0], 1, D) for m in jnp.split(mod, 6, axis=-1)]


def _layer(x, u, mod, mod_next, s_ret, s_gla_t, s_rw, rope, p):
    b, l, _ = x.shape
    _, _, gt1, sh2, sc2, gt2 = _split_mod(mod)
    sh_next, sc_next = _split_mod(mod_next)[:2]
    u2d = u.reshape(b * l, D)

    def proj(w, tn, bias=None, **kw):
        bias = jnp.zeros((w.shape[1],), F32) if bias is None else bias
        return _proj(u2d, w, bias, tn, **kw).reshape(b, l, -1)

    pr = proj(p["w_ret"], 1024)
    pg = proj(p["w_gla"], 1024)
    plr = proj(p["w_glr"], LANES, out_dtype=F32)
    rw = proj(p["w_rw"], 1024)
    rlo = proj(p["w_rlo"], N_RW_LORA, out_dtype=F32)
    gates = proj(p["w_merge"], 1024, bias=p["b_merge"], act="sigmoid")

    cos2, sin2 = rope if rope is not None else (jnp.ones((l, RET_DK), F32), jnp.zeros((l, RET_DK), F32))
    orf, orb, ret_fin = _retention(pr, cos2, sin2, s_ret, rope is not None)
    ogf, ogb, gla_fin_t = _gla(pg, plr, p["gla_w2p"], p["gla_b2p"], s_gla_t)
    r, v, kk, lw, kd, be, bonus, grw = _rw_prep(rw, rlo, p["rwkv_shift"], p["rwkv_kk"], p["rwkv_ka"],
                                                p["rwkv_rk"].reshape(-1), p["rwkv_w0"], p["rwkv_wb"],
                                                p["rwkv_a0"], p["rwkv_ab"], p["rwkv_gb"])
    yf, yb, rw_fin = _rw_scan(r, v, kk, lw, kd, be, s_rw)
    x1, u2, logits_t = _merge(orf, orb, pr, ogf, ogb, pg, yf, yb, bonus, grw, gates, x, gt1, sh2, sc2,
                              p["w_br"], p["w_out"], p["ret_gn"], p["gla_gn"], p["rwkv_gn_g"], p["rwkv_gn_b"],
                              p["ln1_g"], p["ln1_b"], p["w_router"], p["b_router"])
    moe_out = _moe(u2, logits_t, p["w_eg"], p["w_eu"], p["w_ed"])
    x2, u_next = _post(x1, moe_out, gt2, p["ln2_g"], p["ln2_b"], sh_next, sc_next)
    return x2, u_next, (ret_fin, gla_fin_t, rw_fin)


_PARAM_NAMES = ("w_in", "rwkv_shift", "ret_gn", "gla_w2", "gla_b", "gla_gn", "rwkv_w0", "rwkv_wb", "rwkv_a0",
                "rwkv_ab", "rwkv_gb", "rwkv_kk", "rwkv_ka", "rwkv_rk", "rwkv_gn_g", "rwkv_gn_b", "w_br", "w_merge",
                "b_merge", "w_out", "ln1_g", "ln1_b", "ln2_g", "ln2_b", "w_rg", "b_rg", "w_re", "b_re", "w_eg",
                "w_eu", "w_ed")


def kernel(x_prompt, x_sample, state_ret, state_gla, state_rwkv, c, c_ctx, w_ada, b_ada, w_in, rwkv_shift, ret_gn,
           gla_w2, gla_b, gla_gn, rwkv_w0, rwkv_wb, rwkv_a0, rwkv_ab, rwkv_gb, rwkv_kk, rwkv_ka, rwkv_rk, rwkv_gn_g,
           rwkv_gn_b, w_br, w_merge, b_merge, w_out, ln1_g, ln1_b, ln2_g, ln2_b, w_rg, b_rg, w_re, b_re, w_eg, w_eu,
           w_ed):
    params = dict(zip(_PARAM_NAMES, (w_in, rwkv_shift, ret_gn, gla_w2, gla_b, gla_gn, rwkv_w0, rwkv_wb, rwkv_a0,
                                     rwkv_ab, rwkv_gb, rwkv_kk, rwkv_ka, rwkv_rk, rwkv_gn_g, rwkv_gn_b, w_br, w_merge,
                                     b_merge, w_out, ln1_g, ln1_b, ln2_g, ln2_b, w_rg, b_rg, w_re, b_re, w_eg, w_eu,
                                     w_ed)))
    bc, bl = x_prompt.shape[0], x_sample.shape[0]
    rope = _rope_tables(x_sample.shape[1])
    mod_rows = 2 * SUBLANES
    c_all = jnp.concatenate([c, c_ctx[None, :], jnp.zeros((mod_rows - bl - 1, D), F32)], axis=0)
    z_ret = jnp.zeros((bc, 2, RET_H, RET_DK, RET_DV), F32)
    z_gla_t = jnp.zeros((bc, 2, GLA_H, GLA_DV, GLA_DK), F32)
    z_rw = jnp.zeros((bc, 2, RW_H, RW_N, RW_N), F32)
    mods = [_modulation(c_all, w_ada[layer], b_ada[layer]) for layer in range(DEPTH)]
    mods.append(jnp.zeros_like(mods[0]))
    mods_lat = [m[:bl] for m in mods]
    mods_ctx = [jnp.broadcast_to(m[bl:bl + 1], (bc, 6 * D)) for m in mods]
    h_ctx, h_lat = x_prompt, x_sample
    sh, sc = _split_mod(mods_ctx[0])[:2]
    u_ctx = _lnmod(h_ctx, sh, sc)
    sh, sc = _split_mod(mods_lat[0])[:2]
    u_lat = _lnmod(h_lat, sh, sc)
    new_ret, new_gla, new_rw = [], [], []
    for layer in range(DEPTH):
        p = _layer_weights({k: v[layer] for k, v in params.items()})
        h_ctx, u_ctx, (s_ret, s_gla_t, s_rw) = _layer(h_ctx, u_ctx, mods_ctx[layer], mods_ctx[layer + 1],
                                                      z_ret, z_gla_t, z_rw, None, p)
        new_ret.append(s_ret)
        new_gla.append(jnp.swapaxes(s_gla_t, -1, -2))
        new_rw.append(jnp.swapaxes(s_rw, -1, -2))
        h_lat, u_lat, _ = _layer(h_lat, u_lat, mods_lat[layer], mods_lat[layer + 1], state_ret[:, layer],
                                 jnp.swapaxes(state_gla[:, layer], -1, -2),
                                 jnp.swapaxes(state_rwkv[:, layer], -1, -2), rope, p)
    return (h_ctx, h_lat, jnp.stack(new_ret, axis=1), jnp.stack(new_gla, axis=1), jnp.stack(new_rw, axis=1))
```
